```python
import math
import jax, jax.numpy as jnp
from jax import lax
import numpy as np

D_MODEL = 1024
BATCH = 16
SEQ = 2048
DEPTH = 4

N_MLA_HEADS = 4
MLA_NOPE_DIM = 64
MLA_ROPE_DIM = 32
MLA_QK_DIM = MLA_NOPE_DIM + MLA_ROPE_DIM
MLA_V_DIM = 64
MLA_Q_RANK = 384
MLA_KV_RANK = 256
MLA_WIDTH = N_MLA_HEADS * MLA_V_DIM
N_FOX_HEADS = 4
FOX_HEAD_DIM = 64
FOX_WIDTH = N_FOX_HEADS * FOX_HEAD_DIM
FOX_FORGET_BIAS_INIT = 3.0
S5_GROUPS = 16
S5_GROUP_CH = 16
S5_STATE = 64
S5_WIDTH = S5_GROUPS * S5_GROUP_CH
N_BRANCHES = 3
BRANCH_WIDTH = 256
D_FF = 2816
CONV_WIDTH = 3
Q_BLOCK = 128
ROPE_THETA = 10000.0
NORM_EPS = 1e-6
NEG_INF = -1e30
IN_WIDTHS = (MLA_Q_RANK, MLA_KV_RANK, MLA_ROPE_DIM,
             FOX_WIDTH, FOX_WIDTH, FOX_WIDTH, N_FOX_HEADS,
             S5_WIDTH, N_BRANCHES * D_MODEL)
D_IN = sum(IN_WIDTHS)

kernel_name = "hybrid_mla_fox_s5_gated_trunk"


def rms_norm(x, gain):
    x32 = x.astype(jnp.float32)
    y = x32 * lax.rsqrt(jnp.mean(x32 * x32, axis=-1, keepdims=True) + NORM_EPS)
    return (y * gain.astype(jnp.float32)).astype(x.dtype)


def rope_tables(positions):
    inv_freq = ROPE_THETA ** (-jnp.arange(0, MLA_ROPE_DIM, 2, dtype=jnp.float32) / MLA_ROPE_DIM)
    ang = positions.astype(jnp.float32)[..., None] * inv_freq
    return jnp.cos(ang)[:, :, None, :], jnp.sin(ang)[:, :, None, :]


def apply_rope_tail(x, cos, sin):
    x_pass, x_rot = x[..., :MLA_NOPE_DIM], x[..., MLA_NOPE_DIM:]
    x1, x2 = jnp.split(x_rot, 2, axis=-1)
    rot = jnp.concatenate([x1 * cos - x2 * sin, x2 * cos + x1 * sin], axis=-1)
    return jnp.concatenate([x_pass, rot.astype(x.dtype)], axis=-1)


def causal_block_attention(q, k, v, log_forget_cum=None):
    seq = q.shape[1]
    scale = q.shape[-1] ** -0.5
    outs = []
    for i in range(seq // Q_BLOCK):
        q_lo, q_hi = i * Q_BLOCK, (i + 1) * Q_BLOCK
        qb, kb, vb = q[:, q_lo:q_hi], k[:, :q_hi], v[:, :q_hi]
        s = jnp.einsum('bqhd,bkhd->bhqk', qb, kb).astype(jnp.float32) * scale
        if log_forget_cum is not None:
            c_q = jnp.transpose(log_forget_cum[:, q_lo:q_hi], (0, 2, 1))[..., :, None]
            c_k = jnp.transpose(log_forget_cum[:, :q_hi], (0, 2, 1))[..., None, :]
            s = s + (c_q - c_k)
        mask = (q_lo + jnp.arange(Q_BLOCK))[:, None] >= jnp.arange(q_hi)[None, :]
        p = jax.nn.softmax(jnp.where(mask, s, NEG_INF), axis=-1).astype(v.dtype)
        outs.append(jnp.einsum('bhqk,bkhd->bqhd', p, vb))
    return jnp.concatenate(outs, axis=1)


def s5_branch(u, lam_re, lam_im, b_re, b_im, c_re, c_im, d, log_step, w_glu, b_glu):
    bsz, seq, _ = u.shape
    u32 = u.astype(jnp.float32).reshape(bsz, seq, S5_GROUPS, S5_GROUP_CH)
    lam = lax.complex(lam_re.astype(jnp.float32), lam_im.astype(jnp.float32))
    step = jnp.exp(log_step.astype(jnp.float32))[:, None]
    lam_bar = jnp.exp(lam * step)
    b_mat = lax.complex(b_re.astype(jnp.float32), b_im.astype(jnp.float32))
    b_bar = ((lam_bar - 1.0) / lam)[..., None] * b_mat
    bu = jnp.einsum('gph,bsgh->bsgp', b_bar, u32.astype(jnp.complex64))
    a = jnp.broadcast_to(lam_bar, bu.shape)

    def combine(e1, e2):
        a1, x1 = e1
        a2, x2 = e2
        return a1 * a2, a2 * x1 + x2

    _, state = lax.associative_scan(combine, (a, bu), axis=1)
    c_mat = lax.complex(c_re.astype(jnp.float32), c_im.astype(jnp.float32))
    y = jnp.real(jnp.einsum('ghp,bsgp->bsgh', c_mat, state)) + d.astype(jnp.float32) * u32
    y = jax.nn.gelu(y.reshape(bsz, seq, S5_WIDTH))
    y = y * jax.nn.sigmoid(y @ w_glu.astype(jnp.float32) + b_glu.astype(jnp.float32))
    return y.astype(u.dtype)


def causal_depthwise_conv(h, w):
    seq = h.shape[1]
    hp = jnp.pad(h, ((0, 0), (CONV_WIDTH - 1, 0), (0, 0)))
    return sum(w[j] * hp[:, j:j + seq] for j in range(CONV_WIDTH))


def mixer_block(h, cos, sin, w_in, q_lat_norm_g, w_uq, kv_lat_norm_g, w_ukv,
                mla_q_norm_g, mla_k_norm_g, fox_q_norm_g, fox_k_norm_g, fox_f_bias,
                s5_lambda_re, s5_lambda_im, s5_b_re, s5_b_im, s5_c_re, s5_c_im, s5_d,
                s5_log_step, s5_w_glu, s5_b_glu, w_branch, w_out):
    bsz, seq, _ = h.shape
    proj = h @ w_in
    split_points = np.cumsum(IN_WIDTHS)[:-1].tolist()
    c_q, c_kv, k_r, fq, fk, fv, f_logit, u, gate_logits = jnp.split(proj, split_points, axis=-1)

    q = jnp.einsum('bsr,rhd->bshd', rms_norm(c_q, q_lat_norm_g), w_uq)
    kv = jnp.einsum('bsr,rhd->bshd', rms_norm(c_kv, kv_lat_norm_g), w_ukv)
    k_nope, v_mla = kv[..., :MLA_NOPE_DIM], kv[..., MLA_NOPE_DIM:]
    k_rope = jnp.broadcast_to(k_r[:, :, None, :], (bsz, seq, N_MLA_HEADS, MLA_ROPE_DIM))
    k_mla = jnp.concatenate([k_nope, k_rope], axis=-1)
    q_mla = apply_rope_tail(rms_norm(q, mla_q_norm_g), cos, sin)
    k_mla = apply_rope_tail(rms_norm(k_mla, mla_k_norm_g), cos, sin)
    o_mla = causal_block_attention(q_mla, k_mla, v_mla).reshape(bsz, seq, MLA_WIDTH)

    q_fox = rms_norm(fq.reshape(bsz, seq, N_FOX_HEADS, FOX_HEAD_DIM), fox_q_norm_g)
    k_fox = rms_norm(fk.reshape(bsz, seq, N_FOX_HEADS, FOX_HEAD_DIM), fox_k_norm_g)
    v_fox = fv.reshape(bsz, seq, N_FOX_HEADS, FOX_HEAD_DIM)
    log_f = jax.nn.log_sigmoid((f_logit + fox_f_bias).astype(jnp.float32))
    cum_log_f = lax.cumsum(log_f, axis=1)
    o_fox = causal_block_attention(q_fox, k_fox, v_fox, cum_log_f).reshape(bsz, seq, FOX_WIDTH)

    o_s5 = s5_branch(u, s5_lambda_re, s5_lambda_im, s5_b_re, s5_b_im, s5_c_re, s5_c_im,
                     s5_d, s5_log_step, s5_w_glu, s5_b_glu)

    branches = jnp.stack([o_mla, o_fox, o_s5], axis=2)
    gates = jax.nn.sigmoid(gate_logits.reshape(bsz, seq, N_BRANCHES, D_MODEL))
    projected = jnp.einsum('bsnw,nwd->bsnd', branches, w_branch)
    merged = jnp.einsum('bsnd,bsnd->bsd', gates, projected)
    return merged @ w_out


def conv_gated_ffn(h, w_up, conv_w, w_down):
    up = causal_depthwise_conv(h @ w_up, conv_w)
    gate, val = jnp.split(up, 2, axis=-1)
    return (jax.nn.silu(gate) * val) @ w_down


def _fwd_setup_inputs(seed: int = 0) -> dict:
    key = jax.random.key(seed)
    ks = iter(jax.random.split(key, 40))
    f32 = jnp.float32
    L = DEPTH

    def nrm(shape, scale):
        return scale * jax.random.normal(next(ks), shape, f32)

    def gain(shape):
        return 1.0 + nrm(shape, 0.02)

    x = nrm((BATCH, SEQ, D_MODEL), 1.0)
    offset = jax.random.randint(next(ks), (BATCH, 1), 0, 4096, dtype=jnp.int32)
    positions = offset + jnp.arange(SEQ, dtype=jnp.int32)[None, :]

    attn_norm_g = gain((L, D_MODEL))
    w_in = nrm((L, D_MODEL, D_IN), D_MODEL ** -0.5)
    q_lat_norm_g = gain((L, MLA_Q_RANK))
    w_uq = nrm((L, MLA_Q_RANK, N_MLA_HEADS, MLA_QK_DIM), MLA_Q_RANK ** -0.5)
    kv_lat_norm_g = gain((L, MLA_KV_RANK))
    w_ukv = nrm((L, MLA_KV_RANK, N_MLA_HEADS, MLA_NOPE_DIM + MLA_V_DIM), MLA_KV_RANK ** -0.5)
    mla_q_norm_g = gain((L, MLA_QK_DIM))
    mla_k_norm_g = gain((L, MLA_QK_DIM))
    fox_q_norm_g = gain((L, FOX_HEAD_DIM))
    fox_k_norm_g = gain((L, FOX_HEAD_DIM))
    fox_f_bias = FOX_FORGET_BIAS_INIT + nrm((L, N_FOX_HEADS), 0.5)
    s5_lambda_re = -0.5 + nrm((L, S5_GROUPS, S5_STATE), 0.01)
    s5_lambda_im = math.pi * jnp.arange(S5_STATE, dtype=f32) + nrm((L, S5_GROUPS, S5_STATE), 0.01)
    s5_b_re = nrm((L, S5_GROUPS, S5_STATE, S5_GROUP_CH), (2 * S5_GROUP_CH) ** -0.5)
    s5_b_im = nrm((L, S5_GROUPS, S5_STATE, S5_GROUP_CH), (2 * S5_GROUP_CH) ** -0.5)
    s5_c_re = nrm((L, S5_GROUPS, S5_GROUP_CH, S5_STATE), (2 * S5_STATE) ** -0.5)
    s5_c_im = nrm((L, S5_GROUPS, S5_GROUP_CH, S5_STATE), (2 * S5_STATE) ** -0.5)
    s5_d = nrm((L, S5_GROUPS, S5_GROUP_CH), 0.5)
    s5_log_step = jnp.log(jax.random.uniform(next(ks), (L, S5_GROUPS), f32, minval=0.001, maxval=0.1))
    s5_w_glu = nrm((L, S5_WIDTH, S5_WIDTH), S5_WIDTH ** -0.5)
    s5_b_glu = nrm((L, S5_WIDTH), 0.02)
    w_branch = nrm((L, N_BRANCHES, BRANCH_WIDTH, D_MODEL), BRANCH_WIDTH ** -0.5)
    w_out = nrm((L, D_MODEL, D_MODEL), D_MODEL ** -0.5)
    ffn_norm_g = gain((L, D_MODEL))
    w_up = nrm((L, D_MODEL, 2 * D_FF), D_MODEL ** -0.5)
    ffn_conv_w = nrm((L, CONV_WIDTH, 2 * D_FF), 0.2).at[:, CONV_WIDTH - 1].add(1.0)
    w_down = nrm((L, D_FF, D_MODEL), D_FF ** -0.5)
    return {"x": x, "positions": positions, "attn_norm_g": attn_norm_g, "w_in": w_in,
            "q_lat_norm_g": q_lat_norm_g, "w_uq": w_uq, "kv_lat_norm_g": kv_lat_norm_g,
            "w_ukv": w_ukv, "mla_q_norm_g": mla_q_norm_g, "mla_k_norm_g": mla_k_norm_g,
            "fox_q_norm_g": fox_q_norm_g, "fox_k_norm_g": fox_k_norm_g, "fox_f_bias": fox_f_bias,
            "s5_lambda_re": s5_lambda_re, "s5_lambda_im": s5_lambda_im, "s5_b_re": s5_b_re,
            "s5_b_im": s5_b_im, "s5_c_re": s5_c_re, "s5_c_im": s5_c_im, "s5_d": s5_d,
            "s5_log_step": s5_log_step, "s5_w_glu": s5_w_glu, "s5_b_glu": s5_b_glu,
            "w_branch": w_branch, "w_out": w_out, "ffn_norm_g": ffn_norm_g, "w_up": w_up,
            "ffn_conv_w": ffn_conv_w, "w_down": w_down}


def _fwd_reference(x, positions, attn_norm_g, w_in, q_lat_norm_g, w_uq, kv_lat_norm_g, w_ukv,
              mla_q_norm_g, mla_k_norm_g, fox_q_norm_g, fox_k_norm_g, fox_f_bias,
              s5_lambda_re, s5_lambda_im, s5_b_re, s5_b_im, s5_c_re, s5_c_im, s5_d,
              s5_log_step, s5_w_glu, s5_b_glu, w_branch, w_out, ffn_norm_g, w_up,
              ffn_conv_w, w_down):
    cos, sin = rope_tables(positions)
    for l in range(DEPTH):
        h = rms_norm(x, attn_norm_g[l])
        x = x + mixer_block(h, cos, sin, w_in[l], q_lat_norm_g[l], w_uq[l], kv_lat_norm_g[l],
                            w_ukv[l], mla_q_norm_g[l], mla_k_norm_g[l], fox_q_norm_g[l],
                            fox_k_norm_g[l], fox_f_bias[l], s5_lambda_re[l], s5_lambda_im[l],
                            s5_b_re[l], s5_b_im[l], s5_c_re[l], s5_c_im[l], s5_d[l],
                            s5_log_step[l], s5_w_glu[l], s5_b_glu[l], w_branch[l], w_out[l])
        h = rms_norm(x, ffn_norm_g[l])
        x = x + conv_gated_ffn(h, w_up[l], ffn_conv_w[l], w_down[l])
    return x


import jax as _jax
import jax.numpy as _jnp

TWIN_FORMAT = 'train_step'
FWD_PARAMS = ['x', 'positions', 'attn_norm_g', 'w_in', 'q_lat_norm_g', 'w_uq', 'kv_lat_norm_g', 'w_ukv', 'mla_q_norm_g', 'mla_k_norm_g', 'fox_q_norm_g', 'fox_k_norm_g', 'fox_f_bias', 's5_lambda_re', 's5_lambda_im', 's5_b_re', 's5_b_im', 's5_c_re', 's5_c_im', 's5_d', 's5_log_step', 's5_w_glu', 's5_b_glu', 'w_branch', 'w_out', 'ffn_norm_g', 'w_up', 'ffn_conv_w', 'w_down']
TWIN_WEIGHTS = ['attn_norm_g', 'w_in', 'q_lat_norm_g', 'w_uq', 'kv_lat_norm_g', 'w_ukv', 'mla_q_norm_g', 'mla_k_norm_g', 'fox_q_norm_g', 'fox_k_norm_g', 'fox_f_bias', 's5_lambda_re', 's5_lambda_im', 's5_b_re', 's5_b_im', 's5_c_re', 's5_c_im', 's5_d', 's5_log_step', 's5_w_glu', 's5_b_glu', 'w_branch', 'w_out', 'ffn_norm_g', 'w_up', 'ffn_conv_w', 'w_down']
TWIN_DIFF_INPUT = 'x'
TWIN_INPUTS = ['x', 'positions', 'attn_norm_g', 'w_in', 'q_lat_norm_g', 'w_uq', 'kv_lat_norm_g', 'w_ukv', 'mla_q_norm_g', 'mla_k_norm_g', 'fox_q_norm_g', 'fox_k_norm_g', 'fox_f_bias', 's5_lambda_re', 's5_lambda_im', 's5_b_re', 's5_b_im', 's5_c_re', 's5_c_im', 's5_d', 's5_log_step', 's5_w_glu', 's5_b_glu', 'w_branch', 'w_out', 'ffn_norm_g', 'w_up', 'ffn_conv_w', 'w_down', 'loss_target', 'm_attn_norm_g', 'm_w_in', 'm_q_lat_norm_g', 'm_w_uq', 'm_kv_lat_norm_g', 'm_w_ukv', 'm_mla_q_norm_g', 'm_mla_k_norm_g', 'm_fox_q_norm_g', 'm_fox_k_norm_g', 'm_fox_f_bias', 'm_s5_lambda_re', 'm_s5_lambda_im', 'm_s5_b_re', 'm_s5_b_im', 'm_s5_c_re', 'm_s5_c_im', 'm_s5_d', 'm_s5_log_step', 'm_s5_w_glu', 'm_s5_b_glu', 'm_w_branch', 'm_w_out', 'm_ffn_norm_g', 'm_w_up', 'm_ffn_conv_w', 'm_w_down', 'v_attn_norm_g', 'v_w_in', 'v_q_lat_norm_g', 'v_w_uq', 'v_kv_lat_norm_g', 'v_w_ukv', 'v_mla_q_norm_g', 'v_mla_k_norm_g', 'v_fox_q_norm_g', 'v_fox_k_norm_g', 'v_fox_f_bias', 'v_s5_lambda_re', 'v_s5_lambda_im', 'v_s5_b_re', 'v_s5_b_im', 'v_s5_c_re', 'v_s5_c_im', 'v_s5_d', 'v_s5_log_step', 'v_s5_w_glu', 'v_s5_b_glu', 'v_w_branch', 'v_w_out', 'v_ffn_norm_g', 'v_w_up', 'v_ffn_conv_w', 'v_w_down']
TWIN_OUTPUTS = ['loss', 'grad_x', 'grad_attn_norm_g', 'grad_w_in', 'grad_q_lat_norm_g', 'grad_w_uq', 'grad_kv_lat_norm_g', 'grad_w_ukv', 'grad_mla_q_norm_g', 'grad_mla_k_norm_g', 'grad_fox_q_norm_g', 'grad_fox_k_norm_g', 'grad_fox_f_bias', 'grad_s5_lambda_re', 'grad_s5_lambda_im', 'grad_s5_b_re', 'grad_s5_b_im', 'grad_s5_c_re', 'grad_s5_c_im', 'grad_s5_d', 'grad_s5_log_step', 'grad_s5_w_glu', 'grad_s5_b_glu', 'grad_w_branch', 'grad_w_out', 'grad_ffn_norm_g', 'grad_w_up', 'grad_ffn_conv_w', 'grad_w_down', 'delta_attn_norm_g', 'delta_w_in', 'delta_q_lat_norm_g', 'delta_w_uq', 'delta_kv_lat_norm_g', 'delta_w_ukv', 'delta_mla_q_norm_g', 'delta_mla_k_norm_g', 'delta_fox_q_norm_g', 'delta_fox_k_norm_g', 'delta_fox_f_bias', 'delta_s5_lambda_re', 'delta_s5_lambda_im', 'delta_s5_b_re', 'delta_s5_b_im', 'delta_s5_c_re', 'delta_s5_c_im', 'delta_s5_d', 'delta_s5_log_step', 'delta_s5_w_glu', 'delta_s5_b_glu', 'delta_w_branch', 'delta_w_out', 'delta_ffn_norm_g', 'delta_w_up', 'delta_ffn_conv_w', 'delta_w_down', 'new_m_attn_norm_g', 'new_m_w_in', 'new_m_q_lat_norm_g', 'new_m_w_uq', 'new_m_kv_lat_norm_g', 'new_m_w_ukv', 'new_m_mla_q_norm_g', 'new_m_mla_k_norm_g', 'new_m_fox_q_norm_g', 'new_m_fox_k_norm_g', 'new_m_fox_f_bias', 'new_m_s5_lambda_re', 'new_m_s5_lambda_im', 'new_m_s5_b_re', 'new_m_s5_b_im', 'new_m_s5_c_re', 'new_m_s5_c_im', 'new_m_s5_d', 'new_m_s5_log_step', 'new_m_s5_w_glu', 'new_m_s5_b_glu', 'new_m_w_branch', 'new_m_w_out', 'new_m_ffn_norm_g', 'new_m_w_up', 'new_m_ffn_conv_w', 'new_m_w_down', 'new_v_attn_norm_g', 'new_v_w_in', 'new_v_q_lat_norm_g', 'new_v_w_uq', 'new_v_kv_lat_norm_g', 'new_v_w_ukv', 'new_v_mla_q_norm_g', 'new_v_mla_k_norm_g', 'new_v_fox_q_norm_g', 'new_v_fox_k_norm_g', 'new_v_fox_f_bias', 'new_v_s5_lambda_re', 'new_v_s5_lambda_im', 'new_v_s5_b_re', 'new_v_s5_b_im', 'new_v_s5_c_re', 'new_v_s5_c_im', 'new_v_s5_d', 'new_v_s5_log_step', 'new_v_s5_w_glu', 'new_v_s5_b_glu', 'new_v_w_branch', 'new_v_w_out', 'new_v_ffn_norm_g', 'new_v_w_up', 'new_v_ffn_conv_w', 'new_v_w_down']
TWIN_LEAF_KINDS = {'loss': 'loss', 'grad_x': 'grad_x', 'grad_attn_norm_g': 'grad_w', 'grad_w_in': 'grad_w', 'grad_q_lat_norm_g': 'grad_w', 'grad_w_uq': 'grad_w', 'grad_kv_lat_norm_g': 'grad_w', 'grad_w_ukv': 'grad_w', 'grad_mla_q_norm_g': 'grad_w', 'grad_mla_k_norm_g': 'grad_w', 'grad_fox_q_norm_g': 'grad_w', 'grad_fox_k_norm_g': 'grad_w', 'grad_fox_f_bias': 'grad_w', 'grad_s5_lambda_re': 'grad_w', 'grad_s5_lambda_im': 'grad_w', 'grad_s5_b_re': 'grad_w', 'grad_s5_b_im': 'grad_w', 'grad_s5_c_re': 'grad_w', 'grad_s5_c_im': 'grad_w', 'grad_s5_d': 'grad_w', 'grad_s5_log_step': 'grad_w', 'grad_s5_w_glu': 'grad_w', 'grad_s5_b_glu': 'grad_w', 'grad_w_branch': 'grad_w', 'grad_w_out': 'grad_w', 'grad_ffn_norm_g': 'grad_w', 'grad_w_up': 'grad_w', 'grad_ffn_conv_w': 'grad_w', 'grad_w_down': 'grad_w', 'delta_attn_norm_g': 'delta_w', 'delta_w_in': 'delta_w', 'delta_q_lat_norm_g': 'delta_w', 'delta_w_uq': 'delta_w', 'delta_kv_lat_norm_g': 'delta_w', 'delta_w_ukv': 'delta_w', 'delta_mla_q_norm_g': 'delta_w', 'delta_mla_k_norm_g': 'delta_w', 'delta_fox_q_norm_g': 'delta_w', 'delta_fox_k_norm_g': 'delta_w', 'delta_fox_f_bias': 'delta_w', 'delta_s5_lambda_re': 'delta_w', 'delta_s5_lambda_im': 'delta_w', 'delta_s5_b_re': 'delta_w', 'delta_s5_b_im': 'delta_w', 'delta_s5_c_re': 'delta_w', 'delta_s5_c_im': 'delta_w', 'delta_s5_d': 'delta_w', 'delta_s5_log_step': 'delta_w', 'delta_s5_w_glu': 'delta_w', 'delta_s5_b_glu': 'delta_w', 'delta_w_branch': 'delta_w', 'delta_w_out': 'delta_w', 'delta_ffn_norm_g': 'delta_w', 'delta_w_up': 'delta_w', 'delta_ffn_conv_w': 'delta_w', 'delta_w_down': 'delta_w', 'new_m_attn_norm_g': 'new_m', 'new_m_w_in': 'new_m', 'new_m_q_lat_norm_g': 'new_m', 'new_m_w_uq': 'new_m', 'new_m_kv_lat_norm_g': 'new_m', 'new_m_w_ukv': 'new_m', 'new_m_mla_q_norm_g': 'new_m', 'new_m_mla_k_norm_g': 'new_m', 'new_m_fox_q_norm_g': 'new_m', 'new_m_fox_k_norm_g': 'new_m', 'new_m_fox_f_bias': 'new_m', 'new_m_s5_lambda_re': 'new_m', 'new_m_s5_lambda_im': 'new_m', 'new_m_s5_b_re': 'new_m', 'new_m_s5_b_im': 'new_m', 'new_m_s5_c_re': 'new_m', 'new_m_s5_c_im': 'new_m', 'new_m_s5_d': 'new_m', 'new_m_s5_log_step': 'new_m', 'new_m_s5_w_glu': 'new_m', 'new_m_s5_b_glu': 'new_m', 'new_m_w_branch': 'new_m', 'new_m_w_out': 'new_m', 'new_m_ffn_norm_g': 'new_m', 'new_m_w_up': 'new_m', 'new_m_ffn_conv_w': 'new_m', 'new_m_w_down': 'new_m', 'new_v_attn_norm_g': 'new_v', 'new_v_w_in': 'new_v', 'new_v_q_lat_norm_g': 'new_v', 'new_v_w_uq': 'new_v', 'new_v_kv_lat_norm_g': 'new_v', 'new_v_w_ukv': 'new_v', 'new_v_mla_q_norm_g': 'new_v', 'new_v_mla_k_norm_g': 'new_v', 'new_v_fox_q_norm_g': 'new_v', 'new_v_fox_k_norm_g': 'new_v', 'new_v_fox_f_bias': 'new_v', 'new_v_s5_lambda_re': 'new_v', 'new_v_s5_lambda_im': 'new_v', 'new_v_s5_b_re': 'new_v', 'new_v_s5_b_im': 'new_v', 'new_v_s5_c_re': 'new_v', 'new_v_s5_c_im': 'new_v', 'new_v_s5_d': 'new_v', 'new_v_s5_log_step': 'new_v', 'new_v_s5_w_glu': 'new_v', 'new_v_s5_b_glu': 'new_v', 'new_v_w_branch': 'new_v', 'new_v_w_out': 'new_v', 'new_v_ffn_norm_g': 'new_v', 'new_v_w_up': 'new_v', 'new_v_ffn_conv_w': 'new_v', 'new_v_w_down': 'new_v'}


def _forward(args):
    return _fwd_reference(*[args[k] for k in FWD_PARAMS])


def _output_shape():
    out = _jax.eval_shape(lambda: _forward(_fwd_setup_inputs(0)))
    return out.shape, out.dtype

N_MICROBATCH = 1
ADAM_LR = 0.001
ADAM_B1 = 0.9
ADAM_B2 = 0.999
ADAM_EPS = 1e-08
ADAM_WD = 0.01
ADAM_STEP = 10
PER_EXAMPLE_BATCH_AXIS = {'x': 0, 'positions': 0, 'loss_target': 0}
SHARED_INPUTS = []
_WEIGHT_DTYPES = {'attn_norm_g': _jnp.float32, 'w_in': _jnp.float32, 'q_lat_norm_g': _jnp.float32, 'w_uq': _jnp.float32, 'kv_lat_norm_g': _jnp.float32, 'w_ukv': _jnp.float32, 'mla_q_norm_g': _jnp.float32, 'mla_k_norm_g': _jnp.float32, 'fox_q_norm_g': _jnp.float32, 'fox_k_norm_g': _jnp.float32, 'fox_f_bias': _jnp.float32, 's5_lambda_re': _jnp.float32, 's5_lambda_im': _jnp.float32, 's5_b_re': _jnp.float32, 's5_b_im': _jnp.float32, 's5_c_re': _jnp.float32, 's5_c_im': _jnp.float32, 's5_d': _jnp.float32, 's5_log_step': _jnp.float32, 's5_w_glu': _jnp.float32, 's5_b_glu': _jnp.float32, 'w_branch': _jnp.float32, 'w_out': _jnp.float32, 'ffn_norm_g': _jnp.float32, 'w_up': _jnp.float32, 'ffn_conv_w': _jnp.float32, 'w_down': _jnp.float32}
MOMENT_SCALE = {'attn_norm_g': 1.044045e+00, 'w_in': 1.549758e-01, 'q_lat_norm_g': 1.220903e-01, 'w_uq': 1.328098e-01, 'kv_lat_norm_g': 5.884804e-01, 'w_ukv': 2.507072e-01, 'mla_q_norm_g': 5.101660e-01, 'mla_k_norm_g': 5.084700e-01, 'fox_q_norm_g': 6.000707e+00, 'fox_k_norm_g': 5.977619e+00, 'fox_f_bias': 6.961536e+01, 's5_lambda_re': 2.624045e-02, 's5_lambda_im': 2.031457e-02, 's5_b_re': 1.349481e-02, 's5_b_im': 1.321467e-02, 's5_c_re': 2.676848e-02, 's5_c_im': 2.518502e-02, 's5_d': 2.707870e+00, 's5_log_step': 9.305171e+00, 's5_w_glu': 1.479226e-01, 's5_b_glu': 9.630421e-01, 'w_branch': 1.473964e-01, 'w_out': 2.578451e-01, 'ffn_norm_g': 3.212418e+01, 'w_up': 3.123481e-01, 'ffn_conv_w': 3.245652e+00, 'w_down': 4.827771e-01}


def _to_microbatches(a, axis):
    t = _jnp.moveaxis(a, axis, 0)
    t = t.reshape((N_MICROBATCH, t.shape[0] // N_MICROBATCH) + t.shape[1:])
    return _jnp.moveaxis(t, 1, axis + 1)


def setup_inputs(seed: int = 0) -> dict:
    inp = _fwd_setup_inputs(seed)
    key = _jax.random.fold_in(_jax.random.key(seed), 7919)
    shape, _ = _output_shape()
    out = dict(inp)
    out["loss_target"] = _jax.random.normal(_jax.random.fold_in(key, 0), shape, _jnp.float32)
    for i, name in enumerate(TWIN_WEIGHTS):
        w = inp[name].astype(_jnp.float32)
        if MOMENT_SCALE is None:
            s = _jnp.sqrt(_jnp.mean(_jnp.square(w)) + 1e-30)
        else:
            s = MOMENT_SCALE[name]
        km, kv = _jax.random.split(_jax.random.fold_in(key, i + 1))
        out[name] = w
        out["m_" + name] = s * _jax.random.normal(km, w.shape, _jnp.float32)
        out["v_" + name] = (s * s) * _jax.random.uniform(kv, w.shape, _jnp.float32, 0.5, 1.5)
    if N_MICROBATCH > 1:
        for name, axis in PER_EXAMPLE_BATCH_AXIS.items():
            out[name] = _to_microbatches(out[name], axis)
    return {'x': out['x'], 'positions': out['positions'], 'attn_norm_g': out['attn_norm_g'], 'w_in': out['w_in'], 'q_lat_norm_g': out['q_lat_norm_g'], 'w_uq': out['w_uq'], 'kv_lat_norm_g': out['kv_lat_norm_g'], 'w_ukv': out['w_ukv'], 'mla_q_norm_g': out['mla_q_norm_g'], 'mla_k_norm_g': out['mla_k_norm_g'], 'fox_q_norm_g': out['fox_q_norm_g'], 'fox_k_norm_g': out['fox_k_norm_g'], 'fox_f_bias': out['fox_f_bias'], 's5_lambda_re': out['s5_lambda_re'], 's5_lambda_im': out['s5_lambda_im'], 's5_b_re': out['s5_b_re'], 's5_b_im': out['s5_b_im'], 's5_c_re': out['s5_c_re'], 's5_c_im': out['s5_c_im'], 's5_d': out['s5_d'], 's5_log_step': out['s5_log_step'], 's5_w_glu': out['s5_w_glu'], 's5_b_glu': out['s5_b_glu'], 'w_branch': out['w_branch'], 'w_out': out['w_out'], 'ffn_norm_g': out['ffn_norm_g'], 'w_up': out['w_up'], 'ffn_conv_w': out['ffn_conv_w'], 'w_down': out['w_down'], 'loss_target': out['loss_target'], 'm_attn_norm_g': out['m_attn_norm_g'], 'm_w_in': out['m_w_in'], 'm_q_lat_norm_g': out['m_q_lat_norm_g'], 'm_w_uq': out['m_w_uq'], 'm_kv_lat_norm_g': out['m_kv_lat_norm_g'], 'm_w_ukv': out['m_w_ukv'], 'm_mla_q_norm_g': out['m_mla_q_norm_g'], 'm_mla_k_norm_g': out['m_mla_k_norm_g'], 'm_fox_q_norm_g': out['m_fox_q_norm_g'], 'm_fox_k_norm_g': out['m_fox_k_norm_g'], 'm_fox_f_bias': out['m_fox_f_bias'], 'm_s5_lambda_re': out['m_s5_lambda_re'], 'm_s5_lambda_im': out['m_s5_lambda_im'], 'm_s5_b_re': out['m_s5_b_re'], 'm_s5_b_im': out['m_s5_b_im'], 'm_s5_c_re': out['m_s5_c_re'], 'm_s5_c_im': out['m_s5_c_im'], 'm_s5_d': out['m_s5_d'], 'm_s5_log_step': out['m_s5_log_step'], 'm_s5_w_glu': out['m_s5_w_glu'], 'm_s5_b_glu': out['m_s5_b_glu'], 'm_w_branch': out['m_w_branch'], 'm_w_out': out['m_w_out'], 'm_ffn_norm_g': out['m_ffn_norm_g'], 'm_w_up': out['m_w_up'], 'm_ffn_conv_w': out['m_ffn_conv_w'], 'm_w_down': out['m_w_down'], 'v_attn_norm_g': out['v_attn_norm_g'], 'v_w_in': out['v_w_in'], 'v_q_lat_norm_g': out['v_q_lat_norm_g'], 'v_w_uq': out['v_w_uq'], 'v_kv_lat_norm_g': out['v_kv_lat_norm_g'], 'v_w_ukv': out['v_w_ukv'], 'v_mla_q_norm_g': out['v_mla_q_norm_g'], 'v_mla_k_norm_g': out['v_mla_k_norm_g'], 'v_fox_q_norm_g': out['v_fox_q_norm_g'], 'v_fox_k_norm_g': out['v_fox_k_norm_g'], 'v_fox_f_bias': out['v_fox_f_bias'], 'v_s5_lambda_re': out['v_s5_lambda_re'], 'v_s5_lambda_im': out['v_s5_lambda_im'], 'v_s5_b_re': out['v_s5_b_re'], 'v_s5_b_im': out['v_s5_b_im'], 'v_s5_c_re': out['v_s5_c_re'], 'v_s5_c_im': out['v_s5_c_im'], 'v_s5_d': out['v_s5_d'], 'v_s5_log_step': out['v_s5_log_step'], 'v_s5_w_glu': out['v_s5_w_glu'], 'v_s5_b_glu': out['v_s5_b_glu'], 'v_w_branch': out['v_w_branch'], 'v_w_out': out['v_w_out'], 'v_ffn_norm_g': out['v_ffn_norm_g'], 'v_w_up': out['v_w_up'], 'v_ffn_conv_w': out['v_ffn_conv_w'], 'v_w_down': out['v_w_down']}


def _loss(weights, diff, rest, loss_target):
    with _jax.named_scope("forward"):
        args = {**rest, TWIN_DIFF_INPUT: diff, **{k: w.astype(_WEIGHT_DTYPES[k]) for k, w in weights.items()}}
        y = _forward(args)
    with _jax.named_scope("loss_head"):
        err = _jnp.square(y.astype(_jnp.float32) - loss_target)
        return 0.5 * _jnp.sum(_jnp.mean(err, axis=-1)) if err.ndim else 0.5 * err


def _adamw(w, g, m, v):
    m = ADAM_B1 * m + (1.0 - ADAM_B1) * g
    v = ADAM_B2 * v + (1.0 - ADAM_B2) * _jnp.square(g)
    m_hat = m / (1.0 - ADAM_B1 ** ADAM_STEP)
    v_hat = v / (1.0 - ADAM_B2 ** ADAM_STEP)
    delta = -ADAM_LR * (m_hat / (_jnp.sqrt(v_hat) + ADAM_EPS) + ADAM_WD * w)
    return delta, m, v


def reference(x, positions, attn_norm_g, w_in, q_lat_norm_g, w_uq, kv_lat_norm_g, w_ukv, mla_q_norm_g, mla_k_norm_g, fox_q_norm_g, fox_k_norm_g, fox_f_bias, s5_lambda_re, s5_lambda_im, s5_b_re, s5_b_im, s5_c_re, s5_c_im, s5_d, s5_log_step, s5_w_glu, s5_b_glu, w_branch, w_out, ffn_norm_g, w_up, ffn_conv_w, w_down, loss_target, m_attn_norm_g, m_w_in, m_q_lat_norm_g, m_w_uq, m_kv_lat_norm_g, m_w_ukv, m_mla_q_norm_g, m_mla_k_norm_g, m_fox_q_norm_g, m_fox_k_norm_g, m_fox_f_bias, m_s5_lambda_re, m_s5_lambda_im, m_s5_b_re, m_s5_b_im, m_s5_c_re, m_s5_c_im, m_s5_d, m_s5_log_step, m_s5_w_glu, m_s5_b_glu, m_w_branch, m_w_out, m_ffn_norm_g, m_w_up, m_ffn_conv_w, m_w_down, v_attn_norm_g, v_w_in, v_q_lat_norm_g, v_w_uq, v_kv_lat_norm_g, v_w_ukv, v_mla_q_norm_g, v_mla_k_norm_g, v_fox_q_norm_g, v_fox_k_norm_g, v_fox_f_bias, v_s5_lambda_re, v_s5_lambda_im, v_s5_b_re, v_s5_b_im, v_s5_c_re, v_s5_c_im, v_s5_d, v_s5_log_step, v_s5_w_glu, v_s5_b_glu, v_w_branch, v_w_out, v_ffn_norm_g, v_w_up, v_ffn_conv_w, v_w_down):
    given = dict(x=x, positions=positions, attn_norm_g=attn_norm_g, w_in=w_in, q_lat_norm_g=q_lat_norm_g, w_uq=w_uq, kv_lat_norm_g=kv_lat_norm_g, w_ukv=w_ukv, mla_q_norm_g=mla_q_norm_g, mla_k_norm_g=mla_k_norm_g, fox_q_norm_g=fox_q_norm_g, fox_k_norm_g=fox_k_norm_g, fox_f_bias=fox_f_bias, s5_lambda_re=s5_lambda_re, s5_lambda_im=s5_lambda_im, s5_b_re=s5_b_re, s5_b_im=s5_b_im, s5_c_re=s5_c_re, s5_c_im=s5_c_im, s5_d=s5_d, s5_log_step=s5_log_step, s5_w_glu=s5_w_glu, s5_b_glu=s5_b_glu, w_branch=w_branch, w_out=w_out, ffn_norm_g=ffn_norm_g, w_up=w_up, ffn_conv_w=ffn_conv_w, w_down=w_down, loss_target=loss_target, m_attn_norm_g=m_attn_norm_g, m_w_in=m_w_in, m_q_lat_norm_g=m_q_lat_norm_g, m_w_uq=m_w_uq, m_kv_lat_norm_g=m_kv_lat_norm_g, m_w_ukv=m_w_ukv, m_mla_q_norm_g=m_mla_q_norm_g, m_mla_k_norm_g=m_mla_k_norm_g, m_fox_q_norm_g=m_fox_q_norm_g, m_fox_k_norm_g=m_fox_k_norm_g, m_fox_f_bias=m_fox_f_bias, m_s5_lambda_re=m_s5_lambda_re, m_s5_lambda_im=m_s5_lambda_im, m_s5_b_re=m_s5_b_re, m_s5_b_im=m_s5_b_im, m_s5_c_re=m_s5_c_re, m_s5_c_im=m_s5_c_im, m_s5_d=m_s5_d, m_s5_log_step=m_s5_log_step, m_s5_w_glu=m_s5_w_glu, m_s5_b_glu=m_s5_b_glu, m_w_branch=m_w_branch, m_w_out=m_w_out, m_ffn_norm_g=m_ffn_norm_g, m_w_up=m_w_up, m_ffn_conv_w=m_ffn_conv_w, m_w_down=m_w_down, v_attn_norm_g=v_attn_norm_g, v_w_in=v_w_in, v_q_lat_norm_g=v_q_lat_norm_g, v_w_uq=v_w_uq, v_kv_lat_norm_g=v_kv_lat_norm_g, v_w_ukv=v_w_ukv, v_mla_q_norm_g=v_mla_q_norm_g, v_mla_k_norm_g=v_mla_k_norm_g, v_fox_q_norm_g=v_fox_q_norm_g, v_fox_k_norm_g=v_fox_k_norm_g, v_fox_f_bias=v_fox_f_bias, v_s5_lambda_re=v_s5_lambda_re, v_s5_lambda_im=v_s5_lambda_im, v_s5_b_re=v_s5_b_re, v_s5_b_im=v_s5_b_im, v_s5_c_re=v_s5_c_re, v_s5_c_im=v_s5_c_im, v_s5_d=v_s5_d, v_s5_log_step=v_s5_log_step, v_s5_w_glu=v_s5_w_glu, v_s5_b_glu=v_s5_b_glu, v_w_branch=v_w_branch, v_w_out=v_w_out, v_ffn_norm_g=v_ffn_norm_g, v_w_up=v_w_up, v_ffn_conv_w=v_ffn_conv_w, v_w_down=v_w_down)
    weights = {n: given[n] for n in TWIN_WEIGHTS}
    shared = {n: given[n] for n in SHARED_INPUTS}
    per_example = {n: given[n] for n in ['x', 'positions']}
    grad_fn = _jax.value_and_grad(_loss, argnums=(0, 1))

    def one_microbatch(ex, loss_target):
        ex = dict(ex)
        diff = ex.pop(TWIN_DIFF_INPUT)
        return grad_fn(weights, diff, {**shared, **ex}, loss_target)

    if N_MICROBATCH == 1:
        loss, (grad_w, grad_x) = one_microbatch(per_example, given["loss_target"])
    else:
        def body(carry, xs):
            loss_sum, grad_sum = carry
            l_k, (gw_k, gx_k) = one_microbatch(xs[0], xs[1])
            with _jax.named_scope("update"):
                return (loss_sum + l_k, _jax.tree.map(_jnp.add, grad_sum, gw_k)), gx_k

        init = (_jnp.zeros((), _jnp.float32), _jax.tree.map(_jnp.zeros_like, weights))
        (loss, grad_w), grad_x = _jax.lax.scan(body, init, (per_example, given["loss_target"]))
    with _jax.named_scope("update"):
        delta_w, new_m, new_v = {}, {}, {}
        for n in TWIN_WEIGHTS:
            delta_w[n], new_m[n], new_v[n] = _adamw(weights[n], grad_w[n], given["m_" + n], given["v_" + n])
    return (loss, grad_x, *[grad_w[n] for n in TWIN_WEIGHTS], *[delta_w[n] for n in TWIN_WEIGHTS],
            *[new_m[n] for n in TWIN_WEIGHTS], *[new_v[n] for n in TWIN_WEIGHTS])
```

```python
import functools
import math

import jax
import jax.numpy as jnp
from jax import lax
from jax.experimental import pallas as pl
from jax.experimental.pallas import tpu as pltpu

F32, BF16 = jnp.float32, jnp.bfloat16
NORM_EPS = 1e-6
NEG_INF = -1e30
ROPE_THETA = 10000.0
LANE = 128
N_HEADS = 4
NOPE, ROPE, QK_DIM, V_DIM = 64, 32, 96, 64
Q_RANK, KV_RANK = 384, 256
FOX_DIM = 64
S5_G, S5_H, S5_P = 16, 16, 64
S5_C = S5_G * S5_P
BW = 256
VMEM_LIMIT = 56 << 20
ADAM_LR, ADAM_B1, ADAM_B2, ADAM_EPS, ADAM_WD, ADAM_STEP = 0.001, 0.9, 0.999, 1e-08, 0.01, 10


def _pick(n, cands):
    for c in cands:
        if n % c == 0:
            return c
    return n


def _cp(*sem):
    return pltpu.CompilerParams(dimension_semantics=sem, vmem_limit_bytes=VMEM_LIMIT)


def _dg(a, b, ca, cb):
    return lax.dot_general(a.astype(BF16), b.astype(BF16), (((ca,), (cb,)), ((), ())),
                           preferred_element_type=F32)


@jax.custom_vjp
def dot_nn(a, b):
    return _dg(a, b, 1, 0)


@jax.custom_vjp
def dot_nt(a, b):
    return _dg(a, b, 1, 1)


@jax.custom_vjp
def dot_tn(a, b):
    return _dg(a, b, 0, 0)


dot_nn.defvjp(lambda a, b: (dot_nn(a, b), (a, b)),
              lambda r, g: (dot_nt(g, r[1]).astype(r[0].dtype), dot_tn(r[0], g).astype(r[1].dtype)))
dot_nt.defvjp(lambda a, b: (dot_nt(a, b), (a, b)),
              lambda r, g: (dot_nn(g, r[1]).astype(r[0].dtype), dot_tn(g, r[0]).astype(r[1].dtype)))
dot_tn.defvjp(lambda a, b: (dot_tn(a, b), (a, b)),
              lambda r, g: (dot_nt(r[1], g).astype(r[0].dtype), dot_nn(r[0], g).astype(r[1].dtype)))


def xdot(a, b):
    return jnp.dot(a, b, precision=lax.Precision.HIGHEST, preferred_element_type=F32)


def mm(a, b, mode, *, name, add=None, out_dtype=F32):
    if mode == "nn":
        (M, K), N = a.shape, b.shape[1]
    elif mode == "nt":
        (M, K), N = a.shape, b.shape[0]
    else:
        (K, M), N = a.shape, b.shape[1]
    tm = _pick(M, (512, 256, 128))
    tn = _pick(N, (512, 384, 256, 128))
    tk = _pick(K, (512, 256, 128))
    nk = K // tk
    ca, cb = {"nn": (1, 0), "nt": (1, 1), "tn": (0, 0)}[mode]

    def body(*refs):
        if add is None:
            a_ref, b_ref, o_ref, acc = refs
        else:
            a_ref, b_ref, add_ref, o_ref, acc = refs
        kk = pl.program_id(2)

        @pl.when(kk == 0)
        def _():
            acc[...] = jnp.zeros_like(acc)

        acc[...] += _dg(a_ref[...], b_ref[...], ca, cb)

        @pl.when(kk == nk - 1)
        def _():
            r = acc[...]
            if add is not None:
                r = r + add_ref[...].astype(F32)
            o_ref[...] = r.astype(out_dtype)

    a_spec = (pl.BlockSpec((tm, tk), lambda i, j, k: (i, k)) if mode != "tn"
              else pl.BlockSpec((tk, tm), lambda i, j, k: (k, i)))
    b_spec = (pl.BlockSpec((tn, tk), lambda i, j, k: (j, k)) if mode == "nt"
              else pl.BlockSpec((tk, tn), lambda i, j, k: (k, j)))
    in_specs, args = [a_spec, b_spec], [a, b]
    if add is not None:
        in_specs.append(pl.BlockSpec((tm, tn), lambda i, j, k: (i, j)))
        args.append(add)
    return pl.pallas_call(
        body, name=name, grid=(M // tm, N // tn, nk),
        in_specs=in_specs, out_specs=pl.BlockSpec((tm, tn), lambda i, j, k: (i, j)),
        out_shape=jax.ShapeDtypeStruct((M, N), out_dtype),
        scratch_shapes=[pltpu.VMEM((tm, tn), F32)],
        compiler_params=_cp("parallel", "parallel", "arbitrary"),
    )(*args)


def _row_spec(tm, width, col):
    if callable(col):
        return pl.BlockSpec((tm, width), lambda i, j: (i, col(j)))
    return pl.BlockSpec((tm, width), lambda i, j: (i, col))


def _const_spec(c):
    return pl.BlockSpec(c.shape, lambda i, j: (0,) * c.ndim)


def rowwise(name, fn, rows, consts, outs, *, tm=256, nj=1):
    M = rows[0][0].shape[0]
    tm = _pick(M, (tm, 128, 64, 32, 16, 8))
    nr, nc = len(rows), len(consts)

    def body(*refs):
        vals = [r[...].astype(F32) for r in refs[:nr + nc]]
        res = fn(*vals)
        for o_ref, r in zip(refs[nr + nc:], res):
            o_ref[...] = r.astype(o_ref.dtype)

    return pl.pallas_call(
        body, name=name, grid=(M // tm, nj),
        in_specs=[_row_spec(tm, w, c) for _, w, c in rows] + [_const_spec(c) for c in consts],
        out_specs=[pl.BlockSpec((tm, w), lambda i, j: (i, j)) for w, _ in outs],
        out_shape=[jax.ShapeDtypeStruct((M, nj * w), dt) for w, dt in outs],
        compiler_params=_cp("parallel", "parallel"),
    )(*[r[0] for r in rows], *consts)


def rowwise_vjp(name, fn, rows, consts, cts, diff, *, tm=256, nj=1):
    M = rows[0][0].shape[0]
    tm = _pick(M, (tm, 128, 64, 32, 16, 8))
    nr, nc, nt, nd = len(rows), len(consts), len(cts), len(diff)

    def body(*refs):
        vals = [r[...].astype(F32) for r in refs[:nr + nc + nt]]
        rv, cv, tv = vals[:nr], vals[nr:nr + nc], vals[nr + nc:]
        grow, gconst = refs[nr + nc + nt:nr + nc + nt + nd], refs[nr + nc + nt + nd:]

        def f(*dargs):
            full = list(rv)
            for pos, val in zip(diff, dargs[:nd]):
                full[pos] = val
            return tuple(fn(*full, *dargs[nd:]))

        _, vjp = jax.vjp(f, *[rv[p] for p in diff], *cv)
        g = vjp(tuple(tv))
        for o_ref, gv in zip(grow, g[:nd]):
            o_ref[...] = gv.astype(o_ref.dtype)
        first = jnp.logical_and(pl.program_id(0) == 0, pl.program_id(1) == 0)
        for o_ref, gv in zip(gconst, g[nd:]):
            @pl.when(first)
            def _(o_ref=o_ref, gv=gv):
                o_ref[...] = gv

            @pl.when(jnp.logical_not(first))
            def _(o_ref=o_ref, gv=gv):
                o_ref[...] += gv

    out_specs = ([pl.BlockSpec((tm, rows[p][1]), lambda i, j: (i, j)) for p in diff]
                 + [_const_spec(c) for c in consts])
    out_shape = ([jax.ShapeDtypeStruct((M, nj * rows[p][1]), F32) for p in diff]
                 + [jax.ShapeDtypeStruct(c.shape, F32) for c in consts])
    res = pl.pallas_call(
        body, name=name, grid=(M // tm, nj),
        in_specs=([_row_spec(tm, w, c) for _, w, c in rows] + [_const_spec(c) for c in consts]
                  + [_row_spec(tm, w, c) for _, w, c in cts]),
        out_specs=out_specs, out_shape=out_shape,
        compiler_params=_cp("arbitrary", "arbitrary"),
    )(*[r[0] for r in rows], *consts, *[t[0] for t in cts])
    return res[:nd], res[nd:]


def _lane(shape=(1, LANE)):
    return lax.broadcasted_iota(jnp.int32, shape, len(shape) - 1)


def _sigmoid(x):
    return 1.0 / (1.0 + jnp.exp(-x))


def _rms(x, g, n):
    return x * lax.rsqrt(jnp.sum(x * x, axis=-1, keepdims=True) * (1.0 / n) + NORM_EPS) * g


def fn_rms(n):
    return lambda x, g: (_rms(x, g, n),)


def _rope(x, cos_t, sin_t):
    i = lax.broadcasted_iota(jnp.int32, (LANE, LANE), 0)
    j = lax.broadcasted_iota(jnp.int32, (LANE, LANE), 1)
    half = ROPE // 2
    lo = jnp.logical_and(jnp.logical_and(j >= NOPE, j < NOPE + half), i == j + half)
    hi = jnp.logical_and(jnp.logical_and(j >= NOPE + half, j < NOPE + ROPE), i == j - half)
    perm = jnp.where(hi, 1.0, 0.0) - jnp.where(lo, 1.0, 0.0)
    return x * cos_t + xdot(x, perm) * sin_t


def fn_qpost(q, cos_t, sin_t, g):
    return (_rope(_rms(q, g, QK_DIM), cos_t, sin_t),)


def fn_kpost(kn, small, cos_t, sin_t, g):
    lane = _lane()
    rope_lanes = jnp.logical_and(lane >= NOPE, lane < NOPE + ROPE)
    kc = kn + jnp.where(rope_lanes, small, 0.0)
    return (_rope(_rms(kc, g, QK_DIM), cos_t, sin_t),)


def fn_foxnorm(x, g):
    first = _lane() < FOX_DIM
    sq = x * x
    s0 = jnp.sum(jnp.where(first, sq, 0.0), axis=-1, keepdims=True)
    s1 = jnp.sum(jnp.where(first, 0.0, sq), axis=-1, keepdims=True)
    r0 = lax.rsqrt(s0 * (1.0 / FOX_DIM) + NORM_EPS)
    r1 = lax.rsqrt(s1 * (1.0 / FOX_DIM) + NORM_EPS)
    return (x * jnp.where(first, r0, r1) * g,)


F_LANE0 = NOPE + ROPE


def fn_fgate(small, bias):
    z = small + bias
    lf = jnp.minimum(z, 0.0) - jnp.log(1.0 + jnp.exp(-jnp.abs(z)))
    lane = _lane()
    return (jnp.where(jnp.logical_and(lane >= F_LANE0, lane < F_LANE0 + N_HEADS), lf, 0.0),)


def _gelu(y):
    return 0.5 * y * (1.0 + jnp.tanh(math.sqrt(2.0 / math.pi) * (y + 0.044715 * (y * y * y))))


def fn_s5post(ypre, u, d, wglu, bglu):
    y = _gelu(ypre + d * u)
    return (y * _sigmoid(dot_nn(y, wglu) + bglu),)


def fn_merge(om, of, os_, g0, g1, g2, wb0, wb1, wb2):
    return (_sigmoid(g0) * dot_nn(om, wb0) + _sigmoid(g1) * dot_nn(of, wb1)
            + _sigmoid(g2) * dot_nn(os_, wb2),)


def fn_add5(a, b):
    return (a[:, 0:LANE] + a[:, LANE:2 * LANE] + a[:, 2 * LANE:3 * LANE] + a[:, 3 * LANE:4 * LANE] + b,)


def fn_addt(a, b):
    return (a + b,)


def fn_s5params(lre, lim, lstep, btr, bti, ctr, cti):
    C = S5_C
    grp = lax.broadcasted_iota(jnp.int32, (LANE, C), 1) >> 6
    expand = jnp.where(lax.broadcasted_iota(jnp.int32, (LANE, C), 0) == grp, 1.0, 0.0)
    lane = _lane()
    st = jnp.where(lane < S5_G, jnp.exp(lstep), 0.0)
    step = jnp.sum(xdot(jnp.broadcast_to(st, (8, LANE)), expand), axis=0, keepdims=True) * 0.125
    zr, zi = lre * step, lim * step
    er = jnp.exp(zr)
    lbr, lbi = er * jnp.cos(zi), er * jnp.sin(zi)
    den = lre * lre + lim * lim
    nr = lbr - 1.0
    cfr = (nr * lre + lbi * lim) / den
    cfi = (lbi * lre - nr * lim) / den
    bbr = cfr * btr - cfi * bti
    bbi = cfr * bti + cfi * btr
    rg = lax.broadcasted_iota(jnp.int32, (BW, C), 0) >> 4
    cg = lax.broadcasted_iota(jnp.int32, (BW, C), 1) >> 6
    mb = jnp.where(rg == cg, 1.0, 0.0)
    b_re = jnp.concatenate([bbr] * S5_G, axis=0) * mb
    b_im = jnp.concatenate([bbi] * S5_G, axis=0) * mb
    ecol = jnp.where(lax.broadcasted_iota(jnp.int32, (LANE, BW), 0)
                     == (lax.broadcasted_iota(jnp.int32, (LANE, BW), 1) & 15), 1.0, 0.0)
    mc = jnp.where((lax.broadcasted_iota(jnp.int32, (C, BW), 0) >> 6)
                   == (lax.broadcasted_iota(jnp.int32, (C, BW), 1) >> 4), 1.0, 0.0)
    c_top = xdot(ctr, ecol) * mc
    c_bot = -(xdot(cti, ecol) * mc)
    return lbr, lbi, b_re, b_im, c_top, c_bot


def s5_params(p):
    def body(lre, lim, ls, btr, bti, ctr, cti, lb_ref, bre_ref, bim_ref, ct_ref, cb_ref):
        lbr, lbi, b_re, b_im, c_top, c_bot = fn_s5params(
            lre[...], lim[...], ls[...], btr[...], bti[...], ctr[...], cti[...])
        lb_ref[0:1, :] = lbr
        lb_ref[1:2, :] = lbi
        bre_ref[...] = b_re.astype(BF16)
        bim_ref[...] = b_im.astype(BF16)
        ct_ref[...] = c_top.astype(BF16)
        cb_ref[...] = c_bot.astype(BF16)

    return pl.pallas_call(
        body, name="s5_params",
        out_shape=[jax.ShapeDtypeStruct((2, S5_C), F32), jax.ShapeDtypeStruct((BW, S5_C), BF16),
                   jax.ShapeDtypeStruct((BW, S5_C), BF16), jax.ShapeDtypeStruct((S5_C, BW), BF16),
                   jax.ShapeDtypeStruct((S5_C, BW), BF16)],
        compiler_params=pltpu.CompilerParams(vmem_limit_bytes=VMEM_LIMIT),
    )(p["lre"], p["lim"], p["lstep"], p["btr"], p["bti"], p["ctr"], p["cti"])


S5_PARAM_NAMES = ("lre", "lim", "lstep", "btr", "bti", "ctr", "cti")


def s5_params_vjp(p, dl_r, dl_i, db_re, db_im, dc_top, dc_bot):
    def body(lre, lim, ls, btr, bti, ctr, cti, dlr, dli, dbr, dbi, dct, dcb, *outs):
        args = [r[...] for r in (lre, lim, ls, btr, bti, ctr, cti)]
        _, vjp = jax.vjp(fn_s5params, *args)
        g = vjp((jnp.sum(dlr[...], axis=0, keepdims=True), jnp.sum(dli[...], axis=0, keepdims=True),
                 dbr[...], dbi[...], dct[...], dcb[...]))
        for o_ref, gv in zip(outs, g):
            o_ref[...] = gv

    return pl.pallas_call(
        body, name="s5_params_vjp",
        out_shape=[jax.ShapeDtypeStruct(p[n].shape, F32) for n in S5_PARAM_NAMES],
        compiler_params=pltpu.CompilerParams(vmem_limit_bytes=VMEM_LIMIT),
    )(*[p[n] for n in S5_PARAM_NAMES], dl_r, dl_i, db_re, db_im, dc_top, dc_bot)


def _attn_tile(q, k, v, cq, ckt, *, hpt, dk, q0, tile):
    tq, S = q.shape[0], k.shape[0]
    row = q0 + lax.broadcasted_iota(jnp.int32, (tq, S), 0)
    col = lax.broadcasted_iota(jnp.int32, (tq, S), 1)
    causal = row >= col
    lane = _lane()
    out = jnp.zeros((tq, LANE), F32)
    for h in range(hpt):
        if hpt > 1:
            mine = (lane >> int(math.log2(LANE // hpt))) == h
            qh = jnp.where(mine, q, 0.0)
        else:
            qh = q
        s = dot_nt(qh, k) * (dk ** -0.5)
        if cq is not None:
            head = tile * hpt + h
            cqh = jnp.sum(jnp.where(lane == F_LANE0 + head, cq, 0.0), axis=1, keepdims=True)
            sub = lax.broadcasted_iota(jnp.int32, (8, 1), 0)
            ckh = jnp.sum(jnp.where(sub == head, ckt, 0.0), axis=0, keepdims=True)
            s = s + (cqh - ckh)
        s = jnp.where(causal, s, NEG_INF)
        m = lax.stop_gradient(jnp.max(s, axis=-1, keepdims=True))
        e = jnp.exp(s - m)
        p = e / jnp.sum(e, axis=-1, keepdims=True)
        oh = dot_nn(p, v)
        out = out + (jnp.where(mine, oh, 0.0) if hpt > 1 else oh)
    return out


def attention(name, q, k, v, *, B, S, ntile, hpt, dk, qc=0, kc=0, vc=0, cum=None, ckt=None, do=None, tq=256):
    tq = _pick(S, (tq, 128))
    nq = S // tq
    M = B * S
    bias = cum is not None
    kw = dict(hpt=hpt, dk=dk)

    def load(refs):
        q_ref, k_ref, v_ref = refs[:3]
        qv, kv, vv = q_ref[...].astype(F32), k_ref[...].astype(F32), v_ref[...].astype(F32)
        if bias:
            return qv, kv, vv, refs[3][...], refs[4][0]
        return qv, kv, vv, None, None

    nin = 5 if bias else 3

    def fwd_body(*refs):
        qv, kv, vv, cq, ck = load(refs)
        o = _attn_tile(qv, kv, vv, cq, ck, q0=pl.program_id(2) * tq, tile=pl.program_id(1), **kw)
        refs[nin][...] = o.astype(refs[nin].dtype)

    def bwd_body(*refs):
        qv, kv, vv, cq, ck = load(refs)
        dov = refs[nin][...].astype(F32)
        outs = refs[nin + 1:]
        q0, tile = pl.program_id(2) * tq, pl.program_id(1)
        if bias:
            f = lambda a, b, c, d, e: _attn_tile(a, b, c, d, e, q0=q0, tile=tile, **kw)
            _, vjp = jax.vjp(f, qv, kv, vv, cq, ck)
        else:
            f = lambda a, b, c: _attn_tile(a, b, c, None, None, q0=q0, tile=tile, **kw)
            _, vjp = jax.vjp(f, qv, kv, vv)
        g = vjp(dov)
        outs[0][...] = g[0]
        if bias:
            outs[3][...] = g[3]
        first = pl.program_id(2) == 0
        acc = [(outs[1], g[1]), (outs[2], g[2])] + ([(outs[4].at[0], g[4])] if bias else [])
        for o_ref, gv in acc:
            @pl.when(first)
            def _(o_ref=o_ref, gv=gv):
                o_ref[...] = gv

            @pl.when(jnp.logical_not(first))
            def _(o_ref=o_ref, gv=gv):
                o_ref[...] += gv

    qspec = lambda c: pl.BlockSpec((tq, LANE), lambda b, t, i: (b * nq + i, c + t))
    kspec = lambda c: pl.BlockSpec((S, LANE), lambda b, t, i: (b, c + t))
    in_specs, args = [qspec(qc), kspec(kc), kspec(vc)], [q, k, v]
    if bias:
        in_specs += [pl.BlockSpec((tq, LANE), lambda b, t, i: (b * nq + i, 0)),
                     pl.BlockSpec((1, 8, S), lambda b, t, i: (b, 0, 0))]
        args += [cum, ckt]
    if do is None:
        return pl.pallas_call(
            fwd_body, name=name, grid=(B, ntile, nq), in_specs=in_specs, out_specs=qspec(0),
            out_shape=jax.ShapeDtypeStruct((M, ntile * LANE), BF16),
            compiler_params=_cp("parallel", "parallel", "parallel"),
        )(*args)
    in_specs.append(qspec(0))
    args.append(do)
    out_specs = [qspec(0), kspec(0), kspec(0)]
    out_shape = [jax.ShapeDtypeStruct((M, ntile * LANE), F32)] * 3
    if bias:
        out_specs += [qspec(0), pl.BlockSpec((1, 8, S), lambda b, t, i: (b * ntile + t, 0, 0))]
        out_shape += [jax.ShapeDtypeStruct((M, ntile * LANE), F32),
                      jax.ShapeDtypeStruct((B * ntile, 8, S), F32)]
    return pl.pallas_call(
        bwd_body, name=name, grid=(B, ntile, nq), in_specs=in_specs, out_specs=out_specs,
        out_shape=out_shape, compiler_params=_cp("parallel", "parallel", "arbitrary"),
    )(*args)


def seq_cumsum(name, x, *, B, S, reverse):
    nb = S // LANE

    def body(x_ref, o_ref):
        r = lax.broadcasted_iota(jnp.int32, (LANE, LANE), 0)
        c = lax.broadcasted_iota(jnp.int32, (LANE, LANE), 1)
        tri = jnp.where((r <= c) if reverse else (r >= c), 1.0, 0.0)
        carry = jnp.zeros((1, LANE), F32)
        for blk in (range(nb - 1, -1, -1) if reverse else range(nb)):
            xb = x_ref[blk * LANE:(blk + 1) * LANE, :]
            o_ref[blk * LANE:(blk + 1) * LANE, :] = xdot(tri, xb) + carry
            carry = carry + jnp.sum(xb, axis=0, keepdims=True)

    return pl.pallas_call(
        body, name=name, grid=(B,), in_specs=[pl.BlockSpec((S, LANE), lambda b: (b, 0))],
        out_specs=pl.BlockSpec((S, LANE), lambda b: (b, 0)),
        out_shape=jax.ShapeDtypeStruct(x.shape, F32), compiler_params=_cp("parallel"),
    )(x)


SCAN_ROWS = 64


def _shift_rows(ref, r0, rows, d, up):
    if d % 8 == 0:
        return ref[pl.ds(r0 + d if up else r0 - d, rows), :]
    if up:
        win = ref[pl.ds(r0, rows + 8), :]
        return pltpu.roll(win, rows + 8 - d, 0)[0:rows, :]
    win = ref[pl.ds(r0 - 8, rows + 8), :]
    return pltpu.roll(win, d, 0)[8:rows + 8, :]


def s5_scan(name, x_re, x_im, lam, *, B, S, reverse, state=None):
    C = S5_C
    cw = LANE
    R = _pick(S, (SCAN_ROWS,))
    pad = max(S // 2, 8)
    nsteps = int(math.log2(S))
    assert 1 << nsteps == S
    base = 0 if reverse else pad
    with_grad = state is not None

    def body(*refs):
        if with_grad:
            xr, xi, lam_ref, sr, si, o_r, o_i, dl_r, dl_i, a_r, a_i, b_r, b_i = refs
        else:
            xr, xi, lam_ref, o_r, o_i, a_r, a_i, b_r, b_i = refs
        zero = jnp.zeros((pad, cw), F32)
        z0 = S if reverse else 0
        for buf in (a_r, a_i, b_r, b_i):
            buf[z0:z0 + pad, :] = zero
        a_r[base:base + S, :] = xr[...]
        a_i[base:base + S, :] = xi[...]
        mr = lam_ref[0:1, :]
        mi = -lam_ref[1:2, :] if reverse else lam_ref[1:2, :]
        src, dst = (a_r, a_i), (b_r, b_i)
        for step in range(nsteps):
            d = 1 << step
            last = step == nsteps - 1

            def chunk(c, _, src=src, dst=dst, d=d, last=last, mr=mr, mi=mi):
                r0 = pl.multiple_of(base + c * R, 8)
                pr = _shift_rows(src[0], r0, R, d, reverse)
                pi = _shift_rows(src[1], r0, R, d, reverse)
                nr = src[0][pl.ds(r0, R), :] + mr * pr - mi * pi
                ni = src[1][pl.ds(r0, R), :] + mr * pi + mi * pr
                if last:
                    o0 = pl.multiple_of(c * R, 8)
                    o_r[pl.ds(o0, R), :] = nr
                    o_i[pl.ds(o0, R), :] = ni
                else:
                    dst[0][pl.ds(r0, R), :] = nr
                    dst[1][pl.ds(r0, R), :] = ni
                return 0

            lax.fori_loop(0, S // R, chunk, 0)
            mr, mi = mr * mr - mi * mi, 2.0 * mr * mi
            src, dst = dst, src
        if with_grad:
            def fold(v):
                return jnp.sum(v.reshape(R // 8, 8, cw), axis=0)

            def accum(c, carry, first=False):
                r0 = 0 if first else pl.multiple_of(c * R, 8)
                gr, gi = o_r[pl.ds(r0, R), :], o_i[pl.ds(r0, R), :]
                if first:
                    keep = lax.broadcasted_iota(jnp.int32, (R, 1), 0) >= 1
                    pr = jnp.where(keep, pltpu.roll(sr[0:R, :], 1, 0), 0.0)
                    pi = jnp.where(keep, pltpu.roll(si[0:R, :], 1, 0), 0.0)
                else:
                    pr = _shift_rows(sr, r0, R, 1, False)
                    pi = _shift_rows(si, r0, R, 1, False)
                return (carry[0] + fold(gr * pr + gi * pi), carry[1] + fold(gi * pr - gr * pi))

            acc = accum(0, (jnp.zeros((8, cw), F32), jnp.zeros((8, cw), F32)), first=True)
            acc = lax.fori_loop(1, S // R, accum, acc)
            dl_r[...] = acc[0]
            dl_i[...] = acc[1]

    seq = pl.BlockSpec((S, cw), lambda b, j: (b, j))
    in_specs = [seq, seq, pl.BlockSpec((2, cw), lambda b, j: (0, j))]
    args = [x_re, x_im, lam]
    out_specs = [seq, seq]
    out_shape = [jax.ShapeDtypeStruct(x_re.shape, F32)] * 2
    if with_grad:
        in_specs += [seq, seq]
        args += list(state)
        out_specs += [pl.BlockSpec((8, cw), lambda b, j: (b, j))] * 2
        out_shape += [jax.ShapeDtypeStruct((B * 8, C), F32)] * 2
    return pl.pallas_call(
        body, name=name, grid=(B, C // cw), in_specs=in_specs, out_specs=out_specs, out_shape=out_shape,
        scratch_shapes=[pltpu.VMEM((S + pad, cw), F32)] * 4,
        compiler_params=_cp("parallel", "parallel"),
    )(*args)


CONV_CW = 256


def _conv_taps(ref, r0, rows, first):
    cur = ref[pl.ds(r0, rows), :]
    if first:
        row = lax.broadcasted_iota(jnp.int32, (rows, 1), 0)
        p1 = jnp.where(row >= 1, pltpu.roll(cur, 1, 0), 0.0)
        p2 = jnp.where(row >= 2, pltpu.roll(cur, 2, 0), 0.0)
    else:
        p1 = _shift_rows(ref, r0, rows, 1, False)
        p2 = _shift_rows(ref, r0, rows, 2, False)
    return cur, p1, p2


def _conv_apply(w_ref, taps):
    return w_ref[2:3, :] * taps[0] + w_ref[1:2, :] * taps[1] + w_ref[0:1, :] * taps[2]


def conv_gate_fwd(up, conv_w, *, B, S):
    M, F2 = up.shape
    F = F2 // 2
    cw = _pick(F, (CONV_CW, LANE))
    nf = F // cw
    R = _pick(S, (SCAN_ROWS,))

    def body(g_ref, v_ref, wg_ref, wv_ref, o_ref):
        def chunk(c, _, first=False):
            r0 = 0 if first else pl.multiple_of(c * R, 8)
            cg = _conv_apply(wg_ref, _conv_taps(g_ref, r0, R, first))
            cv = _conv_apply(wv_ref, _conv_taps(v_ref, r0, R, first))
            o_ref[pl.ds(r0, R), :] = (cg * _sigmoid(cg) * cv).astype(o_ref.dtype)
            return 0

        chunk(0, 0, first=True)
        lax.fori_loop(1, S // R, chunk, 0)

    seq = lambda off: pl.BlockSpec((S, cw), lambda b, j: (b, off + j))
    wsp = lambda off: pl.BlockSpec((3, cw), lambda b, j: (0, off + j))
    return pl.pallas_call(
        body, name="conv_gate", grid=(B, nf), in_specs=[seq(0), seq(nf), wsp(0), wsp(nf)],
        out_specs=seq(0), out_shape=jax.ShapeDtypeStruct((M, F), BF16),
        compiler_params=_cp("parallel", "parallel"),
    )(up, up, conv_w, conv_w)


def conv_gate_bwd(up, conv_w, dact, *, B, S):
    M, F2 = up.shape
    F = F2 // 2
    cw = _pick(F, (CONV_CW, LANE))
    nf = F // cw
    R = _pick(S, (SCAN_ROWS,))
    nchunk = S // R

    def body(s_ref, p_ref, ws_ref, wp_ref, da_ref, du_ref, dw_ref, dc_ref):
        is_gate = pl.program_id(0) < nf
        dc_ref[S:S + 8, :] = jnp.zeros((8, cw), F32)

        def fold(v):
            return jnp.sum(v.reshape(R // 8, 8, cw), axis=0)

        def pass1(c, acc, first=False):
            r0 = 0 if first else pl.multiple_of(c * R, 8)
            taps = _conv_taps(s_ref, r0, R, first)
            cs = _conv_apply(ws_ref, taps)
            cp = _conv_apply(wp_ref, _conv_taps(p_ref, r0, R, first))
            da = da_ref[pl.ds(r0, R), :]
            sg = _sigmoid(cs)
            d_gate = da * cp * (sg * (1.0 + cs * (1.0 - sg)))
            sp = _sigmoid(cp)
            d_val = da * (cp * sp)
            dc = jnp.where(is_gate, d_gate, d_val)
            dc_ref[pl.ds(r0, R), :] = dc
            return (acc[0] + fold(dc * taps[2]), acc[1] + fold(dc * taps[1]), acc[2] + fold(dc * taps[0]))

        z = jnp.zeros((8, cw), F32)
        acc = pass1(0, (z, z, z), first=True)
        acc = lax.fori_loop(1, nchunk, pass1, acc)

        def pass2(c, _):
            r0 = pl.multiple_of(c * R, 8)
            n0 = dc_ref[pl.ds(r0, R), :]
            n1 = _shift_rows(dc_ref, r0, R, 1, True)
            n2 = _shift_rows(dc_ref, r0, R, 2, True)
            du_ref[pl.ds(r0, R), :] = ws_ref[2:3, :] * n0 + ws_ref[1:2, :] * n1 + ws_ref[0:1, :] * n2
            return 0

        lax.fori_loop(0, nchunk, pass2, 0)
        first_b = pl.program_id(1) == 0
        for tap in range(3):
            tot = jnp.sum(acc[tap], axis=0, keepdims=True)

            @pl.when(first_b)
            def _(tap=tap, tot=tot):
                dw_ref[tap:tap + 1, :] = tot

            @pl.when(jnp.logical_not(first_b))
            def _(tap=tap, tot=tot):
                dw_ref[tap:tap + 1, :] += tot

    n2 = 2 * nf
    seq = lambda f: pl.BlockSpec((S, cw), lambda j, b: (b, f(j)))
    wsp = lambda f: pl.BlockSpec((3, cw), lambda j, b: (0, f(j)))
    same, other, act_col = (lambda j: j), (lambda j: (j + nf) % n2), (lambda j: j % nf)
    return pl.pallas_call(
        body, name="conv_gate_vjp", grid=(n2, B),
        in_specs=[seq(same), seq(other), wsp(same), wsp(other), seq(act_col)],
        out_specs=[seq(same), wsp(same)],
        out_shape=[jax.ShapeDtypeStruct((M, F2), F32), jax.ShapeDtypeStruct((3, F2), F32)],
        scratch_shapes=[pltpu.VMEM((S + 8, cw), F32)],
        compiler_params=_cp("parallel", "arbitrary"),
    )(up, up, conv_w, conv_w, dact)


def loss_head(y, target):
    M, D = y.shape
    tm = _pick(M, (256, 128, 64, 32, 16, 8))

    def body(y_ref, t_ref, dy_ref, l_ref):
        diff = y_ref[...] - t_ref[...]
        dy_ref[...] = diff * (1.0 / D)
        part = jnp.sum(jnp.sum(diff * diff, axis=1, keepdims=True), axis=0, keepdims=True)

        @pl.when(pl.program_id(0) == 0)
        def _():
            l_ref[...] = jnp.zeros_like(l_ref)

        l_ref[...] += part

    row = pl.BlockSpec((tm, D), lambda i: (i, 0))
    return pl.pallas_call(
        body, name="loss_head", grid=(M // tm,), in_specs=[row, row],
        out_specs=[row, pl.BlockSpec((8, LANE), lambda i: (0, 0))],
        out_shape=[jax.ShapeDtypeStruct((M, D), F32), jax.ShapeDtypeStruct((8, LANE), F32)],
        compiler_params=_cp("arbitrary"),
    )(y, target)


def adamw(name, w, g, m, v):
    R, C = w.shape
    tr = _pick(R, (256, 128, 64, 32, 16, 8))

    def body(w_ref, g_ref, m_ref, v_ref, d_ref, nm_ref, nv_ref):
        gv = g_ref[...]
        nm = ADAM_B1 * m_ref[...] + (1.0 - ADAM_B1) * gv
        nv = ADAM_B2 * v_ref[...] + (1.0 - ADAM_B2) * (gv * gv)
        m_hat = nm / (1.0 - ADAM_B1 ** ADAM_STEP)
        v_hat = nv / (1.0 - ADAM_B2 ** ADAM_STEP)
        d_ref[...] = -ADAM_LR * (m_hat / (jnp.sqrt(v_hat) + ADAM_EPS) + ADAM_WD * w_ref[...])
        nm_ref[...] = nm
        nv_ref[...] = nv

    blk = pl.BlockSpec((tr, C), lambda i: (i, 0))
    return pl.pallas_call(
        body, name=name, grid=(R // tr,), in_specs=[blk] * 4, out_specs=[blk] * 3,
        out_shape=[jax.ShapeDtypeStruct((R, C), F32)] * 3, compiler_params=_cp("parallel"),
    )(w, g, m, v)


def _seg(D):
    o = 3 * D
    return dict(ckv=o, fq=o + 256, fk=o + 512, fv=o + 768, u=o + 1024, small=o + 1280, cq=o + 1536, P=o + 1920)


def _pad_last(a, n):
    return jnp.pad(a, [(0, 0)] * (a.ndim - 1) + [(0, n - a.shape[-1])])


def _place(a, lo, n=LANE):
    return jnp.pad(a, [(0, 0)] * (a.ndim - 1) + [(lo, n - lo - a.shape[-1])])


def prep_weights(w):
    L, D = w["attn_norm_g"].shape
    win = w["w_in"]
    z = lambda n: jnp.zeros((L, D, n), win.dtype)
    g0 = 1700
    wp = jnp.concatenate(
        [win[..., g0:g0 + 3 * D], win[..., 384:640], win[..., 672:928], win[..., 928:1184], win[..., 1184:1440],
         win[..., 1444:1700], z(NOPE), win[..., 640:672], win[..., 1440:1444], z(LANE - F_LANE0 - N_HEADS), z(LANE),
         win[..., 0:384]], axis=-1)
    wukv, wb = w["w_ukv"], w["w_branch"]
    row3 = lambda a: a[:, None, :]
    return dict(
        g1=row3(w["attn_norm_g"]), Wp=wp, gql=row3(w["q_lat_norm_g"]), gkvl=row3(w["kv_lat_norm_g"]),
        Wuq=_pad_last(w["w_uq"], LANE).reshape(L, Q_RANK, N_HEADS * LANE),
        Wk=_pad_last(wukv[..., :NOPE], LANE).reshape(L, KV_RANK, N_HEADS * LANE),
        Wv=_pad_last(wukv[..., NOPE:], LANE).reshape(L, KV_RANK, N_HEADS * LANE),
        gq=row3(_pad_last(w["mla_q_norm_g"], LANE)), gk=row3(_pad_last(w["mla_k_norm_g"], LANE)),
        gfq=row3(jnp.tile(w["fox_q_norm_g"], (1, 2))), gfk=row3(jnp.tile(w["fox_k_norm_g"], (1, 2))),
        fbias=row3(_place(w["fox_f_bias"], F_LANE0)),
        lre=w["s5_lambda_re"].reshape(L, 1, S5_C), lim=w["s5_lambda_im"].reshape(L, 1, S5_C),
        lstep=row3(_pad_last(w["s5_log_step"], LANE)),
        btr=jnp.transpose(w["s5_b_re"], (0, 3, 1, 2)).reshape(L, S5_H, S5_C),
        bti=jnp.transpose(w["s5_b_im"], (0, 3, 1, 2)).reshape(L, S5_H, S5_C),
        ctr=_pad_last(jnp.transpose(w["s5_c_re"], (0, 1, 3, 2)).reshape(L, S5_C, S5_H), LANE),
        cti=_pad_last(jnp.transpose(w["s5_c_im"], (0, 1, 3, 2)).reshape(L, S5_C, S5_H), LANE),
        s5d=w["s5_d"].reshape(L, 1, BW), Wglu=w["s5_w_glu"], bglu=row3(w["s5_b_glu"]),
        Wb0=jnp.pad(wb[:, 0].reshape(L, N_HEADS, V_DIM, D), ((0, 0), (0, 0), (0, LANE - V_DIM), (0, 0))
                    ).reshape(L, N_HEADS * LANE, D),
        Wb1=wb[:, 1], Wb2=wb[:, 2], Wout=w["w_out"], g2=row3(w["ffn_norm_g"]), Wup=w["w_up"],
        convw=w["ffn_conv_w"], Wdown=w["w_down"],
    )


def unprep_grads(G, D):
    L = G["g1"].shape[0]
    s = _seg(D)
    dwp = G["Wp"]
    sm = s["small"]
    w_in = jnp.concatenate(
        [dwp[..., s["cq"]:s["cq"] + 384], dwp[..., s["ckv"]:s["ckv"] + 256], dwp[..., sm + NOPE:sm + NOPE + ROPE],
         dwp[..., s["fq"]:s["fq"] + 768], dwp[..., sm + F_LANE0:sm + F_LANE0 + N_HEADS],
         dwp[..., s["u"]:s["u"] + 256], dwp[..., 0:3 * D]], axis=-1)
    heads = lambda a, rows, keep: a.reshape(L, rows, N_HEADS, LANE)[..., :keep]
    wb0 = G["Wb0"].reshape(L, N_HEADS, LANE, D)[:, :, :V_DIM].reshape(L, BW, D)
    gf = lambda a: a[:, 0, :FOX_DIM] + a[:, 0, FOX_DIM:]
    return dict(
        attn_norm_g=G["g1"][:, 0], w_in=w_in, q_lat_norm_g=G["gql"][:, 0], w_uq=heads(G["Wuq"], Q_RANK, QK_DIM),
        kv_lat_norm_g=G["gkvl"][:, 0],
        w_ukv=jnp.concatenate([heads(G["Wk"], KV_RANK, NOPE), heads(G["Wv"], KV_RANK, V_DIM)], axis=-1),
        mla_q_norm_g=G["gq"][:, 0, :QK_DIM], mla_k_norm_g=G["gk"][:, 0, :QK_DIM],
        fox_q_norm_g=gf(G["gfq"]), fox_k_norm_g=gf(G["gfk"]),
        fox_f_bias=G["fbias"][:, 0, F_LANE0:F_LANE0 + N_HEADS],
        s5_lambda_re=G["lre"].reshape(L, S5_G, S5_P), s5_lambda_im=G["lim"].reshape(L, S5_G, S5_P),
        s5_b_re=jnp.transpose(G["btr"].reshape(L, S5_H, S5_G, S5_P), (0, 2, 3, 1)),
        s5_b_im=jnp.transpose(G["bti"].reshape(L, S5_H, S5_G, S5_P), (0, 2, 3, 1)),
        s5_c_re=jnp.transpose(G["ctr"][..., :S5_H].reshape(L, S5_G, S5_P, S5_H), (0, 1, 3, 2)),
        s5_c_im=jnp.transpose(G["cti"][..., :S5_H].reshape(L, S5_G, S5_P, S5_H), (0, 1, 3, 2)),
        s5_d=G["s5d"].reshape(L, S5_G, S5_H), s5_log_step=G["lstep"][:, 0, :S5_G],
        s5_w_glu=G["Wglu"], s5_b_glu=G["bglu"][:, 0],
        w_branch=jnp.stack([wb0, G["Wb1"], G["Wb2"]], axis=1), w_out=G["Wout"], ffn_norm_g=G["g2"][:, 0],
        w_up=G["Wup"], ffn_conv_w=G["convw"], w_down=G["Wdown"],
    )


def rope_tables(positions):
    inv_freq = ROPE_THETA ** (-jnp.arange(0, ROPE, 2, dtype=F32) / ROPE)
    ang = positions.astype(F32)[..., None] * inv_freq
    cos, sin = jnp.cos(ang), jnp.sin(ang)
    ones = jnp.ones(ang.shape[:-1] + (NOPE,), F32)
    zeros = jnp.zeros(ang.shape[:-1] + (LANE - NOPE - ROPE,), F32)
    cos_t = jnp.concatenate([ones, cos, cos, zeros], axis=-1)
    sin_t = jnp.concatenate([0.0 * ones, sin, sin, zeros], axis=-1)
    return cos_t.reshape(-1, LANE), sin_t.reshape(-1, LANE)


def fn_rms_res(n):
    return lambda x, g: (_rms(x, g, n), x)


def _s5_mats(p):
    return s5_params({k: p[k] for k in S5_PARAM_NAMES})


def layer_fwd(x, p, cos_t, sin_t, B, S):
    M, D = x.shape
    s = _seg(D)
    sm = s["small"] // LANE
    head = lambda j: j
    h = rowwise("rms_attn", fn_rms(D), [(x, D, 0)], [p["g1"]], [(D, BF16)])[0]
    proj = mm(h, p["Wp"], "nn", name="in_proj")
    cnq = rowwise("latq_norm", fn_rms(Q_RANK), [(proj, Q_RANK, s["cq"] // Q_RANK)], [p["gql"]], [(Q_RANK, BF16)])[0]
    cnkv = rowwise("latkv_norm", fn_rms(KV_RANK), [(proj, KV_RANK, s["ckv"] // KV_RANK)], [p["gkvl"]],
                   [(KV_RANK, BF16)])[0]
    qraw = mm(cnq, p["Wuq"], "nn", name="q_up")
    kn = mm(cnkv, p["Wk"], "nn", name="k_up")
    v5 = mm(cnkv, p["Wv"], "nn", name="v_up", out_dtype=BF16)
    qrot = rowwise("q_post", fn_qpost, [(qraw, LANE, head), (cos_t, LANE, 0), (sin_t, LANE, 0)], [p["gq"]],
                   [(LANE, BF16)], nj=N_HEADS)[0]
    krot = rowwise("k_post", fn_kpost, [(kn, LANE, head), (proj, LANE, sm), (cos_t, LANE, 0), (sin_t, LANE, 0)],
                   [p["gk"]], [(LANE, BF16)], nj=N_HEADS)[0]
    omla = attention("mla_attn", qrot, krot, v5, B=B, S=S, ntile=N_HEADS, hpt=1, dk=QK_DIM)
    fq0, fk0 = s["fq"] // LANE, s["fk"] // LANE
    qf = rowwise("foxq_norm", fn_foxnorm, [(proj, LANE, lambda j: fq0 + j)], [p["gfq"]], [(LANE, BF16)], nj=2)[0]
    kf = rowwise("foxk_norm", fn_foxnorm, [(proj, LANE, lambda j: fk0 + j)], [p["gfk"]], [(LANE, BF16)], nj=2)[0]
    lf = rowwise("fgate", fn_fgate, [(proj, LANE, sm)], [p["fbias"]], [(LANE, F32)])[0]
    cum = seq_cumsum("fox_cumsum", lf, B=B, S=S, reverse=False)
    ckt = _pad_rows8(jnp.transpose(cum.reshape(B, S, LANE)[:, :, F_LANE0:F_LANE0 + N_HEADS], (0, 2, 1)))
    ofox = attention("fox_attn", qf, kf, proj, B=B, S=S, ntile=2, hpt=2, dk=FOX_DIM, vc=s["fv"] // LANE,
                     cum=cum, ckt=ckt)
    lam, bre, bim, ctop, cbot = _s5_mats(p)
    u16 = proj[:, s["u"]:s["u"] + BW].astype(BF16)
    bur = mm(u16, bre, "nn", name="s5_bu_re")
    bui = mm(u16, bim, "nn", name="s5_bu_im")
    sr, si = s5_scan("s5_scan", bur, bui, lam, B=B, S=S, reverse=False)
    ypre = mm(si, cbot, "nn", name="s5_y_im", add=mm(sr, ctop, "nn", name="s5_y_re"))
    os5 = rowwise("s5_post", fn_s5post, [(ypre, BW, 0), (proj, BW, s["u"] // BW)],
                  [p["s5d"], p["Wglu"], p["bglu"]], [(BW, BF16)])[0]
    merged = rowwise("merge", fn_merge,
                     [(omla, N_HEADS * LANE, 0), (ofox, BW, 0), (os5, BW, 0), (proj, D, 0), (proj, D, 1), (proj, D, 2)],
                     [p["Wb0"], p["Wb1"], p["Wb2"]], [(D, BF16)])[0]
    xmid = mm(merged, p["Wout"], "nn", name="out_proj", add=x)
    h2 = rowwise("rms_ffn", fn_rms(D), [(xmid, D, 0)], [p["g2"]], [(D, BF16)])[0]
    up = mm(h2, p["Wup"], "nn", name="ffn_up")
    act = conv_gate_fwd(up, p["convw"], B=B, S=S)
    xout = mm(act, p["Wdown"], "nn", name="ffn_down", add=xmid)
    saved = dict(x=x, h=h, proj=proj, cnq=cnq, cnkv=cnkv, qraw=qraw, kn=kn, v5=v5, qrot=qrot, krot=krot, qf=qf,
                 kf=kf, cum=cum, ckt=ckt, omla=omla, ofox=ofox, os5=os5, u16=u16, sr=sr, si=si, ypre=ypre,
                 merged=merged, xmid=xmid, h2=h2, up=up, act=act)
    return xout, saved


def _pad_rows8(a):
    return jnp.pad(a, ((0, 0), (0, 8 - a.shape[1]), (0, 0)))


def layer_bwd(dx, p, sv, cos_t, sin_t, B, S):
    M, D = dx.shape
    s = _seg(D)
    sm = s["small"] // LANE
    head = lambda j: j
    proj = sv["proj"]
    dact = mm(dx, p["Wdown"], "nt", name="ffn_down_dx")
    d_wdown = mm(sv["act"], dx, "tn", name="ffn_down_dw")
    dup, d_convw = conv_gate_bwd(sv["up"], p["convw"], dact, B=B, S=S)
    dh2 = mm(dup, p["Wup"], "nt", name="ffn_up_dx")
    d_wup = mm(sv["h2"], dup, "tn", name="ffn_up_dw")
    (dxmid,), (d_g2,) = rowwise_vjp("rms_ffn_vjp", fn_rms_res(D), [(sv["xmid"], D, 0)], [p["g2"]],
                                    [(dh2, D, 0), (dx, D, 0)], [0])
    dmerged = mm(dxmid, p["Wout"], "nt", name="out_proj_dx")
    d_wout = mm(sv["merged"], dxmid, "tn", name="out_proj_dw")
    (dom, dof, dos, dg0, dg1, dg2), (d_wb0, d_wb1, d_wb2) = rowwise_vjp(
        "merge_vjp", fn_merge,
        [(sv["omla"], N_HEADS * LANE, 0), (sv["ofox"], BW, 0), (sv["os5"], BW, 0), (proj, D, 0), (proj, D, 1),
         (proj, D, 2)], [p["Wb0"], p["Wb1"], p["Wb2"]], [(dmerged, D, 0)], [0, 1, 2, 3, 4, 5], tm=128)
    lam, bre, bim, ctop, cbot = _s5_mats(p)
    (dypre, du_a), (d_s5d, d_wglu, d_bglu) = rowwise_vjp(
        "s5_post_vjp", fn_s5post, [(sv["ypre"], BW, 0), (proj, BW, s["u"] // BW)],
        [p["s5d"], p["Wglu"], p["bglu"]], [(dos, BW, 0)], [0, 1])
    dsr = mm(dypre, ctop, "nt", name="s5_y_re_dx")
    dsi = mm(dypre, cbot, "nt", name="s5_y_im_dx")
    d_ctop = mm(sv["sr"], dypre, "tn", name="s5_y_re_dw")
    d_cbot = mm(sv["si"], dypre, "tn", name="s5_y_im_dw")
    gr, gi, dl_r, dl_i = s5_scan("s5_scan_vjp", dsr, dsi, lam, B=B, S=S, reverse=True, state=(sv["sr"], sv["si"]))
    du = mm(gi, bim, "nt", name="s5_bu_im_dx", add=mm(gr, bre, "nt", name="s5_bu_re_dx", add=du_a))
    d_bre = mm(sv["u16"], gr, "tn", name="s5_bu_re_dw")
    d_bim = mm(sv["u16"], gi, "tn", name="s5_bu_im_dw")
    d_s5 = s5_params_vjp({k: p[k] for k in S5_PARAM_NAMES}, dl_r, dl_i, d_bre, d_bim, d_ctop, d_cbot)
    dqf, dkf, dfv, dcq_t, dckt_t = attention("fox_attn_vjp", sv["qf"], sv["kf"], proj, B=B, S=S, ntile=2, hpt=2,
                                             dk=FOX_DIM, vc=s["fv"] // LANE, cum=sv["cum"], ckt=sv["ckt"], do=dof)
    dck = dckt_t.reshape(B, 2, 8, S)
    dck = jnp.transpose(dck[:, 0, :N_HEADS] + dck[:, 1, :N_HEADS], (0, 2, 1)).reshape(M, N_HEADS)
    dcum = rowwise("fox_dcum", lambda a, b: (a[:, 0:LANE] + a[:, LANE:2 * LANE] + b,),
                   [(dcq_t, 2 * LANE, 0), (_place(dck, F_LANE0), LANE, 0)], [], [(LANE, F32)])[0]
    dlf = seq_cumsum("fox_cumsum_vjp", dcum, B=B, S=S, reverse=True)
    (dsmall_f,), (d_fbias,) = rowwise_vjp("fgate_vjp", fn_fgate, [(proj, LANE, sm)], [p["fbias"]],
                                          [(dlf, LANE, 0)], [0])
    fq0, fk0 = s["fq"] // LANE, s["fk"] // LANE
    (dfq,), (d_gfq,) = rowwise_vjp("foxq_norm_vjp", fn_foxnorm, [(proj, LANE, lambda j: fq0 + j)], [p["gfq"]],
                                   [(dqf, LANE, head)], [0], nj=2)
    (dfk,), (d_gfk,) = rowwise_vjp("foxk_norm_vjp", fn_foxnorm, [(proj, LANE, lambda j: fk0 + j)], [p["gfk"]],
                                   [(dkf, LANE, head)], [0], nj=2)
    dqrot, dkrot, dv5 = attention("mla_attn_vjp", sv["qrot"], sv["krot"], sv["v5"], B=B, S=S, ntile=N_HEADS,
                                  hpt=1, dk=QK_DIM, do=dom)
    (dqraw,), (d_gq,) = rowwise_vjp("q_post_vjp", fn_qpost,
                                    [(sv["qraw"], LANE, head), (cos_t, LANE, 0), (sin_t, LANE, 0)], [p["gq"]],
                                    [(dqrot, LANE, head)], [0], nj=N_HEADS)
    (dkn, dsmall_k), (d_gk,) = rowwise_vjp(
        "k_post_vjp", fn_kpost, [(sv["kn"], LANE, head), (proj, LANE, sm), (cos_t, LANE, 0), (sin_t, LANE, 0)],
        [p["gk"]], [(dkrot, LANE, head)], [0, 1], nj=N_HEADS)
    dsmall = rowwise("small_sum", fn_add5, [(dsmall_k, N_HEADS * LANE, 0), (dsmall_f, LANE, 0)], [], [(LANE, F32)])[0]
    dcnq = mm(dqraw, p["Wuq"], "nt", name="q_up_dx")
    d_wuq = mm(sv["cnq"], dqraw, "tn", name="q_up_dw")
    dcnkv = mm(dv5, p["Wv"], "nt", name="v_up_dx", add=mm(dkn, p["Wk"], "nt", name="k_up_dx"))
    d_wk = mm(sv["cnkv"], dkn, "tn", name="k_up_dw")
    d_wv = mm(sv["cnkv"], dv5, "tn", name="v_up_dw")
    (dcq,), (d_gql,) = rowwise_vjp("latq_norm_vjp", fn_rms(Q_RANK), [(proj, Q_RANK, s["cq"] // Q_RANK)], [p["gql"]],
                                   [(dcnq, Q_RANK, 0)], [0])
    (dckv,), (d_gkvl,) = rowwise_vjp("latkv_norm_vjp", fn_rms(KV_RANK), [(proj, KV_RANK, s["ckv"] // KV_RANK)],
                                     [p["gkvl"]], [(dcnkv, KV_RANK, 0)], [0])
    dproj = jnp.concatenate([dg0, dg1, dg2, dckv, dfq, dfk, dfv, du, dsmall, jnp.zeros((M, LANE), F32), dcq], axis=1)
    dh = mm(dproj, p["Wp"], "nt", name="in_proj_dx")
    d_wp = mm(sv["h"], dproj, "tn", name="in_proj_dw")
    (dxin,), (d_g1,) = rowwise_vjp("rms_attn_vjp", fn_rms_res(D), [(sv["x"], D, 0)], [p["g1"]],
                                   [(dh, D, 0), (dxmid, D, 0)], [0])
    grads = dict(g1=d_g1, Wp=d_wp, gql=d_gql, gkvl=d_gkvl, Wuq=d_wuq, Wk=d_wk, Wv=d_wv, gq=d_gq, gk=d_gk,
                 gfq=d_gfq, gfk=d_gfk, fbias=d_fbias, s5d=d_s5d, Wglu=d_wglu, bglu=d_bglu, Wb0=d_wb0, Wb1=d_wb1,
                 Wb2=d_wb2, Wout=d_wout, g2=d_g2, Wup=d_wup, convw=d_convw, Wdown=d_wdown)
    grads.update(dict(zip(S5_PARAM_NAMES, d_s5)))
    return dxin, grads


def local_step(x, positions, target, w):
    B, S, D = x.shape
    M = B * S
    P = prep_weights(w)
    cos_t, sin_t = rope_tables(positions)

    def fwd(xc, p):
        return layer_fwd(xc, p, cos_t, sin_t, B, S)

    y, saved = lax.scan(fwd, x.reshape(M, D), P)
    dy, sq = loss_head(y, target.reshape(M, D))

    def bwd(dxc, ps):
        return layer_bwd(dxc, ps[0], ps[1], cos_t, sin_t, B, S)

    dx0, G = lax.scan(bwd, dy, (P, saved), reverse=True)
    return sq, dx0.reshape(B, S, D), unprep_grads(G, D)


MESH = pl.DeviceIdType.MESH
ANY = pl.BlockSpec(memory_space=pl.ANY)
N_CHIPS = 4
SHARDED = ("w_in", "w_uq", "w_ukv", "s5_w_glu", "w_branch", "w_out", "w_up", "ffn_conv_w", "w_down")
MINOR = ("w_branch", "w_up", "ffn_conv_w")
F32_TRAVEL = ("ffn_conv_w",)
WEIGHTS = ("attn_norm_g", "w_in", "q_lat_norm_g", "w_uq", "kv_lat_norm_g", "w_ukv", "mla_q_norm_g", "mla_k_norm_g",
           "fox_q_norm_g", "fox_k_norm_g", "fox_f_bias", "s5_lambda_re", "s5_lambda_im", "s5_b_re", "s5_b_im",
           "s5_c_re", "s5_c_im", "s5_d", "s5_log_step", "s5_w_glu", "s5_b_glu", "w_branch", "w_out", "ffn_norm_g",
           "w_up", "ffn_conv_w", "w_down")
SMALL = tuple(n for n in WEIGHTS if n not in SHARDED)


def shard3(name, a):
    L = a.shape[0]
    if name in ("w_uq", "w_ukv"):
        return a.reshape(L, a.shape[1], -1)
    if name == "w_branch":
        return a.reshape(L, -1, a.shape[-1])
    return a


def full4(name, a):
    L = a.shape[0]
    if name == "w_in":
        return jnp.transpose(a.reshape(L, a.shape[1], N_CHIPS, -1), (0, 2, 1, 3))
    if name in MINOR:
        return a.reshape(L, 1, -1, a.shape[-1])
    a = a.reshape(L, a.shape[1], -1)
    return a.reshape(L, N_CHIPS, a.shape[1] // N_CHIPS, a.shape[2])


def from_full4(name, a, ref_tail):
    L = a.shape[0]
    if name == "w_in":
        a = jnp.transpose(a, (0, 2, 1, 3))
    return a.reshape((L,) + tuple(ref_tail))


def _where():
    x, y, c = lax.axis_index("x"), lax.axis_index("y"), lax.axis_index("c")
    chips = [(1 - x, y), (x, 1 - y), (1 - x, 1 - y)]
    return (x, y, c), 2 * x + y, (x, y, 1 - c), chips, [2 * cx + cy for cx, cy in chips]


def _view(minor, ref4, layers, k):
    if minor:
        cs = ref4.shape[3] // N_CHIPS
        return ref4.at[layers, 0, :, pl.ds(pl.multiple_of(k * cs, LANE), cs)]
    return ref4.at[layers, k]


def _remote(src, dst, ssem, rsem, dev):
    return pltpu.make_async_remote_copy(src_ref=src, dst_ref=dst, send_sem=ssem, recv_sem=rsem,
                                        device_id=dev, device_id_type=MESH)


def gather_weights(shards, minor):
    n = len(shards)
    L = shards[0].shape[0]
    Lh = L // 2
    out_shape = []
    for a, mn in zip(shards, minor):
        _, r, cs = a.shape
        out_shape.append(jax.ShapeDtypeStruct((L, 1, r, N_CHIPS * cs) if mn else (L, N_CHIPS, r, cs), a.dtype))

    def body(*refs):
        w, g = refs[:n], refs[n:2 * n]
        send, recv, loc = refs[2 * n:]
        (x, y, c), me, sib, chips, cidx = _where()
        mine, other, every = pl.ds(c * Lh, Lh), pl.ds((1 - c) * Lh, Lh), pl.ds(0, L)
        dst = lambda i, layers, k: _view(minor[i], g[i], layers, k)
        local = [pltpu.make_async_copy(w[i], dst(i, every, me), loc.at[i]) for i in range(n)]
        first = [_remote(w[i].at[mine], dst(i, mine, me), send.at[i, j], recv.at[i, j], (*chips[j], c))
                 for i in range(n) for j in range(3)]
        for cp in local + first:
            cp.start()
        passed = []
        for i in range(n):
            for j in range(3):
                blk = dst(i, mine, cidx[j])
                _remote(blk, blk, send.at[i, j], recv.at[i, j], (*chips[j], c)).wait_recv()
                fwd = _remote(blk, blk, send.at[i, 3 + j], recv.at[i, 3 + j], sib)
                fwd.start()
                passed.append(fwd)
        for i in range(n):
            for j in range(3):
                blk = dst(i, other, cidx[j])
                _remote(blk, blk, send.at[i, 3 + j], recv.at[i, 3 + j], sib).wait_recv()
        for cp in first + passed:
            cp.wait_send()
        for cp in local:
            cp.wait()

    return pl.pallas_call(
        body, name="gather_weights", in_specs=[ANY] * n, out_specs=[ANY] * n, out_shape=out_shape,
        scratch_shapes=[pltpu.SemaphoreType.DMA((n, 6)), pltpu.SemaphoreType.DMA((n, 6)),
                        pltpu.SemaphoreType.DMA((n,))],
    )(*shards)


def sibling_halves(grads):
    n = len(grads)
    L = grads[0].shape[0]
    Lh = L // 2
    half = [jax.ShapeDtypeStruct((Lh,) + a.shape[1:], a.dtype) for a in grads]

    def body(*refs):
        g, own, got = refs[:n], refs[n:2 * n], refs[2 * n:3 * n]
        send, recv, loc = refs[3 * n:]
        (x, y, c), me, sib, chips, cidx = _where()
        mine, other = pl.ds(c * Lh, Lh), pl.ds((1 - c) * Lh, Lh)
        local = [pltpu.make_async_copy(g[i].at[mine], own[i], loc.at[i]) for i in range(n)]
        out = [_remote(g[i].at[other], got[i], send.at[i], recv.at[i], sib) for i in range(n)]
        for cp in local + out:
            cp.start()
        for cp in out:
            cp.wait()
        for cp in local:
            cp.wait()

    res = pl.pallas_call(
        body, name="grad_sibling_halves", in_specs=[ANY] * n, out_specs=[ANY] * (2 * n), out_shape=half + half,
        scratch_shapes=[pltpu.SemaphoreType.DMA((n,)), pltpu.SemaphoreType.DMA((n,)), pltpu.SemaphoreType.DMA((n,))],
    )(*grads)
    return res[:n], res[n:]


def scatter_chip_sums(sums, travel, minor):
    n = len(sums)
    Lh = sums[0].shape[0]
    mine_shape, got_shape = [], []
    for a, t, mn in zip(sums, travel, minor):
        r, cs = a.shape[2], (a.shape[3] // N_CHIPS if mn else a.shape[3])
        mine_shape.append(jax.ShapeDtypeStruct((Lh, r, cs), a.dtype))
        got_shape.append(jax.ShapeDtypeStruct((3, Lh, r, cs), t.dtype))

    def body(*refs):
        s32, s16, mine, got = refs[:n], refs[n:2 * n], refs[2 * n:3 * n], refs[3 * n:4 * n]
        send, recv, loc = refs[4 * n:]
        (x, y, c), me, sib, chips, cidx = _where()
        every = pl.ds(0, Lh)
        local = [pltpu.make_async_copy(_view(minor[i], s32[i], every, me), mine[i], loc.at[i]) for i in range(n)]
        out = [_remote(_view(minor[i], s16[i], every, cidx[j]), got[i].at[j], send.at[i, j], recv.at[i, j],
                       (*chips[j], c)) for i in range(n) for j in range(3)]
        for cp in local + out:
            cp.start()
        for cp in out:
            cp.wait()
        for cp in local:
            cp.wait()

    res = pl.pallas_call(
        body, name="grad_scatter", in_specs=[ANY] * (2 * n), out_specs=[ANY] * (2 * n),
        out_shape=mine_shape + got_shape,
        scratch_shapes=[pltpu.SemaphoreType.DMA((n, 3)), pltpu.SemaphoreType.DMA((n, 3)),
                        pltpu.SemaphoreType.DMA((n,))],
    )(*sums, *travel)
    return res[:n], res[n:]


def share_halves(halves):
    n = len(halves)
    Lh = halves[0].shape[0]
    out_shape = [jax.ShapeDtypeStruct((2 * Lh,) + a.shape[1:], a.dtype) for a in halves]

    def body(*refs):
        h, full = refs[:n], refs[n:2 * n]
        send, recv, loc = refs[2 * n:]
        (x, y, c), me, sib, chips, cidx = _where()
        mine, other = pl.ds(c * Lh, Lh), pl.ds((1 - c) * Lh, Lh)
        local = [pltpu.make_async_copy(h[i], full[i].at[mine], loc.at[i]) for i in range(n)]
        out = [_remote(h[i], full[i].at[mine], send.at[i], recv.at[i], sib) for i in range(n)]
        for cp in local + out:
            cp.start()
        for i in range(n):
            out[i].wait_send()
            _remote(h[i], full[i].at[other], send.at[i], recv.at[i], sib).wait_recv()
        for cp in local:
            cp.wait()

    return pl.pallas_call(
        body, name="grad_share_halves", in_specs=[ANY] * n, out_specs=[ANY] * n, out_shape=out_shape,
        scratch_shapes=[pltpu.SemaphoreType.DMA((n,)), pltpu.SemaphoreType.DMA((n,)), pltpu.SemaphoreType.DMA((n,))],
    )(*halves)


N_DEV = 8


def allreduce_small(v):
    R = v.shape[0]

    def body(x_ref, sum_ref, all_ref, send, recv, loc):
        (x, y, c), me, sib, chips, cidx = _where()

        def rows(px, py, pc):
            return all_ref.at[4 * px + 2 * py + pc]

        def copy(k, block, to, src=None):
            return _remote(rows(*block) if src is None else src, rows(*block), send.at[k], recv.at[k], to)

        mine = pltpu.make_async_copy(x_ref, rows(x, y, c), loc)
        mine.start()
        first = [copy(0, (x, y, c), sib, src=x_ref)]
        first += [copy(1 + j, (x, y, c), (*chip, c), src=x_ref) for j, chip in enumerate(chips)]
        for cp in first:
            cp.start()
        passed = [copy(4 + j, (*chip, c), sib) for j, chip in enumerate(chips)]
        for j, chip in enumerate(chips):
            copy(1 + j, (*chip, c), (x, y, c)).wait_recv()
            passed[j].start()
        copy(0, (x, y, 1 - c), (x, y, c)).wait_recv()
        for j, chip in enumerate(chips):
            copy(4 + j, (*chip, 1 - c), (x, y, c)).wait_recv()
        for cp in first + passed:
            cp.wait_send()
        mine.wait()
        acc = all_ref[0]
        for d in range(1, N_DEV):
            acc = acc + all_ref[d]
        sum_ref[...] = acc

    vm = pl.BlockSpec(memory_space=pltpu.VMEM)
    return pl.pallas_call(
        body, name="allreduce_small", in_specs=[vm], out_specs=[vm, vm],
        out_shape=[jax.ShapeDtypeStruct((R, LANE), F32), jax.ShapeDtypeStruct((N_DEV, R, LANE), F32)],
        scratch_shapes=[pltpu.SemaphoreType.DMA((7,)), pltpu.SemaphoreType.DMA((7,)), pltpu.SemaphoreType.DMA],
        compiler_params=pltpu.CompilerParams(vmem_limit_bytes=VMEM_LIMIT),
    )(v)[0]


EW_BLOCK_BYTES = 2 << 20


def _ew_rows(rows, cols):
    for tr in (1024, 512, 256, 128, 64, 32, 16, 8):
        if rows % tr == 0 and tr * cols * 4 <= EW_BLOCK_BYTES:
            return tr
    return rows


def add_pair(name, a, b, travel_dtype):
    R, C = a.shape
    tr = _ew_rows(R, C)

    def body(a_ref, b_ref, s_ref, t_ref):
        s = a_ref[...] + b_ref[...]
        s_ref[...] = s
        t_ref[...] = s.astype(t_ref.dtype)

    blk = pl.BlockSpec((tr, C), lambda i: (i, 0))
    return pl.pallas_call(
        body, name=name, grid=(R // tr,), in_specs=[blk, blk], out_specs=[blk, blk],
        out_shape=[jax.ShapeDtypeStruct((R, C), F32), jax.ShapeDtypeStruct((R, C), travel_dtype)],
        compiler_params=_cp("parallel"),
    )(a, b)


def add_four(name, mine, got):
    R, C = mine.shape
    tr = _ew_rows(R, C)

    def body(m_ref, g0, g1, g2, o_ref):
        o_ref[...] = ((m_ref[...] + g0[0].astype(F32)) + g1[0].astype(F32)) + g2[0].astype(F32)

    blk = pl.BlockSpec((tr, C), lambda i: (i, 0))
    slot = lambda j: pl.BlockSpec((1, tr, C), lambda i: (j, i, 0))
    return pl.pallas_call(
        body, name=name, grid=(R // tr,), in_specs=[blk, slot(0), slot(1), slot(2)], out_specs=blk,
        out_shape=jax.ShapeDtypeStruct((R, C), F32), compiler_params=_cp("parallel"),
    )(mine, got, got, got)


def reduce_scatter_grads(full_grads):
    names = list(SHARDED)
    minor = [nm in MINOR for nm in names]
    g4 = [full4(nm, full_grads[nm]) for nm in names]
    own, got = sibling_halves(g4)
    sums, travel = [], []
    for nm, a, b in zip(names, own, got):
        s, t = add_pair("chip_sum_" + nm, a.reshape(-1, a.shape[-1]), b.reshape(-1, b.shape[-1]),
                        F32 if nm in F32_TRAVEL else BF16)
        sums.append(s.reshape(a.shape))
        travel.append(t.reshape(a.shape))
    mine, arrived = scatter_chip_sums(sums, travel, minor)
    halves = []
    for nm, a, b in zip(names, mine, arrived):
        f = add_four("shard_sum_" + nm, a.reshape(-1, a.shape[-1]), b.reshape(3, -1, b.shape[-1]))
        halves.append(f.reshape(a.shape))
    return dict(zip(names, share_halves(halves)))


def pack_small(tree, extra=None):
    parts = [tree[nm].reshape(-1) for nm in SMALL]
    parts.append(jnp.zeros((1,), F32) if extra is None else extra.reshape(-1))
    flat = jnp.concatenate(parts)
    rows = -(-flat.shape[0] // (8 * LANE)) * 8
    return jnp.pad(flat, (0, rows * LANE - flat.shape[0])).reshape(rows, LANE)


def unpack_small(packed, like):
    flat = packed.reshape(-1)
    out, at = {}, 0
    for nm in SMALL:
        size = math.prod(like[nm].shape)
        out[nm] = flat[at:at + size].reshape(like[nm].shape)
        at += size
    return out, flat[at]


def kernel(x, positions, attn_norm_g, w_in, q_lat_norm_g, w_uq, kv_lat_norm_g, w_ukv, mla_q_norm_g, mla_k_norm_g, fox_q_norm_g, fox_k_norm_g, fox_f_bias, s5_lambda_re, s5_lambda_im, s5_b_re, s5_b_im, s5_c_re, s5_c_im, s5_d, s5_log_step, s5_w_glu, s5_b_glu, w_branch, w_out, ffn_norm_g, w_up, ffn_conv_w, w_down, loss_target, m_attn_norm_g, m_w_in, m_q_lat_norm_g, m_w_uq, m_kv_lat_norm_g, m_w_ukv, m_mla_q_norm_g, m_mla_k_norm_g, m_fox_q_norm_g, m_fox_k_norm_g, m_fox_f_bias, m_s5_lambda_re, m_s5_lambda_im, m_s5_b_re, m_s5_b_im, m_s5_c_re, m_s5_c_im, m_s5_d, m_s5_log_step, m_s5_w_glu, m_s5_b_glu, m_w_branch, m_w_out, m_ffn_norm_g, m_w_up, m_ffn_conv_w, m_w_down, v_attn_norm_g, v_w_in, v_q_lat_norm_g, v_w_uq, v_kv_lat_norm_g, v_w_ukv, v_mla_q_norm_g, v_mla_k_norm_g, v_fox_q_norm_g, v_fox_k_norm_g, v_fox_f_bias, v_s5_lambda_re, v_s5_lambda_im, v_s5_b_re, v_s5_b_im, v_s5_c_re, v_s5_c_im, v_s5_d, v_s5_log_step, v_s5_w_glu, v_s5_b_glu, v_w_branch, v_w_out, v_ffn_norm_g, v_w_up, v_ffn_conv_w, v_w_down):
    given = dict(locals())
    w = {nm: given[nm] for nm in WEIGHTS}
    m = {nm: given["m_" + nm] for nm in WEIGHTS}
    v = {nm: given["v_" + nm] for nm in WEIGHTS}
    D = x.shape[-1]

    minor = [nm in MINOR for nm in SHARDED]
    shards = [shard3(nm, w[nm]).astype(F32 if nm in F32_TRAVEL else BF16) for nm in SHARDED]
    gathered = gather_weights(shards, minor)
    full = dict(w)
    for nm, g4 in zip(SHARDED, gathered):
        tail = list(w[nm].shape[1:])
        axis = (len(tail) - 1) if nm in MINOR or nm == "w_in" else 0
        tail[axis] *= N_CHIPS
        full[nm] = from_full4(nm, g4, tail)

    sq, grad_x, gw = local_step(x, positions, loss_target, full)

    big = reduce_scatter_grads(gw)
    total, sq_sum = unpack_small(allreduce_small(pack_small(gw, sq[0:1, 0:1])), w)
    loss = 0.5 * sq_sum / D

    grads, delta, new_m, new_v = {}, {}, {}, {}
    for nm in SHARDED:
        g = big[nm]
        two = lambda a: shard3(nm, a).reshape(-1, g.shape[-1])
        d2, m2, v2 = adamw("adamw_" + nm, two(w[nm]), g.reshape(-1, g.shape[-1]), two(m[nm]), two(v[nm]))
        grads[nm] = g.reshape(w[nm].shape)
        delta[nm], new_m[nm], new_v[nm] = (a.reshape(w[nm].shape) for a in (d2, m2, v2))
    d2, m2, v2 = adamw("adamw_small", pack_small(w), pack_small(total), pack_small(m), pack_small(v))
    for tree, packed in ((delta, d2), (new_m, m2), (new_v, v2)):
        tree.update(unpack_small(packed, w)[0])
    grads.update(total)
    return (loss, grad_x, *[grads[nm] for nm in WEIGHTS], *[delta[nm] for nm in WEIGHTS],
            *[new_m[nm] for nm in WEIGHTS], *[new_v[nm] for nm in WEIGHTS])
```

```python
import functools
import math

import jax
import jax.numpy as jnp
from jax import lax
from jax.experimental import pallas as pl
from jax.experimental.pallas import tpu as pltpu

F32, BF16 = jnp.float32, jnp.bfloat16
NORM_EPS = 1e-6
NEG_INF = -1e30
ROPE_THETA = 10000.0
LANE = 128
N_HEADS = 4
NOPE, ROPE, QK_DIM, V_DIM = 64, 32, 96, 64
Q_RANK, KV_RANK = 384, 256
FOX_DIM = 64
S5_G, S5_H, S5_P = 16, 16, 64
S5_C = S5_G * S5_P
BW = 256
VMEM_LIMIT = 56 << 20
ADAM_LR, ADAM_B1, ADAM_B2, ADAM_EPS, ADAM_WD, ADAM_STEP = 0.001, 0.9, 0.999, 1e-08, 0.01, 10


def _pick(n, cands):
    for c in cands:
        if n % c == 0:
            return c
    return n


def _cp(*sem):
    return pltpu.CompilerParams(dimension_semantics=sem, vmem_limit_bytes=VMEM_LIMIT)


def _dg(a, b, ca, cb):
    return lax.dot_general(a.astype(BF16), b.astype(BF16), (((ca,), (cb,)), ((), ())),
                           preferred_element_type=F32)


@jax.custom_vjp
def dot_nn(a, b):
    return _dg(a, b, 1, 0)


@jax.custom_vjp
def dot_nt(a, b):
    return _dg(a, b, 1, 1)


@jax.custom_vjp
def dot_tn(a, b):
    return _dg(a, b, 0, 0)


dot_nn.defvjp(lambda a, b: (dot_nn(a, b), (a, b)),
              lambda r, g: (dot_nt(g, r[1]).astype(r[0].dtype), dot_tn(r[0], g).astype(r[1].dtype)))
dot_nt.defvjp(lambda a, b: (dot_nt(a, b), (a, b)),
              lambda r, g: (dot_nn(g, r[1]).astype(r[0].dtype), dot_tn(g, r[0]).astype(r[1].dtype)))
dot_tn.defvjp(lambda a, b: (dot_tn(a, b), (a, b)),
              lambda r, g: (dot_nt(r[1], g).astype(r[0].dtype), dot_nn(r[0], g).astype(r[1].dtype)))


def xdot(a, b):
    return jnp.dot(a, b, precision=lax.Precision.HIGHEST, preferred_element_type=F32)


MM_VMEM_BUDGET = 36 << 20
MM_STEP_S, MM_HBM_BPS = 0.4e-6, 2.5e12


def _divisors(n, cands):
    return [c for c in cands if n % c == 0] or [n]


def _mm_tiles(M, K, N, ab, bb, ob, addb):
    best = None
    for tm in _divisors(M, (2048, 1024, 512, 256, 128)):
        for tn in _divisors(N, (2048, 1664, 1536, 1408, 1280, 1024, 768, 640, 512, 384, 256, 128)):
            for tk in _divisors(K, (4096, 2048, 1664, 1536, 1408, 1024, 768, 512, 384, 256, 128)):
                vmem = 2 * (tm * tk * ab + tk * tn * bb + tm * tn * (ob + addb)) + (tm * tn * 4 if tk != K else 0)
                if vmem > MM_VMEM_BUDGET:
                    continue
                steps = (M // tm) * (N // tn) * (K // tk)
                traffic = M * K * ab * (N // tn) + K * N * bb * (M // tm) + M * N * (ob + addb)
                cost = steps * MM_STEP_S + traffic / MM_HBM_BPS
                if best is None or cost < best[0]:
                    best = (cost, tm, tn, tk)
    assert best is not None, (M, K, N)
    return best[1:]


def mm(a, b, mode, *, name, add=None, out_dtype=F32, layer=None):
    bk, bn = b.shape[-2:]
    if mode == "nn":
        (M, K), N = a.shape, bn
    else:
        (K, M), N = a.shape, bn
    assert bk == K, (name, a.shape, b.shape)
    isz = lambda x: jnp.dtype(x.dtype).itemsize
    tm, tn, tk = _mm_tiles(M, K, N, isz(a), isz(b), jnp.dtype(out_dtype).itemsize, 0 if add is None else isz(add))
    nk = K // tk
    ca = 1 if mode == "nn" else 0

    def body(*refs):
        a_ref, b_ref = refs[:2]
        add_ref = refs[2] if add is not None else None
        o_ref = refs[3 if add is not None else 2]

        def finish(r):
            if add is not None:
                r = r + add_ref[...].astype(F32)
            o_ref[...] = r.astype(out_dtype)

        part = _dg(a_ref[...], b_ref[...], ca, 0)
        if nk == 1:
            finish(part)
            return
        acc = refs[-1]
        kk = pl.program_id(2)

        @pl.when(kk == 0)
        def _():
            acc[...] = part

        @pl.when(kk > 0)
        def _():
            acc[...] += part

        @pl.when(kk == nk - 1)
        def _():
            finish(acc[...])

    a_spec = (pl.BlockSpec((tm, tk), lambda i, j, k: (i, k)) if mode == "nn"
              else pl.BlockSpec((tk, tm), lambda i, j, k: (k, i)))
    b_spec = (pl.BlockSpec((tk, tn), lambda i, j, k: (k, j)) if layer is None
              else pl.BlockSpec((None, tk, tn), lambda i, j, k: (layer, k, j)))
    in_specs, args = [a_spec, b_spec], [a, b]
    if add is not None:
        in_specs.append(pl.BlockSpec((tm, tn), lambda i, j, k: (i, j)))
        args.append(add)
    return pl.pallas_call(
        body, name=name, grid=(M // tm, N // tn, nk),
        in_specs=in_specs, out_specs=pl.BlockSpec((tm, tn), lambda i, j, k: (i, j)),
        out_shape=jax.ShapeDtypeStruct((M, N), out_dtype),
        scratch_shapes=[pltpu.VMEM((tm, tn), F32)] if nk > 1 else [],
        compiler_params=_cp("parallel", "parallel", "arbitrary"),
    )(*args)


def _row_spec(tm, width, col):
    if callable(col):
        return pl.BlockSpec((tm, width), lambda i, j: (i, col(j)))
    return pl.BlockSpec((tm, width), lambda i, j: (i, col))


def _const_spec(c):
    return pl.BlockSpec(c.shape, lambda i, j: (0,) * c.ndim)


def rowwise(name, fn, rows, consts, outs, *, tm=256, nj=1):
    M = rows[0][0].shape[0]
    tm = _pick(M, (tm, 128, 64, 32, 16, 8))
    nr, nc = len(rows), len(consts)

    def body(*refs):
        vals = [r[...].astype(F32) for r in refs[:nr]] + [r[...] for r in refs[nr:nr + nc]]
        res = fn(*vals)
        for o_ref, r in zip(refs[nr + nc:], res):
            o_ref[...] = r.astype(o_ref.dtype)

    return pl.pallas_call(
        body, name=name, grid=(M // tm, nj),
        in_specs=[_row_spec(tm, w, c) for _, w, c in rows] + [_const_spec(c) for c in consts],
        out_specs=[pl.BlockSpec((tm, w), lambda i, j: (i, j)) for w, _ in outs],
        out_shape=[jax.ShapeDtypeStruct((M, nj * w), dt) for w, dt in outs],
        compiler_params=_cp("parallel", "parallel"),
    )(*[r[0] for r in rows], *consts)


def rowwise_vjp(name, fn, rows, consts, cts, diff, *, tm=256, nj=1, gdt=None):
    M = rows[0][0].shape[0]
    tm = _pick(M, (tm, 128, 64, 32, 16, 8))
    nr, nc, nt, nd = len(rows), len(consts), len(cts), len(diff)
    gdt = [F32] * nd if gdt is None else gdt

    def body(*refs):
        vals = [r[...].astype(F32) for r in refs[:nr + nc + nt]]
        rv, cv, tv = vals[:nr], vals[nr:nr + nc], vals[nr + nc:]
        grow, gconst = refs[nr + nc + nt:nr + nc + nt + nd], refs[nr + nc + nt + nd:]

        def f(*dargs):
            full = list(rv)
            for pos, val in zip(diff, dargs[:nd]):
                full[pos] = val
            return tuple(fn(*full, *dargs[nd:]))

        _, vjp = jax.vjp(f, *[rv[p] for p in diff], *cv)
        g = vjp(tuple(tv))
        for o_ref, gv in zip(grow, g[:nd]):
            o_ref[...] = gv.astype(o_ref.dtype)
        first = jnp.logical_and(pl.program_id(0) == 0, pl.program_id(1) == 0)
        for o_ref, gv in zip(gconst, g[nd:]):
            @pl.when(first)
            def _(o_ref=o_ref, gv=gv):
                o_ref[...] = gv

            @pl.when(jnp.logical_not(first))
            def _(o_ref=o_ref, gv=gv):
                o_ref[...] += gv

    out_specs = ([pl.BlockSpec((tm, rows[p][1]), lambda i, j: (i, j)) for p in diff]
                 + [_const_spec(c) for c in consts])
    out_shape = ([jax.ShapeDtypeStruct((M, nj * rows[p][1]), dt) for p, dt in zip(diff, gdt)]
                 + [jax.ShapeDtypeStruct(c.shape, F32) for c in consts])
    res = pl.pallas_call(
        body, name=name, grid=(M // tm, nj),
        in_specs=([_row_spec(tm, w, c) for _, w, c in rows] + [_const_spec(c) for c in consts]
                  + [_row_spec(tm, w, c) for _, w, c in cts]),
        out_specs=out_specs, out_shape=out_shape,
        compiler_params=_cp("arbitrary", "arbitrary"),
    )(*[r[0] for r in rows], *consts, *[t[0] for t in cts])
    return res[:nd], res[nd:]


def _lane(shape=(1, LANE)):
    return lax.broadcasted_iota(jnp.int32, shape, len(shape) - 1)


def _sigmoid(x):
    return 1.0 / (1.0 + jnp.exp(-x))


def _rms(x, g, n):
    return x * lax.rsqrt(jnp.sum(x * x, axis=-1, keepdims=True) * (1.0 / n) + NORM_EPS) * g


def fn_rms(n):
    return lambda x, g: (_rms(x, g, n),)


def _rope(x, cos_t, sin_t):
    i = lax.broadcasted_iota(jnp.int32, (LANE, LANE), 0)
    j = lax.broadcasted_iota(jnp.int32, (LANE, LANE), 1)
    half = ROPE // 2
    lo = jnp.logical_and(jnp.logical_and(j >= NOPE, j < NOPE + half), i == j + half)
    hi = jnp.logical_and(jnp.logical_and(j >= NOPE + half, j < NOPE + ROPE), i == j - half)
    perm = jnp.where(hi, 1.0, 0.0) - jnp.where(lo, 1.0, 0.0)
    return x * cos_t + xdot(x, perm) * sin_t


def fn_qpost(q, cos_t, sin_t, g):
    return (_rope(_rms(q, g, QK_DIM), cos_t, sin_t),)


def fn_kpost(kn, small, cos_t, sin_t, g):
    lane = _lane()
    rope_lanes = jnp.logical_and(lane >= NOPE, lane < NOPE + ROPE)
    kc = kn + jnp.where(rope_lanes, small, 0.0)
    return (_rope(_rms(kc, g, QK_DIM), cos_t, sin_t),)


def fn_foxnorm(x, g):
    first = _lane() < FOX_DIM
    sq = x * x
    s0 = jnp.sum(jnp.where(first, sq, 0.0), axis=-1, keepdims=True)
    s1 = jnp.sum(jnp.where(first, 0.0, sq), axis=-1, keepdims=True)
    r0 = lax.rsqrt(s0 * (1.0 / FOX_DIM) + NORM_EPS)
    r1 = lax.rsqrt(s1 * (1.0 / FOX_DIM) + NORM_EPS)
    return (x * jnp.where(first, r0, r1) * g,)


F_LANE0 = NOPE + ROPE


def fn_fgate(small, bias):
    z = small + bias
    lf = jnp.minimum(z, 0.0) - jnp.log(1.0 + jnp.exp(-jnp.abs(z)))
    lane = _lane()
    return (jnp.where(jnp.logical_and(lane >= F_LANE0, lane < F_LANE0 + N_HEADS), lf, 0.0),)


def _gelu(y):
    return 0.5 * y * (1.0 + jnp.tanh(math.sqrt(2.0 / math.pi) * (y + 0.044715 * (y * y * y))))


def fn_s5post(ypre, u, d, wglu, bglu):
    y = _gelu(ypre + d * u)
    return (y * _sigmoid(dot_nn(y, wglu) + bglu),)


def fn_merge(om, of, os_, g0, g1, g2, wb0, wb1, wb2):
    return (_sigmoid(g0) * dot_nn(om, wb0) + _sigmoid(g1) * dot_nn(of, wb1)
            + _sigmoid(g2) * dot_nn(os_, wb2),)


def fn_add5(a, b):
    return (a[:, 0:LANE] + a[:, LANE:2 * LANE] + a[:, 2 * LANE:3 * LANE] + a[:, 3 * LANE:4 * LANE] + b,)


def fn_addt(a, b):
    return (a + b,)


def fn_s5params(lre, lim, lstep, btr, bti, ctr, cti):
    C = S5_C
    grp = lax.broadcasted_iota(jnp.int32, (LANE, C), 1) >> 6
    expand = jnp.where(lax.broadcasted_iota(jnp.int32, (LANE, C), 0) == grp, 1.0, 0.0)
    lane = _lane()
    st = jnp.where(lane < S5_G, jnp.exp(lstep), 0.0)
    step = jnp.sum(xdot(jnp.broadcast_to(st, (8, LANE)), expand), axis=0, keepdims=True) * 0.125
    zr, zi = lre * step, lim * step
    er = jnp.exp(zr)
    lbr, lbi = er * jnp.cos(zi), er * jnp.sin(zi)
    den = lre * lre + lim * lim
    nr = lbr - 1.0
    cfr = (nr * lre + lbi * lim) / den
    cfi = (lbi * lre - nr * lim) / den
    bbr = cfr * btr - cfi * bti
    bbi = cfr * bti + cfi * btr
    rg = lax.broadcasted_iota(jnp.int32, (BW, C), 0) >> 4
    cg = lax.broadcasted_iota(jnp.int32, (BW, C), 1) >> 6
    mb = jnp.where(rg == cg, 1.0, 0.0)
    b_re = jnp.concatenate([bbr] * S5_G, axis=0) * mb
    b_im = jnp.concatenate([bbi] * S5_G, axis=0) * mb
    ecol = jnp.where(lax.broadcasted_iota(jnp.int32, (LANE, BW), 0)
                     == (lax.broadcasted_iota(jnp.int32, (LANE, BW), 1) & 15), 1.0, 0.0)
    mc = jnp.where((lax.broadcasted_iota(jnp.int32, (C, BW), 0) >> 6)
                   == (lax.broadcasted_iota(jnp.int32, (C, BW), 1) >> 4), 1.0, 0.0)
    c_top = xdot(ctr, ecol) * mc
    c_bot = -(xdot(cti, ecol) * mc)
    return lbr, lbi, b_re, b_im, c_top, c_bot


def s5_params(p):
    def body(lre, lim, ls, btr, bti, ctr, cti, lb_ref, bre_ref, bim_ref, ct_ref, cb_ref):
        lbr, lbi, b_re, b_im, c_top, c_bot = fn_s5params(
            lre[...], lim[...], ls[...], btr[...], bti[...], ctr[...], cti[...])
        lb_ref[0:1, :] = lbr
        lb_ref[1:2, :] = lbi
        bre_ref[...] = b_re.astype(BF16)
        bim_ref[...] = b_im.astype(BF16)
        ct_ref[...] = c_top.astype(BF16)
        cb_ref[...] = c_bot.astype(BF16)

    return pl.pallas_call(
        body, name="s5_params",
        out_shape=[jax.ShapeDtypeStruct((2, S5_C), F32), jax.ShapeDtypeStruct((BW, S5_C), BF16),
                   jax.ShapeDtypeStruct((BW, S5_C), BF16), jax.ShapeDtypeStruct((S5_C, BW), BF16),
                   jax.ShapeDtypeStruct((S5_C, BW), BF16)],
        compiler_params=pltpu.CompilerParams(vmem_limit_bytes=VMEM_LIMIT),
    )(p["lre"], p["lim"], p["lstep"], p["btr"], p["bti"], p["ctr"], p["cti"])


S5_PARAM_NAMES = ("lre", "lim", "lstep", "btr", "bti", "ctr", "cti")


def s5_params_vjp(p, dl_r, dl_i, db_re, db_im, dc_top, dc_bot):
    def body(lre, lim, ls, btr, bti, ctr, cti, dlr, dli, dbr, dbi, dct, dcb, *outs):
        args = [r[...] for r in (lre, lim, ls, btr, bti, ctr, cti)]
        _, vjp = jax.vjp(fn_s5params, *args)
        g = vjp((jnp.sum(dlr[...], axis=0, keepdims=True), jnp.sum(dli[...], axis=0, keepdims=True),
                 dbr[...], dbi[...], dct[...], dcb[...]))
        for o_ref, gv in zip(outs, g):
            o_ref[...] = gv

    return pl.pallas_call(
        body, name="s5_params_vjp",
        out_shape=[jax.ShapeDtypeStruct(p[n].shape, F32) for n in S5_PARAM_NAMES],
        compiler_params=pltpu.CompilerParams(vmem_limit_bytes=VMEM_LIMIT),
    )(*[p[n] for n in S5_PARAM_NAMES], dl_r, dl_i, db_re, db_im, dc_top, dc_bot)


def _attn_tile(q, k, v, cq, ckt, *, hpt, dk, q0, tile):
    tq, S = q.shape[0], k.shape[0]
    row = q0 + lax.broadcasted_iota(jnp.int32, (tq, S), 0)
    col = lax.broadcasted_iota(jnp.int32, (tq, S), 1)
    causal = row >= col
    lane = _lane()
    out = jnp.zeros((tq, LANE), F32)
    for h in range(hpt):
        if hpt > 1:
            mine = (lane >> int(math.log2(LANE // hpt))) == h
            qh = jnp.where(mine, q, 0.0)
        else:
            qh = q
        s = dot_nt(qh, k) * (dk ** -0.5)
        if cq is not None:
            head = tile * hpt + h
            cqh = jnp.sum(jnp.where(lane == F_LANE0 + head, cq, 0.0), axis=1, keepdims=True)
            sub = lax.broadcasted_iota(jnp.int32, (8, 1), 0)
            ckh = jnp.sum(jnp.where(sub == head, ckt, 0.0), axis=0, keepdims=True)
            s = s + (cqh - ckh)
        s = jnp.where(causal, s, NEG_INF)
        m = lax.stop_gradient(jnp.max(s, axis=-1, keepdims=True))
        e = jnp.exp(s - m)
        p = e / jnp.sum(e, axis=-1, keepdims=True)
        oh = dot_nn(p, v)
        out = out + (jnp.where(mine, oh, 0.0) if hpt > 1 else oh)
    return out


def attention(name, q, k, v, *, B, S, ntile, hpt, dk, qc=0, kc=0, vc=0, cum=None, ckt=None, do=None, tq=256):
    tq = _pick(S, (tq, 128))
    nq = S // tq
    M = B * S
    bias = cum is not None
    kw = dict(hpt=hpt, dk=dk)

    def load(refs):
        q_ref, k_ref, v_ref = refs[:3]
        qv, kv, vv = q_ref[...].astype(F32), k_ref[...].astype(F32), v_ref[...].astype(F32)
        if bias:
            return qv, kv, vv, refs[3][...], refs[4][0]
        return qv, kv, vv, None, None

    nin = 5 if bias else 3

    def fwd_body(*refs):
        qv, kv, vv, cq, ck = load(refs)
        o = _attn_tile(qv, kv, vv, cq, ck, q0=pl.program_id(2) * tq, tile=pl.program_id(1), **kw)
        refs[nin][...] = o.astype(refs[nin].dtype)

    def bwd_body(*refs):
        qv, kv, vv, cq, ck = load(refs)
        dov = refs[nin][...].astype(F32)
        outs = refs[nin + 1:]
        q0, tile = pl.program_id(2) * tq, pl.program_id(1)
        if bias:
            f = lambda a, b, c, d, e: _attn_tile(a, b, c, d, e, q0=q0, tile=tile, **kw)
            _, vjp = jax.vjp(f, qv, kv, vv, cq, ck)
        else:
            f = lambda a, b, c: _attn_tile(a, b, c, None, None, q0=q0, tile=tile, **kw)
            _, vjp = jax.vjp(f, qv, kv, vv)
        g = vjp(dov)
        outs[0][...] = g[0]
        if bias:
            outs[3][...] = g[3]
        first = pl.program_id(2) == 0
        acc = [(outs[1], g[1]), (outs[2], g[2])] + ([(outs[4].at[0], g[4])] if bias else [])
        for o_ref, gv in acc:
            @pl.when(first)
            def _(o_ref=o_ref, gv=gv):
                o_ref[...] = gv

            @pl.when(jnp.logical_not(first))
            def _(o_ref=o_ref, gv=gv):
                o_ref[...] += gv

    qspec = lambda c: pl.BlockSpec((tq, LANE), lambda b, t, i: (b * nq + i, c + t))
    kspec = lambda c: pl.BlockSpec((S, LANE), lambda b, t, i: (b, c + t))
    in_specs, args = [qspec(qc), kspec(kc), kspec(vc)], [q, k, v]
    if bias:
        in_specs += [pl.BlockSpec((tq, LANE), lambda b, t, i: (b * nq + i, 0)),
                     pl.BlockSpec((1, 8, S), lambda b, t, i: (b, 0, 0))]
        args += [cum, ckt]
    if do is None:
        return pl.pallas_call(
            fwd_body, name=name, grid=(B, ntile, nq), in_specs=in_specs, out_specs=qspec(0),
            out_shape=jax.ShapeDtypeStruct((M, ntile * LANE), BF16),
            compiler_params=_cp("parallel", "parallel", "parallel"),
        )(*args)
    in_specs.append(qspec(0))
    args.append(do)
    out_specs = [qspec(0), kspec(0), kspec(0)]
    out_shape = [jax.ShapeDtypeStruct((M, ntile * LANE), F32)] * 3
    if bias:
        out_specs += [qspec(0), pl.BlockSpec((1, 8, S), lambda b, t, i: (b * ntile + t, 0, 0))]
        out_shape += [jax.ShapeDtypeStruct((M, ntile * LANE), F32),
                      jax.ShapeDtypeStruct((B * ntile, 8, S), F32)]
    return pl.pallas_call(
        bwd_body, name=name, grid=(B, ntile, nq), in_specs=in_specs, out_specs=out_specs,
        out_shape=out_shape, compiler_params=_cp("parallel", "parallel", "arbitrary"),
    )(*args)


def seq_cumsum(name, x, *, B, S, reverse):
    nb = S // LANE

    def body(x_ref, o_ref):
        r = lax.broadcasted_iota(jnp.int32, (LANE, LANE), 0)
        c = lax.broadcasted_iota(jnp.int32, (LANE, LANE), 1)
        tri = jnp.where((r <= c) if reverse else (r >= c), 1.0, 0.0)
        carry = jnp.zeros((1, LANE), F32)
        for blk in (range(nb - 1, -1, -1) if reverse else range(nb)):
            xb = x_ref[blk * LANE:(blk + 1) * LANE, :]
            o_ref[blk * LANE:(blk + 1) * LANE, :] = xdot(tri, xb) + carry
            carry = carry + jnp.sum(xb, axis=0, keepdims=True)

    return pl.pallas_call(
        body, name=name, grid=(B,), in_specs=[pl.BlockSpec((S, LANE), lambda b: (b, 0))],
        out_specs=pl.BlockSpec((S, LANE), lambda b: (b, 0)),
        out_shape=jax.ShapeDtypeStruct(x.shape, F32), compiler_params=_cp("parallel"),
    )(x)


SCAN_ROWS = 64


def _shift_rows(ref, r0, rows, d, up):
    if d % 8 == 0:
        return ref[pl.ds(r0 + d if up else r0 - d, rows), :]
    if up:
        win = ref[pl.ds(r0, rows + 8), :]
        return pltpu.roll(win, rows + 8 - d, 0)[0:rows, :]
    win = ref[pl.ds(r0 - 8, rows + 8), :]
    return pltpu.roll(win, d, 0)[8:rows + 8, :]


def s5_scan(name, x_re, x_im, lam, *, B, S, reverse, state=None):
    C = S5_C
    cw = LANE
    R = _pick(S, (SCAN_ROWS,))
    pad = max(S // 2, 8)
    nsteps = int(math.log2(S))
    assert 1 << nsteps == S
    base = 0 if reverse else pad
    with_grad = state is not None

    def body(*refs):
        if with_grad:
            xr, xi, lam_ref, sr, si, o_r, o_i, dl_r, dl_i, a_r, a_i, b_r, b_i = refs
        else:
            xr, xi, lam_ref, o_r, o_i, a_r, a_i, b_r, b_i = refs
        zero = jnp.zeros((pad, cw), F32)
        z0 = S if reverse else 0
        for buf in (a_r, a_i, b_r, b_i):
            buf[z0:z0 + pad, :] = zero
        a_r[base:base + S, :] = xr[...]
        a_i[base:base + S, :] = xi[...]
        mr = lam_ref[0:1, :]
        mi = -lam_ref[1:2, :] if reverse else lam_ref[1:2, :]
        src, dst = (a_r, a_i), (b_r, b_i)
        for step in range(nsteps):
            d = 1 << step
            last = step == nsteps - 1

            def chunk(c, _, src=src, dst=dst, d=d, last=last, mr=mr, mi=mi):
                r0 = pl.multiple_of(base + c * R, 8)
                pr = _shift_rows(src[0], r0, R, d, reverse)
                pi = _shift_rows(src[1], r0, R, d, reverse)
                nr = src[0][pl.ds(r0, R), :] + mr * pr - mi * pi
                ni = src[1][pl.ds(r0, R), :] + mr * pi + mi * pr
                if last:
                    o0 = pl.multiple_of(c * R, 8)
                    o_r[pl.ds(o0, R), :] = nr
                    o_i[pl.ds(o0, R), :] = ni
                else:
                    dst[0][pl.ds(r0, R), :] = nr
                    dst[1][pl.ds(r0, R), :] = ni
                return 0

            lax.fori_loop(0, S // R, chunk, 0)
            mr, mi = mr * mr - mi * mi, 2.0 * mr * mi
            src, dst = dst, src
        if with_grad:
            def fold(v):
                return jnp.sum(v.reshape(R // 8, 8, cw), axis=0)

            def accum(c, carry, first=False):
                r0 = 0 if first else pl.multiple_of(c * R, 8)
                gr, gi = o_r[pl.ds(r0, R), :], o_i[pl.ds(r0, R), :]
                if first:
                    keep = lax.broadcasted_iota(jnp.int32, (R, 1), 0) >= 1
                    pr = jnp.where(keep, pltpu.roll(sr[0:R, :], 1, 0), 0.0)
                    pi = jnp.where(keep, pltpu.roll(si[0:R, :], 1, 0), 0.0)
                else:
                    pr = _shift_rows(sr, r0, R, 1, False)
                    pi = _shift_rows(si, r0, R, 1, False)
                return (carry[0] + fold(gr * pr + gi * pi), carry[1] + fold(gi * pr - gr * pi))

            acc = accum(0, (jnp.zeros((8, cw), F32), jnp.zeros((8, cw), F32)), first=True)
            acc = lax.fori_loop(1, S // R, accum, acc)
            dl_r[...] = acc[0]
            dl_i[...] = acc[1]

    seq = pl.BlockSpec((S, cw), lambda b, j: (b, j))
    in_specs = [seq, seq, pl.BlockSpec((2, cw), lambda b, j: (0, j))]
    args = [x_re, x_im, lam]
    out_specs = [seq, seq]
    out_shape = [jax.ShapeDtypeStruct(x_re.shape, F32)] * 2
    if with_grad:
        in_specs += [seq, seq]
        args += list(state)
        out_specs += [pl.BlockSpec((8, cw), lambda b, j: (b, j))] * 2
        out_shape += [jax.ShapeDtypeStruct((B * 8, C), F32)] * 2
    return pl.pallas_call(
        body, name=name, grid=(B, C // cw), in_specs=in_specs, out_specs=out_specs, out_shape=out_shape,
        scratch_shapes=[pltpu.VMEM((S + pad, cw), F32)] * 4,
        compiler_params=_cp("parallel", "parallel"),
    )(*args)


CONV_CW = 256


def _conv_taps(ref, r0, rows, first):
    cur = ref[pl.ds(r0, rows), :]
    if first:
        row = lax.broadcasted_iota(jnp.int32, (rows, 1), 0)
        p1 = jnp.where(row >= 1, pltpu.roll(cur, 1, 0), 0.0)
        p2 = jnp.where(row >= 2, pltpu.roll(cur, 2, 0), 0.0)
    else:
        p1 = _shift_rows(ref, r0, rows, 1, False)
        p2 = _shift_rows(ref, r0, rows, 2, False)
    return cur, p1, p2


def _conv_apply(w_ref, taps):
    return w_ref[2:3, :] * taps[0] + w_ref[1:2, :] * taps[1] + w_ref[0:1, :] * taps[2]


def conv_gate_fwd(up, conv_w, *, B, S):
    M, F2 = up.shape
    F = F2 // 2
    cw = _pick(F, (CONV_CW, LANE))
    nf = F // cw
    R = _pick(S, (SCAN_ROWS,))

    def body(g_ref, v_ref, wg_ref, wv_ref, o_ref):
        def chunk(c, _, first=False):
            r0 = 0 if first else pl.multiple_of(c * R, 8)
            cg = _conv_apply(wg_ref, _conv_taps(g_ref, r0, R, first))
            cv = _conv_apply(wv_ref, _conv_taps(v_ref, r0, R, first))
            o_ref[pl.ds(r0, R), :] = (cg * _sigmoid(cg) * cv).astype(o_ref.dtype)
            return 0

        chunk(0, 0, first=True)
        lax.fori_loop(1, S // R, chunk, 0)

    seq = lambda off: pl.BlockSpec((S, cw), lambda b, j: (b, off + j))
    wsp = lambda off: pl.BlockSpec((3, cw), lambda b, j: (0, off + j))
    return pl.pallas_call(
        body, name="conv_gate", grid=(B, nf), in_specs=[seq(0), seq(nf), wsp(0), wsp(nf)],
        out_specs=seq(0), out_shape=jax.ShapeDtypeStruct((M, F), BF16),
        compiler_params=_cp("parallel", "parallel"),
    )(up, up, conv_w, conv_w)


def conv_gate_bwd(up, conv_w, dact, *, B, S):
    M, F2 = up.shape
    F = F2 // 2
    cw = _pick(F, (CONV_CW, LANE))
    nf = F // cw
    R = _pick(S, (SCAN_ROWS,))
    nchunk = S // R

    def body(s_ref, p_ref, ws_ref, wp_ref, da_ref, du_ref, dw_ref, dc_ref):
        is_gate = pl.program_id(0) < nf
        dc_ref[S:S + 8, :] = jnp.zeros((8, cw), F32)

        def fold(v):
            return jnp.sum(v.reshape(R // 8, 8, cw), axis=0)

        def pass1(c, acc, first=False):
            r0 = 0 if first else pl.multiple_of(c * R, 8)
            taps = _conv_taps(s_ref, r0, R, first)
            cs = _conv_apply(ws_ref, taps)
            cp = _conv_apply(wp_ref, _conv_taps(p_ref, r0, R, first))
            da = da_ref[pl.ds(r0, R), :]
            sg = _sigmoid(cs)
            d_gate = da * cp * (sg * (1.0 + cs * (1.0 - sg)))
            sp = _sigmoid(cp)
            d_val = da * (cp * sp)
            dc = jnp.where(is_gate, d_gate, d_val)
            dc_ref[pl.ds(r0, R), :] = dc
            return (acc[0] + fold(dc * taps[2]), acc[1] + fold(dc * taps[1]), acc[2] + fold(dc * taps[0]))

        z = jnp.zeros((8, cw), F32)
        acc = pass1(0, (z, z, z), first=True)
        acc = lax.fori_loop(1, nchunk, pass1, acc)

        def pass2(c, _):
            r0 = pl.multiple_of(c * R, 16)
            n0 = dc_ref[pl.ds(r0, R), :]
            n1 = _shift_rows(dc_ref, r0, R, 1, True)
            n2 = _shift_rows(dc_ref, r0, R, 2, True)
            du = ws_ref[2:3, :] * n0 + ws_ref[1:2, :] * n1 + ws_ref[0:1, :] * n2
            du_ref[pl.ds(r0, R), :] = du.astype(du_ref.dtype)
            return 0

        lax.fori_loop(0, nchunk, pass2, 0)
        first_b = pl.program_id(1) == 0
        for tap in range(3):
            tot = jnp.sum(acc[tap], axis=0, keepdims=True)

            @pl.when(first_b)
            def _(tap=tap, tot=tot):
                dw_ref[tap:tap + 1, :] = tot

            @pl.when(jnp.logical_not(first_b))
            def _(tap=tap, tot=tot):
                dw_ref[tap:tap + 1, :] += tot

    n2 = 2 * nf
    seq = lambda f: pl.BlockSpec((S, cw), lambda j, b: (b, f(j)))
    wsp = lambda f: pl.BlockSpec((3, cw), lambda j, b: (0, f(j)))
    same, other, act_col = (lambda j: j), (lambda j: (j + nf) % n2), (lambda j: j % nf)
    return pl.pallas_call(
        body, name="conv_gate_vjp", grid=(n2, B),
        in_specs=[seq(same), seq(other), wsp(same), wsp(other), seq(act_col)],
        out_specs=[seq(same), wsp(same)],
        out_shape=[jax.ShapeDtypeStruct((M, F2), BF16), jax.ShapeDtypeStruct((3, F2), F32)],
        scratch_shapes=[pltpu.VMEM((S + 8, cw), F32)],
        compiler_params=_cp("parallel", "arbitrary"),
    )(up, up, conv_w, conv_w, dact)


def loss_head(y, target):
    M, D = y.shape
    tm = _pick(M, (256, 128, 64, 32, 16, 8))

    def body(y_ref, t_ref, dy_ref, l_ref):
        diff = y_ref[...] - t_ref[...]
        dy_ref[...] = diff * (1.0 / D)
        part = jnp.sum(jnp.sum(diff * diff, axis=1, keepdims=True), axis=0, keepdims=True)

        @pl.when(pl.program_id(0) == 0)
        def _():
            l_ref[...] = jnp.zeros_like(l_ref)

        l_ref[...] += part

    row = pl.BlockSpec((tm, D), lambda i: (i, 0))
    return pl.pallas_call(
        body, name="loss_head", grid=(M // tm,), in_specs=[row, row],
        out_specs=[row, pl.BlockSpec((8, LANE), lambda i: (0, 0))],
        out_shape=[jax.ShapeDtypeStruct((M, D), F32), jax.ShapeDtypeStruct((8, LANE), F32)],
        compiler_params=_cp("arbitrary"),
    )(y, target)


def adamw(name, w, g, m, v):
    R, C = w.shape
    tr = _pick(R, (256, 128, 64, 32, 16, 8))

    def body(w_ref, g_ref, m_ref, v_ref, d_ref, nm_ref, nv_ref):
        gv = g_ref[...]
        nm = ADAM_B1 * m_ref[...] + (1.0 - ADAM_B1) * gv
        nv = ADAM_B2 * v_ref[...] + (1.0 - ADAM_B2) * (gv * gv)
        m_hat = nm / (1.0 - ADAM_B1 ** ADAM_STEP)
        v_hat = nv / (1.0 - ADAM_B2 ** ADAM_STEP)
        d_ref[...] = -ADAM_LR * (m_hat / (jnp.sqrt(v_hat) + ADAM_EPS) + ADAM_WD * w_ref[...])
        nm_ref[...] = nm
        nv_ref[...] = nv

    blk = pl.BlockSpec((tr, C), lambda i: (i, 0))
    return pl.pallas_call(
        body, name=name, grid=(R // tr,), in_specs=[blk] * 4, out_specs=[blk] * 3,
        out_shape=[jax.ShapeDtypeStruct((R, C), F32)] * 3, compiler_params=_cp("parallel"),
    )(w, g, m, v)


def _seg(D):
    o = 3 * D
    return dict(ckv=o, fq=o + 256, fk=o + 512, fv=o + 768, u=o + 1024, small=o + 1280, cq=o + 1536, P=o + 1920)


def _pad_last(a, n):
    return jnp.pad(a, [(0, 0)] * (a.ndim - 1) + [(0, n - a.shape[-1])])


def _place(a, lo, n=LANE):
    return jnp.pad(a, [(0, 0)] * (a.ndim - 1) + [(lo, n - lo - a.shape[-1])])


def prep_weights(w):
    L, D = w["attn_norm_g"].shape
    win = w["w_in"]
    z = lambda n: jnp.zeros((L, D, n), win.dtype)
    g0 = 1700
    wp = jnp.concatenate(
        [win[..., g0:g0 + 3 * D], win[..., 384:640], win[..., 672:928], win[..., 928:1184], win[..., 1184:1440],
         win[..., 1444:1700], z(NOPE), win[..., 640:672], win[..., 1440:1444], z(LANE - F_LANE0 - N_HEADS), z(LANE),
         win[..., 0:384]], axis=-1)
    wukv, wb = w["w_ukv"], w["w_branch"]
    row3 = lambda a: a[:, None, :]
    return dict(
        g1=row3(w["attn_norm_g"]), Wp=wp, gql=row3(w["q_lat_norm_g"]), gkvl=row3(w["kv_lat_norm_g"]),
        Wuq=_pad_last(w["w_uq"], LANE).reshape(L, Q_RANK, N_HEADS * LANE),
        Wk=_pad_last(wukv[..., :NOPE], LANE).reshape(L, KV_RANK, N_HEADS * LANE),
        Wv=_pad_last(wukv[..., NOPE:], LANE).reshape(L, KV_RANK, N_HEADS * LANE),
        gq=row3(_pad_last(w["mla_q_norm_g"], LANE)), gk=row3(_pad_last(w["mla_k_norm_g"], LANE)),
        gfq=row3(jnp.tile(w["fox_q_norm_g"], (1, 2))), gfk=row3(jnp.tile(w["fox_k_norm_g"], (1, 2))),
        fbias=row3(_place(w["fox_f_bias"], F_LANE0)),
        lre=w["s5_lambda_re"].reshape(L, 1, S5_C), lim=w["s5_lambda_im"].reshape(L, 1, S5_C),
        lstep=row3(_pad_last(w["s5_log_step"], LANE)),
        btr=jnp.transpose(w["s5_b_re"], (0, 3, 1, 2)).reshape(L, S5_H, S5_C),
        bti=jnp.transpose(w["s5_b_im"], (0, 3, 1, 2)).reshape(L, S5_H, S5_C),
        ctr=_pad_last(jnp.transpose(w["s5_c_re"], (0, 1, 3, 2)).reshape(L, S5_C, S5_H), LANE),
        cti=_pad_last(jnp.transpose(w["s5_c_im"], (0, 1, 3, 2)).reshape(L, S5_C, S5_H), LANE),
        s5d=w["s5_d"].reshape(L, 1, BW), Wglu=w["s5_w_glu"], bglu=row3(w["s5_b_glu"]),
        Wb0=jnp.pad(wb[:, 0].reshape(L, N_HEADS, V_DIM, D), ((0, 0), (0, 0), (0, LANE - V_DIM), (0, 0))
                    ).reshape(L, N_HEADS * LANE, D),
        Wb1=wb[:, 1], Wb2=wb[:, 2], Wout=w["w_out"], g2=row3(w["ffn_norm_g"]), Wup=w["w_up"],
        convw=w["ffn_conv_w"], Wdown=w["w_down"],
    )


BIG_KEYS = ("Wp", "Wuq", "Wk", "Wv", "Wout", "Wup", "Wdown")


def with_transposes(P):
    out = dict(P)
    for k in BIG_KEYS:
        out[k + "T"] = jnp.swapaxes(P[k], 1, 2)
    return out


def layer_params(P, l):
    return {k: (v if k in BIG_KEYS or k[:-1] in BIG_KEYS else v[l]) for k, v in P.items()}


def unprep_grads(G, D):
    L = G["g1"].shape[0]
    s = _seg(D)
    dwp = G["Wp"]
    sm = s["small"]
    w_in = jnp.concatenate(
        [dwp[..., s["cq"]:s["cq"] + 384], dwp[..., s["ckv"]:s["ckv"] + 256], dwp[..., sm + NOPE:sm + NOPE + ROPE],
         dwp[..., s["fq"]:s["fq"] + 768], dwp[..., sm + F_LANE0:sm + F_LANE0 + N_HEADS],
         dwp[..., s["u"]:s["u"] + 256], dwp[..., 0:3 * D]], axis=-1)
    heads = lambda a, rows, keep: a.reshape(L, rows, N_HEADS, LANE)[..., :keep]
    wb0 = G["Wb0"].reshape(L, N_HEADS, LANE, D)[:, :, :V_DIM].reshape(L, BW, D)
    gf = lambda a: a[:, 0, :FOX_DIM] + a[:, 0, FOX_DIM:]
    return dict(
        attn_norm_g=G["g1"][:, 0], w_in=w_in, q_lat_norm_g=G["gql"][:, 0], w_uq=heads(G["Wuq"], Q_RANK, QK_DIM),
        kv_lat_norm_g=G["gkvl"][:, 0],
        w_ukv=jnp.concatenate([heads(G["Wk"], KV_RANK, NOPE), heads(G["Wv"], KV_RANK, V_DIM)], axis=-1),
        mla_q_norm_g=G["gq"][:, 0, :QK_DIM], mla_k_norm_g=G["gk"][:, 0, :QK_DIM],
        fox_q_norm_g=gf(G["gfq"]), fox_k_norm_g=gf(G["gfk"]),
        fox_f_bias=G["fbias"][:, 0, F_LANE0:F_LANE0 + N_HEADS],
        s5_lambda_re=G["lre"].reshape(L, S5_G, S5_P), s5_lambda_im=G["lim"].reshape(L, S5_G, S5_P),
        s5_b_re=jnp.transpose(G["btr"].reshape(L, S5_H, S5_G, S5_P), (0, 2, 3, 1)),
        s5_b_im=jnp.transpose(G["bti"].reshape(L, S5_H, S5_G, S5_P), (0, 2, 3, 1)),
        s5_c_re=jnp.transpose(G["ctr"][..., :S5_H].reshape(L, S5_G, S5_P, S5_H), (0, 1, 3, 2)),
        s5_c_im=jnp.transpose(G["cti"][..., :S5_H].reshape(L, S5_G, S5_P, S5_H), (0, 1, 3, 2)),
        s5_d=G["s5d"].reshape(L, S5_G, S5_H), s5_log_step=G["lstep"][:, 0, :S5_G],
        s5_w_glu=G["Wglu"], s5_b_glu=G["bglu"][:, 0],
        w_branch=jnp.stack([wb0, G["Wb1"], G["Wb2"]], axis=1), w_out=G["Wout"], ffn_norm_g=G["g2"][:, 0],
        w_up=G["Wup"], ffn_conv_w=G["convw"], w_down=G["Wdown"],
    )


def rope_tables(positions):
    inv_freq = ROPE_THETA ** (-jnp.arange(0, ROPE, 2, dtype=F32) / ROPE)
    ang = positions.astype(F32)[..., None] * inv_freq
    cos, sin = jnp.cos(ang), jnp.sin(ang)
    ones = jnp.ones(ang.shape[:-1] + (NOPE,), F32)
    zeros = jnp.zeros(ang.shape[:-1] + (LANE - NOPE - ROPE,), F32)
    cos_t = jnp.concatenate([ones, cos, cos, zeros], axis=-1)
    sin_t = jnp.concatenate([0.0 * ones, sin, sin, zeros], axis=-1)
    return cos_t.reshape(-1, LANE), sin_t.reshape(-1, LANE)


def fn_rms_res(n):
    return lambda x, g: (_rms(x, g, n), x)


def _s5_mats(p):
    return s5_params({k: p[k] for k in S5_PARAM_NAMES})


def layer_fwd(x, p, l, cos_t, sin_t, B, S):
    M, D = x.shape
    s = _seg(D)
    sm = s["small"] // LANE
    head = lambda j: j
    h = rowwise("rms_attn", fn_rms(D), [(x, D, 0)], [p["g1"]], [(D, BF16)])[0]
    proj = mm(h, p["Wp"], "nn", name="in_proj", layer=l)
    cnq = rowwise("latq_norm", fn_rms(Q_RANK), [(proj, Q_RANK, s["cq"] // Q_RANK)], [p["gql"]], [(Q_RANK, BF16)])[0]
    cnkv = rowwise("latkv_norm", fn_rms(KV_RANK), [(proj, KV_RANK, s["ckv"] // KV_RANK)], [p["gkvl"]],
                   [(KV_RANK, BF16)])[0]
    qraw = mm(cnq, p["Wuq"], "nn", name="q_up", layer=l)
    kn = mm(cnkv, p["Wk"], "nn", name="k_up", layer=l)
    v5 = mm(cnkv, p["Wv"], "nn", name="v_up", out_dtype=BF16, layer=l)
    qrot = rowwise("q_post", fn_qpost, [(qraw, LANE, head), (cos_t, LANE, 0), (sin_t, LANE, 0)], [p["gq"]],
                   [(LANE, BF16)], nj=N_HEADS)[0]
    krot = rowwise("k_post", fn_kpost, [(kn, LANE, head), (proj, LANE, sm), (cos_t, LANE, 0), (sin_t, LANE, 0)],
                   [p["gk"]], [(LANE, BF16)], nj=N_HEADS)[0]
    omla = attention("mla_attn", qrot, krot, v5, B=B, S=S, ntile=N_HEADS, hpt=1, dk=QK_DIM)
    fq0, fk0 = s["fq"] // LANE, s["fk"] // LANE
    qf = rowwise("foxq_norm", fn_foxnorm, [(proj, LANE, lambda j: fq0 + j)], [p["gfq"]], [(LANE, BF16)], nj=2)[0]
    kf = rowwise("foxk_norm", fn_foxnorm, [(proj, LANE, lambda j: fk0 + j)], [p["gfk"]], [(LANE, BF16)], nj=2)[0]
    lf = rowwise("fgate", fn_fgate, [(proj, LANE, sm)], [p["fbias"]], [(LANE, F32)])[0]
    cum = seq_cumsum("fox_cumsum", lf, B=B, S=S, reverse=False)
    ckt = _pad_rows8(jnp.transpose(cum.reshape(B, S, LANE)[:, :, F_LANE0:F_LANE0 + N_HEADS], (0, 2, 1)))
    ofox = attention("fox_attn", qf, kf, proj, B=B, S=S, ntile=2, hpt=2, dk=FOX_DIM, vc=s["fv"] // LANE,
                     cum=cum, ckt=ckt)
    lam, bre, bim, ctop, cbot = _s5_mats(p)
    u16 = proj[:, s["u"]:s["u"] + BW].astype(BF16)
    bur = mm(u16, bre, "nn", name="s5_bu_re")
    bui = mm(u16, bim, "nn", name="s5_bu_im")
    sr, si = s5_scan("s5_scan", bur, bui, lam, B=B, S=S, reverse=False)
    ypre = mm(si, cbot, "nn", name="s5_y_im", add=mm(sr, ctop, "nn", name="s5_y_re"))
    os5 = rowwise("s5_post", fn_s5post, [(ypre, BW, 0), (proj, BW, s["u"] // BW)],
                  [p["s5d"], p["Wglu"], p["bglu"]], [(BW, BF16)])[0]
    merged = rowwise("merge", fn_merge,
                     [(omla, N_HEADS * LANE, 0), (ofox, BW, 0), (os5, BW, 0), (proj, D, 0), (proj, D, 1), (proj, D, 2)],
                     [p["Wb0"], p["Wb1"], p["Wb2"]], [(D, BF16)])[0]
    xmid = mm(merged, p["Wout"], "nn", name="out_proj", add=x, layer=l)
    h2 = rowwise("rms_ffn", fn_rms(D), [(xmid, D, 0)], [p["g2"]], [(D, BF16)])[0]
    up = mm(h2, p["Wup"], "nn", name="ffn_up", layer=l)
    act = conv_gate_fwd(up, p["convw"], B=B, S=S)
    xout = mm(act, p["Wdown"], "nn", name="ffn_down", add=xmid, layer=l)
    saved = dict(x=x, h=h, proj=proj, cnq=cnq, cnkv=cnkv, qraw=qraw, kn=kn, v5=v5, qrot=qrot, krot=krot, qf=qf,
                 kf=kf, cum=cum, ckt=ckt, omla=omla, ofox=ofox, os5=os5, u16=u16, sr=sr, si=si, ypre=ypre,
                 merged=merged, xmid=xmid, h2=h2, up=up, act=act)
    return xout, saved


def _pad_rows8(a):
    return jnp.pad(a, ((0, 0), (0, 8 - a.shape[1]), (0, 0)))


def layer_bwd(dx, p, l, sv, cos_t, sin_t, B, S):
    M, D = dx.shape
    s = _seg(D)
    sm = s["small"] // LANE
    head = lambda j: j
    proj = sv["proj"]
    dact = mm(dx, p["WdownT"], "nn", name="ffn_down_dx", layer=l)
    d_wdown = mm(sv["act"], dx, "tn", name="ffn_down_dw")
    dup, d_convw = conv_gate_bwd(sv["up"], p["convw"], dact, B=B, S=S)
    dh2 = mm(dup, p["WupT"], "nn", name="ffn_up_dx", layer=l)
    d_wup = mm(sv["h2"], dup, "tn", name="ffn_up_dw")
    (dxmid,), (d_g2,) = rowwise_vjp("rms_ffn_vjp", fn_rms_res(D), [(sv["xmid"], D, 0)], [p["g2"]],
                                    [(dh2, D, 0), (dx, D, 0)], [0])
    dmerged = mm(dxmid, p["WoutT"], "nn", name="out_proj_dx", layer=l)
    d_wout = mm(sv["merged"], dxmid, "tn", name="out_proj_dw")
    (dom, dof, dos, dg0, dg1, dg2), (d_wb0, d_wb1, d_wb2) = rowwise_vjp(
        "merge_vjp", fn_merge,
        [(sv["omla"], N_HEADS * LANE, 0), (sv["ofox"], BW, 0), (sv["os5"], BW, 0), (proj, D, 0), (proj, D, 1),
         (proj, D, 2)], [p["Wb0"], p["Wb1"], p["Wb2"]], [(dmerged, D, 0)], [0, 1, 2, 3, 4, 5], tm=128,
        gdt=[BF16, BF16, F32, BF16, BF16, BF16])
    lam, bre, bim, ctop, cbot = _s5_mats(p)
    (dypre, du_a), (d_s5d, d_wglu, d_bglu) = rowwise_vjp(
        "s5_post_vjp", fn_s5post, [(sv["ypre"], BW, 0), (proj, BW, s["u"] // BW)],
        [p["s5d"], p["Wglu"], p["bglu"]], [(dos, BW, 0)], [0, 1], gdt=[BF16, F32])
    dsr = mm(dypre, ctop.T, "nn", name="s5_y_re_dx")
    dsi = mm(dypre, cbot.T, "nn", name="s5_y_im_dx")
    d_ctop = mm(sv["sr"], dypre, "tn", name="s5_y_re_dw")
    d_cbot = mm(sv["si"], dypre, "tn", name="s5_y_im_dw")
    gr, gi, dl_r, dl_i = s5_scan("s5_scan_vjp", dsr, dsi, lam, B=B, S=S, reverse=True, state=(sv["sr"], sv["si"]))
    du = mm(gi, bim.T, "nn", name="s5_bu_im_dx", out_dtype=BF16,
            add=mm(gr, bre.T, "nn", name="s5_bu_re_dx", add=du_a))
    d_bre = mm(sv["u16"], gr, "tn", name="s5_bu_re_dw")
    d_bim = mm(sv["u16"], gi, "tn", name="s5_bu_im_dw")
    d_s5 = s5_params_vjp({k: p[k] for k in S5_PARAM_NAMES}, dl_r, dl_i, d_bre, d_bim, d_ctop, d_cbot)
    dqf, dkf, dfv, dcq_t, dckt_t = attention("fox_attn_vjp", sv["qf"], sv["kf"], proj, B=B, S=S, ntile=2, hpt=2,
                                             dk=FOX_DIM, vc=s["fv"] // LANE, cum=sv["cum"], ckt=sv["ckt"], do=dof)
    dck = dckt_t.reshape(B, 2, 8, S)
    dck = jnp.transpose(dck[:, 0, :N_HEADS] + dck[:, 1, :N_HEADS], (0, 2, 1)).reshape(M, N_HEADS)
    dcum = rowwise("fox_dcum", lambda a, b: (a[:, 0:LANE] + a[:, LANE:2 * LANE] + b,),
                   [(dcq_t, 2 * LANE, 0), (_place(dck, F_LANE0), LANE, 0)], [], [(LANE, F32)])[0]
    dlf = seq_cumsum("fox_cumsum_vjp", dcum, B=B, S=S, reverse=True)
    (dsmall_f,), (d_fbias,) = rowwise_vjp("fgate_vjp", fn_fgate, [(proj, LANE, sm)], [p["fbias"]],
                                          [(dlf, LANE, 0)], [0])
    fq0, fk0 = s["fq"] // LANE, s["fk"] // LANE
    (dfq,), (d_gfq,) = rowwise_vjp("foxq_norm_vjp", fn_foxnorm, [(proj, LANE, lambda j: fq0 + j)], [p["gfq"]],
                                   [(dqf, LANE, head)], [0], nj=2, gdt=[BF16])
    (dfk,), (d_gfk,) = rowwise_vjp("foxk_norm_vjp", fn_foxnorm, [(proj, LANE, lambda j: fk0 + j)], [p["gfk"]],
                                   [(dkf, LANE, head)], [0], nj=2, gdt=[BF16])
    dqrot, dkrot, dv5 = attention("mla_attn_vjp", sv["qrot"], sv["krot"], sv["v5"], B=B, S=S, ntile=N_HEADS,
                                  hpt=1, dk=QK_DIM, do=dom)
    (dqraw,), (d_gq,) = rowwise_vjp("q_post_vjp", fn_qpost,
                                    [(sv["qraw"], LANE, head), (cos_t, LANE, 0), (sin_t, LANE, 0)], [p["gq"]],
                                    [(dqrot, LANE, head)], [0], nj=N_HEADS, gdt=[BF16])
    (dkn, dsmall_k), (d_gk,) = rowwise_vjp(
        "k_post_vjp", fn_kpost, [(sv["kn"], LANE, head), (proj, LANE, sm), (cos_t, LANE, 0), (sin_t, LANE, 0)],
        [p["gk"]], [(dkrot, LANE, head)], [0, 1], nj=N_HEADS, gdt=[BF16, F32])
    dsmall = rowwise("small_sum", fn_add5, [(dsmall_k, N_HEADS * LANE, 0), (dsmall_f, LANE, 0)], [], [(LANE, BF16)])[0]
    dcnq = mm(dqraw, p["WuqT"], "nn", name="q_up_dx", layer=l)
    d_wuq = mm(sv["cnq"], dqraw, "tn", name="q_up_dw")
    dcnkv = mm(dv5, p["WvT"], "nn", name="v_up_dx", layer=l, add=mm(dkn, p["WkT"], "nn", name="k_up_dx", layer=l))
    d_wk = mm(sv["cnkv"], dkn, "tn", name="k_up_dw")
    d_wv = mm(sv["cnkv"], dv5, "tn", name="v_up_dw")
    (dcq,), (d_gql,) = rowwise_vjp("latq_norm_vjp", fn_rms(Q_RANK), [(proj, Q_RANK, s["cq"] // Q_RANK)], [p["gql"]],
                                   [(dcnq, Q_RANK, 0)], [0], gdt=[BF16])
    (dckv,), (d_gkvl,) = rowwise_vjp("latkv_norm_vjp", fn_rms(KV_RANK), [(proj, KV_RANK, s["ckv"] // KV_RANK)],
                                     [p["gkvl"]], [(dcnkv, KV_RANK, 0)], [0], gdt=[BF16])
    dproj = jnp.concatenate([dg0, dg1, dg2, dckv, dfq, dfk, dfv.astype(BF16), du, dsmall,
                             jnp.zeros((M, LANE), BF16), dcq], axis=1)
    dh = mm(dproj, p["WpT"], "nn", name="in_proj_dx", layer=l)
    d_wp = mm(sv["h"], dproj, "tn", name="in_proj_dw")
    (dxin,), (d_g1,) = rowwise_vjp("rms_attn_vjp", fn_rms_res(D), [(sv["x"], D, 0)], [p["g1"]],
                                   [(dh, D, 0), (dxmid, D, 0)], [0])
    grads = dict(g1=d_g1, Wp=d_wp, gql=d_gql, gkvl=d_gkvl, Wuq=d_wuq, Wk=d_wk, Wv=d_wv, gq=d_gq, gk=d_gk,
                 gfq=d_gfq, gfk=d_gfk, fbias=d_fbias, s5d=d_s5d, Wglu=d_wglu, bglu=d_bglu, Wb0=d_wb0, Wb1=d_wb1,
                 Wb2=d_wb2, Wout=d_wout, g2=d_g2, Wup=d_wup, convw=d_convw, Wdown=d_wdown)
    grads.update(dict(zip(S5_PARAM_NAMES, d_s5)))
    return dxin, grads


def local_step(x, positions, target, w):
    B, S, D = x.shape
    M = B * S
    P = with_transposes(prep_weights(w))
    L = P["g1"].shape[0]
    cos_t, sin_t = rope_tables(positions)
    xc, saved = x.reshape(M, D), []
    for l in range(L):
        xc, sv = layer_fwd(xc, layer_params(P, l), l, cos_t, sin_t, B, S)
        saved.append(sv)
    dxc, sq = loss_head(xc, target.reshape(M, D))
    grads = [None] * L
    for l in reversed(range(L)):
        dxc, grads[l] = layer_bwd(dxc, layer_params(P, l), l, saved[l], cos_t, sin_t, B, S)
    G = {k: jnp.stack([g[k] for g in grads]) for k in grads[0]}
    return sq, dxc.reshape(B, S, D), unprep_grads(G, D)


MESH = pl.DeviceIdType.MESH
ANY = pl.BlockSpec(memory_space=pl.ANY)
N_CHIPS = 4
SHARDED = ("w_in", "w_uq", "w_ukv", "s5_w_glu", "w_branch", "w_out", "w_up", "ffn_conv_w", "w_down")
MINOR = ("w_branch", "w_up", "ffn_conv_w")
F32_TRAVEL = ("ffn_conv_w",)
WEIGHTS = ("attn_norm_g", "w_in", "q_lat_norm_g", "w_uq", "kv_lat_norm_g", "w_ukv", "mla_q_norm_g", "mla_k_norm_g",
           "fox_q_norm_g", "fox_k_norm_g", "fox_f_bias", "s5_lambda_re", "s5_lambda_im", "s5_b_re", "s5_b_im",
           "s5_c_re", "s5_c_im", "s5_d", "s5_log_step", "s5_w_glu", "s5_b_glu", "w_branch", "w_out", "ffn_norm_g",
           "w_up", "ffn_conv_w", "w_down")
SMALL = tuple(n for n in WEIGHTS if n not in SHARDED)


def shard3(name, a):
    L = a.shape[0]
    if name in ("w_uq", "w_ukv"):
        return a.reshape(L, a.shape[1], -1)
    if name == "w_branch":
        return a.reshape(L, -1, a.shape[-1])
    return a


def travel3(name, a):
    a = shard3(name, a)
    return a.reshape(a.shape[0], -1, LANE) if name == "w_in" else a


def full4(name, a):
    L = a.shape[0]
    if name == "w_in":
        return jnp.transpose(a.reshape(L, a.shape[1], N_CHIPS, -1), (0, 2, 1, 3)).reshape(L, N_CHIPS, -1, LANE)
    if name in MINOR:
        return a.reshape(L, 1, -1, a.shape[-1])
    a = a.reshape(L, a.shape[1], -1)
    return a.reshape(L, N_CHIPS, a.shape[1] // N_CHIPS, a.shape[2])


def from_full4(name, a, ref_tail):
    L = a.shape[0]
    if name == "w_in":
        a = jnp.transpose(a.reshape(L, N_CHIPS, ref_tail[0], -1), (0, 2, 1, 3))
    return a.reshape((L,) + tuple(ref_tail))


def _where():
    x, y, c = lax.axis_index("x"), lax.axis_index("y"), lax.axis_index("c")
    chips = [(1 - x, y), (x, 1 - y), (1 - x, 1 - y)]
    return (x, y, c), 2 * x + y, (x, y, 1 - c), chips, [2 * cx + cy for cx, cy in chips]


def _view(minor, ref4, layers, k):
    if minor:
        cs = ref4.shape[3] // N_CHIPS
        return ref4.at[layers, 0, :, pl.ds(pl.multiple_of(k * cs, LANE), cs)]
    return ref4.at[layers, k]


def _remote(src, dst, ssem, rsem, dev):
    return pltpu.make_async_remote_copy(src_ref=src, dst_ref=dst, send_sem=ssem, recv_sem=rsem,
                                        device_id=dev, device_id_type=MESH)


def gather_weights(shards, minor):
    n = len(shards)
    L = shards[0].shape[0]
    Lh = L // 2
    out_shape = []
    for a, mn in zip(shards, minor):
        _, r, cs = a.shape
        out_shape.append(jax.ShapeDtypeStruct((L, 1, r, N_CHIPS * cs) if mn else (L, N_CHIPS, r, cs), a.dtype))

    def body(*refs):
        w, g = refs[:n], refs[n:2 * n]
        send, recv, loc = refs[2 * n:]
        (x, y, c), me, sib, chips, cidx = _where()
        mine, other, every = pl.ds(c * Lh, Lh), pl.ds((1 - c) * Lh, Lh), pl.ds(0, L)
        dst = lambda i, layers, k: _view(minor[i], g[i], layers, k)
        local = [pltpu.make_async_copy(w[i], dst(i, every, me), loc.at[i]) for i in range(n)]
        first = [_remote(w[i].at[mine], dst(i, mine, me), send.at[i, j], recv.at[i, j], (*chips[j], c))
                 for i in range(n) for j in range(3)]
        for cp in local + first:
            cp.start()
        passed = []
        for i in range(n):
            for j in range(3):
                blk = dst(i, mine, cidx[j])
                _remote(blk, blk, send.at[i, j], recv.at[i, j], (*chips[j], c)).wait_recv()
                fwd = _remote(blk, blk, send.at[i, 3 + j], recv.at[i, 3 + j], sib)
                fwd.start()
                passed.append(fwd)
        for i in range(n):
            for j in range(3):
                blk = dst(i, other, cidx[j])
                _remote(blk, blk, send.at[i, 3 + j], recv.at[i, 3 + j], sib).wait_recv()
        for cp in first + passed:
            cp.wait_send()
        for cp in local:
            cp.wait()

    return pl.pallas_call(
        body, name="gather_weights", in_specs=[ANY] * n, out_specs=[ANY] * n, out_shape=out_shape,
        scratch_shapes=[pltpu.SemaphoreType.DMA((n, 6)), pltpu.SemaphoreType.DMA((n, 6)),
                        pltpu.SemaphoreType.DMA((n,))],
    )(*shards)


def sibling_halves(grads):
    n = len(grads)
    L = grads[0].shape[0]
    Lh = L // 2
    half = [jax.ShapeDtypeStruct((Lh,) + a.shape[1:], a.dtype) for a in grads]

    def body(*refs):
        g, own, got = refs[:n], refs[n:2 * n], refs[2 * n:3 * n]
        send, recv, loc = refs[3 * n:]
        (x, y, c), me, sib, chips, cidx = _where()
        mine, other = pl.ds(c * Lh, Lh), pl.ds((1 - c) * Lh, Lh)
        local = [pltpu.make_async_copy(g[i].at[mine], own[i], loc.at[i]) for i in range(n)]
        out = [_remote(g[i].at[other], got[i], send.at[i], recv.at[i], sib) for i in range(n)]
        for cp in local + out:
            cp.start()
        for cp in out:
            cp.wait()
        for cp in local:
            cp.wait()

    res = pl.pallas_call(
        body, name="grad_sibling_halves", in_specs=[ANY] * n, out_specs=[ANY] * (2 * n), out_shape=half + half,
        scratch_shapes=[pltpu.SemaphoreType.DMA((n,)), pltpu.SemaphoreType.DMA((n,)), pltpu.SemaphoreType.DMA((n,))],
    )(*grads)
    return res[:n], res[n:]


def scatter_chip_sums(sums, travel, minor):
    n = len(sums)
    Lh = sums[0].shape[0]
    mine_shape, got_shape = [], []
    for a, t, mn in zip(sums, travel, minor):
        r, cs = a.shape[2], (a.shape[3] // N_CHIPS if mn else a.shape[3])
        mine_shape.append(jax.ShapeDtypeStruct((Lh, r, cs), a.dtype))
        got_shape.append(jax.ShapeDtypeStruct((3, Lh, r, cs), t.dtype))

    def body(*refs):
        s32, s16, mine, got = refs[:n], refs[n:2 * n], refs[2 * n:3 * n], refs[3 * n:4 * n]
        send, recv, loc = refs[4 * n:]
        (x, y, c), me, sib, chips, cidx = _where()
        every = pl.ds(0, Lh)
        local = [pltpu.make_async_copy(_view(minor[i], s32[i], every, me), mine[i], loc.at[i]) for i in range(n)]
        out = [_remote(_view(minor[i], s16[i], every, cidx[j]), got[i].at[j], send.at[i, j], recv.at[i, j],
                       (*chips[j], c)) for i in range(n) for j in range(3)]
        for cp in local + out:
            cp.start()
        for cp in out:
            cp.wait()
        for cp in local:
            cp.wait()

    res = pl.pallas_call(
        body, name="grad_scatter", in_specs=[ANY] * (2 * n), out_specs=[ANY] * (2 * n),
        out_shape=mine_shape + got_shape,
        scratch_shapes=[pltpu.SemaphoreType.DMA((n, 3)), pltpu.SemaphoreType.DMA((n, 3)),
                        pltpu.SemaphoreType.DMA((n,))],
    )(*sums, *travel)
    return res[:n], res[n:]


def share_halves(halves):
    n = len(halves)
    Lh = halves[0].shape[0]
    out_shape = [jax.ShapeDtypeStruct((2 * Lh,) + a.shape[1:], a.dtype) for a in halves]

    def body(*refs):
        h, full = refs[:n], refs[n:2 * n]
        send, recv, loc = refs[2 * n:]
        (x, y, c), me, sib, chips, cidx = _where()
        mine, other = pl.ds(c * Lh, Lh), pl.ds((1 - c) * Lh, Lh)
        local = [pltpu.make_async_copy(h[i], full[i].at[mine], loc.at[i]) for i in range(n)]
        out = [_remote(h[i], full[i].at[mine], send.at[i], recv.at[i], sib) for i in range(n)]
        for cp in local + out:
            cp.start()
        for i in range(n):
            out[i].wait_send()
            _remote(h[i], full[i].at[other], send.at[i], recv.at[i], sib).wait_recv()
        for cp in local:
            cp.wait()

    return pl.pallas_call(
        body, name="grad_share_halves", in_specs=[ANY] * n, out_specs=[ANY] * n, out_shape=out_shape,
        scratch_shapes=[pltpu.SemaphoreType.DMA((n,)), pltpu.SemaphoreType.DMA((n,)), pltpu.SemaphoreType.DMA((n,))],
    )(*halves)


N_DEV = 8


def allreduce_small(v):
    R = v.shape[0]

    def body(x_ref, sum_ref, all_ref, send, recv, loc):
        (x, y, c), me, sib, chips, cidx = _where()

        def rows(px, py, pc):
            return all_ref.at[4 * px + 2 * py + pc]

        def copy(k, block, to, src=None):
            return _remote(rows(*block) if src is None else src, rows(*block), send.at[k], recv.at[k], to)

        mine = pltpu.make_async_copy(x_ref, rows(x, y, c), loc)
        mine.start()
        first = [copy(0, (x, y, c), sib, src=x_ref)]
        first += [copy(1 + j, (x, y, c), (*chip, c), src=x_ref) for j, chip in enumerate(chips)]
        for cp in first:
            cp.start()
        passed = [copy(4 + j, (*chip, c), sib) for j, chip in enumerate(chips)]
        for j, chip in enumerate(chips):
            copy(1 + j, (*chip, c), (x, y, c)).wait_recv()
            passed[j].start()
        copy(0, (x, y, 1 - c), (x, y, c)).wait_recv()
        for j, chip in enumerate(chips):
            copy(4 + j, (*chip, 1 - c), (x, y, c)).wait_recv()
        for cp in first + passed:
            cp.wait_send()
        mine.wait()
        acc = all_ref[0]
        for d in range(1, N_DEV):
            acc = acc + all_ref[d]
        sum_ref[...] = acc

    vm = pl.BlockSpec(memory_space=pltpu.VMEM)
    return pl.pallas_call(
        body, name="allreduce_small", in_specs=[vm], out_specs=[vm, vm],
        out_shape=[jax.ShapeDtypeStruct((R, LANE), F32), jax.ShapeDtypeStruct((N_DEV, R, LANE), F32)],
        scratch_shapes=[pltpu.SemaphoreType.DMA((7,)), pltpu.SemaphoreType.DMA((7,)), pltpu.SemaphoreType.DMA],
        compiler_params=pltpu.CompilerParams(vmem_limit_bytes=VMEM_LIMIT),
    )(v)[0]


EW_BLOCK_BYTES = 2 << 20


def _ew_rows(rows, cols):
    for tr in (1024, 512, 256, 128, 64, 32, 16, 8):
        if rows % tr == 0 and tr * cols * 4 <= EW_BLOCK_BYTES:
            return tr
    return rows


def add_pair(name, a, b, travel_dtype):
    R, C = a.shape
    tr = _ew_rows(R, C)

    def body(a_ref, b_ref, s_ref, t_ref):
        s = a_ref[...] + b_ref[...]
        s_ref[...] = s
        t_ref[...] = s.astype(t_ref.dtype)

    blk = pl.BlockSpec((tr, C), lambda i: (i, 0))
    return pl.pallas_call(
        body, name=name, grid=(R // tr,), in_specs=[blk, blk], out_specs=[blk, blk],
        out_shape=[jax.ShapeDtypeStruct((R, C), F32), jax.ShapeDtypeStruct((R, C), travel_dtype)],
        compiler_params=_cp("parallel"),
    )(a, b)


def add_four(name, mine, got):
    R, C = mine.shape
    tr = _ew_rows(R, C)

    def body(m_ref, g0, g1, g2, o_ref):
        o_ref[...] = ((m_ref[...] + g0[0].astype(F32)) + g1[0].astype(F32)) + g2[0].astype(F32)

    blk = pl.BlockSpec((tr, C), lambda i: (i, 0))
    slot = lambda j: pl.BlockSpec((1, tr, C), lambda i: (j, i, 0))
    return pl.pallas_call(
        body, name=name, grid=(R // tr,), in_specs=[blk, slot(0), slot(1), slot(2)], out_specs=blk,
        out_shape=jax.ShapeDtypeStruct((R, C), F32), compiler_params=_cp("parallel"),
    )(mine, got, got, got)


def reduce_scatter_grads(full_grads):
    names = list(SHARDED)
    minor = [nm in MINOR for nm in names]
    g4 = [full4(nm, full_grads[nm]) for nm in names]
    own, got = sibling_halves(g4)
    sums, travel = [], []
    for nm, a, b in zip(names, own, got):
        s, t = add_pair("chip_sum_" + nm, a.reshape(-1, a.shape[-1]), b.reshape(-1, b.shape[-1]),
                        F32 if nm in F32_TRAVEL else BF16)
        sums.append(s.reshape(a.shape))
        travel.append(t.reshape(a.shape))
    mine, arrived = scatter_chip_sums(sums, travel, minor)
    halves = []
    for nm, a, b in zip(names, mine, arrived):
        f = add_four("shard_sum_" + nm, a.reshape(-1, a.shape[-1]), b.reshape(3, -1, b.shape[-1]))
        halves.append(f.reshape(a.shape))
    return dict(zip(names, share_halves(halves)))


def pack_small(tree, extra=None):
    parts = [tree[nm].reshape(-1) for nm in SMALL]
    parts.append(jnp.zeros((1,), F32) if extra is None else extra.reshape(-1))
    flat = jnp.concatenate(parts)
    rows = -(-flat.shape[0] // (8 * LANE)) * 8
    return jnp.pad(flat, (0, rows * LANE - flat.shape[0])).reshape(rows, LANE)


def unpack_small(packed, like):
    flat = packed.reshape(-1)
    out, at = {}, 0
    for nm in SMALL:
        size = math.prod(like[nm].shape)
        out[nm] = flat[at:at + size].reshape(like[nm].shape)
        at += size
    return out, flat[at]


def kernel(x, positions, attn_norm_g, w_in, q_lat_norm_g, w_uq, kv_lat_norm_g, w_ukv, mla_q_norm_g, mla_k_norm_g, fox_q_norm_g, fox_k_norm_g, fox_f_bias, s5_lambda_re, s5_lambda_im, s5_b_re, s5_b_im, s5_c_re, s5_c_im, s5_d, s5_log_step, s5_w_glu, s5_b_glu, w_branch, w_out, ffn_norm_g, w_up, ffn_conv_w, w_down, loss_target, m_attn_norm_g, m_w_in, m_q_lat_norm_g, m_w_uq, m_kv_lat_norm_g, m_w_ukv, m_mla_q_norm_g, m_mla_k_norm_g, m_fox_q_norm_g, m_fox_k_norm_g, m_fox_f_bias, m_s5_lambda_re, m_s5_lambda_im, m_s5_b_re, m_s5_b_im, m_s5_c_re, m_s5_c_im, m_s5_d, m_s5_log_step, m_s5_w_glu, m_s5_b_glu, m_w_branch, m_w_out, m_ffn_norm_g, m_w_up, m_ffn_conv_w, m_w_down, v_attn_norm_g, v_w_in, v_q_lat_norm_g, v_w_uq, v_kv_lat_norm_g, v_w_ukv, v_mla_q_norm_g, v_mla_k_norm_g, v_fox_q_norm_g, v_fox_k_norm_g, v_fox_f_bias, v_s5_lambda_re, v_s5_lambda_im, v_s5_b_re, v_s5_b_im, v_s5_c_re, v_s5_c_im, v_s5_d, v_s5_log_step, v_s5_w_glu, v_s5_b_glu, v_w_branch, v_w_out, v_ffn_norm_g, v_w_up, v_ffn_conv_w, v_w_down):
    given = dict(locals())
    w = {nm: given[nm] for nm in WEIGHTS}
    m = {nm: given["m_" + nm] for nm in WEIGHTS}
    v = {nm: given["v_" + nm] for nm in WEIGHTS}
    D = x.shape[-1]

    minor = [nm in MINOR for nm in SHARDED]
    shards = [travel3(nm, w[nm]).astype(F32 if nm in F32_TRAVEL else BF16) for nm in SHARDED]
    gathered = gather_weights(shards, minor)
    full = dict(w)
    for nm, g4 in zip(SHARDED, gathered):
        tail = list(w[nm].shape[1:])
        axis = (len(tail) - 1) if nm in MINOR or nm == "w_in" else 0
        tail[axis] *= N_CHIPS
        full[nm] = from_full4(nm, g4, tail)

    sq, grad_x, gw = local_step(x, positions, loss_target, full)

    big = reduce_scatter_grads(gw)
    total, sq_sum = unpack_small(allreduce_small(pack_small(gw, sq[0:1, 0:1])), w)
    loss = 0.5 * sq_sum / D

    grads, delta, new_m, new_v = {}, {}, {}, {}
    for nm in SHARDED:
        g = big[nm].reshape(shard3(nm, w[nm]).shape)
        two = lambda a: shard3(nm, a).reshape(-1, g.shape[-1])
        d2, m2, v2 = adamw("adamw_" + nm, two(w[nm]), g.reshape(-1, g.shape[-1]), two(m[nm]), two(v[nm]))
        grads[nm] = g.reshape(w[nm].shape)
        delta[nm], new_m[nm], new_v[nm] = (a.reshape(w[nm].shape) for a in (d2, m2, v2))
    d2, m2, v2 = adamw("adamw_small", pack_small(w), pack_small(total), pack_small(m), pack_small(v))
    for tree, packed in ((delta, d2), (new_m, m2), (new_v, v2)):
        tree.update(unpack_small(packed, w)[0])
    grads.update(total)
    return (loss, grad_x, *[grads[nm] for nm in WEIGHTS], *[delta[nm] for nm in WEIGHTS],
            *[new_m[nm] for nm in WEIGHTS], *[new_v[nm] for nm in WEIGHTS])
```

```python
import functools
import math

import jax
import jax.numpy as jnp
from jax import lax
from jax.experimental import pallas as pl
from jax.experimental.pallas import tpu as pltpu

F32, BF16 = jnp.float32, jnp.bfloat16
NORM_EPS = 1e-6
NEG_INF = -1e30
ROPE_THETA = 10000.0
LANE = 128
N_HEADS = 4
NOPE, ROPE, QK_DIM, V_DIM = 64, 32, 96, 64
Q_RANK, KV_RANK = 384, 256
FOX_DIM = 64
S5_G, S5_H, S5_P = 16, 16, 64
S5_C = S5_G * S5_P
BW = 256
VMEM_LIMIT = 56 << 20
ADAM_LR, ADAM_B1, ADAM_B2, ADAM_EPS, ADAM_WD, ADAM_STEP = 0.001, 0.9, 0.999, 1e-08, 0.01, 10


def _pick(n, cands):
    for c in cands:
        if n % c == 0:
            return c
    return n


def _cp(*sem):
    return pltpu.CompilerParams(dimension_semantics=sem, vmem_limit_bytes=VMEM_LIMIT)


def _dg(a, b, ca, cb):
    return lax.dot_general(a.astype(BF16), b.astype(BF16), (((ca,), (cb,)), ((), ())),
                           preferred_element_type=F32)


@jax.custom_vjp
def dot_nn(a, b):
    return _dg(a, b, 1, 0)


@jax.custom_vjp
def dot_nt(a, b):
    return _dg(a, b, 1, 1)


@jax.custom_vjp
def dot_tn(a, b):
    return _dg(a, b, 0, 0)


dot_nn.defvjp(lambda a, b: (dot_nn(a, b), (a, b)),
              lambda r, g: (dot_nt(g, r[1]).astype(r[0].dtype), dot_tn(r[0], g).astype(r[1].dtype)))
dot_nt.defvjp(lambda a, b: (dot_nt(a, b), (a, b)),
              lambda r, g: (dot_nn(g, r[1]).astype(r[0].dtype), dot_tn(g, r[0]).astype(r[1].dtype)))
dot_tn.defvjp(lambda a, b: (dot_tn(a, b), (a, b)),
              lambda r, g: (dot_nt(r[1], g).astype(r[0].dtype), dot_nn(r[0], g).astype(r[1].dtype)))


def xdot(a, b):
    return jnp.dot(a, b, precision=lax.Precision.HIGHEST, preferred_element_type=F32)


MM_VMEM_BUDGET = 36 << 20
MM_STEP_S, MM_HBM_BPS = 0.4e-6, 2.5e12


def _divisors(n, cands):
    return [c for c in cands if n % c == 0] or [n]


def _mm_tiles(M, K, N, ab, bb, ob, addb):
    best = None
    for tm in _divisors(M, (2048, 1024, 512, 256, 128)):
        for tn in _divisors(N, (2048, 1664, 1536, 1408, 1280, 1024, 768, 640, 512, 384, 256, 128)):
            for tk in _divisors(K, (4096, 2048, 1664, 1536, 1408, 1024, 768, 512, 384, 256, 128)):
                vmem = 2 * (tm * tk * ab + tk * tn * bb + tm * tn * (ob + addb)) + (tm * tn * 4 if tk != K else 0)
                if vmem > MM_VMEM_BUDGET:
                    continue
                steps = (M // tm) * (N // tn) * (K // tk)
                traffic = M * K * ab * (N // tn) + K * N * bb * (M // tm) + M * N * (ob + addb)
                cost = steps * MM_STEP_S + traffic / MM_HBM_BPS
                if best is None or cost < best[0]:
                    best = (cost, tm, tn, tk)
    assert best is not None, (M, K, N)
    return best[1:]


def mm(a, b, mode, *, name, add=None, out_dtype=F32, layer=None):
    bk, bn = b.shape[-2:]
    if mode == "nn":
        (M, K), N = a.shape, bn
    else:
        (K, M), N = a.shape, bn
    assert bk == K, (name, a.shape, b.shape)
    isz = lambda x: jnp.dtype(x.dtype).itemsize
    tm, tn, tk = _mm_tiles(M, K, N, isz(a), isz(b), jnp.dtype(out_dtype).itemsize, 0 if add is None else isz(add))
    nk = K // tk
    ca = 1 if mode == "nn" else 0

    def body(*refs):
        a_ref, b_ref = refs[:2]
        add_ref = refs[2] if add is not None else None
        o_ref = refs[3 if add is not None else 2]

        def finish(r):
            if add is not None:
                r = r + add_ref[...].astype(F32)
            o_ref[...] = r.astype(out_dtype)

        part = _dg(a_ref[...], b_ref[...], ca, 0)
        if nk == 1:
            finish(part)
            return
        acc = refs[-1]
        kk = pl.program_id(2)

        @pl.when(kk == 0)
        def _():
            acc[...] = part

        @pl.when(kk > 0)
        def _():
            acc[...] += part

        @pl.when(kk == nk - 1)
        def _():
            finish(acc[...])

    a_spec = (pl.BlockSpec((tm, tk), lambda i, j, k: (i, k)) if mode == "nn"
              else pl.BlockSpec((tk, tm), lambda i, j, k: (k, i)))
    b_spec = (pl.BlockSpec((tk, tn), lambda i, j, k: (k, j)) if layer is None
              else pl.BlockSpec((None, tk, tn), lambda i, j, k: (layer, k, j)))
    in_specs, args = [a_spec, b_spec], [a, b]
    if add is not None:
        in_specs.append(pl.BlockSpec((tm, tn), lambda i, j, k: (i, j)))
        args.append(add)
    return pl.pallas_call(
        body, name=name, grid=(M // tm, N // tn, nk),
        in_specs=in_specs, out_specs=pl.BlockSpec((tm, tn), lambda i, j, k: (i, j)),
        out_shape=jax.ShapeDtypeStruct((M, N), out_dtype),
        scratch_shapes=[pltpu.VMEM((tm, tn), F32)] if nk > 1 else [],
        compiler_params=_cp("parallel", "parallel", "arbitrary"),
    )(*args)


def _row_spec(tm, width, col):
    if callable(col):
        return pl.BlockSpec((tm, width), lambda i, j: (i, col(j)))
    return pl.BlockSpec((tm, width), lambda i, j: (i, col))


def _const_spec(c):
    return pl.BlockSpec(c.shape, lambda i, j: (0,) * c.ndim)


def rowwise(name, fn, rows, consts, outs, *, tm=256, nj=1):
    M = rows[0][0].shape[0]
    tm = _pick(M, (tm, 128, 64, 32, 16, 8))
    nr, nc = len(rows), len(consts)

    def body(*refs):
        vals = [r[...].astype(F32) for r in refs[:nr]] + [r[...] for r in refs[nr:nr + nc]]
        res = fn(*vals)
        for o_ref, r in zip(refs[nr + nc:], res):
            o_ref[...] = r.astype(o_ref.dtype)

    return pl.pallas_call(
        body, name=name, grid=(M // tm, nj),
        in_specs=[_row_spec(tm, w, c) for _, w, c in rows] + [_const_spec(c) for c in consts],
        out_specs=[pl.BlockSpec((tm, w), lambda i, j: (i, j)) for w, _ in outs],
        out_shape=[jax.ShapeDtypeStruct((M, nj * w), dt) for w, dt in outs],
        compiler_params=_cp("parallel", "parallel"),
    )(*[r[0] for r in rows], *consts)


def rowwise_vjp(name, fn, rows, consts, cts, diff, *, tm=256, nj=1, gdt=None):
    M = rows[0][0].shape[0]
    tm = _pick(M, (tm, 128, 64, 32, 16, 8))
    nr, nc, nt, nd = len(rows), len(consts), len(cts), len(diff)
    gdt = [F32] * nd if gdt is None else gdt

    def body(*refs):
        vals = [r[...].astype(F32) for r in refs[:nr + nc + nt]]
        rv, cv, tv = vals[:nr], vals[nr:nr + nc], vals[nr + nc:]
        grow, gconst = refs[nr + nc + nt:nr + nc + nt + nd], refs[nr + nc + nt + nd:]

        def f(*dargs):
            full = list(rv)
            for pos, val in zip(diff, dargs[:nd]):
                full[pos] = val
            return tuple(fn(*full, *dargs[nd:]))

        _, vjp = jax.vjp(f, *[rv[p] for p in diff], *cv)
        g = vjp(tuple(tv))
        for o_ref, gv in zip(grow, g[:nd]):
            o_ref[...] = gv.astype(o_ref.dtype)
        first = jnp.logical_and(pl.program_id(0) == 0, pl.program_id(1) == 0)
        for o_ref, gv in zip(gconst, g[nd:]):
            @pl.when(first)
            def _(o_ref=o_ref, gv=gv):
                o_ref[...] = gv

            @pl.when(jnp.logical_not(first))
            def _(o_ref=o_ref, gv=gv):
                o_ref[...] += gv

    out_specs = ([pl.BlockSpec((tm, rows[p][1]), lambda i, j: (i, j)) for p in diff]
                 + [_const_spec(c) for c in consts])
    out_shape = ([jax.ShapeDtypeStruct((M, nj * rows[p][1]), dt) for p, dt in zip(diff, gdt)]
                 + [jax.ShapeDtypeStruct(c.shape, F32) for c in consts])
    res = pl.pallas_call(
        body, name=name, grid=(M // tm, nj),
        in_specs=([_row_spec(tm, w, c) for _, w, c in rows] + [_const_spec(c) for c in consts]
                  + [_row_spec(tm, w, c) for _, w, c in cts]),
        out_specs=out_specs, out_shape=out_shape,
        compiler_params=_cp("arbitrary", "arbitrary"),
    )(*[r[0] for r in rows], *consts, *[t[0] for t in cts])
    return res[:nd], res[nd:]


def _lane(shape=(1, LANE)):
    return lax.broadcasted_iota(jnp.int32, shape, len(shape) - 1)


def _sigmoid(x):
    return 1.0 / (1.0 + jnp.exp(-x))


def _rms(x, g, n):
    return x * lax.rsqrt(jnp.sum(x * x, axis=-1, keepdims=True) * (1.0 / n) + NORM_EPS) * g


def fn_rms(n):
    return lambda x, g: (_rms(x, g, n),)


def _rope(x, cos_t, sin_t):
    i = lax.broadcasted_iota(jnp.int32, (LANE, LANE), 0)
    j = lax.broadcasted_iota(jnp.int32, (LANE, LANE), 1)
    half = ROPE // 2
    lo = jnp.logical_and(jnp.logical_and(j >= NOPE, j < NOPE + half), i == j + half)
    hi = jnp.logical_and(jnp.logical_and(j >= NOPE + half, j < NOPE + ROPE), i == j - half)
    perm = jnp.where(hi, 1.0, 0.0) - jnp.where(lo, 1.0, 0.0)
    return x * cos_t + xdot(x, perm) * sin_t


def fn_qpost(q, cos_t, sin_t, g):
    return (_rope(_rms(q, g, QK_DIM), cos_t, sin_t),)


def fn_kpost(kn, small, cos_t, sin_t, g):
    lane = _lane()
    rope_lanes = jnp.logical_and(lane >= NOPE, lane < NOPE + ROPE)
    kc = kn + jnp.where(rope_lanes, small, 0.0)
    return (_rope(_rms(kc, g, QK_DIM), cos_t, sin_t),)


def fn_foxnorm(x, g):
    first = _lane() < FOX_DIM
    sq = x * x
    s0 = jnp.sum(jnp.where(first, sq, 0.0), axis=-1, keepdims=True)
    s1 = jnp.sum(jnp.where(first, 0.0, sq), axis=-1, keepdims=True)
    r0 = lax.rsqrt(s0 * (1.0 / FOX_DIM) + NORM_EPS)
    r1 = lax.rsqrt(s1 * (1.0 / FOX_DIM) + NORM_EPS)
    return (x * jnp.where(first, r0, r1) * g,)


F_LANE0 = NOPE + ROPE


def fn_fgate(small, bias):
    z = small + bias
    lf = jnp.minimum(z, 0.0) - jnp.log(1.0 + jnp.exp(-jnp.abs(z)))
    lane = _lane()
    return (jnp.where(jnp.logical_and(lane >= F_LANE0, lane < F_LANE0 + N_HEADS), lf, 0.0),)


def _gelu(y):
    return 0.5 * y * (1.0 + jnp.tanh(math.sqrt(2.0 / math.pi) * (y + 0.044715 * (y * y * y))))


def fn_s5post(ypre, u, d, wglu, bglu):
    y = _gelu(ypre + d * u)
    return (y * _sigmoid(dot_nn(y, wglu) + bglu),)


def fn_merge(om, of, os_, g0, g1, g2, wb0, wb1, wb2):
    return (_sigmoid(g0) * dot_nn(om, wb0) + _sigmoid(g1) * dot_nn(of, wb1)
            + _sigmoid(g2) * dot_nn(os_, wb2),)


def fn_add5(a, b):
    return (a[:, 0:LANE] + a[:, LANE:2 * LANE] + a[:, 2 * LANE:3 * LANE] + a[:, 3 * LANE:4 * LANE] + b,)


def fn_addt(a, b):
    return (a + b,)


def fn_s5params(lre, lim, lstep, btr, bti, ctr, cti):
    C = S5_C
    grp = lax.broadcasted_iota(jnp.int32, (LANE, C), 1) >> 6
    expand = jnp.where(lax.broadcasted_iota(jnp.int32, (LANE, C), 0) == grp, 1.0, 0.0)
    lane = _lane()
    st = jnp.where(lane < S5_G, jnp.exp(lstep), 0.0)
    step = jnp.sum(xdot(jnp.broadcast_to(st, (8, LANE)), expand), axis=0, keepdims=True) * 0.125
    zr, zi = lre * step, lim * step
    er = jnp.exp(zr)
    lbr, lbi = er * jnp.cos(zi), er * jnp.sin(zi)
    den = lre * lre + lim * lim
    nr = lbr - 1.0
    cfr = (nr * lre + lbi * lim) / den
    cfi = (lbi * lre - nr * lim) / den
    bbr = cfr * btr - cfi * bti
    bbi = cfr * bti + cfi * btr
    rg = lax.broadcasted_iota(jnp.int32, (BW, C), 0) >> 4
    cg = lax.broadcasted_iota(jnp.int32, (BW, C), 1) >> 6
    mb = jnp.where(rg == cg, 1.0, 0.0)
    b_re = jnp.concatenate([bbr] * S5_G, axis=0) * mb
    b_im = jnp.concatenate([bbi] * S5_G, axis=0) * mb
    ecol = jnp.where(lax.broadcasted_iota(jnp.int32, (LANE, BW), 0)
                     == (lax.broadcasted_iota(jnp.int32, (LANE, BW), 1) & 15), 1.0, 0.0)
    mc = jnp.where((lax.broadcasted_iota(jnp.int32, (C, BW), 0) >> 6)
                   == (lax.broadcasted_iota(jnp.int32, (C, BW), 1) >> 4), 1.0, 0.0)
    c_top = xdot(ctr, ecol) * mc
    c_bot = -(xdot(cti, ecol) * mc)
    return lbr, lbi, b_re, b_im, c_top, c_bot


def s5_params(p):
    def body(lre, lim, ls, btr, bti, ctr, cti, lb_ref, bre_ref, bim_ref, ct_ref, cb_ref):
        lbr, lbi, b_re, b_im, c_top, c_bot = fn_s5params(
            lre[...], lim[...], ls[...], btr[...], bti[...], ctr[...], cti[...])
        lb_ref[0:1, :] = lbr
        lb_ref[1:2, :] = lbi
        bre_ref[...] = b_re.astype(BF16)
        bim_ref[...] = b_im.astype(BF16)
        ct_ref[...] = c_top.astype(BF16)
        cb_ref[...] = c_bot.astype(BF16)

    return pl.pallas_call(
        body, name="s5_params",
        out_shape=[jax.ShapeDtypeStruct((2, S5_C), F32), jax.ShapeDtypeStruct((BW, S5_C), BF16),
                   jax.ShapeDtypeStruct((BW, S5_C), BF16), jax.ShapeDtypeStruct((S5_C, BW), BF16),
                   jax.ShapeDtypeStruct((S5_C, BW), BF16)],
        compiler_params=pltpu.CompilerParams(vmem_limit_bytes=VMEM_LIMIT),
    )(p["lre"], p["lim"], p["lstep"], p["btr"], p["bti"], p["ctr"], p["cti"])


S5_PARAM_NAMES = ("lre", "lim", "lstep", "btr", "bti", "ctr", "cti")


def s5_params_vjp(p, dl_r, dl_i, db_re, db_im, dc_top, dc_bot):
    def body(lre, lim, ls, btr, bti, ctr, cti, dlr, dli, dbr, dbi, dct, dcb, *outs):
        args = [r[...] for r in (lre, lim, ls, btr, bti, ctr, cti)]
        _, vjp = jax.vjp(fn_s5params, *args)
        g = vjp((jnp.sum(dlr[...], axis=0, keepdims=True), jnp.sum(dli[...], axis=0, keepdims=True),
                 dbr[...], dbi[...], dct[...], dcb[...]))
        for o_ref, gv in zip(outs, g):
            o_ref[...] = gv

    return pl.pallas_call(
        body, name="s5_params_vjp",
        out_shape=[jax.ShapeDtypeStruct(p[n].shape, F32) for n in S5_PARAM_NAMES],
        compiler_params=pltpu.CompilerParams(vmem_limit_bytes=VMEM_LIMIT),
    )(*[p[n] for n in S5_PARAM_NAMES], dl_r, dl_i, db_re, db_im, dc_top, dc_bot)


def _attn_tile(q, k, v, cq, ckt, *, hpt, dk, q0, tile):
    tq, S = q.shape[0], k.shape[0]
    row = q0 + lax.broadcasted_iota(jnp.int32, (tq, S), 0)
    col = lax.broadcasted_iota(jnp.int32, (tq, S), 1)
    causal = row >= col
    lane = _lane()
    out = jnp.zeros((tq, LANE), F32)
    for h in range(hpt):
        if hpt > 1:
            mine = (lane >> int(math.log2(LANE // hpt))) == h
            qh = jnp.where(mine, q, 0.0)
        else:
            qh = q
        s = dot_nt(qh, k) * (dk ** -0.5)
        if cq is not None:
            head = tile * hpt + h
            cqh = jnp.sum(jnp.where(lane == F_LANE0 + head, cq, 0.0), axis=1, keepdims=True)
            sub = lax.broadcasted_iota(jnp.int32, (8, 1), 0)
            ckh = jnp.sum(jnp.where(sub == head, ckt, 0.0), axis=0, keepdims=True)
            s = s + (cqh - ckh)
        s = jnp.where(causal, s, NEG_INF)
        m = lax.stop_gradient(jnp.max(s, axis=-1, keepdims=True))
        e = jnp.exp(s - m)
        p = e / jnp.sum(e, axis=-1, keepdims=True)
        oh = dot_nn(p, v)
        out = out + (jnp.where(mine, oh, 0.0) if hpt > 1 else oh)
    return out


def attention(name, q, k, v, *, B, S, ntile, hpt, dk, qc=0, kc=0, vc=0, cum=None, ckt=None, do=None, tq=256):
    tq = _pick(S, (tq, 128))
    nq = S // tq
    M = B * S
    bias = cum is not None
    kw = dict(hpt=hpt, dk=dk)

    def load(refs, sk):
        q_ref, k_ref, v_ref = refs[:3]
        qv, kv, vv = q_ref[...].astype(F32), k_ref[0:sk, :].astype(F32), v_ref[0:sk, :].astype(F32)
        if bias:
            return qv, kv, vv, refs[3][...], refs[4][0, :, 0:sk]
        return qv, kv, vv, None, None

    nin = 5 if bias else 3

    def per_query_block(run):
        for g in range(nq):
            @pl.when(pl.program_id(2) == g)
            def _(g=g):
                run(g, (g + 1) * tq)

    def fwd_body(*refs):
        tile = pl.program_id(1)

        def run(g, sk):
            qv, kv, vv, cq, ck = load(refs, sk)
            o = _attn_tile(qv, kv, vv, cq, ck, q0=g * tq, tile=tile, **kw)
            refs[nin][...] = o.astype(refs[nin].dtype)

        per_query_block(run)

    def bwd_body(*refs):
        outs = refs[nin + 1:]
        tile = pl.program_id(1)

        @pl.when(pl.program_id(2) == 0)
        def _():
            for o_ref in (outs[1], outs[2]) + ((outs[4],) if bias else ()):
                o_ref[...] = jnp.zeros_like(o_ref)

        def run(g, sk):
            qv, kv, vv, cq, ck = load(refs, sk)
            dov = refs[nin][...].astype(F32)
            if bias:
                f = lambda a, b, c, d, e: _attn_tile(a, b, c, d, e, q0=g * tq, tile=tile, **kw)
                _, vjp = jax.vjp(f, qv, kv, vv, cq, ck)
            else:
                f = lambda a, b, c: _attn_tile(a, b, c, None, None, q0=g * tq, tile=tile, **kw)
                _, vjp = jax.vjp(f, qv, kv, vv)
            gr = vjp(dov)
            outs[0][...] = gr[0]
            outs[1][0:sk, :] += gr[1]
            outs[2][0:sk, :] += gr[2]
            if bias:
                outs[3][...] = gr[3]
                outs[4][0, :, 0:sk] += gr[4]

        per_query_block(run)

    qspec = lambda c: pl.BlockSpec((tq, LANE), lambda b, t, i: (b * nq + i, c + t))
    kspec = lambda c: pl.BlockSpec((S, LANE), lambda b, t, i: (b, c + t))
    in_specs, args = [qspec(qc), kspec(kc), kspec(vc)], [q, k, v]
    if bias:
        in_specs += [pl.BlockSpec((tq, LANE), lambda b, t, i: (b * nq + i, 0)),
                     pl.BlockSpec((1, 8, S), lambda b, t, i: (b, 0, 0))]
        args += [cum, ckt]
    if do is None:
        return pl.pallas_call(
            fwd_body, name=name, grid=(B, ntile, nq), in_specs=in_specs, out_specs=qspec(0),
            out_shape=jax.ShapeDtypeStruct((M, ntile * LANE), BF16),
            compiler_params=_cp("parallel", "parallel", "parallel"),
        )(*args)
    in_specs.append(qspec(0))
    args.append(do)
    out_specs = [qspec(0), kspec(0), kspec(0)]
    out_shape = [jax.ShapeDtypeStruct((M, ntile * LANE), F32)] * 3
    if bias:
        out_specs += [qspec(0), pl.BlockSpec((1, 8, S), lambda b, t, i: (b * ntile + t, 0, 0))]
        out_shape += [jax.ShapeDtypeStruct((M, ntile * LANE), F32),
                      jax.ShapeDtypeStruct((B * ntile, 8, S), F32)]
    return pl.pallas_call(
        bwd_body, name=name, grid=(B, ntile, nq), in_specs=in_specs, out_specs=out_specs,
        out_shape=out_shape, compiler_params=_cp("parallel", "parallel", "arbitrary"),
    )(*args)


def seq_cumsum(name, x, *, B, S, reverse):
    nb = S // LANE

    def body(x_ref, o_ref):
        r = lax.broadcasted_iota(jnp.int32, (LANE, LANE), 0)
        c = lax.broadcasted_iota(jnp.int32, (LANE, LANE), 1)
        tri = jnp.where((r <= c) if reverse else (r >= c), 1.0, 0.0)
        carry = jnp.zeros((1, LANE), F32)
        for blk in (range(nb - 1, -1, -1) if reverse else range(nb)):
            xb = x_ref[blk * LANE:(blk + 1) * LANE, :]
            o_ref[blk * LANE:(blk + 1) * LANE, :] = xdot(tri, xb) + carry
            carry = carry + jnp.sum(xb, axis=0, keepdims=True)

    return pl.pallas_call(
        body, name=name, grid=(B,), in_specs=[pl.BlockSpec((S, LANE), lambda b: (b, 0))],
        out_specs=pl.BlockSpec((S, LANE), lambda b: (b, 0)),
        out_shape=jax.ShapeDtypeStruct(x.shape, F32), compiler_params=_cp("parallel"),
    )(x)


SCAN_ROWS = 64


def _shift_rows(ref, r0, rows, d, up):
    if d % 8 == 0:
        return ref[pl.ds(r0 + d if up else r0 - d, rows), :]
    if up:
        win = ref[pl.ds(r0, rows + 8), :]
        return pltpu.roll(win, rows + 8 - d, 0)[0:rows, :]
    win = ref[pl.ds(r0 - 8, rows + 8), :]
    return pltpu.roll(win, d, 0)[8:rows + 8, :]


def s5_scan(name, x_re, x_im, lam, *, B, S, reverse, state=None):
    C = S5_C
    cw = LANE
    R = _pick(S, (SCAN_ROWS,))
    pad = max(S // 2, 8)
    nsteps = int(math.log2(S))
    assert 1 << nsteps == S
    base = 0 if reverse else pad
    with_grad = state is not None

    def body(*refs):
        if with_grad:
            xr, xi, lam_ref, sr, si, o_r, o_i, dl_r, dl_i, a_r, a_i, b_r, b_i = refs
        else:
            xr, xi, lam_ref, o_r, o_i, a_r, a_i, b_r, b_i = refs
        zero = jnp.zeros((pad, cw), F32)
        z0 = S if reverse else 0
        for buf in (a_r, a_i, b_r, b_i):
            buf[z0:z0 + pad, :] = zero
        a_r[base:base + S, :] = xr[...]
        a_i[base:base + S, :] = xi[...]
        mr = lam_ref[0:1, :]
        mi = -lam_ref[1:2, :] if reverse else lam_ref[1:2, :]
        src, dst = (a_r, a_i), (b_r, b_i)
        for step in range(nsteps):
            d = 1 << step
            last = step == nsteps - 1

            def chunk(c, _, src=src, dst=dst, d=d, last=last, mr=mr, mi=mi):
                r0 = pl.multiple_of(base + c * R, 8)
                pr = _shift_rows(src[0], r0, R, d, reverse)
                pi = _shift_rows(src[1], r0, R, d, reverse)
                nr = src[0][pl.ds(r0, R), :] + mr * pr - mi * pi
                ni = src[1][pl.ds(r0, R), :] + mr * pi + mi * pr
                if last:
                    o0 = pl.multiple_of(c * R, 8)
                    o_r[pl.ds(o0, R), :] = nr
                    o_i[pl.ds(o0, R), :] = ni
                else:
                    dst[0][pl.ds(r0, R), :] = nr
                    dst[1][pl.ds(r0, R), :] = ni
                return 0

            lax.fori_loop(0, S // R, chunk, 0)
            mr, mi = mr * mr - mi * mi, 2.0 * mr * mi
            src, dst = dst, src
        if with_grad:
            def fold(v):
                return jnp.sum(v.reshape(R // 8, 8, cw), axis=0)

            def accum(c, carry, first=False):
                r0 = 0 if first else pl.multiple_of(c * R, 8)
                gr, gi = o_r[pl.ds(r0, R), :], o_i[pl.ds(r0, R), :]
                if first:
                    keep = lax.broadcasted_iota(jnp.int32, (R, 1), 0) >= 1
                    pr = jnp.where(keep, pltpu.roll(sr[0:R, :], 1, 0), 0.0)
                    pi = jnp.where(keep, pltpu.roll(si[0:R, :], 1, 0), 0.0)
                else:
                    pr = _shift_rows(sr, r0, R, 1, False)
                    pi = _shift_rows(si, r0, R, 1, False)
                return (carry[0] + fold(gr * pr + gi * pi), carry[1] + fold(gi * pr - gr * pi))

            acc = accum(0, (jnp.zeros((8, cw), F32), jnp.zeros((8, cw), F32)), first=True)
            acc = lax.fori_loop(1, S // R, accum, acc)
            dl_r[...] = acc[0]
            dl_i[...] = acc[1]

    seq = pl.BlockSpec((S, cw), lambda b, j: (b, j))
    in_specs = [seq, seq, pl.BlockSpec((2, cw), lambda b, j: (0, j))]
    args = [x_re, x_im, lam]
    out_specs = [seq, seq]
    out_shape = [jax.ShapeDtypeStruct(x_re.shape, F32)] * 2
    if with_grad:
        in_specs += [seq, seq]
        args += list(state)
        out_specs += [pl.BlockSpec((8, cw), lambda b, j: (b, j))] * 2
        out_shape += [jax.ShapeDtypeStruct((B * 8, C), F32)] * 2
    return pl.pallas_call(
        body, name=name, grid=(B, C // cw), in_specs=in_specs, out_specs=out_specs, out_shape=out_shape,
        scratch_shapes=[pltpu.VMEM((S + pad, cw), F32)] * 4,
        compiler_params=_cp("parallel", "parallel"),
    )(*args)


CONV_CW = 256


def _conv_taps(ref, r0, rows, first):
    cur = ref[pl.ds(r0, rows), :]
    if first:
        row = lax.broadcasted_iota(jnp.int32, (rows, 1), 0)
        p1 = jnp.where(row >= 1, pltpu.roll(cur, 1, 0), 0.0)
        p2 = jnp.where(row >= 2, pltpu.roll(cur, 2, 0), 0.0)
    else:
        p1 = _shift_rows(ref, r0, rows, 1, False)
        p2 = _shift_rows(ref, r0, rows, 2, False)
    return cur, p1, p2


def _conv_apply(w_ref, taps):
    return w_ref[2:3, :] * taps[0] + w_ref[1:2, :] * taps[1] + w_ref[0:1, :] * taps[2]


def conv_gate_fwd(up, conv_w, *, B, S):
    M, F2 = up.shape
    F = F2 // 2
    cw = _pick(F, (CONV_CW, LANE))
    nf = F // cw
    R = _pick(S, (SCAN_ROWS,))

    def body(g_ref, v_ref, wg_ref, wv_ref, o_ref):
        def chunk(c, _, first=False):
            r0 = 0 if first else pl.multiple_of(c * R, 8)
            cg = _conv_apply(wg_ref, _conv_taps(g_ref, r0, R, first))
            cv = _conv_apply(wv_ref, _conv_taps(v_ref, r0, R, first))
            o_ref[pl.ds(r0, R), :] = (cg * _sigmoid(cg) * cv).astype(o_ref.dtype)
            return 0

        chunk(0, 0, first=True)
        lax.fori_loop(1, S // R, chunk, 0)

    seq = lambda off: pl.BlockSpec((S, cw), lambda b, j: (b, off + j))
    wsp = lambda off: pl.BlockSpec((3, cw), lambda b, j: (0, off + j))
    return pl.pallas_call(
        body, name="conv_gate", grid=(B, nf), in_specs=[seq(0), seq(nf), wsp(0), wsp(nf)],
        out_specs=seq(0), out_shape=jax.ShapeDtypeStruct((M, F), BF16),
        compiler_params=_cp("parallel", "parallel"),
    )(up, up, conv_w, conv_w)


def conv_gate_bwd(up, conv_w, dact, *, B, S):
    M, F2 = up.shape
    F = F2 // 2
    cw = _pick(F, (CONV_CW, LANE))
    nf = F // cw
    R = _pick(S, (SCAN_ROWS,))
    nchunk = S // R

    def body(s_ref, p_ref, ws_ref, wp_ref, da_ref, du_ref, dw_ref, dc_ref):
        is_gate = pl.program_id(0) < nf
        dc_ref[S:S + 8, :] = jnp.zeros((8, cw), F32)

        def fold(v):
            return jnp.sum(v.reshape(R // 8, 8, cw), axis=0)

        def pass1(c, acc, first=False):
            r0 = 0 if first else pl.multiple_of(c * R, 8)
            taps = _conv_taps(s_ref, r0, R, first)
            cs = _conv_apply(ws_ref, taps)
            cp = _conv_apply(wp_ref, _conv_taps(p_ref, r0, R, first))
            da = da_ref[pl.ds(r0, R), :]
            sg = _sigmoid(cs)
            d_gate = da * cp * (sg * (1.0 + cs * (1.0 - sg)))
            sp = _sigmoid(cp)
            d_val = da * (cp * sp)
            dc = jnp.where(is_gate, d_gate, d_val)
            dc_ref[pl.ds(r0, R), :] = dc
            return (acc[0] + fold(dc * taps[2]), acc[1] + fold(dc * taps[1]), acc[2] + fold(dc * taps[0]))

        z = jnp.zeros((8, cw), F32)
        acc = pass1(0, (z, z, z), first=True)
        acc = lax.fori_loop(1, nchunk, pass1, acc)

        def pass2(c, _):
            r0 = pl.multiple_of(c * R, 16)
            n0 = dc_ref[pl.ds(r0, R), :]
            n1 = _shift_rows(dc_ref, r0, R, 1, True)
            n2 = _shift_rows(dc_ref, r0, R, 2, True)
            du = ws_ref[2:3, :] * n0 + ws_ref[1:2, :] * n1 + ws_ref[0:1, :] * n2
            du_ref[pl.ds(r0, R), :] = du.astype(du_ref.dtype)
            return 0

        lax.fori_loop(0, nchunk, pass2, 0)
        first_b = pl.program_id(1) == 0
        for tap in range(3):
            tot = jnp.sum(acc[tap], axis=0, keepdims=True)

            @pl.when(first_b)
            def _(tap=tap, tot=tot):
                dw_ref[tap:tap + 1, :] = tot

            @pl.when(jnp.logical_not(first_b))
            def _(tap=tap, tot=tot):
                dw_ref[tap:tap + 1, :] += tot

    n2 = 2 * nf
    seq = lambda f: pl.BlockSpec((S, cw), lambda j, b: (b, f(j)))
    wsp = lambda f: pl.BlockSpec((3, cw), lambda j, b: (0, f(j)))
    same, other, act_col = (lambda j: j), (lambda j: (j + nf) % n2), (lambda j: j % nf)
    return pl.pallas_call(
        body, name="conv_gate_vjp", grid=(n2, B),
        in_specs=[seq(same), seq(other), wsp(same), wsp(other), seq(act_col)],
        out_specs=[seq(same), wsp(same)],
        out_shape=[jax.ShapeDtypeStruct((M, F2), BF16), jax.ShapeDtypeStruct((3, F2), F32)],
        scratch_shapes=[pltpu.VMEM((S + 8, cw), F32)],
        compiler_params=_cp("parallel", "arbitrary"),
    )(up, up, conv_w, conv_w, dact)


def loss_head(y, target):
    M, D = y.shape
    tm = _pick(M, (256, 128, 64, 32, 16, 8))

    def body(y_ref, t_ref, dy_ref, l_ref):
        diff = y_ref[...] - t_ref[...]
        dy_ref[...] = diff * (1.0 / D)
        part = jnp.sum(jnp.sum(diff * diff, axis=1, keepdims=True), axis=0, keepdims=True)

        @pl.when(pl.program_id(0) == 0)
        def _():
            l_ref[...] = jnp.zeros_like(l_ref)

        l_ref[...] += part

    row = pl.BlockSpec((tm, D), lambda i: (i, 0))
    return pl.pallas_call(
        body, name="loss_head", grid=(M // tm,), in_specs=[row, row],
        out_specs=[row, pl.BlockSpec((8, LANE), lambda i: (0, 0))],
        out_shape=[jax.ShapeDtypeStruct((M, D), F32), jax.ShapeDtypeStruct((8, LANE), F32)],
        compiler_params=_cp("arbitrary"),
    )(y, target)


def adamw(name, w, g, m, v):
    R, C = w.shape
    tr = _pick(R, (256, 128, 64, 32, 16, 8))

    def body(w_ref, g_ref, m_ref, v_ref, d_ref, nm_ref, nv_ref):
        gv = g_ref[...]
        nm = ADAM_B1 * m_ref[...] + (1.0 - ADAM_B1) * gv
        nv = ADAM_B2 * v_ref[...] + (1.0 - ADAM_B2) * (gv * gv)
        m_hat = nm / (1.0 - ADAM_B1 ** ADAM_STEP)
        v_hat = nv / (1.0 - ADAM_B2 ** ADAM_STEP)
        d_ref[...] = -ADAM_LR * (m_hat / (jnp.sqrt(v_hat) + ADAM_EPS) + ADAM_WD * w_ref[...])
        nm_ref[...] = nm
        nv_ref[...] = nv

    blk = pl.BlockSpec((tr, C), lambda i: (i, 0))
    return pl.pallas_call(
        body, name=name, grid=(R // tr,), in_specs=[blk] * 4, out_specs=[blk] * 3,
        out_shape=[jax.ShapeDtypeStruct((R, C), F32)] * 3, compiler_params=_cp("parallel"),
    )(w, g, m, v)


def _seg(D):
    o = 3 * D
    return dict(ckv=o, fq=o + 256, fk=o + 512, fv=o + 768, u=o + 1024, small=o + 1280, cq=o + 1536, P=o + 1920)


def _pad_last(a, n):
    return jnp.pad(a, [(0, 0)] * (a.ndim - 1) + [(0, n - a.shape[-1])])


def _place(a, lo, n=LANE):
    return jnp.pad(a, [(0, 0)] * (a.ndim - 1) + [(lo, n - lo - a.shape[-1])])


def prep_weights(w):
    L, D = w["attn_norm_g"].shape
    win = w["w_in"]
    z = lambda n: jnp.zeros((L, D, n), win.dtype)
    g0 = 1700
    wp = jnp.concatenate(
        [win[..., g0:g0 + 3 * D], win[..., 384:640], win[..., 672:928], win[..., 928:1184], win[..., 1184:1440],
         win[..., 1444:1700], z(NOPE), win[..., 640:672], win[..., 1440:1444], z(LANE - F_LANE0 - N_HEADS), z(LANE),
         win[..., 0:384]], axis=-1)
    wukv, wb = w["w_ukv"], w["w_branch"]
    row3 = lambda a: a[:, None, :]
    return dict(
        g1=row3(w["attn_norm_g"]), Wp=wp, gql=row3(w["q_lat_norm_g"]), gkvl=row3(w["kv_lat_norm_g"]),
        Wuq=_pad_last(w["w_uq"], LANE).reshape(L, Q_RANK, N_HEADS * LANE),
        Wk=_pad_last(wukv[..., :NOPE], LANE).reshape(L, KV_RANK, N_HEADS * LANE),
        Wv=_pad_last(wukv[..., NOPE:], LANE).reshape(L, KV_RANK, N_HEADS * LANE),
        gq=row3(_pad_last(w["mla_q_norm_g"], LANE)), gk=row3(_pad_last(w["mla_k_norm_g"], LANE)),
        gfq=row3(jnp.tile(w["fox_q_norm_g"], (1, 2))), gfk=row3(jnp.tile(w["fox_k_norm_g"], (1, 2))),
        fbias=row3(_place(w["fox_f_bias"], F_LANE0)),
        lre=w["s5_lambda_re"].reshape(L, 1, S5_C), lim=w["s5_lambda_im"].reshape(L, 1, S5_C),
        lstep=row3(_pad_last(w["s5_log_step"], LANE)),
        btr=jnp.transpose(w["s5_b_re"], (0, 3, 1, 2)).reshape(L, S5_H, S5_C),
        bti=jnp.transpose(w["s5_b_im"], (0, 3, 1, 2)).reshape(L, S5_H, S5_C),
        ctr=_pad_last(jnp.transpose(w["s5_c_re"], (0, 1, 3, 2)).reshape(L, S5_C, S5_H), LANE),
        cti=_pad_last(jnp.transpose(w["s5_c_im"], (0, 1, 3, 2)).reshape(L, S5_C, S5_H), LANE),
        s5d=w["s5_d"].reshape(L, 1, BW), Wglu=w["s5_w_glu"], bglu=row3(w["s5_b_glu"]),
        Wb0=jnp.pad(wb[:, 0].reshape(L, N_HEADS, V_DIM, D), ((0, 0), (0, 0), (0, LANE - V_DIM), (0, 0))
                    ).reshape(L, N_HEADS * LANE, D),
        Wb1=wb[:, 1], Wb2=wb[:, 2], Wout=w["w_out"], g2=row3(w["ffn_norm_g"]), Wup=w["w_up"],
        convw=w["ffn_conv_w"], Wdown=w["w_down"],
    )


BIG_KEYS = ("Wp", "Wuq", "Wk", "Wv", "Wout", "Wup", "Wdown")


def with_transposes(P):
    out = dict(P)
    for k in BIG_KEYS:
        out[k + "T"] = jnp.swapaxes(P[k], 1, 2)
    return out


def layer_params(P, l):
    return {k: (v if k in BIG_KEYS or k[:-1] in BIG_KEYS else v[l]) for k, v in P.items()}


def unprep_grads(G, D):
    L = G["g1"].shape[0]
    s = _seg(D)
    dwp = G["Wp"]
    sm = s["small"]
    w_in = jnp.concatenate(
        [dwp[..., s["cq"]:s["cq"] + 384], dwp[..., s["ckv"]:s["ckv"] + 256], dwp[..., sm + NOPE:sm + NOPE + ROPE],
         dwp[..., s["fq"]:s["fq"] + 768], dwp[..., sm + F_LANE0:sm + F_LANE0 + N_HEADS],
         dwp[..., s["u"]:s["u"] + 256], dwp[..., 0:3 * D]], axis=-1)
    heads = lambda a, rows, keep: a.reshape(L, rows, N_HEADS, LANE)[..., :keep]
    wb0 = G["Wb0"].reshape(L, N_HEADS, LANE, D)[:, :, :V_DIM].reshape(L, BW, D)
    gf = lambda a: a[:, 0, :FOX_DIM] + a[:, 0, FOX_DIM:]
    return dict(
        attn_norm_g=G["g1"][:, 0], w_in=w_in, q_lat_norm_g=G["gql"][:, 0], w_uq=heads(G["Wuq"], Q_RANK, QK_DIM),
        kv_lat_norm_g=G["gkvl"][:, 0],
        w_ukv=jnp.concatenate([heads(G["Wk"], KV_RANK, NOPE), heads(G["Wv"], KV_RANK, V_DIM)], axis=-1),
        mla_q_norm_g=G["gq"][:, 0, :QK_DIM], mla_k_norm_g=G["gk"][:, 0, :QK_DIM],
        fox_q_norm_g=gf(G["gfq"]), fox_k_norm_g=gf(G["gfk"]),
        fox_f_bias=G["fbias"][:, 0, F_LANE0:F_LANE0 + N_HEADS],
        s5_lambda_re=G["lre"].reshape(L, S5_G, S5_P), s5_lambda_im=G["lim"].reshape(L, S5_G, S5_P),
        s5_b_re=jnp.transpose(G["btr"].reshape(L, S5_H, S5_G, S5_P), (0, 2, 3, 1)),
        s5_b_im=jnp.transpose(G["bti"].reshape(L, S5_H, S5_G, S5_P), (0, 2, 3, 1)),
        s5_c_re=jnp.transpose(G["ctr"][..., :S5_H].reshape(L, S5_G, S5_P, S5_H), (0, 1, 3, 2)),
        s5_c_im=jnp.transpose(G["cti"][..., :S5_H].reshape(L, S5_G, S5_P, S5_H), (0, 1, 3, 2)),
        s5_d=G["s5d"].reshape(L, S5_G, S5_H), s5_log_step=G["lstep"][:, 0, :S5_G],
        s5_w_glu=G["Wglu"], s5_b_glu=G["bglu"][:, 0],
        w_branch=jnp.stack([wb0, G["Wb1"], G["Wb2"]], axis=1), w_out=G["Wout"], ffn_norm_g=G["g2"][:, 0],
        w_up=G["Wup"], ffn_conv_w=G["convw"], w_down=G["Wdown"],
    )


def rope_tables(positions):
    inv_freq = ROPE_THETA ** (-jnp.arange(0, ROPE, 2, dtype=F32) / ROPE)
    ang = positions.astype(F32)[..., None] * inv_freq
    cos, sin = jnp.cos(ang), jnp.sin(ang)
    ones = jnp.ones(ang.shape[:-1] + (NOPE,), F32)
    zeros = jnp.zeros(ang.shape[:-1] + (LANE - NOPE - ROPE,), F32)
    cos_t = jnp.concatenate([ones, cos, cos, zeros], axis=-1)
    sin_t = jnp.concatenate([0.0 * ones, sin, sin, zeros], axis=-1)
    return cos_t.reshape(-1, LANE), sin_t.reshape(-1, LANE)


def fn_rms_res(n):
    return lambda x, g: (_rms(x, g, n), x)


def _s5_mats(p):
    return s5_params({k: p[k] for k in S5_PARAM_NAMES})


def layer_fwd(x, p, l, cos_t, sin_t, B, S):
    M, D = x.shape
    s = _seg(D)
    sm = s["small"] // LANE
    head = lambda j: j
    h = rowwise("rms_attn", fn_rms(D), [(x, D, 0)], [p["g1"]], [(D, BF16)])[0]
    proj = mm(h, p["Wp"], "nn", name="in_proj", layer=l)
    cnq = rowwise("latq_norm", fn_rms(Q_RANK), [(proj, Q_RANK, s["cq"] // Q_RANK)], [p["gql"]], [(Q_RANK, BF16)])[0]
    cnkv = rowwise("latkv_norm", fn_rms(KV_RANK), [(proj, KV_RANK, s["ckv"] // KV_RANK)], [p["gkvl"]],
                   [(KV_RANK, BF16)])[0]
    qraw = mm(cnq, p["Wuq"], "nn", name="q_up", layer=l)
    kn = mm(cnkv, p["Wk"], "nn", name="k_up", layer=l)
    v5 = mm(cnkv, p["Wv"], "nn", name="v_up", out_dtype=BF16, layer=l)
    qrot = rowwise("q_post", fn_qpost, [(qraw, LANE, head), (cos_t, LANE, 0), (sin_t, LANE, 0)], [p["gq"]],
                   [(LANE, BF16)], nj=N_HEADS)[0]
    krot = rowwise("k_post", fn_kpost, [(kn, LANE, head), (proj, LANE, sm), (cos_t, LANE, 0), (sin_t, LANE, 0)],
                   [p["gk"]], [(LANE, BF16)], nj=N_HEADS)[0]
    omla = attention("mla_attn", qrot, krot, v5, B=B, S=S, ntile=N_HEADS, hpt=1, dk=QK_DIM)
    fq0, fk0 = s["fq"] // LANE, s["fk"] // LANE
    qf = rowwise("foxq_norm", fn_foxnorm, [(proj, LANE, lambda j: fq0 + j)], [p["gfq"]], [(LANE, BF16)], nj=2)[0]
    kf = rowwise("foxk_norm", fn_foxnorm, [(proj, LANE, lambda j: fk0 + j)], [p["gfk"]], [(LANE, BF16)], nj=2)[0]
    lf = rowwise("fgate", fn_fgate, [(proj, LANE, sm)], [p["fbias"]], [(LANE, F32)])[0]
    cum = seq_cumsum("fox_cumsum", lf, B=B, S=S, reverse=False)
    ckt = _pad_rows8(jnp.transpose(cum.reshape(B, S, LANE)[:, :, F_LANE0:F_LANE0 + N_HEADS], (0, 2, 1)))
    ofox = attention("fox_attn", qf, kf, proj, B=B, S=S, ntile=2, hpt=2, dk=FOX_DIM, vc=s["fv"] // LANE,
                     cum=cum, ckt=ckt)
    lam, bre, bim, ctop, cbot = _s5_mats(p)
    u16 = proj[:, s["u"]:s["u"] + BW].astype(BF16)
    bur = mm(u16, bre, "nn", name="s5_bu_re")
    bui = mm(u16, bim, "nn", name="s5_bu_im")
    sr, si = s5_scan("s5_scan", bur, bui, lam, B=B, S=S, reverse=False)
    ypre = mm(si, cbot, "nn", name="s5_y_im", add=mm(sr, ctop, "nn", name="s5_y_re"))
    os5 = rowwise("s5_post", fn_s5post, [(ypre, BW, 0), (proj, BW, s["u"] // BW)],
                  [p["s5d"], p["Wglu"], p["bglu"]], [(BW, BF16)])[0]
    merged = rowwise("merge", fn_merge,
                     [(omla, N_HEADS * LANE, 0), (ofox, BW, 0), (os5, BW, 0), (proj, D, 0), (proj, D, 1), (proj, D, 2)],
                     [p["Wb0"], p["Wb1"], p["Wb2"]], [(D, BF16)])[0]
    xmid = mm(merged, p["Wout"], "nn", name="out_proj", add=x, layer=l)
    h2 = rowwise("rms_ffn", fn_rms(D), [(xmid, D, 0)], [p["g2"]], [(D, BF16)])[0]
    up = mm(h2, p["Wup"], "nn", name="ffn_up", layer=l)
    act = conv_gate_fwd(up, p["convw"], B=B, S=S)
    xout = mm(act, p["Wdown"], "nn", name="ffn_down", add=xmid, layer=l)
    saved = dict(x=x, h=h, proj=proj, cnq=cnq, cnkv=cnkv, qraw=qraw, kn=kn, v5=v5, qrot=qrot, krot=krot, qf=qf,
                 kf=kf, cum=cum, ckt=ckt, omla=omla, ofox=ofox, os5=os5, u16=u16, sr=sr, si=si, ypre=ypre,
                 merged=merged, xmid=xmid, h2=h2, up=up, act=act)
    return xout, saved


def _pad_rows8(a):
    return jnp.pad(a, ((0, 0), (0, 8 - a.shape[1]), (0, 0)))


def layer_bwd(dx, p, l, sv, cos_t, sin_t, B, S):
    M, D = dx.shape
    s = _seg(D)
    sm = s["small"] // LANE
    head = lambda j: j
    proj = sv["proj"]
    dact = mm(dx, p["WdownT"], "nn", name="ffn_down_dx", layer=l)
    d_wdown = mm(sv["act"], dx, "tn", name="ffn_down_dw")
    dup, d_convw = conv_gate_bwd(sv["up"], p["convw"], dact, B=B, S=S)
    dh2 = mm(dup, p["WupT"], "nn", name="ffn_up_dx", layer=l)
    d_wup = mm(sv["h2"], dup, "tn", name="ffn_up_dw")
    (dxmid,), (d_g2,) = rowwise_vjp("rms_ffn_vjp", fn_rms_res(D), [(sv["xmid"], D, 0)], [p["g2"]],
                                    [(dh2, D, 0), (dx, D, 0)], [0])
    dmerged = mm(dxmid, p["WoutT"], "nn", name="out_proj_dx", layer=l)
    d_wout = mm(sv["merged"], dxmid, "tn", name="out_proj_dw")
    (dom, dof, dos, dg0, dg1, dg2), (d_wb0, d_wb1, d_wb2) = rowwise_vjp(
        "merge_vjp", fn_merge,
        [(sv["omla"], N_HEADS * LANE, 0), (sv["ofox"], BW, 0), (sv["os5"], BW, 0), (proj, D, 0), (proj, D, 1),
         (proj, D, 2)], [p["Wb0"], p["Wb1"], p["Wb2"]], [(dmerged, D, 0)], [0, 1, 2, 3, 4, 5], tm=128,
        gdt=[BF16, BF16, F32, BF16, BF16, BF16])
    lam, bre, bim, ctop, cbot = _s5_mats(p)
    (dypre, du_a), (d_s5d, d_wglu, d_bglu) = rowwise_vjp(
        "s5_post_vjp", fn_s5post, [(sv["ypre"], BW, 0), (proj, BW, s["u"] // BW)],
        [p["s5d"], p["Wglu"], p["bglu"]], [(dos, BW, 0)], [0, 1], gdt=[BF16, F32])
    dsr = mm(dypre, ctop.T, "nn", name="s5_y_re_dx")
    dsi = mm(dypre, cbot.T, "nn", name="s5_y_im_dx")
    d_ctop = mm(sv["sr"], dypre, "tn", name="s5_y_re_dw")
    d_cbot = mm(sv["si"], dypre, "tn", name="s5_y_im_dw")
    gr, gi, dl_r, dl_i = s5_scan("s5_scan_vjp", dsr, dsi, lam, B=B, S=S, reverse=True, state=(sv["sr"], sv["si"]))
    du = mm(gi, bim.T, "nn", name="s5_bu_im_dx", out_dtype=BF16,
            add=mm(gr, bre.T, "nn", name="s5_bu_re_dx", add=du_a))
    d_bre = mm(sv["u16"], gr, "tn", name="s5_bu_re_dw")
    d_bim = mm(sv["u16"], gi, "tn", name="s5_bu_im_dw")
    d_s5 = s5_params_vjp({k: p[k] for k in S5_PARAM_NAMES}, dl_r, dl_i, d_bre, d_bim, d_ctop, d_cbot)
    dqf, dkf, dfv, dcq_t, dckt_t = attention("fox_attn_vjp", sv["qf"], sv["kf"], proj, B=B, S=S, ntile=2, hpt=2,
                                             dk=FOX_DIM, vc=s["fv"] // LANE, cum=sv["cum"], ckt=sv["ckt"], do=dof)
    dck = dckt_t.reshape(B, 2, 8, S)
    dck = jnp.transpose(dck[:, 0, :N_HEADS] + dck[:, 1, :N_HEADS], (0, 2, 1)).reshape(M, N_HEADS)
    dcum = rowwise("fox_dcum", lambda a, b: (a[:, 0:LANE] + a[:, LANE:2 * LANE] + b,),
                   [(dcq_t, 2 * LANE, 0), (_place(dck, F_LANE0), LANE, 0)], [], [(LANE, F32)])[0]
    dlf = seq_cumsum("fox_cumsum_vjp", dcum, B=B, S=S, reverse=True)
    (dsmall_f,), (d_fbias,) = rowwise_vjp("fgate_vjp", fn_fgate, [(proj, LANE, sm)], [p["fbias"]],
                                          [(dlf, LANE, 0)], [0])
    fq0, fk0 = s["fq"] // LANE, s["fk"] // LANE
    (dfq,), (d_gfq,) = rowwise_vjp("foxq_norm_vjp", fn_foxnorm, [(proj, LANE, lambda j: fq0 + j)], [p["gfq"]],
                                   [(dqf, LANE, head)], [0], nj=2, gdt=[BF16])
    (dfk,), (d_gfk,) = rowwise_vjp("foxk_norm_vjp", fn_foxnorm, [(proj, LANE, lambda j: fk0 + j)], [p["gfk"]],
                                   [(dkf, LANE, head)], [0], nj=2, gdt=[BF16])
    dqrot, dkrot, dv5 = attention("mla_attn_vjp", sv["qrot"], sv["krot"], sv["v5"], B=B, S=S, ntile=N_HEADS,
                                  hpt=1, dk=QK_DIM, do=dom)
    (dqraw,), (d_gq,) = rowwise_vjp("q_post_vjp", fn_qpost,
                                    [(sv["qraw"], LANE, head), (cos_t, LANE, 0), (sin_t, LANE, 0)], [p["gq"]],
                                    [(dqrot, LANE, head)], [0], nj=N_HEADS, gdt=[BF16])
    (dkn, dsmall_k), (d_gk,) = rowwise_vjp(
        "k_post_vjp", fn_kpost, [(sv["kn"], LANE, head), (proj, LANE, sm), (cos_t, LANE, 0), (sin_t, LANE, 0)],
        [p["gk"]], [(dkrot, LANE, head)], [0, 1], nj=N_HEADS, gdt=[BF16, F32])
    dsmall = rowwise("small_sum", fn_add5, [(dsmall_k, N_HEADS * LANE, 0), (dsmall_f, LANE, 0)], [], [(LANE, BF16)])[0]
    dcnq = mm(dqraw, p["WuqT"], "nn", name="q_up_dx", layer=l)
    d_wuq = mm(sv["cnq"], dqraw, "tn", name="q_up_dw")
    dcnkv = mm(dv5, p["WvT"], "nn", name="v_up_dx", layer=l, add=mm(dkn, p["WkT"], "nn", name="k_up_dx", layer=l))
    d_wk = mm(sv["cnkv"], dkn, "tn", name="k_up_dw")
    d_wv = mm(sv["cnkv"], dv5, "tn", name="v_up_dw")
    (dcq,), (d_gql,) = rowwise_vjp("latq_norm_vjp", fn_rms(Q_RANK), [(proj, Q_RANK, s["cq"] // Q_RANK)], [p["gql"]],
                                   [(dcnq, Q_RANK, 0)], [0], gdt=[BF16])
    (dckv,), (d_gkvl,) = rowwise_vjp("latkv_norm_vjp", fn_rms(KV_RANK), [(proj, KV_RANK, s["ckv"] // KV_RANK)],
                                     [p["gkvl"]], [(dcnkv, KV_RANK, 0)], [0], gdt=[BF16])
    dproj = jnp.concatenate([dg0, dg1, dg2, dckv, dfq, dfk, dfv.astype(BF16), du, dsmall,
                             jnp.zeros((M, LANE), BF16), dcq], axis=1)
    dh = mm(dproj, p["WpT"], "nn", name="in_proj_dx", layer=l)
    d_wp = mm(sv["h"], dproj, "tn", name="in_proj_dw")
    (dxin,), (d_g1,) = rowwise_vjp("rms_attn_vjp", fn_rms_res(D), [(sv["x"], D, 0)], [p["g1"]],
                                   [(dh, D, 0), (dxmid, D, 0)], [0])
    grads = dict(g1=d_g1, Wp=d_wp, gql=d_gql, gkvl=d_gkvl, Wuq=d_wuq, Wk=d_wk, Wv=d_wv, gq=d_gq, gk=d_gk,
                 gfq=d_gfq, gfk=d_gfk, fbias=d_fbias, s5d=d_s5d, Wglu=d_wglu, bglu=d_bglu, Wb0=d_wb0, Wb1=d_wb1,
                 Wb2=d_wb2, Wout=d_wout, g2=d_g2, Wup=d_wup, convw=d_convw, Wdown=d_wdown)
    grads.update(dict(zip(S5_PARAM_NAMES, d_s5)))
    return dxin, grads


def local_step(x, positions, target, w):
    B, S, D = x.shape
    M = B * S
    P = with_transposes(prep_weights(w))
    L = P["g1"].shape[0]
    cos_t, sin_t = rope_tables(positions)
    xc, saved = x.reshape(M, D), []
    for l in range(L):
        xc, sv = layer_fwd(xc, layer_params(P, l), l, cos_t, sin_t, B, S)
        saved.append(sv)
    dxc, sq = loss_head(xc, target.reshape(M, D))
    grads = [None] * L
    for l in reversed(range(L)):
        dxc, grads[l] = layer_bwd(dxc, layer_params(P, l), l, saved[l], cos_t, sin_t, B, S)
    G = {k: jnp.stack([g[k] for g in grads]) for k in grads[0]}
    return sq, dxc.reshape(B, S, D), unprep_grads(G, D)


MESH = pl.DeviceIdType.MESH
ANY = pl.BlockSpec(memory_space=pl.ANY)
N_CHIPS = 4
SHARDED = ("w_in", "w_uq", "w_ukv", "s5_w_glu", "w_branch", "w_out", "w_up", "ffn_conv_w", "w_down")
MINOR = ("w_branch", "w_up", "ffn_conv_w")
F32_TRAVEL = ("ffn_conv_w",)
WEIGHTS = ("attn_norm_g", "w_in", "q_lat_norm_g", "w_uq", "kv_lat_norm_g", "w_ukv", "mla_q_norm_g", "mla_k_norm_g",
           "fox_q_norm_g", "fox_k_norm_g", "fox_f_bias", "s5_lambda_re", "s5_lambda_im", "s5_b_re", "s5_b_im",
           "s5_c_re", "s5_c_im", "s5_d", "s5_log_step", "s5_w_glu", "s5_b_glu", "w_branch", "w_out", "ffn_norm_g",
           "w_up", "ffn_conv_w", "w_down")
SMALL = tuple(n for n in WEIGHTS if n not in SHARDED)


def shard3(name, a):
    L = a.shape[0]
    if name in ("w_uq", "w_ukv"):
        return a.reshape(L, a.shape[1], -1)
    if name == "w_branch":
        return a.reshape(L, -1, a.shape[-1])
    return a


def full4(name, a):
    L = a.shape[0]
    if name == "w_in":
        return jnp.transpose(a.reshape(L, a.shape[1], N_CHIPS, -1), (0, 2, 1, 3))
    if name in MINOR:
        return a.reshape(L, 1, -1, a.shape[-1])
    a = a.reshape(L, a.shape[1], -1)
    return a.reshape(L, N_CHIPS, a.shape[1] // N_CHIPS, a.shape[2])


def from_full4(name, a, ref_tail):
    L = a.shape[0]
    if name == "w_in":
        a = jnp.transpose(a, (0, 2, 1, 3))
    return a.reshape((L,) + tuple(ref_tail))


def _where():
    x, y, c = lax.axis_index("x"), lax.axis_index("y"), lax.axis_index("c")
    chips = [(1 - x, y), (x, 1 - y), (1 - x, 1 - y)]
    return (x, y, c), 2 * x + y, (x, y, 1 - c), chips, [2 * cx + cy for cx, cy in chips]


def _view(minor, ref4, layers, k):
    if minor:
        cs = ref4.shape[3] // N_CHIPS
        return ref4.at[layers, 0, :, pl.ds(pl.multiple_of(k * cs, LANE), cs)]
    return ref4.at[layers, k]


def _remote(src, dst, ssem, rsem, dev):
    return pltpu.make_async_remote_copy(src_ref=src, dst_ref=dst, send_sem=ssem, recv_sem=rsem,
                                        device_id=dev, device_id_type=MESH)


def gather_weights(shards, minor):
    n = len(shards)
    L = shards[0].shape[0]
    Lh = L // 2
    out_shape = []
    for a, mn in zip(shards, minor):
        _, r, cs = a.shape
        out_shape.append(jax.ShapeDtypeStruct((L, 1, r, N_CHIPS * cs) if mn else (L, N_CHIPS, r, cs), a.dtype))

    def body(*refs):
        w, g = refs[:n], refs[n:2 * n]
        send, recv = refs[2 * n:]
        (x, y, c), me, sib, chips, cidx = _where()
        mine, other, every = pl.ds(c * Lh, Lh), pl.ds((1 - c) * Lh, Lh), pl.ds(0, L)
        dst = lambda i, layers, k: _view(minor[i], g[i], layers, k)
        local = [_remote(w[i], dst(i, every, me), send.at[i, 6], recv.at[i, 6], sib) for i in range(n)]
        first = [_remote(w[i].at[mine], dst(i, mine, me), send.at[i, j], recv.at[i, j], (*chips[j], c))
                 for i in range(n) for j in range(3)]
        for cp in local + first:
            cp.start()
        passed = []
        for i in range(n):
            for j in range(3):
                blk = dst(i, mine, cidx[j])
                _remote(blk, blk, send.at[i, j], recv.at[i, j], (*chips[j], c)).wait_recv()
                fwd = _remote(blk, blk, send.at[i, 3 + j], recv.at[i, 3 + j], sib)
                fwd.start()
                passed.append(fwd)
        for i in range(n):
            for j in range(3):
                blk = dst(i, other, cidx[j])
                _remote(blk, blk, send.at[i, 3 + j], recv.at[i, 3 + j], sib).wait_recv()
        for cp in first + passed:
            cp.wait_send()
        for cp in local:
            cp.wait()

    return pl.pallas_call(
        body, name="gather_weights", in_specs=[ANY] * n, out_specs=[ANY] * n, out_shape=out_shape,
        scratch_shapes=[pltpu.SemaphoreType.DMA((n, 7)), pltpu.SemaphoreType.DMA((n, 7))],
    )(*shards)


def sibling_halves(grads):
    n = len(grads)
    L = grads[0].shape[0]
    Lh = L // 2
    half = [jax.ShapeDtypeStruct((Lh,) + a.shape[1:], a.dtype) for a in grads]

    def body(*refs):
        g, got = refs[:n], refs[n:2 * n]
        send, recv = refs[2 * n:]
        (x, y, c), me, sib, chips, cidx = _where()
        other = pl.ds((1 - c) * Lh, Lh)
        out = [_remote(g[i].at[other], got[i], send.at[i], recv.at[i], sib) for i in range(n)]
        for cp in out:
            cp.start()
        for cp in out:
            cp.wait()

    got = pl.pallas_call(
        body, name="grad_sibling_halves", in_specs=[ANY] * n, out_specs=[ANY] * n, out_shape=half,
        scratch_shapes=[pltpu.SemaphoreType.DMA((n,)), pltpu.SemaphoreType.DMA((n,))],
    )(*grads)
    c = lax.axis_index("c")
    own = [lax.dynamic_slice_in_dim(a, c * Lh, Lh, axis=0) for a in grads]
    return own, got


def scatter_chip_sums(sums, travel, minor):
    n = len(sums)
    Lh = sums[0].shape[0]
    got_shape = []
    for a, t, mn in zip(sums, travel, minor):
        r, cs = a.shape[2], (a.shape[3] // N_CHIPS if mn else a.shape[3])
        got_shape.append(jax.ShapeDtypeStruct((3, Lh, r, cs), t.dtype))

    def body(*refs):
        s16, got = refs[:n], refs[n:2 * n]
        send, recv = refs[2 * n:]
        (x, y, c), me, sib, chips, cidx = _where()
        every = pl.ds(0, Lh)
        out = [_remote(_view(minor[i], s16[i], every, cidx[j]), got[i].at[j], send.at[i, j], recv.at[i, j],
                       (*chips[j], c)) for i in range(n) for j in range(3)]
        for cp in out:
            cp.start()
        for cp in out:
            cp.wait()

    got = pl.pallas_call(
        body, name="grad_scatter", in_specs=[ANY] * n, out_specs=[ANY] * n, out_shape=got_shape,
        scratch_shapes=[pltpu.SemaphoreType.DMA((n, 3)), pltpu.SemaphoreType.DMA((n, 3))],
    )(*travel)
    me = 2 * lax.axis_index("x") + lax.axis_index("y")
    mine = []
    for a, mn in zip(sums, minor):
        if mn:
            cs = a.shape[3] // N_CHIPS
            mine.append(lax.dynamic_slice_in_dim(a[:, 0], me * cs, cs, axis=2))
        else:
            mine.append(lax.dynamic_index_in_dim(a, me, axis=1, keepdims=False))
    return mine, got


def share_halves(halves):
    n = len(halves)
    Lh = halves[0].shape[0]
    out_shape = [jax.ShapeDtypeStruct(a.shape, a.dtype) for a in halves]

    def body(*refs):
        h, got = refs[:n], refs[n:2 * n]
        send, recv = refs[2 * n:]
        (x, y, c), me, sib, chips, cidx = _where()
        out = [_remote(h[i], got[i], send.at[i], recv.at[i], sib) for i in range(n)]
        for cp in out:
            cp.start()
        for cp in out:
            cp.wait()

    got = pl.pallas_call(
        body, name="grad_share_halves", in_specs=[ANY] * n, out_specs=[ANY] * n, out_shape=out_shape,
        scratch_shapes=[pltpu.SemaphoreType.DMA((n,)), pltpu.SemaphoreType.DMA((n,))],
    )(*halves)
    south = lax.axis_index("c") == 0
    return [jnp.where(south, jnp.concatenate([a, b]), jnp.concatenate([b, a])) for a, b in zip(halves, got)]


N_DEV = 8


def allreduce_small(v):
    R = v.shape[0]

    def body(x_ref, sum_ref, all_ref, send, recv, loc):
        (x, y, c), me, sib, chips, cidx = _where()

        def rows(px, py, pc):
            return all_ref.at[4 * px + 2 * py + pc]

        def copy(k, block, to, src=None):
            return _remote(rows(*block) if src is None else src, rows(*block), send.at[k], recv.at[k], to)

        mine = pltpu.make_async_copy(x_ref, rows(x, y, c), loc)
        mine.start()
        first = [copy(0, (x, y, c), sib, src=x_ref)]
        first += [copy(1 + j, (x, y, c), (*chip, c), src=x_ref) for j, chip in enumerate(chips)]
        for cp in first:
            cp.start()
        passed = [copy(4 + j, (*chip, c), sib) for j, chip in enumerate(chips)]
        for j, chip in enumerate(chips):
            copy(1 + j, (*chip, c), (x, y, c)).wait_recv()
            passed[j].start()
        copy(0, (x, y, 1 - c), (x, y, c)).wait_recv()
        for j, chip in enumerate(chips):
            copy(4 + j, (*chip, 1 - c), (x, y, c)).wait_recv()
        for cp in first + passed:
            cp.wait_send()
        mine.wait()
        acc = all_ref[0]
        for d in range(1, N_DEV):
            acc = acc + all_ref[d]
        sum_ref[...] = acc

    vm = pl.BlockSpec(memory_space=pltpu.VMEM)
    return pl.pallas_call(
        body, name="allreduce_small", in_specs=[vm], out_specs=[vm, vm],
        out_shape=[jax.ShapeDtypeStruct((R, LANE), F32), jax.ShapeDtypeStruct((N_DEV, R, LANE), F32)],
        scratch_shapes=[pltpu.SemaphoreType.DMA((7,)), pltpu.SemaphoreType.DMA((7,)), pltpu.SemaphoreType.DMA],
        compiler_params=pltpu.CompilerParams(vmem_limit_bytes=VMEM_LIMIT),
    )(v)[0]


EW_BLOCK_BYTES = 2 << 20


def _ew_rows(rows, cols):
    for tr in (1024, 512, 256, 128, 64, 32, 16, 8):
        if rows % tr == 0 and tr * cols * 4 <= EW_BLOCK_BYTES:
            return tr
    return rows


def add_pair(name, a, b, travel_dtype):
    R, C = a.shape
    tr = _ew_rows(R, C)

    def body(a_ref, b_ref, s_ref, t_ref):
        s = a_ref[...] + b_ref[...]
        s_ref[...] = s
        t_ref[...] = s.astype(t_ref.dtype)

    blk = pl.BlockSpec((tr, C), lambda i: (i, 0))
    return pl.pallas_call(
        body, name=name, grid=(R // tr,), in_specs=[blk, blk], out_specs=[blk, blk],
        out_shape=[jax.ShapeDtypeStruct((R, C), F32), jax.ShapeDtypeStruct((R, C), travel_dtype)],
        compiler_params=_cp("parallel"),
    )(a, b)


def add_four(name, mine, got):
    R, C = mine.shape
    tr = _ew_rows(R, C)

    def body(m_ref, g0, g1, g2, o_ref):
        o_ref[...] = ((m_ref[...] + g0[0].astype(F32)) + g1[0].astype(F32)) + g2[0].astype(F32)

    blk = pl.BlockSpec((tr, C), lambda i: (i, 0))
    slot = lambda j: pl.BlockSpec((1, tr, C), lambda i: (j, i, 0))
    return pl.pallas_call(
        body, name=name, grid=(R // tr,), in_specs=[blk, slot(0), slot(1), slot(2)], out_specs=blk,
        out_shape=jax.ShapeDtypeStruct((R, C), F32), compiler_params=_cp("parallel"),
    )(mine, got, got, got)


def reduce_scatter_grads(full_grads):
    names = list(SHARDED)
    minor = [nm in MINOR for nm in names]
    g4 = [full4(nm, full_grads[nm]) for nm in names]
    own, got = sibling_halves(g4)
    sums, travel = [], []
    for nm, a, b in zip(names, own, got):
        s, t = add_pair("chip_sum_" + nm, a.reshape(-1, a.shape[-1]), b.reshape(-1, b.shape[-1]),
                        F32 if nm in F32_TRAVEL else BF16)
        sums.append(s.reshape(a.shape))
        travel.append(t.reshape(a.shape))
    mine, arrived = scatter_chip_sums(sums, travel, minor)
    halves = []
    for nm, a, b in zip(names, mine, arrived):
        f = add_four("shard_sum_" + nm, a.reshape(-1, a.shape[-1]), b.reshape(3, -1, b.shape[-1]))
        halves.append(f.reshape(a.shape))
    return dict(zip(names, share_halves(halves)))


def pack_small(tree, extra=None):
    parts = [tree[nm].reshape(-1) for nm in SMALL]
    parts.append(jnp.zeros((1,), F32) if extra is None else extra.reshape(-1))
    flat = jnp.concatenate(parts)
    rows = -(-flat.shape[0] // (8 * LANE)) * 8
    return jnp.pad(flat, (0, rows * LANE - flat.shape[0])).reshape(rows, LANE)


def unpack_small(packed, like):
    flat = packed.reshape(-1)
    out, at = {}, 0
    for nm in SMALL:
        size = math.prod(like[nm].shape)
        out[nm] = flat[at:at + size].reshape(like[nm].shape)
        at += size
    return out, flat[at]


def kernel(x, positions, attn_norm_g, w_in, q_lat_norm_g, w_uq, kv_lat_norm_g, w_ukv, mla_q_norm_g, mla_k_norm_g, fox_q_norm_g, fox_k_norm_g, fox_f_bias, s5_lambda_re, s5_lambda_im, s5_b_re, s5_b_im, s5_c_re, s5_c_im, s5_d, s5_log_step, s5_w_glu, s5_b_glu, w_branch, w_out, ffn_norm_g, w_up, ffn_conv_w, w_down, loss_target, m_attn_norm_g, m_w_in, m_q_lat_norm_g, m_w_uq, m_kv_lat_norm_g, m_w_ukv, m_mla_q_norm_g, m_mla_k_norm_g, m_fox_q_norm_g, m_fox_k_norm_g, m_fox_f_bias, m_s5_lambda_re, m_s5_lambda_im, m_s5_b_re, m_s5_b_im, m_s5_c_re, m_s5_c_im, m_s5_d, m_s5_log_step, m_s5_w_glu, m_s5_b_glu, m_w_branch, m_w_out, m_ffn_norm_g, m_w_up, m_ffn_conv_w, m_w_down, v_attn_norm_g, v_w_in, v_q_lat_norm_g, v_w_uq, v_kv_lat_norm_g, v_w_ukv, v_mla_q_norm_g, v_mla_k_norm_g, v_fox_q_norm_g, v_fox_k_norm_g, v_fox_f_bias, v_s5_lambda_re, v_s5_lambda_im, v_s5_b_re, v_s5_b_im, v_s5_c_re, v_s5_c_im, v_s5_d, v_s5_log_step, v_s5_w_glu, v_s5_b_glu, v_w_branch, v_w_out, v_ffn_norm_g, v_w_up, v_ffn_conv_w, v_w_down):
    given = dict(locals())
    w = {nm: given[nm] for nm in WEIGHTS}
    m = {nm: given["m_" + nm] for nm in WEIGHTS}
    v = {nm: given["v_" + nm] for nm in WEIGHTS}
    D = x.shape[-1]

    minor = [nm in MINOR for nm in SHARDED]
    shards = [shard3(nm, w[nm]).astype(F32 if nm in F32_TRAVEL else BF16) for nm in SHARDED]
    gathered = gather_weights(shards, minor)
    full = dict(w)
    for nm, g4 in zip(SHARDED, gathered):
        tail = list(w[nm].shape[1:])
        axis = (len(tail) - 1) if nm in MINOR or nm == "w_in" else 0
        tail[axis] *= N_CHIPS
        full[nm] = from_full4(nm, g4, tail)

    sq, grad_x, gw = local_step(x, positions, loss_target, full)

    big = reduce_scatter_grads(gw)
    total, sq_sum = unpack_small(allreduce_small(pack_small(gw, sq[0:1, 0:1])), w)
    loss = 0.5 * sq_sum / D

    grads, delta, new_m, new_v = {}, {}, {}, {}
    for nm in SHARDED:
        g = big[nm].reshape(shard3(nm, w[nm]).shape)
        two = lambda a: shard3(nm, a).reshape(-1, g.shape[-1])
        d2, m2, v2 = adamw("adamw_" + nm, two(w[nm]), g.reshape(-1, g.shape[-1]), two(m[nm]), two(v[nm]))
        grads[nm] = g.reshape(w[nm].shape)
        delta[nm], new_m[nm], new_v[nm] = (a.reshape(w[nm].shape) for a in (d2, m2, v2))
    d2, m2, v2 = adamw("adamw_small", pack_small(w), pack_small(total), pack_small(m), pack_small(v))
    for tree, packed in ((delta, d2), (new_m, m2), (new_v, v2)):
        tree.update(unpack_small(packed, w)[0])
    grads.update(total)
    return (loss, grad_x, *[grads[nm] for nm in WEIGHTS], *[delta[nm] for nm in WEIGHTS],
            *[new_m[nm] for nm in WEIGHTS], *[new_v[nm] for nm in WEIGHTS])
```

```python
import functools
import math

import jax
import jax.numpy as jnp
from jax import lax
from jax.experimental import pallas as pl
from jax.experimental.pallas import tpu as pltpu

F32, BF16 = jnp.float32, jnp.bfloat16
NORM_EPS = 1e-6
NEG_INF = -1e30
ROPE_THETA = 10000.0
LANE = 128
N_HEADS = 4
NOPE, ROPE, QK_DIM, V_DIM = 64, 32, 96, 64
Q_RANK, KV_RANK = 384, 256
FOX_DIM = 64
S5_G, S5_H, S5_P = 16, 16, 64
S5_C = S5_G * S5_P
BW = 256
VMEM_LIMIT = 56 << 20
ADAM_LR, ADAM_B1, ADAM_B2, ADAM_EPS, ADAM_WD, ADAM_STEP = 0.001, 0.9, 0.999, 1e-08, 0.01, 10


def _pick(n, cands):
    for c in cands:
        if n % c == 0:
            return c
    return n


def _cp(*sem):
    return pltpu.CompilerParams(dimension_semantics=sem, vmem_limit_bytes=VMEM_LIMIT)


def _dg(a, b, ca, cb):
    return lax.dot_general(a.astype(BF16), b.astype(BF16), (((ca,), (cb,)), ((), ())),
                           preferred_element_type=F32)


@jax.custom_vjp
def dot_nn(a, b):
    return _dg(a, b, 1, 0)


@jax.custom_vjp
def dot_nt(a, b):
    return _dg(a, b, 1, 1)


@jax.custom_vjp
def dot_tn(a, b):
    return _dg(a, b, 0, 0)


dot_nn.defvjp(lambda a, b: (dot_nn(a, b), (a, b)),
              lambda r, g: (dot_nt(g, r[1]).astype(r[0].dtype), dot_tn(r[0], g).astype(r[1].dtype)))
dot_nt.defvjp(lambda a, b: (dot_nt(a, b), (a, b)),
              lambda r, g: (dot_nn(g, r[1]).astype(r[0].dtype), dot_tn(g, r[0]).astype(r[1].dtype)))
dot_tn.defvjp(lambda a, b: (dot_tn(a, b), (a, b)),
              lambda r, g: (dot_nt(r[1], g).astype(r[0].dtype), dot_nn(r[0], g).astype(r[1].dtype)))


def xdot(a, b):
    return jnp.dot(a, b, precision=lax.Precision.HIGHEST, preferred_element_type=F32)


MM_VMEM_BUDGET = 36 << 20
MM_STEP_S, MM_HBM_BPS, MM_VMEM_BPS = 0.4e-6, 2.5e12, 3e12


def _divisors(n, cands):
    return sorted({c for c in cands if n % c == 0} | {n}, reverse=True)


def _mm_tiles(M, K, N, ab, bb, ob, addb):
    best = None
    for tm in _divisors(M, (2048, 1024, 512, 256, 128)):
        for tn in _divisors(N, (2048, 1664, 1536, 1408, 1280, 1024, 768, 640, 512, 384, 256, 128)):
            for tk in _divisors(K, (4096, 2048, 1664, 1536, 1408, 1024, 768, 512, 384, 256, 128)):
                vmem = 2 * (tm * tk * ab + tk * tn * bb + tm * tn * (ob + addb)) + (tm * tn * 4 if tk != K else 0)
                if vmem > MM_VMEM_BUDGET:
                    continue
                nk = K // tk
                steps = (M // tm) * (N // tn) * nk
                traffic = M * K * ab * (N // tn) + K * N * bb * (M // tm) + M * N * (ob + addb)
                cost = steps * MM_STEP_S + traffic / MM_HBM_BPS + (M * N * 8 * nk / MM_VMEM_BPS if nk > 1 else 0)
                if best is None or cost < best[0]:
                    best = (cost, tm, tn, tk)
    assert best is not None, (M, K, N)
    return best[1:]


def mm(a, b, mode, *, name, add=None, out_dtype=F32, layer=None, slot=None):
    bk, bn = b.shape[-2:]
    if mode == "nn":
        (M, K), N = a.shape, bn
    else:
        (K, M), N = a.shape, bn
    assert bk == K, (name, a.shape, b.shape)
    isz = lambda x: jnp.dtype(x.dtype).itemsize
    tm, tn, tk = _mm_tiles(M, K, N, isz(a), isz(b), jnp.dtype(out_dtype).itemsize, 0 if add is None else isz(add))
    nk = K // tk
    ca = 1 if mode == "nn" else 0

    n_in = 2 + (add is not None) + (slot is not None and slot[2] is not None)

    def body(*refs):
        a_ref, b_ref = refs[:2]
        add_ref = refs[2] if add is not None else None
        o_ref = refs[n_in]

        def finish(r):
            if add is not None:
                r = r + add_ref[...].astype(F32)
            o_ref[...] = r.astype(out_dtype)

        part = _dg(a_ref[...], b_ref[...], ca, 0)
        if nk == 1:
            finish(part)
            return
        acc = refs[-1]
        kk = pl.program_id(2)

        @pl.when(kk == 0)
        def _():
            acc[...] = part

        @pl.when(kk > 0)
        def _():
            acc[...] += part

        @pl.when(kk == nk - 1)
        def _():
            finish(acc[...])

    a_spec = (pl.BlockSpec((tm, tk), lambda i, j, k: (i, k)) if mode == "nn"
              else pl.BlockSpec((tk, tm), lambda i, j, k: (k, i)))
    b_spec = (pl.BlockSpec((tk, tn), lambda i, j, k: (k, j)) if layer is None
              else pl.BlockSpec((None, tk, tn), lambda i, j, k: (layer, k, j)))
    in_specs, args = [a_spec, b_spec], [a, b]
    if add is not None:
        in_specs.append(pl.BlockSpec((tm, tn), lambda i, j, k: (i, j)))
        args.append(add)
    out_spec = pl.BlockSpec((tm, tn), lambda i, j, k: (i, j))
    out_shape = jax.ShapeDtypeStruct((M, N), out_dtype)
    aliases = {}
    if slot is not None:
        n_layers, l, buf = slot
        out_spec = pl.BlockSpec((None, tm, tn), lambda i, j, k: (l, i, j))
        out_shape = jax.ShapeDtypeStruct((n_layers, M, N), out_dtype)
        if buf is not None:
            aliases = {len(args): 0}
            in_specs.append(pl.BlockSpec(memory_space=pl.ANY))
            args.append(buf)
    return pl.pallas_call(
        body, name=name, grid=(M // tm, N // tn, nk),
        in_specs=in_specs, out_specs=out_spec, out_shape=out_shape, input_output_aliases=aliases,
        scratch_shapes=[pltpu.VMEM((tm, tn), F32)] if nk > 1 else [],
        compiler_params=_cp("parallel", "parallel", "arbitrary"),
    )(*args)


def _row_spec(tm, width, col):
    if callable(col):
        return pl.BlockSpec((tm, width), lambda i, j: (i, col(j)))
    return pl.BlockSpec((tm, width), lambda i, j: (i, col))


def _const_spec(c):
    return pl.BlockSpec(c.shape, lambda i, j: (0,) * c.ndim)


ROW_BLOCK_BYTES = 6 << 20


def _row_tile(M, widths, tm):
    if tm is None:
        tm = next((t for t in (1024, 512, 256) if t * sum(widths) * 4 <= ROW_BLOCK_BYTES), 128)
    return _pick(M, (tm, 256, 128, 64, 32, 16, 8))


def rowwise(name, fn, rows, consts, outs, *, tm=None, nj=1):
    M = rows[0][0].shape[0]
    tm = _row_tile(M, [w for _, w, _ in rows] + [w for w, _ in outs], tm)
    nr, nc = len(rows), len(consts)

    def body(*refs):
        vals = [r[...].astype(F32) for r in refs[:nr]] + [r[...] for r in refs[nr:nr + nc]]
        res = fn(*vals)
        for o_ref, r in zip(refs[nr + nc:], res):
            o_ref[...] = r.astype(o_ref.dtype)

    return pl.pallas_call(
        body, name=name, grid=(M // tm, nj),
        in_specs=[_row_spec(tm, w, c) for _, w, c in rows] + [_const_spec(c) for c in consts],
        out_specs=[pl.BlockSpec((tm, w), lambda i, j: (i, j)) for w, _ in outs],
        out_shape=[jax.ShapeDtypeStruct((M, nj * w), dt) for w, dt in outs],
        compiler_params=_cp("parallel", "parallel"),
    )(*[r[0] for r in rows], *consts)


def rowwise_vjp(name, fn, rows, consts, cts, diff, *, tm=None, nj=1, gdt=None):
    M = rows[0][0].shape[0]
    tm = _row_tile(M, [w for _, w, _ in rows] + [w for _, w, _ in cts] + [rows[p][1] for p in diff], tm)
    nr, nc, nt, nd = len(rows), len(consts), len(cts), len(diff)
    gdt = [F32] * nd if gdt is None else gdt

    def body(*refs):
        vals = [r[...].astype(F32) for r in refs[:nr + nc + nt]]
        rv, cv, tv = vals[:nr], vals[nr:nr + nc], vals[nr + nc:]
        grow, gconst = refs[nr + nc + nt:nr + nc + nt + nd], refs[nr + nc + nt + nd:]

        def f(*dargs):
            full = list(rv)
            for pos, val in zip(diff, dargs[:nd]):
                full[pos] = val
            return tuple(fn(*full, *dargs[nd:]))

        _, vjp = jax.vjp(f, *[rv[p] for p in diff], *cv)
        g = vjp(tuple(tv))
        for o_ref, gv in zip(grow, g[:nd]):
            o_ref[...] = gv.astype(o_ref.dtype)
        first = jnp.logical_and(pl.program_id(0) == 0, pl.program_id(1) == 0)
        for o_ref, gv in zip(gconst, g[nd:]):
            @pl.when(first)
            def _(o_ref=o_ref, gv=gv):
                o_ref[...] = gv

            @pl.when(jnp.logical_not(first))
            def _(o_ref=o_ref, gv=gv):
                o_ref[...] += gv

    out_specs = ([pl.BlockSpec((tm, rows[p][1]), lambda i, j: (i, j)) for p in diff]
                 + [_const_spec(c) for c in consts])
    out_shape = ([jax.ShapeDtypeStruct((M, nj * rows[p][1]), dt) for p, dt in zip(diff, gdt)]
                 + [jax.ShapeDtypeStruct(c.shape, F32) for c in consts])
    res = pl.pallas_call(
        body, name=name, grid=(M // tm, nj),
        in_specs=([_row_spec(tm, w, c) for _, w, c in rows] + [_const_spec(c) for c in consts]
                  + [_row_spec(tm, w, c) for _, w, c in cts]),
        out_specs=out_specs, out_shape=out_shape,
        compiler_params=_cp("arbitrary", "arbitrary"),
    )(*[r[0] for r in rows], *consts, *[t[0] for t in cts])
    return res[:nd], res[nd:]


def _lane(shape=(1, LANE)):
    return lax.broadcasted_iota(jnp.int32, shape, len(shape) - 1)


def _sigmoid(x):
    return 1.0 / (1.0 + jnp.exp(-x))


def _rms(x, g, n):
    return x * lax.rsqrt(jnp.sum(x * x, axis=-1, keepdims=True) * (1.0 / n) + NORM_EPS) * g


def fn_rms(n):
    return lambda x, g: (_rms(x, g, n),)


def _rope(x, cos_t, sin_t):
    i = lax.broadcasted_iota(jnp.int32, (LANE, LANE), 0)
    j = lax.broadcasted_iota(jnp.int32, (LANE, LANE), 1)
    half = ROPE // 2
    lo = jnp.logical_and(jnp.logical_and(j >= NOPE, j < NOPE + half), i == j + half)
    hi = jnp.logical_and(jnp.logical_and(j >= NOPE + half, j < NOPE + ROPE), i == j - half)
    perm = jnp.where(hi, 1.0, 0.0) - jnp.where(lo, 1.0, 0.0)
    return x * cos_t + xdot(x, perm) * sin_t


def fn_qpost(q, cos_t, sin_t, g):
    return (_rope(_rms(q, g, QK_DIM), cos_t, sin_t),)


def fn_kpost(kn, small, cos_t, sin_t, g):
    lane = _lane()
    rope_lanes = jnp.logical_and(lane >= NOPE, lane < NOPE + ROPE)
    kc = kn + jnp.where(rope_lanes, small, 0.0)
    return (_rope(_rms(kc, g, QK_DIM), cos_t, sin_t),)


def fn_foxnorm(x, g):
    first = _lane() < FOX_DIM
    sq = x * x
    s0 = jnp.sum(jnp.where(first, sq, 0.0), axis=-1, keepdims=True)
    s1 = jnp.sum(jnp.where(first, 0.0, sq), axis=-1, keepdims=True)
    r0 = lax.rsqrt(s0 * (1.0 / FOX_DIM) + NORM_EPS)
    r1 = lax.rsqrt(s1 * (1.0 / FOX_DIM) + NORM_EPS)
    return (x * jnp.where(first, r0, r1) * g,)


F_LANE0 = NOPE + ROPE


def fn_fgate(small, bias):
    z = small + bias
    lf = jnp.minimum(z, 0.0) - jnp.log(1.0 + jnp.exp(-jnp.abs(z)))
    lane = _lane()
    return (jnp.where(jnp.logical_and(lane >= F_LANE0, lane < F_LANE0 + N_HEADS), lf, 0.0),)


def _gelu(y):
    return 0.5 * y * (1.0 + jnp.tanh(math.sqrt(2.0 / math.pi) * (y + 0.044715 * (y * y * y))))


def fn_s5post(ypre, u, d, wglu, bglu):
    y = _gelu(ypre + d * u)
    return (y * _sigmoid(dot_nn(y, wglu) + bglu),)


def fn_merge(om, of, os_, g0, g1, g2, wb0, wb1, wb2):
    return (_sigmoid(g0) * dot_nn(om, wb0) + _sigmoid(g1) * dot_nn(of, wb1)
            + _sigmoid(g2) * dot_nn(os_, wb2),)


def fn_add5(a, b):
    return (a[:, 0:LANE] + a[:, LANE:2 * LANE] + a[:, 2 * LANE:3 * LANE] + a[:, 3 * LANE:4 * LANE] + b,)


def fn_addt(a, b):
    return (a + b,)


def fn_s5params(lre, lim, lstep, btr, bti, ctr, cti):
    C = S5_C
    grp = lax.broadcasted_iota(jnp.int32, (LANE, C), 1) >> 6
    expand = jnp.where(lax.broadcasted_iota(jnp.int32, (LANE, C), 0) == grp, 1.0, 0.0)
    lane = _lane()
    st = jnp.where(lane < S5_G, jnp.exp(lstep), 0.0)
    step = jnp.sum(xdot(jnp.broadcast_to(st, (8, LANE)), expand), axis=0, keepdims=True) * 0.125
    zr, zi = lre * step, lim * step
    er = jnp.exp(zr)
    lbr, lbi = er * jnp.cos(zi), er * jnp.sin(zi)
    den = lre * lre + lim * lim
    nr = lbr - 1.0
    cfr = (nr * lre + lbi * lim) / den
    cfi = (lbi * lre - nr * lim) / den
    bbr = cfr * btr - cfi * bti
    bbi = cfr * bti + cfi * btr
    rg = lax.broadcasted_iota(jnp.int32, (BW, C), 0) >> 4
    cg = lax.broadcasted_iota(jnp.int32, (BW, C), 1) >> 6
    mb = jnp.where(rg == cg, 1.0, 0.0)
    b_re = jnp.concatenate([bbr] * S5_G, axis=0) * mb
    b_im = jnp.concatenate([bbi] * S5_G, axis=0) * mb
    ecol = jnp.where(lax.broadcasted_iota(jnp.int32, (LANE, BW), 0)
                     == (lax.broadcasted_iota(jnp.int32, (LANE, BW), 1) & 15), 1.0, 0.0)
    mc = jnp.where((lax.broadcasted_iota(jnp.int32, (C, BW), 0) >> 6)
                   == (lax.broadcasted_iota(jnp.int32, (C, BW), 1) >> 4), 1.0, 0.0)
    c_top = xdot(ctr, ecol) * mc
    c_bot = -(xdot(cti, ecol) * mc)
    return lbr, lbi, b_re, b_im, c_top, c_bot


def s5_params(p):
    def body(lre, lim, ls, btr, bti, ctr, cti, lb_ref, bre_ref, bim_ref, ct_ref, cb_ref):
        lbr, lbi, b_re, b_im, c_top, c_bot = fn_s5params(
            lre[...], lim[...], ls[...], btr[...], bti[...], ctr[...], cti[...])
        lb_ref[0:1, :] = lbr
        lb_ref[1:2, :] = lbi
        bre_ref[...] = b_re.astype(BF16)
        bim_ref[...] = b_im.astype(BF16)
        ct_ref[...] = c_top.astype(BF16)
        cb_ref[...] = c_bot.astype(BF16)

    return pl.pallas_call(
        body, name="s5_params",
        out_shape=[jax.ShapeDtypeStruct((2, S5_C), F32), jax.ShapeDtypeStruct((BW, S5_C), BF16),
                   jax.ShapeDtypeStruct((BW, S5_C), BF16), jax.ShapeDtypeStruct((S5_C, BW), BF16),
                   jax.ShapeDtypeStruct((S5_C, BW), BF16)],
        compiler_params=pltpu.CompilerParams(vmem_limit_bytes=VMEM_LIMIT),
    )(p["lre"], p["lim"], p["lstep"], p["btr"], p["bti"], p["ctr"], p["cti"])


S5_PARAM_NAMES = ("lre", "lim", "lstep", "btr", "bti", "ctr", "cti")


def s5_params_vjp(p, dl_r, dl_i, db_re, db_im, dc_top, dc_bot):
    def body(lre, lim, ls, btr, bti, ctr, cti, dlr, dli, dbr, dbi, dct, dcb, *outs):
        args = [r[...] for r in (lre, lim, ls, btr, bti, ctr, cti)]
        _, vjp = jax.vjp(fn_s5params, *args)
        g = vjp((jnp.sum(dlr[...], axis=0, keepdims=True), jnp.sum(dli[...], axis=0, keepdims=True),
                 dbr[...], dbi[...], dct[...], dcb[...]))
        for o_ref, gv in zip(outs, g):
            o_ref[...] = gv

    return pl.pallas_call(
        body, name="s5_params_vjp",
        out_shape=[jax.ShapeDtypeStruct(p[n].shape, F32) for n in S5_PARAM_NAMES],
        compiler_params=pltpu.CompilerParams(vmem_limit_bytes=VMEM_LIMIT),
    )(*[p[n] for n in S5_PARAM_NAMES], dl_r, dl_i, db_re, db_im, dc_top, dc_bot)


def _attn_tile(q, k, v, cq, ckt, *, hpt, dk, q0, tile):
    tq, S = q.shape[0], k.shape[0]
    row = q0 + lax.broadcasted_iota(jnp.int32, (tq, S), 0)
    col = lax.broadcasted_iota(jnp.int32, (tq, S), 1)
    causal = row >= col
    lane = _lane()
    out = jnp.zeros((tq, LANE), F32)
    for h in range(hpt):
        if hpt > 1:
            mine = (lane >> int(math.log2(LANE // hpt))) == h
            qh = jnp.where(mine, q, 0.0)
        else:
            qh = q
        s = dot_nt(qh, k) * (dk ** -0.5)
        if cq is not None:
            head = tile * hpt + h
            cqh = jnp.sum(jnp.where(lane == F_LANE0 + head, cq, 0.0), axis=1, keepdims=True)
            sub = lax.broadcasted_iota(jnp.int32, (8, 1), 0)
            ckh = jnp.sum(jnp.where(sub == head, ckt, 0.0), axis=0, keepdims=True)
            s = s + (cqh - ckh)
        s = jnp.where(causal, s, NEG_INF)
        m = lax.stop_gradient(jnp.max(s, axis=-1, keepdims=True))
        e = jnp.exp(s - m)
        p = e / jnp.sum(e, axis=-1, keepdims=True)
        oh = dot_nn(p, v)
        out = out + (jnp.where(mine, oh, 0.0) if hpt > 1 else oh)
    return out


def attention(name, q, k, v, *, B, S, ntile, hpt, dk, qc=0, kc=0, vc=0, cum=None, ckt=None, do=None, tq=256):
    tq = _pick(S, (tq, 128))
    nq = S // tq
    M = B * S
    bias = cum is not None
    kw = dict(hpt=hpt, dk=dk)

    def load(refs, sk):
        q_ref, k_ref, v_ref = refs[:3]
        qv, kv, vv = q_ref[...].astype(F32), k_ref[0:sk, :].astype(F32), v_ref[0:sk, :].astype(F32)
        if bias:
            return qv, kv, vv, refs[3][...], refs[4][0, :, 0:sk]
        return qv, kv, vv, None, None

    nin = 5 if bias else 3

    def per_query_block(run):
        for g in range(nq):
            @pl.when(pl.program_id(2) == g)
            def _(g=g):
                run(g, (g + 1) * tq)

    def fwd_body(*refs):
        tile = pl.program_id(1)

        def run(g, sk):
            qv, kv, vv, cq, ck = load(refs, sk)
            o = _attn_tile(qv, kv, vv, cq, ck, q0=g * tq, tile=tile, **kw)
            refs[nin][...] = o.astype(refs[nin].dtype)

        per_query_block(run)

    def bwd_body(*refs):
        outs = refs[nin + 1:]
        tile = pl.program_id(1)

        @pl.when(pl.program_id(2) == 0)
        def _():
            for o_ref in (outs[1], outs[2]) + ((outs[4],) if bias else ()):
                o_ref[...] = jnp.zeros_like(o_ref)

        def run(g, sk):
            qv, kv, vv, cq, ck = load(refs, sk)
            dov = refs[nin][...].astype(F32)
            if bias:
                f = lambda a, b, c, d, e: _attn_tile(a, b, c, d, e, q0=g * tq, tile=tile, **kw)
                _, vjp = jax.vjp(f, qv, kv, vv, cq, ck)
            else:
                f = lambda a, b, c: _attn_tile(a, b, c, None, None, q0=g * tq, tile=tile, **kw)
                _, vjp = jax.vjp(f, qv, kv, vv)
            gr = vjp(dov)
            outs[0][...] = gr[0]
            outs[1][0:sk, :] += gr[1]
            outs[2][0:sk, :] += gr[2]
            if bias:
                outs[3][...] = gr[3]
                outs[4][0, :, 0:sk] += gr[4]

        per_query_block(run)

    qspec = lambda c: pl.BlockSpec((tq, LANE), lambda b, t, i: (b * nq + i, c + t))
    kspec = lambda c: pl.BlockSpec((S, LANE), lambda b, t, i: (b, c + t))
    in_specs, args = [qspec(qc), kspec(kc), kspec(vc)], [q, k, v]
    if bias:
        in_specs += [pl.BlockSpec((tq, LANE), lambda b, t, i: (b * nq + i, 0)),
                     pl.BlockSpec((1, 8, S), lambda b, t, i: (b, 0, 0))]
        args += [cum, ckt]
    if do is None:
        return pl.pallas_call(
            fwd_body, name=name, grid=(B, ntile, nq), in_specs=in_specs, out_specs=qspec(0),
            out_shape=jax.ShapeDtypeStruct((M, ntile * LANE), BF16),
            compiler_params=_cp("parallel", "parallel", "parallel"),
        )(*args)
    in_specs.append(qspec(0))
    args.append(do)
    out_specs = [qspec(0), kspec(0), kspec(0)]
    out_shape = [jax.ShapeDtypeStruct((M, ntile * LANE), F32)] * 3
    if bias:
        out_specs += [qspec(0), pl.BlockSpec((1, 8, S), lambda b, t, i: (b * ntile + t, 0, 0))]
        out_shape += [jax.ShapeDtypeStruct((M, ntile * LANE), F32),
                      jax.ShapeDtypeStruct((B * ntile, 8, S), F32)]
    return pl.pallas_call(
        bwd_body, name=name, grid=(B, ntile, nq), in_specs=in_specs, out_specs=out_specs,
        out_shape=out_shape, compiler_params=_cp("parallel", "parallel", "arbitrary"),
    )(*args)


def seq_cumsum(name, x, *, B, S, reverse):
    nb = S // LANE

    def body(x_ref, o_ref):
        r = lax.broadcasted_iota(jnp.int32, (LANE, LANE), 0)
        c = lax.broadcasted_iota(jnp.int32, (LANE, LANE), 1)
        tri = jnp.where((r <= c) if reverse else (r >= c), 1.0, 0.0)
        carry = jnp.zeros((1, LANE), F32)
        for blk in (range(nb - 1, -1, -1) if reverse else range(nb)):
            xb = x_ref[blk * LANE:(blk + 1) * LANE, :]
            o_ref[blk * LANE:(blk + 1) * LANE, :] = xdot(tri, xb) + carry
            carry = carry + jnp.sum(xb, axis=0, keepdims=True)

    return pl.pallas_call(
        body, name=name, grid=(B,), in_specs=[pl.BlockSpec((S, LANE), lambda b: (b, 0))],
        out_specs=pl.BlockSpec((S, LANE), lambda b: (b, 0)),
        out_shape=jax.ShapeDtypeStruct(x.shape, F32), compiler_params=_cp("parallel"),
    )(x)


SCAN_ROWS = 64


def _shift_rows(ref, r0, rows, d, up):
    if d % 8 == 0:
        return ref[pl.ds(r0 + d if up else r0 - d, rows), :]
    if up:
        win = ref[pl.ds(r0, rows + 8), :]
        return pltpu.roll(win, rows + 8 - d, 0)[0:rows, :]
    win = ref[pl.ds(r0 - 8, rows + 8), :]
    return pltpu.roll(win, d, 0)[8:rows + 8, :]


def s5_scan(name, x_re, x_im, lam, *, B, S, reverse, state=None):
    C = S5_C
    cw = LANE
    R = _pick(S, (SCAN_ROWS,))
    pad = max(S // 2, 8)
    nsteps = int(math.log2(S))
    assert 1 << nsteps == S
    base = 0 if reverse else pad
    with_grad = state is not None

    def body(*refs):
        if with_grad:
            xr, xi, lam_ref, sr, si, o_r, o_i, dl_r, dl_i, a_r, a_i, b_r, b_i = refs
        else:
            xr, xi, lam_ref, o_r, o_i, a_r, a_i, b_r, b_i = refs
        zero = jnp.zeros((pad, cw), F32)
        z0 = S if reverse else 0
        for buf in (a_r, a_i, b_r, b_i):
            buf[z0:z0 + pad, :] = zero
        a_r[base:base + S, :] = xr[...]
        a_i[base:base + S, :] = xi[...]
        mr = lam_ref[0:1, :]
        mi = -lam_ref[1:2, :] if reverse else lam_ref[1:2, :]
        src, dst = (a_r, a_i), (b_r, b_i)
        for step in range(nsteps):
            d = 1 << step
            last = step == nsteps - 1

            def chunk(c, _, src=src, dst=dst, d=d, last=last, mr=mr, mi=mi):
                r0 = pl.multiple_of(base + c * R, 8)
                pr = _shift_rows(src[0], r0, R, d, reverse)
                pi = _shift_rows(src[1], r0, R, d, reverse)
                nr = src[0][pl.ds(r0, R), :] + mr * pr - mi * pi
                ni = src[1][pl.ds(r0, R), :] + mr * pi + mi * pr
                if last:
                    o0 = pl.multiple_of(c * R, 8)
                    o_r[pl.ds(o0, R), :] = nr
                    o_i[pl.ds(o0, R), :] = ni
                else:
                    dst[0][pl.ds(r0, R), :] = nr
                    dst[1][pl.ds(r0, R), :] = ni
                return 0

            lax.fori_loop(0, S // R, chunk, 0)
            mr, mi = mr * mr - mi * mi, 2.0 * mr * mi
            src, dst = dst, src
        if with_grad:
            def fold(v):
                return jnp.sum(v.reshape(R // 8, 8, cw), axis=0)

            def accum(c, carry, first=False):
                r0 = 0 if first else pl.multiple_of(c * R, 8)
                gr, gi = o_r[pl.ds(r0, R), :], o_i[pl.ds(r0, R), :]
                if first:
                    keep = lax.broadcasted_iota(jnp.int32, (R, 1), 0) >= 1
                    pr = jnp.where(keep, pltpu.roll(sr[0:R, :], 1, 0), 0.0)
                    pi = jnp.where(keep, pltpu.roll(si[0:R, :], 1, 0), 0.0)
                else:
                    pr = _shift_rows(sr, r0, R, 1, False)
                    pi = _shift_rows(si, r0, R, 1, False)
                return (carry[0] + fold(gr * pr + gi * pi), carry[1] + fold(gi * pr - gr * pi))

            acc = accum(0, (jnp.zeros((8, cw), F32), jnp.zeros((8, cw), F32)), first=True)
            acc = lax.fori_loop(1, S // R, accum, acc)
            dl_r[...] = acc[0]
            dl_i[...] = acc[1]

    seq = pl.BlockSpec((S, cw), lambda b, j: (b, j))
    in_specs = [seq, seq, pl.BlockSpec((2, cw), lambda b, j: (0, j))]
    args = [x_re, x_im, lam]
    out_specs = [seq, seq]
    out_shape = [jax.ShapeDtypeStruct(x_re.shape, F32)] * 2
    if with_grad:
        in_specs += [seq, seq]
        args += list(state)
        out_specs += [pl.BlockSpec((8, cw), lambda b, j: (b, j))] * 2
        out_shape += [jax.ShapeDtypeStruct((B * 8, C), F32)] * 2
    return pl.pallas_call(
        body, name=name, grid=(B, C // cw), in_specs=in_specs, out_specs=out_specs, out_shape=out_shape,
        scratch_shapes=[pltpu.VMEM((S + pad, cw), F32)] * 4,
        compiler_params=_cp("parallel", "parallel"),
    )(*args)


CONV_CW = 256


def _conv_taps(ref, r0, rows, first):
    cur = ref[pl.ds(r0, rows), :]
    if first:
        row = lax.broadcasted_iota(jnp.int32, (rows, 1), 0)
        p1 = jnp.where(row >= 1, pltpu.roll(cur, 1, 0), 0.0)
        p2 = jnp.where(row >= 2, pltpu.roll(cur, 2, 0), 0.0)
    else:
        p1 = _shift_rows(ref, r0, rows, 1, False)
        p2 = _shift_rows(ref, r0, rows, 2, False)
    return cur, p1, p2


def _conv_apply(w_ref, taps):
    return w_ref[2:3, :] * taps[0] + w_ref[1:2, :] * taps[1] + w_ref[0:1, :] * taps[2]


def conv_gate_fwd(up, conv_w, *, B, S):
    M, F2 = up.shape
    F = F2 // 2
    cw = _pick(F, (CONV_CW, LANE))
    nf = F // cw
    R = _pick(S, (SCAN_ROWS,))

    def body(g_ref, v_ref, wg_ref, wv_ref, o_ref):
        def chunk(c, _, first=False):
            r0 = 0 if first else pl.multiple_of(c * R, 8)
            cg = _conv_apply(wg_ref, _conv_taps(g_ref, r0, R, first))
            cv = _conv_apply(wv_ref, _conv_taps(v_ref, r0, R, first))
            o_ref[pl.ds(r0, R), :] = (cg * _sigmoid(cg) * cv).astype(o_ref.dtype)
            return 0

        chunk(0, 0, first=True)
        lax.fori_loop(1, S // R, chunk, 0)

    seq = lambda off: pl.BlockSpec((S, cw), lambda b, j: (b, off + j))
    wsp = lambda off: pl.BlockSpec((3, cw), lambda b, j: (0, off + j))
    return pl.pallas_call(
        body, name="conv_gate", grid=(B, nf), in_specs=[seq(0), seq(nf), wsp(0), wsp(nf)],
        out_specs=seq(0), out_shape=jax.ShapeDtypeStruct((M, F), BF16),
        compiler_params=_cp("parallel", "parallel"),
    )(up, up, conv_w, conv_w)


def conv_gate_bwd(up, conv_w, dact, *, B, S):
    M, F2 = up.shape
    F = F2 // 2
    cw = _pick(F, (CONV_CW, LANE))
    nf = F // cw
    R = _pick(S, (SCAN_ROWS,))
    nchunk = S // R

    def body(s_ref, p_ref, ws_ref, wp_ref, da_ref, du_ref, dw_ref, dc_ref):
        is_gate = pl.program_id(0) < nf
        dc_ref[S:S + 8, :] = jnp.zeros((8, cw), F32)

        def fold(v):
            return jnp.sum(v.reshape(R // 8, 8, cw), axis=0)

        first_b = pl.program_id(1) == 0

        def pass1(gate_step):
            def chunk(c, acc, first=False):
                r0 = 0 if first else pl.multiple_of(c * R, 8)
                taps = _conv_taps(s_ref, r0, R, first)
                cp = _conv_apply(wp_ref, _conv_taps(p_ref, r0, R, first))
                da = da_ref[pl.ds(r0, R), :]
                if gate_step:
                    cs = _conv_apply(ws_ref, taps)
                    sg = _sigmoid(cs)
                    dc = da * cp * (sg * (1.0 + cs * (1.0 - sg)))
                else:
                    dc = da * (cp * _sigmoid(cp))
                dc_ref[pl.ds(r0, R), :] = dc
                return (acc[0] + fold(dc * taps[2]), acc[1] + fold(dc * taps[1]), acc[2] + fold(dc * taps[0]))

            z = jnp.zeros((8, cw), F32)
            acc = chunk(0, (z, z, z), first=True)
            acc = lax.fori_loop(1, nchunk, chunk, acc)
            for tap in range(3):
                tot = jnp.sum(acc[tap], axis=0, keepdims=True)

                @pl.when(first_b)
                def _(tap=tap, tot=tot):
                    dw_ref[tap:tap + 1, :] = tot

                @pl.when(jnp.logical_not(first_b))
                def _(tap=tap, tot=tot):
                    dw_ref[tap:tap + 1, :] += tot

        @pl.when(is_gate)
        def _():
            pass1(True)

        @pl.when(jnp.logical_not(is_gate))
        def _():
            pass1(False)

        def pass2(c, _):
            r0 = pl.multiple_of(c * R, 16)
            n0 = dc_ref[pl.ds(r0, R), :]
            n1 = _shift_rows(dc_ref, r0, R, 1, True)
            n2 = _shift_rows(dc_ref, r0, R, 2, True)
            du = ws_ref[2:3, :] * n0 + ws_ref[1:2, :] * n1 + ws_ref[0:1, :] * n2
            du_ref[pl.ds(r0, R), :] = du.astype(du_ref.dtype)
            return 0

        lax.fori_loop(0, nchunk, pass2, 0)

    n2 = 2 * nf
    seq = lambda f: pl.BlockSpec((S, cw), lambda j, b: (b, f(j)))
    wsp = lambda f: pl.BlockSpec((3, cw), lambda j, b: (0, f(j)))
    same, other, act_col = (lambda j: j), (lambda j: (j + nf) % n2), (lambda j: j % nf)
    return pl.pallas_call(
        body, name="conv_gate_vjp", grid=(n2, B),
        in_specs=[seq(same), seq(other), wsp(same), wsp(other), seq(act_col)],
        out_specs=[seq(same), wsp(same)],
        out_shape=[jax.ShapeDtypeStruct((M, F2), BF16), jax.ShapeDtypeStruct((3, F2), F32)],
        scratch_shapes=[pltpu.VMEM((S + 8, cw), F32)],
        compiler_params=_cp("parallel", "arbitrary"),
    )(up, up, conv_w, conv_w, dact)


def loss_head(y, target):
    M, D = y.shape
    tm = _pick(M, (256, 128, 64, 32, 16, 8))

    def body(y_ref, t_ref, dy_ref, l_ref):
        diff = y_ref[...] - t_ref[...]
        dy_ref[...] = diff * (1.0 / D)
        part = jnp.sum(jnp.sum(diff * diff, axis=1, keepdims=True), axis=0, keepdims=True)

        @pl.when(pl.program_id(0) == 0)
        def _():
            l_ref[...] = jnp.zeros_like(l_ref)

        l_ref[...] += part

    row = pl.BlockSpec((tm, D), lambda i: (i, 0))
    return pl.pallas_call(
        body, name="loss_head", grid=(M // tm,), in_specs=[row, row],
        out_specs=[row, pl.BlockSpec((8, LANE), lambda i: (0, 0))],
        out_shape=[jax.ShapeDtypeStruct((M, D), F32), jax.ShapeDtypeStruct((8, LANE), F32)],
        compiler_params=_cp("arbitrary"),
    )(y, target)


def adamw(name, w, g, m, v):
    R, C = w.shape
    tr = _pick(R, (256, 128, 64, 32, 16, 8))

    def body(w_ref, g_ref, m_ref, v_ref, d_ref, nm_ref, nv_ref):
        gv = g_ref[...]
        nm = ADAM_B1 * m_ref[...] + (1.0 - ADAM_B1) * gv
        nv = ADAM_B2 * v_ref[...] + (1.0 - ADAM_B2) * (gv * gv)
        m_hat = nm / (1.0 - ADAM_B1 ** ADAM_STEP)
        v_hat = nv / (1.0 - ADAM_B2 ** ADAM_STEP)
        d_ref[...] = -ADAM_LR * (m_hat / (jnp.sqrt(v_hat) + ADAM_EPS) + ADAM_WD * w_ref[...])
        nm_ref[...] = nm
        nv_ref[...] = nv

    blk = pl.BlockSpec((tr, C), lambda i: (i, 0))
    return pl.pallas_call(
        body, name=name, grid=(R // tr,), in_specs=[blk] * 4, out_specs=[blk] * 3,
        out_shape=[jax.ShapeDtypeStruct((R, C), F32)] * 3, compiler_params=_cp("parallel"),
    )(w, g, m, v)


def _seg(D):
    o = 3 * D
    return dict(ckv=o, fq=o + 256, fk=o + 512, fv=o + 768, u=o + 1024, small=o + 1280, cq=o + 1536, P=o + 1920)


def _pad_last(a, n):
    return jnp.pad(a, [(0, 0)] * (a.ndim - 1) + [(0, n - a.shape[-1])])


def _place(a, lo, n=LANE):
    return jnp.pad(a, [(0, 0)] * (a.ndim - 1) + [(lo, n - lo - a.shape[-1])])


def prep_weights(w):
    L, D = w["attn_norm_g"].shape
    win = w["w_in"]
    z = lambda n: jnp.zeros((L, D, n), win.dtype)
    g0 = 1700
    wp = jnp.concatenate(
        [win[..., g0:g0 + 3 * D], win[..., 384:640], win[..., 672:928], win[..., 928:1184], win[..., 1184:1440],
         win[..., 1444:1700], z(NOPE), win[..., 640:672], win[..., 1440:1444], z(LANE - F_LANE0 - N_HEADS), z(LANE),
         win[..., 0:384]], axis=-1)
    wukv, wb = w["w_ukv"], w["w_branch"]
    row3 = lambda a: a[:, None, :]
    return dict(
        g1=row3(w["attn_norm_g"]), Wp=wp, gql=row3(w["q_lat_norm_g"]), gkvl=row3(w["kv_lat_norm_g"]),
        Wuq=_pad_last(w["w_uq"], LANE).reshape(L, Q_RANK, N_HEADS * LANE),
        Wk=_pad_last(wukv[..., :NOPE], LANE).reshape(L, KV_RANK, N_HEADS * LANE),
        Wv=_pad_last(wukv[..., NOPE:], LANE).reshape(L, KV_RANK, N_HEADS * LANE),
        gq=row3(_pad_last(w["mla_q_norm_g"], LANE)), gk=row3(_pad_last(w["mla_k_norm_g"], LANE)),
        gfq=row3(jnp.tile(w["fox_q_norm_g"], (1, 2))), gfk=row3(jnp.tile(w["fox_k_norm_g"], (1, 2))),
        fbias=row3(_place(w["fox_f_bias"], F_LANE0)),
        lre=w["s5_lambda_re"].reshape(L, 1, S5_C), lim=w["s5_lambda_im"].reshape(L, 1, S5_C),
        lstep=row3(_pad_last(w["s5_log_step"], LANE)),
        btr=jnp.transpose(w["s5_b_re"], (0, 3, 1, 2)).reshape(L, S5_H, S5_C),
        bti=jnp.transpose(w["s5_b_im"], (0, 3, 1, 2)).reshape(L, S5_H, S5_C),
        ctr=_pad_last(jnp.transpose(w["s5_c_re"], (0, 1, 3, 2)).reshape(L, S5_C, S5_H), LANE),
        cti=_pad_last(jnp.transpose(w["s5_c_im"], (0, 1, 3, 2)).reshape(L, S5_C, S5_H), LANE),
        s5d=w["s5_d"].reshape(L, 1, BW), Wglu=w["s5_w_glu"], bglu=row3(w["s5_b_glu"]),
        Wb0=jnp.pad(wb[:, 0].reshape(L, N_HEADS, V_DIM, D), ((0, 0), (0, 0), (0, LANE - V_DIM), (0, 0))
                    ).reshape(L, N_HEADS * LANE, D),
        Wb1=wb[:, 1], Wb2=wb[:, 2], Wout=w["w_out"], g2=row3(w["ffn_norm_g"]), Wup=w["w_up"],
        convw=w["ffn_conv_w"], Wdown=w["w_down"],
    )


BIG_KEYS = ("Wp", "Wuq", "Wk", "Wv", "Wout", "Wup", "Wdown")


def with_transposes(P):
    out = dict(P)
    for k in BIG_KEYS:
        out[k + "T"] = jnp.swapaxes(P[k], 1, 2)
    return out


def layer_params(P, l):
    return {k: (v if k in BIG_KEYS or k[:-1] in BIG_KEYS else v[l]) for k, v in P.items()}


def unprep_grads(G, D):
    L = G["g1"].shape[0]
    s = _seg(D)
    dwp = G["Wp"]
    sm = s["small"]
    w_in = jnp.concatenate(
        [dwp[..., s["cq"]:s["cq"] + 384], dwp[..., s["ckv"]:s["ckv"] + 256], dwp[..., sm + NOPE:sm + NOPE + ROPE],
         dwp[..., s["fq"]:s["fq"] + 768], dwp[..., sm + F_LANE0:sm + F_LANE0 + N_HEADS],
         dwp[..., s["u"]:s["u"] + 256], dwp[..., 0:3 * D]], axis=-1)
    heads = lambda a, rows, keep: a.reshape(L, rows, N_HEADS, LANE)[..., :keep]
    wb0 = G["Wb0"].reshape(L, N_HEADS, LANE, D)[:, :, :V_DIM].reshape(L, BW, D)
    gf = lambda a: a[:, 0, :FOX_DIM] + a[:, 0, FOX_DIM:]
    return dict(
        attn_norm_g=G["g1"][:, 0], w_in=w_in, q_lat_norm_g=G["gql"][:, 0], w_uq=heads(G["Wuq"], Q_RANK, QK_DIM),
        kv_lat_norm_g=G["gkvl"][:, 0],
        w_ukv=jnp.concatenate([heads(G["Wk"], KV_RANK, NOPE), heads(G["Wv"], KV_RANK, V_DIM)], axis=-1),
        mla_q_norm_g=G["gq"][:, 0, :QK_DIM], mla_k_norm_g=G["gk"][:, 0, :QK_DIM],
        fox_q_norm_g=gf(G["gfq"]), fox_k_norm_g=gf(G["gfk"]),
        fox_f_bias=G["fbias"][:, 0, F_LANE0:F_LANE0 + N_HEADS],
        s5_lambda_re=G["lre"].reshape(L, S5_G, S5_P), s5_lambda_im=G["lim"].reshape(L, S5_G, S5_P),
        s5_b_re=jnp.transpose(G["btr"].reshape(L, S5_H, S5_G, S5_P), (0, 2, 3, 1)),
        s5_b_im=jnp.transpose(G["bti"].reshape(L, S5_H, S5_G, S5_P), (0, 2, 3, 1)),
        s5_c_re=jnp.transpose(G["ctr"][..., :S5_H].reshape(L, S5_G, S5_P, S5_H), (0, 1, 3, 2)),
        s5_c_im=jnp.transpose(G["cti"][..., :S5_H].reshape(L, S5_G, S5_P, S5_H), (0, 1, 3, 2)),
        s5_d=G["s5d"].reshape(L, S5_G, S5_H), s5_log_step=G["lstep"][:, 0, :S5_G],
        s5_w_glu=G["Wglu"], s5_b_glu=G["bglu"][:, 0],
        w_branch=jnp.stack([wb0, G["Wb1"], G["Wb2"]], axis=1), w_out=G["Wout"], ffn_norm_g=G["g2"][:, 0],
        w_up=G["Wup"], ffn_conv_w=G["convw"], w_down=G["Wdown"],
    )


def rope_tables(positions):
    inv_freq = ROPE_THETA ** (-jnp.arange(0, ROPE, 2, dtype=F32) / ROPE)
    ang = positions.astype(F32)[..., None] * inv_freq
    cos, sin = jnp.cos(ang), jnp.sin(ang)
    ones = jnp.ones(ang.shape[:-1] + (NOPE,), F32)
    zeros = jnp.zeros(ang.shape[:-1] + (LANE - NOPE - ROPE,), F32)
    cos_t = jnp.concatenate([ones, cos, cos, zeros], axis=-1)
    sin_t = jnp.concatenate([0.0 * ones, sin, sin, zeros], axis=-1)
    return cos_t.reshape(-1, LANE), sin_t.reshape(-1, LANE)


def fn_rms_res(n):
    return lambda x, g: (_rms(x, g, n), x)


def _s5_mats(p):
    return s5_params({k: p[k] for k in S5_PARAM_NAMES})


def layer_fwd(x, p, l, cos_t, sin_t, B, S):
    M, D = x.shape
    s = _seg(D)
    sm = s["small"] // LANE
    head = lambda j: j
    h = rowwise("rms_attn", fn_rms(D), [(x, D, 0)], [p["g1"]], [(D, BF16)])[0]
    proj = mm(h, p["Wp"], "nn", name="in_proj", layer=l)
    cnq = rowwise("latq_norm", fn_rms(Q_RANK), [(proj, Q_RANK, s["cq"] // Q_RANK)], [p["gql"]], [(Q_RANK, BF16)])[0]
    cnkv = rowwise("latkv_norm", fn_rms(KV_RANK), [(proj, KV_RANK, s["ckv"] // KV_RANK)], [p["gkvl"]],
                   [(KV_RANK, BF16)])[0]
    qraw = mm(cnq, p["Wuq"], "nn", name="q_up", layer=l)
    kn = mm(cnkv, p["Wk"], "nn", name="k_up", layer=l)
    v5 = mm(cnkv, p["Wv"], "nn", name="v_up", out_dtype=BF16, layer=l)
    qrot = rowwise("q_post", fn_qpost, [(qraw, LANE, head), (cos_t, LANE, 0), (sin_t, LANE, 0)], [p["gq"]],
                   [(LANE, BF16)], nj=N_HEADS)[0]
    krot = rowwise("k_post", fn_kpost, [(kn, LANE, head), (proj, LANE, sm), (cos_t, LANE, 0), (sin_t, LANE, 0)],
                   [p["gk"]], [(LANE, BF16)], nj=N_HEADS)[0]
    omla = attention("mla_attn", qrot, krot, v5, B=B, S=S, ntile=N_HEADS, hpt=1, dk=QK_DIM)
    fq0, fk0 = s["fq"] // LANE, s["fk"] // LANE
    qf = rowwise("foxq_norm", fn_foxnorm, [(proj, LANE, lambda j: fq0 + j)], [p["gfq"]], [(LANE, BF16)], nj=2)[0]
    kf = rowwise("foxk_norm", fn_foxnorm, [(proj, LANE, lambda j: fk0 + j)], [p["gfk"]], [(LANE, BF16)], nj=2)[0]
    lf = rowwise("fgate", fn_fgate, [(proj, LANE, sm)], [p["fbias"]], [(LANE, F32)])[0]
    cum = seq_cumsum("fox_cumsum", lf, B=B, S=S, reverse=False)
    ckt = _pad_rows8(jnp.transpose(cum.reshape(B, S, LANE)[:, :, F_LANE0:F_LANE0 + N_HEADS], (0, 2, 1)))
    ofox = attention("fox_attn", qf, kf, proj, B=B, S=S, ntile=2, hpt=2, dk=FOX_DIM, vc=s["fv"] // LANE,
                     cum=cum, ckt=ckt)
    lam, bre, bim, ctop, cbot = _s5_mats(p)
    u16 = proj[:, s["u"]:s["u"] + BW].astype(BF16)
    bur = mm(u16, bre, "nn", name="s5_bu_re")
    bui = mm(u16, bim, "nn", name="s5_bu_im")
    sr, si = s5_scan("s5_scan", bur, bui, lam, B=B, S=S, reverse=False)
    ypre = mm(si, cbot, "nn", name="s5_y_im", add=mm(sr, ctop, "nn", name="s5_y_re"))
    os5 = rowwise("s5_post", fn_s5post, [(ypre, BW, 0), (proj, BW, s["u"] // BW)],
                  [p["s5d"], p["Wglu"], p["bglu"]], [(BW, BF16)])[0]
    merged = rowwise("merge", fn_merge,
                     [(omla, N_HEADS * LANE, 0), (ofox, BW, 0), (os5, BW, 0), (proj, D, 0), (proj, D, 1), (proj, D, 2)],
                     [p["Wb0"], p["Wb1"], p["Wb2"]], [(D, BF16)])[0]
    xmid = mm(merged, p["Wout"], "nn", name="out_proj", add=x, layer=l)
    h2 = rowwise("rms_ffn", fn_rms(D), [(xmid, D, 0)], [p["g2"]], [(D, BF16)])[0]
    up = mm(h2, p["Wup"], "nn", name="ffn_up", layer=l)
    act = conv_gate_fwd(up, p["convw"], B=B, S=S)
    xout = mm(act, p["Wdown"], "nn", name="ffn_down", add=xmid, layer=l)
    saved = dict(x=x, h=h, proj=proj, cnq=cnq, cnkv=cnkv, qraw=qraw, kn=kn, v5=v5, qrot=qrot, krot=krot, qf=qf,
                 kf=kf, cum=cum, ckt=ckt, omla=omla, ofox=ofox, os5=os5, u16=u16, sr=sr, si=si, ypre=ypre,
                 merged=merged, xmid=xmid, h2=h2, up=up, act=act)
    return xout, saved


def _pad_rows8(a):
    return jnp.pad(a, ((0, 0), (0, 8 - a.shape[1]), (0, 0)))


STACKED = ("Wp", "Wout", "Wup", "Wdown")


def layer_bwd(dx, p, l, sv, cos_t, sin_t, B, S, stacks):
    M, D = dx.shape
    slot = lambda k: (stacks["L"], l, stacks.get(k))
    s = _seg(D)
    sm = s["small"] // LANE
    head = lambda j: j
    proj = sv["proj"]
    dact = mm(dx, p["WdownT"], "nn", name="ffn_down_dx", layer=l)
    d_wdown = mm(sv["act"], dx, "tn", name="ffn_down_dw", slot=slot("Wdown"))
    dup, d_convw = conv_gate_bwd(sv["up"], p["convw"], dact, B=B, S=S)
    dh2 = mm(dup, p["WupT"], "nn", name="ffn_up_dx", layer=l)
    d_wup = mm(sv["h2"], dup, "tn", name="ffn_up_dw", slot=slot("Wup"))
    (dxmid,), (d_g2,) = rowwise_vjp("rms_ffn_vjp", fn_rms_res(D), [(sv["xmid"], D, 0)], [p["g2"]],
                                    [(dh2, D, 0), (dx, D, 0)], [0])
    dmerged = mm(dxmid, p["WoutT"], "nn", name="out_proj_dx", layer=l)
    d_wout = mm(sv["merged"], dxmid, "tn", name="out_proj_dw", slot=slot("Wout"))
    (dom, dof, dos, dg0, dg1, dg2), (d_wb0, d_wb1, d_wb2) = rowwise_vjp(
        "merge_vjp", fn_merge,
        [(sv["omla"], N_HEADS * LANE, 0), (sv["ofox"], BW, 0), (sv["os5"], BW, 0), (proj, D, 0), (proj, D, 1),
         (proj, D, 2)], [p["Wb0"], p["Wb1"], p["Wb2"]], [(dmerged, D, 0)], [0, 1, 2, 3, 4, 5], tm=128,
        gdt=[BF16, BF16, F32, BF16, BF16, BF16])
    lam, bre, bim, ctop, cbot = _s5_mats(p)
    (dypre, du_a), (d_s5d, d_wglu, d_bglu) = rowwise_vjp(
        "s5_post_vjp", fn_s5post, [(sv["ypre"], BW, 0), (proj, BW, s["u"] // BW)],
        [p["s5d"], p["Wglu"], p["bglu"]], [(dos, BW, 0)], [0, 1], gdt=[BF16, F32])
    dsr = mm(dypre, ctop.T, "nn", name="s5_y_re_dx")
    dsi = mm(dypre, cbot.T, "nn", name="s5_y_im_dx")
    d_ctop = mm(sv["sr"], dypre, "tn", name="s5_y_re_dw")
    d_cbot = mm(sv["si"], dypre, "tn", name="s5_y_im_dw")
    gr, gi, dl_r, dl_i = s5_scan("s5_scan_vjp", dsr, dsi, lam, B=B, S=S, reverse=True, state=(sv["sr"], sv["si"]))
    du = mm(gi, bim.T, "nn", name="s5_bu_im_dx", out_dtype=BF16,
            add=mm(gr, bre.T, "nn", name="s5_bu_re_dx", add=du_a))
    d_bre = mm(sv["u16"], gr, "tn", name="s5_bu_re_dw")
    d_bim = mm(sv["u16"], gi, "tn", name="s5_bu_im_dw")
    d_s5 = s5_params_vjp({k: p[k] for k in S5_PARAM_NAMES}, dl_r, dl_i, d_bre, d_bim, d_ctop, d_cbot)
    dqf, dkf, dfv, dcq_t, dckt_t = attention("fox_attn_vjp", sv["qf"], sv["kf"], proj, B=B, S=S, ntile=2, hpt=2,
                                             dk=FOX_DIM, vc=s["fv"] // LANE, cum=sv["cum"], ckt=sv["ckt"], do=dof)
    dck = dckt_t.reshape(B, 2, 8, S)
    dck = jnp.transpose(dck[:, 0, :N_HEADS] + dck[:, 1, :N_HEADS], (0, 2, 1)).reshape(M, N_HEADS)
    dcum = rowwise("fox_dcum", lambda a, b: (a[:, 0:LANE] + a[:, LANE:2 * LANE] + b,),
                   [(dcq_t, 2 * LANE, 0), (_place(dck, F_LANE0), LANE, 0)], [], [(LANE, F32)])[0]
    dlf = seq_cumsum("fox_cumsum_vjp", dcum, B=B, S=S, reverse=True)
    (dsmall_f,), (d_fbias,) = rowwise_vjp("fgate_vjp", fn_fgate, [(proj, LANE, sm)], [p["fbias"]],
                                          [(dlf, LANE, 0)], [0])
    fq0, fk0 = s["fq"] // LANE, s["fk"] // LANE
    (dfq,), (d_gfq,) = rowwise_vjp("foxq_norm_vjp", fn_foxnorm, [(proj, LANE, lambda j: fq0 + j)], [p["gfq"]],
                                   [(dqf, LANE, head)], [0], nj=2, gdt=[BF16])
    (dfk,), (d_gfk,) = rowwise_vjp("foxk_norm_vjp", fn_foxnorm, [(proj, LANE, lambda j: fk0 + j)], [p["gfk"]],
                                   [(dkf, LANE, head)], [0], nj=2, gdt=[BF16])
    dqrot, dkrot, dv5 = attention("mla_attn_vjp", sv["qrot"], sv["krot"], sv["v5"], B=B, S=S, ntile=N_HEADS,
                                  hpt=1, dk=QK_DIM, do=dom)
    (dqraw,), (d_gq,) = rowwise_vjp("q_post_vjp", fn_qpost,
                                    [(sv["qraw"], LANE, head), (cos_t, LANE, 0), (sin_t, LANE, 0)], [p["gq"]],
                                    [(dqrot, LANE, head)], [0], nj=N_HEADS, gdt=[BF16])
    (dkn, dsmall_k), (d_gk,) = rowwise_vjp(
        "k_post_vjp", fn_kpost, [(sv["kn"], LANE, head), (proj, LANE, sm), (cos_t, LANE, 0), (sin_t, LANE, 0)],
        [p["gk"]], [(dkrot, LANE, head)], [0, 1], nj=N_HEADS, gdt=[BF16, F32])
    dsmall = rowwise("small_sum", fn_add5, [(dsmall_k, N_HEADS * LANE, 0), (dsmall_f, LANE, 0)], [], [(LANE, BF16)])[0]
    dcnq = mm(dqraw, p["WuqT"], "nn", name="q_up_dx", layer=l)
    d_wuq = mm(sv["cnq"], dqraw, "tn", name="q_up_dw")
    dcnkv = mm(dv5, p["WvT"], "nn", name="v_up_dx", layer=l, add=mm(dkn, p["WkT"], "nn", name="k_up_dx", layer=l))
    d_wk = mm(sv["cnkv"], dkn, "tn", name="k_up_dw")
    d_wv = mm(sv["cnkv"], dv5, "tn", name="v_up_dw")
    (dcq,), (d_gql,) = rowwise_vjp("latq_norm_vjp", fn_rms(Q_RANK), [(proj, Q_RANK, s["cq"] // Q_RANK)], [p["gql"]],
                                   [(dcnq, Q_RANK, 0)], [0], gdt=[BF16])
    (dckv,), (d_gkvl,) = rowwise_vjp("latkv_norm_vjp", fn_rms(KV_RANK), [(proj, KV_RANK, s["ckv"] // KV_RANK)],
                                     [p["gkvl"]], [(dcnkv, KV_RANK, 0)], [0], gdt=[BF16])
    dproj = jnp.concatenate([dg0, dg1, dg2, dckv, dfq, dfk, dfv.astype(BF16), du, dsmall,
                             jnp.zeros((M, LANE), BF16), dcq], axis=1)
    dh = mm(dproj, p["WpT"], "nn", name="in_proj_dx", layer=l)
    d_wp = mm(sv["h"], dproj, "tn", name="in_proj_dw", slot=slot("Wp"))
    (dxin,), (d_g1,) = rowwise_vjp("rms_attn_vjp", fn_rms_res(D), [(sv["x"], D, 0)], [p["g1"]],
                                   [(dh, D, 0), (dxmid, D, 0)], [0])
    grads = dict(g1=d_g1, Wp=d_wp, gql=d_gql, gkvl=d_gkvl, Wuq=d_wuq, Wk=d_wk, Wv=d_wv, gq=d_gq, gk=d_gk,
                 gfq=d_gfq, gfk=d_gfk, fbias=d_fbias, s5d=d_s5d, Wglu=d_wglu, bglu=d_bglu, Wb0=d_wb0, Wb1=d_wb1,
                 Wb2=d_wb2, Wout=d_wout, g2=d_g2, Wup=d_wup, convw=d_convw, Wdown=d_wdown)
    grads.update(dict(zip(S5_PARAM_NAMES, d_s5)))
    return dxin, grads


def local_step(x, positions, target, w):
    B, S, D = x.shape
    M = B * S
    P = with_transposes(prep_weights(w))
    L = P["g1"].shape[0]
    cos_t, sin_t = rope_tables(positions)
    xc, saved = x.reshape(M, D), []
    for l in range(L):
        xc, sv = layer_fwd(xc, layer_params(P, l), l, cos_t, sin_t, B, S)
        saved.append(sv)
    dxc, sq = loss_head(xc, target.reshape(M, D))
    grads, stacks = [None] * L, {"L": L}
    for l in reversed(range(L)):
        dxc, grads[l] = layer_bwd(dxc, layer_params(P, l), l, saved[l], cos_t, sin_t, B, S, stacks)
        stacks.update({k: grads[l][k] for k in STACKED})
    G = {k: (stacks[k] if k in STACKED else jnp.stack([g[k] for g in grads])) for k in grads[0]}
    return sq, dxc.reshape(B, S, D), unprep_grads(G, D)


MESH = pl.DeviceIdType.MESH
ANY = pl.BlockSpec(memory_space=pl.ANY)
N_CHIPS = 4
SHARDED = ("w_in", "w_uq", "w_ukv", "s5_w_glu", "w_branch", "w_out", "w_up", "ffn_conv_w", "w_down")
MINOR = ("w_branch", "w_up", "ffn_conv_w")
F32_TRAVEL = ("ffn_conv_w",)
WEIGHTS = ("attn_norm_g", "w_in", "q_lat_norm_g", "w_uq", "kv_lat_norm_g", "w_ukv", "mla_q_norm_g", "mla_k_norm_g",
           "fox_q_norm_g", "fox_k_norm_g", "fox_f_bias", "s5_lambda_re", "s5_lambda_im", "s5_b_re", "s5_b_im",
           "s5_c_re", "s5_c_im", "s5_d", "s5_log_step", "s5_w_glu", "s5_b_glu", "w_branch", "w_out", "ffn_norm_g",
           "w_up", "ffn_conv_w", "w_down")
SMALL = tuple(n for n in WEIGHTS if n not in SHARDED)


def shard3(name, a):
    L = a.shape[0]
    if name in ("w_uq", "w_ukv"):
        return a.reshape(L, a.shape[1], -1)
    if name == "w_branch":
        return a.reshape(L, -1, a.shape[-1])
    return a


def full4(name, a):
    L = a.shape[0]
    if name == "w_in":
        return jnp.transpose(a.reshape(L, a.shape[1], N_CHIPS, -1), (0, 2, 1, 3))
    if name in MINOR:
        return a.reshape(L, 1, -1, a.shape[-1])
    a = a.reshape(L, a.shape[1], -1)
    return a.reshape(L, N_CHIPS, a.shape[1] // N_CHIPS, a.shape[2])


def from_full4(name, a, ref_tail):
    L = a.shape[0]
    if name == "w_in":
        a = jnp.transpose(a, (0, 2, 1, 3))
    return a.reshape((L,) + tuple(ref_tail))


def _where():
    x, y, c = lax.axis_index("x"), lax.axis_index("y"), lax.axis_index("c")
    chips = [(1 - x, y), (x, 1 - y), (1 - x, 1 - y)]
    return (x, y, c), 2 * x + y, (x, y, 1 - c), chips, [2 * cx + cy for cx, cy in chips]


def _view(minor, ref4, layers, k):
    if minor:
        cs = ref4.shape[3] // N_CHIPS
        return ref4.at[layers, 0, :, pl.ds(pl.multiple_of(k * cs, LANE), cs)]
    return ref4.at[layers, k]


def _remote(src, dst, ssem, rsem, dev):
    return pltpu.make_async_remote_copy(src_ref=src, dst_ref=dst, send_sem=ssem, recv_sem=rsem,
                                        device_id=dev, device_id_type=MESH)


def gather_weights(shards, minor):
    n = len(shards)
    L = shards[0].shape[0]
    Lh = L // 2
    out_shape = []
    for a, mn in zip(shards, minor):
        _, r, cs = a.shape
        out_shape.append(jax.ShapeDtypeStruct((L, 1, r, N_CHIPS * cs) if mn else (L, N_CHIPS, r, cs), a.dtype))

    def body(*refs):
        w, g = refs[:n], refs[n:2 * n]
        send, recv = refs[2 * n:]
        (x, y, c), me, sib, chips, cidx = _where()
        mine, other, every = pl.ds(c * Lh, Lh), pl.ds((1 - c) * Lh, Lh), pl.ds(0, L)
        dst = lambda i, layers, k: _view(minor[i], g[i], layers, k)
        local = [_remote(w[i], dst(i, every, me), send.at[i, 6], recv.at[i, 6], sib) for i in range(n)]
        first = [_remote(w[i].at[mine], dst(i, mine, me), send.at[i, j], recv.at[i, j], (*chips[j], c))
                 for i in range(n) for j in range(3)]
        for cp in local + first:
            cp.start()
        passed = []
        for i in range(n):
            for j in range(3):
                blk = dst(i, mine, cidx[j])
                _remote(blk, blk, send.at[i, j], recv.at[i, j], (*chips[j], c)).wait_recv()
                fwd = _remote(blk, blk, send.at[i, 3 + j], recv.at[i, 3 + j], sib)
                fwd.start()
                passed.append(fwd)
        for i in range(n):
            for j in range(3):
                blk = dst(i, other, cidx[j])
                _remote(blk, blk, send.at[i, 3 + j], recv.at[i, 3 + j], sib).wait_recv()
        for cp in first + passed:
            cp.wait_send()
        for cp in local:
            cp.wait()

    return pl.pallas_call(
        body, name="gather_weights", in_specs=[ANY] * n, out_specs=[ANY] * n, out_shape=out_shape,
        scratch_shapes=[pltpu.SemaphoreType.DMA((n, 7)), pltpu.SemaphoreType.DMA((n, 7))],
    )(*shards)


def sibling_halves(grads):
    n = len(grads)
    L = grads[0].shape[0]
    Lh = L // 2
    half = [jax.ShapeDtypeStruct((Lh,) + a.shape[1:], a.dtype) for a in grads]

    def body(*refs):
        g, got = refs[:n], refs[n:2 * n]
        send, recv = refs[2 * n:]
        (x, y, c), me, sib, chips, cidx = _where()
        other = pl.ds((1 - c) * Lh, Lh)
        out = [_remote(g[i].at[other], got[i], send.at[i], recv.at[i], sib) for i in range(n)]
        for cp in out:
            cp.start()
        for cp in out:
            cp.wait()

    got = pl.pallas_call(
        body, name="grad_sibling_halves", in_specs=[ANY] * n, out_specs=[ANY] * n, out_shape=half,
        scratch_shapes=[pltpu.SemaphoreType.DMA((n,)), pltpu.SemaphoreType.DMA((n,))],
    )(*grads)
    c = lax.axis_index("c")
    own = [lax.dynamic_slice_in_dim(a, c * Lh, Lh, axis=0) for a in grads]
    return own, got


def scatter_chip_sums(sums, travel, minor):
    n = len(sums)
    Lh = sums[0].shape[0]
    got_shape = []
    for a, t, mn in zip(sums, travel, minor):
        r, cs = a.shape[2], (a.shape[3] // N_CHIPS if mn else a.shape[3])
        got_shape.append(jax.ShapeDtypeStruct((3, Lh, r, cs), t.dtype))

    def body(*refs):
        s16, got = refs[:n], refs[n:2 * n]
        send, recv = refs[2 * n:]
        (x, y, c), me, sib, chips, cidx = _where()
        every = pl.ds(0, Lh)
        out = [_remote(_view(minor[i], s16[i], every, cidx[j]), got[i].at[j], send.at[i, j], recv.at[i, j],
                       (*chips[j], c)) for i in range(n) for j in range(3)]
        for cp in out:
            cp.start()
        for cp in out:
            cp.wait()

    got = pl.pallas_call(
        body, name="grad_scatter", in_specs=[ANY] * n, out_specs=[ANY] * n, out_shape=got_shape,
        scratch_shapes=[pltpu.SemaphoreType.DMA((n, 3)), pltpu.SemaphoreType.DMA((n, 3))],
    )(*travel)
    me = 2 * lax.axis_index("x") + lax.axis_index("y")
    mine = []
    for a, mn in zip(sums, minor):
        if mn:
            cs = a.shape[3] // N_CHIPS
            mine.append(lax.dynamic_slice_in_dim(a[:, 0], me * cs, cs, axis=2))
        else:
            mine.append(lax.dynamic_index_in_dim(a, me, axis=1, keepdims=False))
    return mine, got


def share_halves(halves):
    n = len(halves)
    Lh = halves[0].shape[0]
    out_shape = [jax.ShapeDtypeStruct(a.shape, a.dtype) for a in halves]

    def body(*refs):
        h, got = refs[:n], refs[n:2 * n]
        send, recv = refs[2 * n:]
        (x, y, c), me, sib, chips, cidx = _where()
        out = [_remote(h[i], got[i], send.at[i], recv.at[i], sib) for i in range(n)]
        for cp in out:
            cp.start()
        for cp in out:
            cp.wait()

    got = pl.pallas_call(
        body, name="grad_share_halves", in_specs=[ANY] * n, out_specs=[ANY] * n, out_shape=out_shape,
        scratch_shapes=[pltpu.SemaphoreType.DMA((n,)), pltpu.SemaphoreType.DMA((n,))],
    )(*halves)
    south = lax.axis_index("c") == 0
    return [jnp.where(south, jnp.concatenate([a, b]), jnp.concatenate([b, a])) for a, b in zip(halves, got)]


N_DEV = 8


def allreduce_small(v):
    R = v.shape[0]

    def body(x_ref, sum_ref, all_ref, send, recv, loc):
        (x, y, c), me, sib, chips, cidx = _where()

        def rows(px, py, pc):
            return all_ref.at[4 * px + 2 * py + pc]

        def copy(k, block, to, src=None):
            return _remote(rows(*block) if src is None else src, rows(*block), send.at[k], recv.at[k], to)

        mine = pltpu.make_async_copy(x_ref, rows(x, y, c), loc)
        mine.start()
        first = [copy(0, (x, y, c), sib, src=x_ref)]
        first += [copy(1 + j, (x, y, c), (*chip, c), src=x_ref) for j, chip in enumerate(chips)]
        for cp in first:
            cp.start()
        passed = [copy(4 + j, (*chip, c), sib) for j, chip in enumerate(chips)]
        for j, chip in enumerate(chips):
            copy(1 + j, (*chip, c), (x, y, c)).wait_recv()
            passed[j].start()
        copy(0, (x, y, 1 - c), (x, y, c)).wait_recv()
        for j, chip in enumerate(chips):
            copy(4 + j, (*chip, 1 - c), (x, y, c)).wait_recv()
        for cp in first + passed:
            cp.wait_send()
        mine.wait()
        acc = all_ref[0]
        for d in range(1, N_DEV):
            acc = acc + all_ref[d]
        sum_ref[...] = acc

    vm = pl.BlockSpec(memory_space=pltpu.VMEM)
    return pl.pallas_call(
        body, name="allreduce_small", in_specs=[vm], out_specs=[vm, vm],
        out_shape=[jax.ShapeDtypeStruct((R, LANE), F32), jax.ShapeDtypeStruct((N_DEV, R, LANE), F32)],
        scratch_shapes=[pltpu.SemaphoreType.DMA((7,)), pltpu.SemaphoreType.DMA((7,)), pltpu.SemaphoreType.DMA],
        compiler_params=pltpu.CompilerParams(vmem_limit_bytes=VMEM_LIMIT),
    )(v)[0]


EW_BLOCK_BYTES = 2 << 20


def _ew_rows(rows, cols):
    for tr in (1024, 512, 256, 128, 64, 32, 16, 8):
        if rows % tr == 0 and tr * cols * 4 <= EW_BLOCK_BYTES:
            return tr
    return rows


def add_pair(name, a, b, travel_dtype):
    R, C = a.shape
    tr = _ew_rows(R, C)

    def body(a_ref, b_ref, s_ref, t_ref):
        s = a_ref[...] + b_ref[...]
        s_ref[...] = s
        t_ref[...] = s.astype(t_ref.dtype)

    blk = pl.BlockSpec((tr, C), lambda i: (i, 0))
    return pl.pallas_call(
        body, name=name, grid=(R // tr,), in_specs=[blk, blk], out_specs=[blk, blk],
        out_shape=[jax.ShapeDtypeStruct((R, C), F32), jax.ShapeDtypeStruct((R, C), travel_dtype)],
        compiler_params=_cp("parallel"),
    )(a, b)


def add_four(name, mine, got):
    R, C = mine.shape
    tr = _ew_rows(R, C)

    def body(m_ref, g0, g1, g2, o_ref):
        o_ref[...] = ((m_ref[...] + g0[0].astype(F32)) + g1[0].astype(F32)) + g2[0].astype(F32)

    blk = pl.BlockSpec((tr, C), lambda i: (i, 0))
    slot = lambda j: pl.BlockSpec((1, tr, C), lambda i: (j, i, 0))
    return pl.pallas_call(
        body, name=name, grid=(R // tr,), in_specs=[blk, slot(0), slot(1), slot(2)], out_specs=blk,
        out_shape=jax.ShapeDtypeStruct((R, C), F32), compiler_params=_cp("parallel"),
    )(mine, got, got, got)


def reduce_scatter_grads(full_grads):
    names = list(SHARDED)
    minor = [nm in MINOR for nm in names]
    g4 = [full4(nm, full_grads[nm]) for nm in names]
    own, got = sibling_halves(g4)
    sums, travel = [], []
    for nm, a, b in zip(names, own, got):
        s, t = add_pair("chip_sum_" + nm, a.reshape(-1, a.shape[-1]), b.reshape(-1, b.shape[-1]),
                        F32 if nm in F32_TRAVEL else BF16)
        sums.append(s.reshape(a.shape))
        travel.append(t.reshape(a.shape))
    mine, arrived = scatter_chip_sums(sums, travel, minor)
    halves = []
    for nm, a, b in zip(names, mine, arrived):
        f = add_four("shard_sum_" + nm, a.reshape(-1, a.shape[-1]), b.reshape(3, -1, b.shape[-1]))
        halves.append(f.reshape(a.shape))
    return dict(zip(names, share_halves(halves)))


def pack_small(tree, extra=None):
    parts = [tree[nm].reshape(-1) for nm in SMALL]
    parts.append(jnp.zeros((1,), F32) if extra is None else extra.reshape(-1))
    blocks = []
    for p in parts:
        rows = _small_rows(p.shape[0])
        blocks.append(jnp.pad(p, (0, rows * LANE - p.shape[0])).reshape(rows, LANE))
    return jnp.concatenate(blocks, axis=0)


def _small_rows(size):
    return -(-size // (8 * LANE)) * 8


def unpack_small(packed, like):
    out, at = {}, 0
    for nm in SMALL:
        size = math.prod(like[nm].shape)
        rows = _small_rows(size)
        out[nm] = packed[at:at + rows].reshape(-1)[:size].reshape(like[nm].shape)
        at += rows
    return out, packed[at, 0]


def kernel(x, positions, attn_norm_g, w_in, q_lat_norm_g, w_uq, kv_lat_norm_g, w_ukv, mla_q_norm_g, mla_k_norm_g, fox_q_norm_g, fox_k_norm_g, fox_f_bias, s5_lambda_re, s5_lambda_im, s5_b_re, s5_b_im, s5_c_re, s5_c_im, s5_d, s5_log_step, s5_w_glu, s5_b_glu, w_branch, w_out, ffn_norm_g, w_up, ffn_conv_w, w_down, loss_target, m_attn_norm_g, m_w_in, m_q_lat_norm_g, m_w_uq, m_kv_lat_norm_g, m_w_ukv, m_mla_q_norm_g, m_mla_k_norm_g, m_fox_q_norm_g, m_fox_k_norm_g, m_fox_f_bias, m_s5_lambda_re, m_s5_lambda_im, m_s5_b_re, m_s5_b_im, m_s5_c_re, m_s5_c_im, m_s5_d, m_s5_log_step, m_s5_w_glu, m_s5_b_glu, m_w_branch, m_w_out, m_ffn_norm_g, m_w_up, m_ffn_conv_w, m_w_down, v_attn_norm_g, v_w_in, v_q_lat_norm_g, v_w_uq, v_kv_lat_norm_g, v_w_ukv, v_mla_q_norm_g, v_mla_k_norm_g, v_fox_q_norm_g, v_fox_k_norm_g, v_fox_f_bias, v_s5_lambda_re, v_s5_lambda_im, v_s5_b_re, v_s5_b_im, v_s5_c_re, v_s5_c_im, v_s5_d, v_s5_log_step, v_s5_w_glu, v_s5_b_glu, v_w_branch, v_w_out, v_ffn_norm_g, v_w_up, v_ffn_conv_w, v_w_down):
    given = dict(locals())
    w = {nm: given[nm] for nm in WEIGHTS}
    m = {nm: given["m_" + nm] for nm in WEIGHTS}
    v = {nm: given["v_" + nm] for nm in WEIGHTS}
    D = x.shape[-1]

    minor = [nm in MINOR for nm in SHARDED]
    shards = [shard3(nm, w[nm]).astype(F32 if nm in F32_TRAVEL else BF16) for nm in SHARDED]
    gathered = gather_weights(shards, minor)
    full = dict(w)
    for nm, g4 in zip(SHARDED, gathered):
        tail = list(w[nm].shape[1:])
        axis = (len(tail) - 1) if nm in MINOR or nm == "w_in" else 0
        tail[axis] *= N_CHIPS
        full[nm] = from_full4(nm, g4, tail)

    sq, grad_x, gw = local_step(x, positions, loss_target, full)

    big = reduce_scatter_grads(gw)
    total, sq_sum = unpack_small(allreduce_small(pack_small(gw, sq[0:1, 0:1])), w)
    loss = 0.5 * sq_sum / D

    grads, delta, new_m, new_v = {}, {}, {}, {}
    for nm in SHARDED:
        g = big[nm].reshape(shard3(nm, w[nm]).shape)
        two = lambda a: shard3(nm, a).reshape(-1, g.shape[-1])
        d2, m2, v2 = adamw("adamw_" + nm, two(w[nm]), g.reshape(-1, g.shape[-1]), two(m[nm]), two(v[nm]))
        grads[nm] = g.reshape(w[nm].shape)
        delta[nm], new_m[nm], new_v[nm] = (a.reshape(w[nm].shape) for a in (d2, m2, v2))
    d2, m2, v2 = adamw("adamw_small", pack_small(w), pack_small(total), pack_small(m), pack_small(v))
    for tree, packed in ((delta, d2), (new_m, m2), (new_v, v2)):
        tree.update(unpack_small(packed, w)[0])
    grads.update(total)
    return (loss, grad_x, *[grads[nm] for nm in WEIGHTS], *[delta[nm] for nm in WEIGHTS],
            *[new_m[nm] for nm in WEIGHTS], *[new_v[nm] for nm in WEIGHTS])
```

```python
import functools
import math

import jax
import jax.numpy as jnp
from jax import lax
from jax.experimental import pallas as pl
from jax.experimental.pallas import tpu as pltpu

F32, BF16 = jnp.float32, jnp.bfloat16
NORM_EPS = 1e-6
NEG_INF = -1e30
ROPE_THETA = 10000.0
LANE = 128
N_HEADS = 4
NOPE, ROPE, QK_DIM, V_DIM = 64, 32, 96, 64
Q_RANK, KV_RANK = 384, 256
FOX_DIM = 64
S5_G, S5_H, S5_P = 16, 16, 64
S5_C = S5_G * S5_P
BW = 256
VMEM_LIMIT = 56 << 20
ADAM_LR, ADAM_B1, ADAM_B2, ADAM_EPS, ADAM_WD, ADAM_STEP = 0.001, 0.9, 0.999, 1e-08, 0.01, 10


def _pick(n, cands):
    for c in cands:
        if n % c == 0:
            return c
    return n


def _cp(*sem):
    return pltpu.CompilerParams(dimension_semantics=sem, vmem_limit_bytes=VMEM_LIMIT)


def _dg(a, b, ca, cb):
    return lax.dot_general(a.astype(BF16), b.astype(BF16), (((ca,), (cb,)), ((), ())),
                           preferred_element_type=F32)


@jax.custom_vjp
def dot_nn(a, b):
    return _dg(a, b, 1, 0)


@jax.custom_vjp
def dot_nt(a, b):
    return _dg(a, b, 1, 1)


@jax.custom_vjp
def dot_tn(a, b):
    return _dg(a, b, 0, 0)


dot_nn.defvjp(lambda a, b: (dot_nn(a, b), (a, b)),
              lambda r, g: (dot_nt(g, r[1]).astype(r[0].dtype), dot_tn(r[0], g).astype(r[1].dtype)))
dot_nt.defvjp(lambda a, b: (dot_nt(a, b), (a, b)),
              lambda r, g: (dot_nn(g, r[1]).astype(r[0].dtype), dot_tn(g, r[0]).astype(r[1].dtype)))
dot_tn.defvjp(lambda a, b: (dot_tn(a, b), (a, b)),
              lambda r, g: (dot_nt(r[1], g).astype(r[0].dtype), dot_nn(r[0], g).astype(r[1].dtype)))


def xdot(a, b):
    return jnp.dot(a, b, precision=lax.Precision.HIGHEST, preferred_element_type=F32)


MM_VMEM_BUDGET = 36 << 20
MM_STEP_S, MM_HBM_BPS, MM_VMEM_BPS = 0.4e-6, 2.5e12, 3e12


def _divisors(n, cands):
    return sorted({c for c in cands if n % c == 0} | {n}, reverse=True)


def _mm_tiles(M, K, N, ab, bb, ob, addb):
    best = None
    for tm in _divisors(M, (2048, 1024, 512, 256, 128)):
        for tn in _divisors(N, (2048, 1664, 1536, 1408, 1280, 1024, 768, 640, 512, 384, 256, 128)):
            for tk in _divisors(K, (4096, 2048, 1664, 1536, 1408, 1024, 768, 512, 384, 256, 128)):
                vmem = 2 * (tm * tk * ab + tk * tn * bb + tm * tn * (ob + addb)) + (tm * tn * 4 if tk != K else 0)
                if vmem > MM_VMEM_BUDGET:
                    continue
                nk = K // tk
                steps = (M // tm) * (N // tn) * nk
                traffic = M * K * ab * (N // tn) + K * N * bb * (M // tm) + M * N * (ob + addb)
                cost = steps * MM_STEP_S + traffic / MM_HBM_BPS + (M * N * 8 * nk / MM_VMEM_BPS if nk > 1 else 0)
                if best is None or cost < best[0]:
                    best = (cost, tm, tn, tk)
    assert best is not None, (M, K, N)
    return best[1:]


def mm(a, b, mode, *, name, add=None, out_dtype=F32, layer=None, slot=None):
    bk, bn = b.shape[-2:]
    if mode == "nn":
        (M, K), N = a.shape, bn
    else:
        (K, M), N = a.shape, bn
    assert bk == K, (name, a.shape, b.shape)
    isz = lambda x: jnp.dtype(x.dtype).itemsize
    tm, tn, tk = _mm_tiles(M, K, N, isz(a), isz(b), jnp.dtype(out_dtype).itemsize, 0 if add is None else isz(add))
    nk = K // tk
    ca = 1 if mode == "nn" else 0

    n_in = 2 + (add is not None) + (slot is not None and slot[2] is not None)

    def body(*refs):
        a_ref, b_ref = refs[:2]
        add_ref = refs[2] if add is not None else None
        o_ref = refs[n_in]

        def finish(r):
            if add is not None:
                r = r + add_ref[...].astype(F32)
            o_ref[...] = r.astype(out_dtype)

        part = _dg(a_ref[...], b_ref[...], ca, 0)
        if nk == 1:
            finish(part)
            return
        acc = refs[-1]
        kk = pl.program_id(2)

        @pl.when(kk == 0)
        def _():
            acc[...] = part

        @pl.when(kk > 0)
        def _():
            acc[...] += part

        @pl.when(kk == nk - 1)
        def _():
            finish(acc[...])

    a_spec = (pl.BlockSpec((tm, tk), lambda i, j, k: (i, k)) if mode == "nn"
              else pl.BlockSpec((tk, tm), lambda i, j, k: (k, i)))
    b_spec = (pl.BlockSpec((tk, tn), lambda i, j, k: (k, j)) if layer is None
              else pl.BlockSpec((None, tk, tn), lambda i, j, k: (layer, k, j)))
    in_specs, args = [a_spec, b_spec], [a, b]
    if add is not None:
        in_specs.append(pl.BlockSpec((tm, tn), lambda i, j, k: (i, j)))
        args.append(add)
    out_spec = pl.BlockSpec((tm, tn), lambda i, j, k: (i, j))
    out_shape = jax.ShapeDtypeStruct((M, N), out_dtype)
    aliases = {}
    if slot is not None:
        n_layers, l, buf = slot
        out_spec = pl.BlockSpec((None, tm, tn), lambda i, j, k: (l, i, j))
        out_shape = jax.ShapeDtypeStruct((n_layers, M, N), out_dtype)
        if buf is not None:
            aliases = {len(args): 0}
            in_specs.append(pl.BlockSpec(memory_space=pl.ANY))
            args.append(buf)
    return pl.pallas_call(
        body, name=name, grid=(M // tm, N // tn, nk),
        in_specs=in_specs, out_specs=out_spec, out_shape=out_shape, input_output_aliases=aliases,
        scratch_shapes=[pltpu.VMEM((tm, tn), F32)] if nk > 1 else [],
        compiler_params=_cp("parallel", "parallel", "arbitrary"),
    )(*args)


def _row_spec(tm, width, col):
    if callable(col):
        return pl.BlockSpec((tm, width), lambda i, j: (i, col(j)))
    return pl.BlockSpec((tm, width), lambda i, j: (i, col))


def _const_spec(c):
    return pl.BlockSpec(c.shape, lambda i, j: (0,) * c.ndim)


ROW_BLOCK_BYTES = 6 << 20


def _row_tile(M, widths, tm):
    if tm is None:
        tm = next((t for t in (1024, 512, 256) if t * sum(widths) * 4 <= ROW_BLOCK_BYTES), 128)
    return _pick(M, (tm, 256, 128, 64, 32, 16, 8))


def rowwise(name, fn, rows, consts, outs, *, tm=None, nj=1):
    M = rows[0][0].shape[0]
    tm = _row_tile(M, [w for _, w, _ in rows] + [w for w, _ in outs], tm)
    nr, nc = len(rows), len(consts)

    def body(*refs):
        vals = [r[...].astype(F32) for r in refs[:nr]] + [r[...] for r in refs[nr:nr + nc]]
        res = fn(*vals)
        for o_ref, r in zip(refs[nr + nc:], res):
            o_ref[...] = r.astype(o_ref.dtype)

    return pl.pallas_call(
        body, name=name, grid=(M // tm, nj),
        in_specs=[_row_spec(tm, w, c) for _, w, c in rows] + [_const_spec(c) for c in consts],
        out_specs=[pl.BlockSpec((tm, w), lambda i, j: (i, j)) for w, _ in outs],
        out_shape=[jax.ShapeDtypeStruct((M, nj * w), dt) for w, dt in outs],
        compiler_params=_cp("parallel", "parallel"),
    )(*[r[0] for r in rows], *consts)


def rowwise_vjp(name, fn, rows, consts, cts, diff, *, tm=None, nj=1, gdt=None):
    M = rows[0][0].shape[0]
    tm = _row_tile(M, [w for _, w, _ in rows] + [w for _, w, _ in cts] + [rows[p][1] for p in diff], tm)
    nr, nc, nt, nd = len(rows), len(consts), len(cts), len(diff)
    gdt = [F32] * nd if gdt is None else gdt

    def body(*refs):
        vals = [r[...].astype(F32) for r in refs[:nr + nc + nt]]
        rv, cv, tv = vals[:nr], vals[nr:nr + nc], vals[nr + nc:]
        grow, gconst = refs[nr + nc + nt:nr + nc + nt + nd], refs[nr + nc + nt + nd:]

        def f(*dargs):
            full = list(rv)
            for pos, val in zip(diff, dargs[:nd]):
                full[pos] = val
            return tuple(fn(*full, *dargs[nd:]))

        _, vjp = jax.vjp(f, *[rv[p] for p in diff], *cv)
        g = vjp(tuple(tv))
        for o_ref, gv in zip(grow, g[:nd]):
            o_ref[...] = gv.astype(o_ref.dtype)
        first = jnp.logical_and(pl.program_id(0) == 0, pl.program_id(1) == 0)
        for o_ref, gv in zip(gconst, g[nd:]):
            @pl.when(first)
            def _(o_ref=o_ref, gv=gv):
                o_ref[...] = gv

            @pl.when(jnp.logical_not(first))
            def _(o_ref=o_ref, gv=gv):
                o_ref[...] += gv

    out_specs = ([pl.BlockSpec((tm, rows[p][1]), lambda i, j: (i, j)) for p in diff]
                 + [_const_spec(c) for c in consts])
    out_shape = ([jax.ShapeDtypeStruct((M, nj * rows[p][1]), dt) for p, dt in zip(diff, gdt)]
                 + [jax.ShapeDtypeStruct(c.shape, F32) for c in consts])
    res = pl.pallas_call(
        body, name=name, grid=(M // tm, nj),
        in_specs=([_row_spec(tm, w, c) for _, w, c in rows] + [_const_spec(c) for c in consts]
                  + [_row_spec(tm, w, c) for _, w, c in cts]),
        out_specs=out_specs, out_shape=out_shape,
        compiler_params=_cp("arbitrary", "arbitrary"),
    )(*[r[0] for r in rows], *consts, *[t[0] for t in cts])
    return res[:nd], res[nd:]


def _lane(shape=(1, LANE)):
    return lax.broadcasted_iota(jnp.int32, shape, len(shape) - 1)


def _sigmoid(x):
    return 1.0 / (1.0 + jnp.exp(-x))


def _rms(x, g, n):
    return x * lax.rsqrt(jnp.sum(x * x, axis=-1, keepdims=True) * (1.0 / n) + NORM_EPS) * g


def fn_rms(n):
    return lambda x, g: (_rms(x, g, n),)


def _rope(x, cos_t, sin_t):
    i = lax.broadcasted_iota(jnp.int32, (LANE, LANE), 0)
    j = lax.broadcasted_iota(jnp.int32, (LANE, LANE), 1)
    half = ROPE // 2
    lo = jnp.logical_and(jnp.logical_and(j >= NOPE, j < NOPE + half), i == j + half)
    hi = jnp.logical_and(jnp.logical_and(j >= NOPE + half, j < NOPE + ROPE), i == j - half)
    perm = jnp.where(hi, 1.0, 0.0) - jnp.where(lo, 1.0, 0.0)
    return x * cos_t + xdot(x, perm) * sin_t


def fn_qpost(q, cos_t, sin_t, g):
    return (_rope(_rms(q, g, QK_DIM), cos_t, sin_t),)


def fn_kpost(kn, small, cos_t, sin_t, g):
    lane = _lane()
    rope_lanes = jnp.logical_and(lane >= NOPE, lane < NOPE + ROPE)
    kc = kn + jnp.where(rope_lanes, small, 0.0)
    return (_rope(_rms(kc, g, QK_DIM), cos_t, sin_t),)


def fn_foxnorm(x, g):
    first = _lane() < FOX_DIM
    sq = x * x
    s0 = jnp.sum(jnp.where(first, sq, 0.0), axis=-1, keepdims=True)
    s1 = jnp.sum(jnp.where(first, 0.0, sq), axis=-1, keepdims=True)
    r0 = lax.rsqrt(s0 * (1.0 / FOX_DIM) + NORM_EPS)
    r1 = lax.rsqrt(s1 * (1.0 / FOX_DIM) + NORM_EPS)
    return (x * jnp.where(first, r0, r1) * g,)


F_LANE0 = NOPE + ROPE


def fn_fgate(small, bias):
    z = small + bias
    lf = jnp.minimum(z, 0.0) - jnp.log(1.0 + jnp.exp(-jnp.abs(z)))
    lane = _lane()
    return (jnp.where(jnp.logical_and(lane >= F_LANE0, lane < F_LANE0 + N_HEADS), lf, 0.0),)


def _gelu(y):
    return 0.5 * y * (1.0 + jnp.tanh(math.sqrt(2.0 / math.pi) * (y + 0.044715 * (y * y * y))))


def fn_s5post(ypre, u, d, wglu, bglu):
    y = _gelu(ypre + d * u)
    return (y * _sigmoid(dot_nn(y, wglu) + bglu),)


def fn_merge(om, of, os_, g0, g1, g2, wb0, wb1, wb2):
    return (_sigmoid(g0) * dot_nn(om, wb0) + _sigmoid(g1) * dot_nn(of, wb1)
            + _sigmoid(g2) * dot_nn(os_, wb2),)


def fn_add5(a, b):
    return (a[:, 0:LANE] + a[:, LANE:2 * LANE] + a[:, 2 * LANE:3 * LANE] + a[:, 3 * LANE:4 * LANE] + b,)


def fn_addt(a, b):
    return (a + b,)


def fn_s5params(lre, lim, lstep, btr, bti, ctr, cti):
    C = S5_C
    grp = lax.broadcasted_iota(jnp.int32, (LANE, C), 1) >> 6
    expand = jnp.where(lax.broadcasted_iota(jnp.int32, (LANE, C), 0) == grp, 1.0, 0.0)
    lane = _lane()
    st = jnp.where(lane < S5_G, jnp.exp(lstep), 0.0)
    step = jnp.sum(xdot(jnp.broadcast_to(st, (8, LANE)), expand), axis=0, keepdims=True) * 0.125
    zr, zi = lre * step, lim * step
    er = jnp.exp(zr)
    lbr, lbi = er * jnp.cos(zi), er * jnp.sin(zi)
    den = lre * lre + lim * lim
    nr = lbr - 1.0
    cfr = (nr * lre + lbi * lim) / den
    cfi = (lbi * lre - nr * lim) / den
    bbr = cfr * btr - cfi * bti
    bbi = cfr * bti + cfi * btr
    rg = lax.broadcasted_iota(jnp.int32, (BW, C), 0) >> 4
    cg = lax.broadcasted_iota(jnp.int32, (BW, C), 1) >> 6
    mb = jnp.where(rg == cg, 1.0, 0.0)
    b_re = jnp.concatenate([bbr] * S5_G, axis=0) * mb
    b_im = jnp.concatenate([bbi] * S5_G, axis=0) * mb
    ecol = jnp.where(lax.broadcasted_iota(jnp.int32, (LANE, BW), 0)
                     == (lax.broadcasted_iota(jnp.int32, (LANE, BW), 1) & 15), 1.0, 0.0)
    mc = jnp.where((lax.broadcasted_iota(jnp.int32, (C, BW), 0) >> 6)
                   == (lax.broadcasted_iota(jnp.int32, (C, BW), 1) >> 4), 1.0, 0.0)
    c_top = xdot(ctr, ecol) * mc
    c_bot = -(xdot(cti, ecol) * mc)
    return lbr, lbi, b_re, b_im, c_top, c_bot


def s5_params(p):
    def body(lre, lim, ls, btr, bti, ctr, cti, lb_ref, bre_ref, bim_ref, ct_ref, cb_ref):
        lbr, lbi, b_re, b_im, c_top, c_bot = fn_s5params(
            lre[...], lim[...], ls[...], btr[...], bti[...], ctr[...], cti[...])
        lb_ref[0:1, :] = lbr
        lb_ref[1:2, :] = lbi
        bre_ref[...] = b_re.astype(BF16)
        bim_ref[...] = b_im.astype(BF16)
        ct_ref[...] = c_top.astype(BF16)
        cb_ref[...] = c_bot.astype(BF16)

    return pl.pallas_call(
        body, name="s5_params",
        out_shape=[jax.ShapeDtypeStruct((2, S5_C), F32), jax.ShapeDtypeStruct((BW, S5_C), BF16),
                   jax.ShapeDtypeStruct((BW, S5_C), BF16), jax.ShapeDtypeStruct((S5_C, BW), BF16),
                   jax.ShapeDtypeStruct((S5_C, BW), BF16)],
        compiler_params=pltpu.CompilerParams(vmem_limit_bytes=VMEM_LIMIT),
    )(p["lre"], p["lim"], p["lstep"], p["btr"], p["bti"], p["ctr"], p["cti"])


S5_PARAM_NAMES = ("lre", "lim", "lstep", "btr", "bti", "ctr", "cti")


def s5_params_vjp(p, dl_r, dl_i, db_re, db_im, dc_top, dc_bot):
    def body(lre, lim, ls, btr, bti, ctr, cti, dlr, dli, dbr, dbi, dct, dcb, *outs):
        args = [r[...] for r in (lre, lim, ls, btr, bti, ctr, cti)]
        _, vjp = jax.vjp(fn_s5params, *args)
        g = vjp((jnp.sum(dlr[...], axis=0, keepdims=True), jnp.sum(dli[...], axis=0, keepdims=True),
                 dbr[...], dbi[...], dct[...], dcb[...]))
        for o_ref, gv in zip(outs, g):
            o_ref[...] = gv

    return pl.pallas_call(
        body, name="s5_params_vjp",
        out_shape=[jax.ShapeDtypeStruct(p[n].shape, F32) for n in S5_PARAM_NAMES],
        compiler_params=pltpu.CompilerParams(vmem_limit_bytes=VMEM_LIMIT),
    )(*[p[n] for n in S5_PARAM_NAMES], dl_r, dl_i, db_re, db_im, dc_top, dc_bot)


def _attn_tile(q, k, v, cq, ckt, *, hpt, dk, q0, tile):
    tq, S = q.shape[0], k.shape[0]
    row = q0 + lax.broadcasted_iota(jnp.int32, (tq, S), 0)
    col = lax.broadcasted_iota(jnp.int32, (tq, S), 1)
    causal = row >= col
    lane = _lane()
    out = jnp.zeros((tq, LANE), F32)
    for h in range(hpt):
        if hpt > 1:
            mine = (lane >> int(math.log2(LANE // hpt))) == h
            qh = jnp.where(mine, q, 0.0)
        else:
            qh = q
        s = dot_nt(qh, k) * (dk ** -0.5)
        if cq is not None:
            head = tile * hpt + h
            cqh = jnp.sum(jnp.where(lane == F_LANE0 + head, cq, 0.0), axis=1, keepdims=True)
            sub = lax.broadcasted_iota(jnp.int32, (8, 1), 0)
            ckh = jnp.sum(jnp.where(sub == head, ckt, 0.0), axis=0, keepdims=True)
            s = s + (cqh - ckh)
        s = jnp.where(causal, s, NEG_INF)
        m = lax.stop_gradient(jnp.max(s, axis=-1, keepdims=True))
        e = jnp.exp(s - m)
        p = e / jnp.sum(e, axis=-1, keepdims=True)
        oh = dot_nn(p, v)
        out = out + (jnp.where(mine, oh, 0.0) if hpt > 1 else oh)
    return out


def attention(name, q, k, v, *, B, S, ntile, hpt, dk, qc=0, kc=0, vc=0, cum=None, ckt=None, do=None, tq=256):
    tq = _pick(S, (tq, 128))
    nq = S // tq
    M = B * S
    bias = cum is not None
    kw = dict(hpt=hpt, dk=dk)

    def load(refs, sk):
        q_ref, k_ref, v_ref = refs[:3]
        qv, kv, vv = q_ref[...].astype(F32), k_ref[0:sk, :].astype(F32), v_ref[0:sk, :].astype(F32)
        if bias:
            return qv, kv, vv, refs[3][...], refs[4][0, :, 0:sk]
        return qv, kv, vv, None, None

    nin = 5 if bias else 3

    def per_query_block(run):
        for g in range(nq):
            @pl.when(pl.program_id(2) == g)
            def _(g=g):
                run(g, (g + 1) * tq)

    def fwd_body(*refs):
        tile = pl.program_id(1)

        def run(g, sk):
            qv, kv, vv, cq, ck = load(refs, sk)
            o = _attn_tile(qv, kv, vv, cq, ck, q0=g * tq, tile=tile, **kw)
            refs[nin][...] = o.astype(refs[nin].dtype)

        per_query_block(run)

    def bwd_body(*refs):
        outs = refs[nin + 1:]
        tile = pl.program_id(1)

        @pl.when(pl.program_id(2) == 0)
        def _():
            for o_ref in (outs[1], outs[2]) + ((outs[4],) if bias else ()):
                o_ref[...] = jnp.zeros_like(o_ref)

        def run(g, sk):
            qv, kv, vv, cq, ck = load(refs, sk)
            dov = refs[nin][...].astype(F32)
            if bias:
                f = lambda a, b, c, d, e: _attn_tile(a, b, c, d, e, q0=g * tq, tile=tile, **kw)
                _, vjp = jax.vjp(f, qv, kv, vv, cq, ck)
            else:
                f = lambda a, b, c: _attn_tile(a, b, c, None, None, q0=g * tq, tile=tile, **kw)
                _, vjp = jax.vjp(f, qv, kv, vv)
            gr = vjp(dov)
            outs[0][...] = gr[0]
            outs[1][0:sk, :] += gr[1]
            outs[2][0:sk, :] += gr[2]
            if bias:
                outs[3][...] = gr[3]
                outs[4][0, :, 0:sk] += gr[4]

        per_query_block(run)

    qspec = lambda c: pl.BlockSpec((tq, LANE), lambda b, t, i: (b * nq + i, c + t))
    kspec = lambda c: pl.BlockSpec((S, LANE), lambda b, t, i: (b, c + t))
    in_specs, args = [qspec(qc), kspec(kc), kspec(vc)], [q, k, v]
    if bias:
        in_specs += [pl.BlockSpec((tq, LANE), lambda b, t, i: (b * nq + i, 0)),
                     pl.BlockSpec((1, 8, S), lambda b, t, i: (b, 0, 0))]
        args += [cum, ckt]
    if do is None:
        return pl.pallas_call(
            fwd_body, name=name, grid=(B, ntile, nq), in_specs=in_specs, out_specs=qspec(0),
            out_shape=jax.ShapeDtypeStruct((M, ntile * LANE), BF16),
            compiler_params=_cp("parallel", "parallel", "parallel"),
        )(*args)
    in_specs.append(qspec(0))
    args.append(do)
    out_specs = [qspec(0), kspec(0), kspec(0)]
    out_shape = [jax.ShapeDtypeStruct((M, ntile * LANE), F32)] * 3
    if bias:
        out_specs += [qspec(0), pl.BlockSpec((1, 8, S), lambda b, t, i: (b * ntile + t, 0, 0))]
        out_shape += [jax.ShapeDtypeStruct((M, ntile * LANE), F32),
                      jax.ShapeDtypeStruct((B * ntile, 8, S), F32)]
    return pl.pallas_call(
        bwd_body, name=name, grid=(B, ntile, nq), in_specs=in_specs, out_specs=out_specs,
        out_shape=out_shape, compiler_params=_cp("parallel", "parallel", "arbitrary"),
    )(*args)


def seq_cumsum(name, x, *, B, S, reverse):
    nb = S // LANE

    def body(x_ref, o_ref):
        r = lax.broadcasted_iota(jnp.int32, (LANE, LANE), 0)
        c = lax.broadcasted_iota(jnp.int32, (LANE, LANE), 1)
        tri = jnp.where((r <= c) if reverse else (r >= c), 1.0, 0.0)
        carry = jnp.zeros((1, LANE), F32)
        for blk in (range(nb - 1, -1, -1) if reverse else range(nb)):
            xb = x_ref[blk * LANE:(blk + 1) * LANE, :]
            o_ref[blk * LANE:(blk + 1) * LANE, :] = xdot(tri, xb) + carry
            carry = carry + jnp.sum(xb, axis=0, keepdims=True)

    return pl.pallas_call(
        body, name=name, grid=(B,), in_specs=[pl.BlockSpec((S, LANE), lambda b: (b, 0))],
        out_specs=pl.BlockSpec((S, LANE), lambda b: (b, 0)),
        out_shape=jax.ShapeDtypeStruct(x.shape, F32), compiler_params=_cp("parallel"),
    )(x)


SCAN_ROWS = 64


def _shift_rows(ref, r0, rows, d, up):
    if d % 8 == 0:
        return ref[pl.ds(r0 + d if up else r0 - d, rows), :]
    if up:
        win = ref[pl.ds(r0, rows + 8), :]
        return pltpu.roll(win, rows + 8 - d, 0)[0:rows, :]
    win = ref[pl.ds(r0 - 8, rows + 8), :]
    return pltpu.roll(win, d, 0)[8:rows + 8, :]


def s5_scan(name, x_re, x_im, lam, *, B, S, reverse, state=None):
    C = S5_C
    cw = LANE
    R = _pick(S, (SCAN_ROWS,))
    pad = max(S // 2, 8)
    nsteps = int(math.log2(S))
    assert 1 << nsteps == S
    base = 0 if reverse else pad
    with_grad = state is not None

    def body(*refs):
        if with_grad:
            xr, xi, lam_ref, sr, si, o_r, o_i, dl_r, dl_i, a_r, a_i, b_r, b_i = refs
        else:
            xr, xi, lam_ref, o_r, o_i, a_r, a_i, b_r, b_i = refs
        zero = jnp.zeros((pad, cw), F32)
        z0 = S if reverse else 0
        for buf in (a_r, a_i, b_r, b_i):
            buf[z0:z0 + pad, :] = zero
        a_r[base:base + S, :] = xr[...]
        a_i[base:base + S, :] = xi[...]
        mr = lam_ref[0:1, :]
        mi = -lam_ref[1:2, :] if reverse else lam_ref[1:2, :]
        src, dst = (a_r, a_i), (b_r, b_i)
        for step in range(nsteps):
            d = 1 << step
            last = step == nsteps - 1

            def chunk(c, _, src=src, dst=dst, d=d, last=last, mr=mr, mi=mi):
                r0 = pl.multiple_of(base + c * R, 8)
                pr = _shift_rows(src[0], r0, R, d, reverse)
                pi = _shift_rows(src[1], r0, R, d, reverse)
                nr = src[0][pl.ds(r0, R), :] + mr * pr - mi * pi
                ni = src[1][pl.ds(r0, R), :] + mr * pi + mi * pr
                if last:
                    o0 = pl.multiple_of(c * R, 8)
                    o_r[pl.ds(o0, R), :] = nr
                    o_i[pl.ds(o0, R), :] = ni
                else:
                    dst[0][pl.ds(r0, R), :] = nr
                    dst[1][pl.ds(r0, R), :] = ni
                return 0

            lax.fori_loop(0, S // R, chunk, 0)
            mr, mi = mr * mr - mi * mi, 2.0 * mr * mi
            src, dst = dst, src
        if with_grad:
            def fold(v):
                return jnp.sum(v.reshape(R // 8, 8, cw), axis=0)

            def accum(c, carry, first=False):
                r0 = 0 if first else pl.multiple_of(c * R, 8)
                gr, gi = o_r[pl.ds(r0, R), :], o_i[pl.ds(r0, R), :]
                if first:
                    keep = lax.broadcasted_iota(jnp.int32, (R, 1), 0) >= 1
                    pr = jnp.where(keep, pltpu.roll(sr[0:R, :], 1, 0), 0.0)
                    pi = jnp.where(keep, pltpu.roll(si[0:R, :], 1, 0), 0.0)
                else:
                    pr = _shift_rows(sr, r0, R, 1, False)
                    pi = _shift_rows(si, r0, R, 1, False)
                return (carry[0] + fold(gr * pr + gi * pi), carry[1] + fold(gi * pr - gr * pi))

            acc = accum(0, (jnp.zeros((8, cw), F32), jnp.zeros((8, cw), F32)), first=True)
            acc = lax.fori_loop(1, S // R, accum, acc)
            dl_r[...] = acc[0]
            dl_i[...] = acc[1]

    seq = pl.BlockSpec((S, cw), lambda b, j: (b, j))
    in_specs = [seq, seq, pl.BlockSpec((2, cw), lambda b, j: (0, j))]
    args = [x_re, x_im, lam]
    out_specs = [seq, seq]
    out_shape = [jax.ShapeDtypeStruct(x_re.shape, F32)] * 2
    if with_grad:
        in_specs += [seq, seq]
        args += list(state)
        out_specs += [pl.BlockSpec((8, cw), lambda b, j: (b, j))] * 2
        out_shape += [jax.ShapeDtypeStruct((B * 8, C), F32)] * 2
    return pl.pallas_call(
        body, name=name, grid=(B, C // cw), in_specs=in_specs, out_specs=out_specs, out_shape=out_shape,
        scratch_shapes=[pltpu.VMEM((S + pad, cw), F32)] * 4,
        compiler_params=_cp("parallel", "parallel"),
    )(*args)


CONV_CW = 256


def _conv_taps(ref, r0, rows, first):
    cur = ref[pl.ds(r0, rows), :]
    if first:
        row = lax.broadcasted_iota(jnp.int32, (rows, 1), 0)
        p1 = jnp.where(row >= 1, pltpu.roll(cur, 1, 0), 0.0)
        p2 = jnp.where(row >= 2, pltpu.roll(cur, 2, 0), 0.0)
    else:
        p1 = _shift_rows(ref, r0, rows, 1, False)
        p2 = _shift_rows(ref, r0, rows, 2, False)
    return cur, p1, p2


def _conv_apply(w_ref, taps):
    return w_ref[2:3, :] * taps[0] + w_ref[1:2, :] * taps[1] + w_ref[0:1, :] * taps[2]


def conv_gate_fwd(up, conv_w, *, B, S):
    M, F2 = up.shape
    F = F2 // 2
    cw = _pick(F, (CONV_CW, LANE))
    nf = F // cw
    R = _pick(S, (SCAN_ROWS,))

    def body(g_ref, v_ref, wg_ref, wv_ref, o_ref):
        def chunk(c, _, first=False):
            r0 = 0 if first else pl.multiple_of(c * R, 8)
            cg = _conv_apply(wg_ref, _conv_taps(g_ref, r0, R, first))
            cv = _conv_apply(wv_ref, _conv_taps(v_ref, r0, R, first))
            o_ref[pl.ds(r0, R), :] = (cg * _sigmoid(cg) * cv).astype(o_ref.dtype)
            return 0

        chunk(0, 0, first=True)
        lax.fori_loop(1, S // R, chunk, 0)

    seq = lambda off: pl.BlockSpec((S, cw), lambda b, j: (b, off + j))
    wsp = lambda off: pl.BlockSpec((3, cw), lambda b, j: (0, off + j))
    return pl.pallas_call(
        body, name="conv_gate", grid=(B, nf), in_specs=[seq(0), seq(nf), wsp(0), wsp(nf)],
        out_specs=seq(0), out_shape=jax.ShapeDtypeStruct((M, F), BF16),
        compiler_params=_cp("parallel", "parallel"),
    )(up, up, conv_w, conv_w)


def conv_gate_bwd(up, conv_w, dact, *, B, S):
    M, F2 = up.shape
    F = F2 // 2
    cw = _pick(F, (CONV_CW, LANE))
    nf = F // cw
    R = _pick(S, (SCAN_ROWS,))
    nchunk = S // R

    def body(s_ref, p_ref, ws_ref, wp_ref, da_ref, du_ref, dw_ref, dc_ref):
        is_gate = pl.program_id(0) < nf
        dc_ref[S:S + 8, :] = jnp.zeros((8, cw), F32)

        def fold(v):
            return jnp.sum(v.reshape(R // 8, 8, cw), axis=0)

        first_b = pl.program_id(1) == 0

        def pass1(gate_step):
            def chunk(c, acc, first=False):
                r0 = 0 if first else pl.multiple_of(c * R, 8)
                taps = _conv_taps(s_ref, r0, R, first)
                cp = _conv_apply(wp_ref, _conv_taps(p_ref, r0, R, first))
                da = da_ref[pl.ds(r0, R), :]
                if gate_step:
                    cs = _conv_apply(ws_ref, taps)
                    sg = _sigmoid(cs)
                    dc = da * cp * (sg * (1.0 + cs * (1.0 - sg)))
                else:
                    dc = da * (cp * _sigmoid(cp))
                dc_ref[pl.ds(r0, R), :] = dc
                return (acc[0] + fold(dc * taps[2]), acc[1] + fold(dc * taps[1]), acc[2] + fold(dc * taps[0]))

            z = jnp.zeros((8, cw), F32)
            acc = chunk(0, (z, z, z), first=True)
            acc = lax.fori_loop(1, nchunk, chunk, acc)
            for tap in range(3):
                tot = jnp.sum(acc[tap], axis=0, keepdims=True)

                @pl.when(first_b)
                def _(tap=tap, tot=tot):
                    dw_ref[tap:tap + 1, :] = tot

                @pl.when(jnp.logical_not(first_b))
                def _(tap=tap, tot=tot):
                    dw_ref[tap:tap + 1, :] += tot

        @pl.when(is_gate)
        def _():
            pass1(True)

        @pl.when(jnp.logical_not(is_gate))
        def _():
            pass1(False)

        def pass2(c, _):
            r0 = pl.multiple_of(c * R, 16)
            n0 = dc_ref[pl.ds(r0, R), :]
            n1 = _shift_rows(dc_ref, r0, R, 1, True)
            n2 = _shift_rows(dc_ref, r0, R, 2, True)
            du = ws_ref[2:3, :] * n0 + ws_ref[1:2, :] * n1 + ws_ref[0:1, :] * n2
            du_ref[pl.ds(r0, R), :] = du.astype(du_ref.dtype)
            return 0

        lax.fori_loop(0, nchunk, pass2, 0)

    n2 = 2 * nf
    seq = lambda f: pl.BlockSpec((S, cw), lambda j, b: (b, f(j)))
    wsp = lambda f: pl.BlockSpec((3, cw), lambda j, b: (0, f(j)))
    same, other, act_col = (lambda j: j), (lambda j: (j + nf) % n2), (lambda j: j % nf)
    return pl.pallas_call(
        body, name="conv_gate_vjp", grid=(n2, B),
        in_specs=[seq(same), seq(other), wsp(same), wsp(other), seq(act_col)],
        out_specs=[seq(same), wsp(same)],
        out_shape=[jax.ShapeDtypeStruct((M, F2), BF16), jax.ShapeDtypeStruct((3, F2), F32)],
        scratch_shapes=[pltpu.VMEM((S + 8, cw), F32)],
        compiler_params=_cp("parallel", "arbitrary"),
    )(up, up, conv_w, conv_w, dact)


def loss_head(y, target):
    M, D = y.shape
    tm = _pick(M, (256, 128, 64, 32, 16, 8))

    def body(y_ref, t_ref, dy_ref, l_ref):
        diff = y_ref[...] - t_ref[...]
        dy_ref[...] = diff * (1.0 / D)
        part = jnp.sum(jnp.sum(diff * diff, axis=1, keepdims=True), axis=0, keepdims=True)

        @pl.when(pl.program_id(0) == 0)
        def _():
            l_ref[...] = jnp.zeros_like(l_ref)

        l_ref[...] += part

    row = pl.BlockSpec((tm, D), lambda i: (i, 0))
    return pl.pallas_call(
        body, name="loss_head", grid=(M // tm,), in_specs=[row, row],
        out_specs=[row, pl.BlockSpec((8, LANE), lambda i: (0, 0))],
        out_shape=[jax.ShapeDtypeStruct((M, D), F32), jax.ShapeDtypeStruct((8, LANE), F32)],
        compiler_params=_cp("arbitrary"),
    )(y, target)


def adamw(name, w, g, m, v):
    R, C = w.shape
    tr = _pick(R, (256, 128, 64, 32, 16, 8))

    def body(w_ref, g_ref, m_ref, v_ref, d_ref, nm_ref, nv_ref):
        gv = g_ref[...]
        nm = ADAM_B1 * m_ref[...] + (1.0 - ADAM_B1) * gv
        nv = ADAM_B2 * v_ref[...] + (1.0 - ADAM_B2) * (gv * gv)
        m_hat = nm / (1.0 - ADAM_B1 ** ADAM_STEP)
        v_hat = nv / (1.0 - ADAM_B2 ** ADAM_STEP)
        d_ref[...] = -ADAM_LR * (m_hat / (jnp.sqrt(v_hat) + ADAM_EPS) + ADAM_WD * w_ref[...])
        nm_ref[...] = nm
        nv_ref[...] = nv

    blk = pl.BlockSpec((tr, C), lambda i: (i, 0))
    return pl.pallas_call(
        body, name=name, grid=(R // tr,), in_specs=[blk] * 4, out_specs=[blk] * 3,
        out_shape=[jax.ShapeDtypeStruct((R, C), F32)] * 3, compiler_params=_cp("parallel"),
    )(w, g, m, v)


def _seg(D):
    o = 3 * D
    return dict(ckv=o, fq=o + 256, fk=o + 512, fv=o + 768, u=o + 1024, small=o + 1280, cq=o + 1536, P=o + 1920)


def _pad_last(a, n):
    return jnp.pad(a, [(0, 0)] * (a.ndim - 1) + [(0, n - a.shape[-1])])


def _place(a, lo, n=LANE):
    return jnp.pad(a, [(0, 0)] * (a.ndim - 1) + [(lo, n - lo - a.shape[-1])])


def _in_segments(D):
    s = _seg(D)
    sm = s["small"]
    return ((0, 384, s["cq"]), (384, 640, s["ckv"]), (640, 672, sm + NOPE), (672, 1440, s["fq"]),
            (1440, 1444, sm + F_LANE0), (1444, 1700, s["u"]), (1700, 1700 + 3 * D, 0))


def _chip_pieces(win4, lo, hi):
    cw = win4.shape[-1]
    out = []
    for k in range(N_CHIPS):
        a, b = max(lo, k * cw), min(hi, (k + 1) * cw)
        if a < b:
            out.append(win4[:, k, :, a - k * cw:b - k * cw])
    return out


def prep_weights(w):
    L, D = w["attn_norm_g"].shape
    win = w["w_in"]
    z = lambda n: [jnp.zeros((L, D, n), win.dtype)]
    cols = lambda lo, hi: _chip_pieces(win, lo, hi)
    g0 = 1700
    wp = jnp.concatenate(
        cols(g0, g0 + 3 * D) + cols(384, 640) + cols(672, 1440) + cols(1444, 1700) + z(NOPE) + cols(640, 672)
        + cols(1440, 1444) + z(LANE - F_LANE0 - N_HEADS) + z(LANE) + cols(0, 384), axis=-1)
    wukv, wb = w["w_ukv"], w["w_branch"]
    row3 = lambda a: a[:, None, :]
    return dict(
        g1=row3(w["attn_norm_g"]), Wp=wp, gql=row3(w["q_lat_norm_g"]), gkvl=row3(w["kv_lat_norm_g"]),
        Wuq=_pad_last(w["w_uq"], LANE).reshape(L, Q_RANK, N_HEADS * LANE),
        Wk=_pad_last(wukv[..., :NOPE], LANE).reshape(L, KV_RANK, N_HEADS * LANE),
        Wv=_pad_last(wukv[..., NOPE:], LANE).reshape(L, KV_RANK, N_HEADS * LANE),
        gq=row3(_pad_last(w["mla_q_norm_g"], LANE)), gk=row3(_pad_last(w["mla_k_norm_g"], LANE)),
        gfq=row3(jnp.tile(w["fox_q_norm_g"], (1, 2))), gfk=row3(jnp.tile(w["fox_k_norm_g"], (1, 2))),
        fbias=row3(_place(w["fox_f_bias"], F_LANE0)),
        lre=w["s5_lambda_re"].reshape(L, 1, S5_C), lim=w["s5_lambda_im"].reshape(L, 1, S5_C),
        lstep=row3(_pad_last(w["s5_log_step"], LANE)),
        btr=jnp.transpose(w["s5_b_re"], (0, 3, 1, 2)).reshape(L, S5_H, S5_C),
        bti=jnp.transpose(w["s5_b_im"], (0, 3, 1, 2)).reshape(L, S5_H, S5_C),
        ctr=_pad_last(jnp.transpose(w["s5_c_re"], (0, 1, 3, 2)).reshape(L, S5_C, S5_H), LANE),
        cti=_pad_last(jnp.transpose(w["s5_c_im"], (0, 1, 3, 2)).reshape(L, S5_C, S5_H), LANE),
        s5d=w["s5_d"].reshape(L, 1, BW), Wglu=w["s5_w_glu"], bglu=row3(w["s5_b_glu"]),
        Wb0=jnp.pad(wb[:, 0].reshape(L, N_HEADS, V_DIM, D), ((0, 0), (0, 0), (0, LANE - V_DIM), (0, 0))
                    ).reshape(L, N_HEADS * LANE, D),
        Wb1=wb[:, 1], Wb2=wb[:, 2], Wout=w["w_out"], g2=row3(w["ffn_norm_g"]), Wup=w["w_up"],
        convw=w["ffn_conv_w"], Wdown=w["w_down"],
    )


BIG_KEYS = ("Wp", "Wuq", "Wk", "Wv", "Wout", "Wup", "Wdown")


def with_transposes(P):
    out = dict(P)
    for k in BIG_KEYS:
        out[k + "T"] = jnp.swapaxes(P[k], 1, 2)
    return out


def layer_params(P, l):
    return {k: (v if k in BIG_KEYS or k[:-1] in BIG_KEYS else v[l]) for k, v in P.items()}


def unprep_grads(G, D):
    L = G["g1"].shape[0]
    dwp = G["Wp"]
    segs = _in_segments(D)
    cw = segs[-1][1] // N_CHIPS
    chips = []
    for k in range(N_CHIPS):
        pieces = []
        for lo, hi, at in segs:
            a, b = max(lo, k * cw), min(hi, (k + 1) * cw)
            if a < b:
                pieces.append(dwp[..., at + a - lo:at + b - lo])
        chips.append(jnp.concatenate(pieces, axis=-1))
    w_in = jnp.stack(chips, axis=1)
    heads = lambda a, rows, keep: a.reshape(L, rows, N_HEADS, LANE)[..., :keep]
    wb0 = G["Wb0"].reshape(L, N_HEADS, LANE, D)[:, :, :V_DIM].reshape(L, BW, D)
    gf = lambda a: a[:, 0, :FOX_DIM] + a[:, 0, FOX_DIM:]
    return dict(
        attn_norm_g=G["g1"][:, 0], w_in=w_in, q_lat_norm_g=G["gql"][:, 0], w_uq=heads(G["Wuq"], Q_RANK, QK_DIM),
        kv_lat_norm_g=G["gkvl"][:, 0],
        w_ukv=jnp.concatenate([heads(G["Wk"], KV_RANK, NOPE), heads(G["Wv"], KV_RANK, V_DIM)], axis=-1),
        mla_q_norm_g=G["gq"][:, 0, :QK_DIM], mla_k_norm_g=G["gk"][:, 0, :QK_DIM],
        fox_q_norm_g=gf(G["gfq"]), fox_k_norm_g=gf(G["gfk"]),
        fox_f_bias=G["fbias"][:, 0, F_LANE0:F_LANE0 + N_HEADS],
        s5_lambda_re=G["lre"].reshape(L, S5_G, S5_P), s5_lambda_im=G["lim"].reshape(L, S5_G, S5_P),
        s5_b_re=jnp.transpose(G["btr"].reshape(L, S5_H, S5_G, S5_P), (0, 2, 3, 1)),
        s5_b_im=jnp.transpose(G["bti"].reshape(L, S5_H, S5_G, S5_P), (0, 2, 3, 1)),
        s5_c_re=jnp.transpose(G["ctr"][..., :S5_H].reshape(L, S5_G, S5_P, S5_H), (0, 1, 3, 2)),
        s5_c_im=jnp.transpose(G["cti"][..., :S5_H].reshape(L, S5_G, S5_P, S5_H), (0, 1, 3, 2)),
        s5_d=G["s5d"].reshape(L, S5_G, S5_H), s5_log_step=G["lstep"][:, 0, :S5_G],
        s5_w_glu=G["Wglu"], s5_b_glu=G["bglu"][:, 0],
        w_branch=jnp.stack([wb0, G["Wb1"], G["Wb2"]], axis=1), w_out=G["Wout"], ffn_norm_g=G["g2"][:, 0],
        w_up=G["Wup"], ffn_conv_w=G["convw"], w_down=G["Wdown"],
    )


def rope_tables(positions):
    inv_freq = ROPE_THETA ** (-jnp.arange(0, ROPE, 2, dtype=F32) / ROPE)
    ang = positions.astype(F32)[..., None] * inv_freq
    cos, sin = jnp.cos(ang), jnp.sin(ang)
    ones = jnp.ones(ang.shape[:-1] + (NOPE,), F32)
    zeros = jnp.zeros(ang.shape[:-1] + (LANE - NOPE - ROPE,), F32)
    cos_t = jnp.concatenate([ones, cos, cos, zeros], axis=-1)
    sin_t = jnp.concatenate([0.0 * ones, sin, sin, zeros], axis=-1)
    return cos_t.reshape(-1, LANE), sin_t.reshape(-1, LANE)


def fn_rms_res(n):
    return lambda x, g: (_rms(x, g, n), x)


def _s5_mats(p):
    return s5_params({k: p[k] for k in S5_PARAM_NAMES})


def layer_fwd(x, p, l, cos_t, sin_t, B, S):
    M, D = x.shape
    s = _seg(D)
    sm = s["small"] // LANE
    head = lambda j: j
    h = rowwise("rms_attn", fn_rms(D), [(x, D, 0)], [p["g1"]], [(D, BF16)])[0]
    proj = mm(h, p["Wp"], "nn", name="in_proj", layer=l)
    cnq = rowwise("latq_norm", fn_rms(Q_RANK), [(proj, Q_RANK, s["cq"] // Q_RANK)], [p["gql"]], [(Q_RANK, BF16)])[0]
    cnkv = rowwise("latkv_norm", fn_rms(KV_RANK), [(proj, KV_RANK, s["ckv"] // KV_RANK)], [p["gkvl"]],
                   [(KV_RANK, BF16)])[0]
    qraw = mm(cnq, p["Wuq"], "nn", name="q_up", layer=l)
    kn = mm(cnkv, p["Wk"], "nn", name="k_up", layer=l)
    v5 = mm(cnkv, p["Wv"], "nn", name="v_up", out_dtype=BF16, layer=l)
    qrot = rowwise("q_post", fn_qpost, [(qraw, LANE, head), (cos_t, LANE, 0), (sin_t, LANE, 0)], [p["gq"]],
                   [(LANE, BF16)], nj=N_HEADS)[0]
    krot = rowwise("k_post", fn_kpost, [(kn, LANE, head), (proj, LANE, sm), (cos_t, LANE, 0), (sin_t, LANE, 0)],
                   [p["gk"]], [(LANE, BF16)], nj=N_HEADS)[0]
    omla = attention("mla_attn", qrot, krot, v5, B=B, S=S, ntile=N_HEADS, hpt=1, dk=QK_DIM)
    fq0, fk0 = s["fq"] // LANE, s["fk"] // LANE
    qf = rowwise("foxq_norm", fn_foxnorm, [(proj, LANE, lambda j: fq0 + j)], [p["gfq"]], [(LANE, BF16)], nj=2)[0]
    kf = rowwise("foxk_norm", fn_foxnorm, [(proj, LANE, lambda j: fk0 + j)], [p["gfk"]], [(LANE, BF16)], nj=2)[0]
    lf = rowwise("fgate", fn_fgate, [(proj, LANE, sm)], [p["fbias"]], [(LANE, F32)])[0]
    cum = seq_cumsum("fox_cumsum", lf, B=B, S=S, reverse=False)
    ckt = _pad_rows8(jnp.transpose(cum.reshape(B, S, LANE)[:, :, F_LANE0:F_LANE0 + N_HEADS], (0, 2, 1)))
    ofox = attention("fox_attn", qf, kf, proj, B=B, S=S, ntile=2, hpt=2, dk=FOX_DIM, vc=s["fv"] // LANE,
                     cum=cum, ckt=ckt)
    lam, bre, bim, ctop, cbot = _s5_mats(p)
    u16 = proj[:, s["u"]:s["u"] + BW].astype(BF16)
    bur = mm(u16, bre, "nn", name="s5_bu_re")
    bui = mm(u16, bim, "nn", name="s5_bu_im")
    sr, si = s5_scan("s5_scan", bur, bui, lam, B=B, S=S, reverse=False)
    ypre = mm(si, cbot, "nn", name="s5_y_im", add=mm(sr, ctop, "nn", name="s5_y_re"))
    os5 = rowwise("s5_post", fn_s5post, [(ypre, BW, 0), (proj, BW, s["u"] // BW)],
                  [p["s5d"], p["Wglu"], p["bglu"]], [(BW, BF16)])[0]
    merged = rowwise("merge", fn_merge,
                     [(omla, N_HEADS * LANE, 0), (ofox, BW, 0), (os5, BW, 0), (proj, D, 0), (proj, D, 1), (proj, D, 2)],
                     [p["Wb0"], p["Wb1"], p["Wb2"]], [(D, BF16)])[0]
    xmid = mm(merged, p["Wout"], "nn", name="out_proj", add=x, layer=l)
    h2 = rowwise("rms_ffn", fn_rms(D), [(xmid, D, 0)], [p["g2"]], [(D, BF16)])[0]
    up = mm(h2, p["Wup"], "nn", name="ffn_up", layer=l)
    act = conv_gate_fwd(up, p["convw"], B=B, S=S)
    xout = mm(act, p["Wdown"], "nn", name="ffn_down", add=xmid, layer=l)
    saved = dict(x=x, h=h, proj=proj, cnq=cnq, cnkv=cnkv, qraw=qraw, kn=kn, v5=v5, qrot=qrot, krot=krot, qf=qf,
                 kf=kf, cum=cum, ckt=ckt, omla=omla, ofox=ofox, os5=os5, u16=u16, sr=sr, si=si, ypre=ypre,
                 merged=merged, xmid=xmid, h2=h2, up=up, act=act)
    return xout, saved


def _pad_rows8(a):
    return jnp.pad(a, ((0, 0), (0, 8 - a.shape[1]), (0, 0)))


STACKED = ("Wp", "Wout", "Wup", "Wdown")


def layer_bwd(dx, p, l, sv, cos_t, sin_t, B, S, stacks):
    M, D = dx.shape
    slot = lambda k: (stacks["L"], l, stacks.get(k))
    s = _seg(D)
    sm = s["small"] // LANE
    head = lambda j: j
    proj = sv["proj"]
    dact = mm(dx, p["WdownT"], "nn", name="ffn_down_dx", layer=l)
    d_wdown = mm(sv["act"], dx, "tn", name="ffn_down_dw", slot=slot("Wdown"))
    dup, d_convw = conv_gate_bwd(sv["up"], p["convw"], dact, B=B, S=S)
    dh2 = mm(dup, p["WupT"], "nn", name="ffn_up_dx", layer=l)
    d_wup = mm(sv["h2"], dup, "tn", name="ffn_up_dw", slot=slot("Wup"))
    (dxmid,), (d_g2,) = rowwise_vjp("rms_ffn_vjp", fn_rms_res(D), [(sv["xmid"], D, 0)], [p["g2"]],
                                    [(dh2, D, 0), (dx, D, 0)], [0])
    dmerged = mm(dxmid, p["WoutT"], "nn", name="out_proj_dx", layer=l)
    d_wout = mm(sv["merged"], dxmid, "tn", name="out_proj_dw", slot=slot("Wout"))
    (dom, dof, dos, dg0, dg1, dg2), (d_wb0, d_wb1, d_wb2) = rowwise_vjp(
        "merge_vjp", fn_merge,
        [(sv["omla"], N_HEADS * LANE, 0), (sv["ofox"], BW, 0), (sv["os5"], BW, 0), (proj, D, 0), (proj, D, 1),
         (proj, D, 2)], [p["Wb0"], p["Wb1"], p["Wb2"]], [(dmerged, D, 0)], [0, 1, 2, 3, 4, 5], tm=128,
        gdt=[BF16, BF16, F32, BF16, BF16, BF16])
    lam, bre, bim, ctop, cbot = _s5_mats(p)
    (dypre, du_a), (d_s5d, d_wglu, d_bglu) = rowwise_vjp(
        "s5_post_vjp", fn_s5post, [(sv["ypre"], BW, 0), (proj, BW, s["u"] // BW)],
        [p["s5d"], p["Wglu"], p["bglu"]], [(dos, BW, 0)], [0, 1], gdt=[BF16, F32])
    dsr = mm(dypre, ctop.T, "nn", name="s5_y_re_dx")
    dsi = mm(dypre, cbot.T, "nn", name="s5_y_im_dx")
    d_ctop = mm(sv["sr"], dypre, "tn", name="s5_y_re_dw")
    d_cbot = mm(sv["si"], dypre, "tn", name="s5_y_im_dw")
    gr, gi, dl_r, dl_i = s5_scan("s5_scan_vjp", dsr, dsi, lam, B=B, S=S, reverse=True, state=(sv["sr"], sv["si"]))
    du = mm(gi, bim.T, "nn", name="s5_bu_im_dx", out_dtype=BF16,
            add=mm(gr, bre.T, "nn", name="s5_bu_re_dx", add=du_a))
    d_bre = mm(sv["u16"], gr, "tn", name="s5_bu_re_dw")
    d_bim = mm(sv["u16"], gi, "tn", name="s5_bu_im_dw")
    d_s5 = s5_params_vjp({k: p[k] for k in S5_PARAM_NAMES}, dl_r, dl_i, d_bre, d_bim, d_ctop, d_cbot)
    dqf, dkf, dfv, dcq_t, dckt_t = attention("fox_attn_vjp", sv["qf"], sv["kf"], proj, B=B, S=S, ntile=2, hpt=2,
                                             dk=FOX_DIM, vc=s["fv"] // LANE, cum=sv["cum"], ckt=sv["ckt"], do=dof)
    dck = dckt_t.reshape(B, 2, 8, S)
    dck = jnp.transpose(dck[:, 0, :N_HEADS] + dck[:, 1, :N_HEADS], (0, 2, 1)).reshape(M, N_HEADS)
    dcum = rowwise("fox_dcum", lambda a, b: (a[:, 0:LANE] + a[:, LANE:2 * LANE] + b,),
                   [(dcq_t, 2 * LANE, 0), (_place(dck, F_LANE0), LANE, 0)], [], [(LANE, F32)])[0]
    dlf = seq_cumsum("fox_cumsum_vjp", dcum, B=B, S=S, reverse=True)
    (dsmall_f,), (d_fbias,) = rowwise_vjp("fgate_vjp", fn_fgate, [(proj, LANE, sm)], [p["fbias"]],
                                          [(dlf, LANE, 0)], [0])
    fq0, fk0 = s["fq"] // LANE, s["fk"] // LANE
    (dfq,), (d_gfq,) = rowwise_vjp("foxq_norm_vjp", fn_foxnorm, [(proj, LANE, lambda j: fq0 + j)], [p["gfq"]],
                                   [(dqf, LANE, head)], [0], nj=2, gdt=[BF16])
    (dfk,), (d_gfk,) = rowwise_vjp("foxk_norm_vjp", fn_foxnorm, [(proj, LANE, lambda j: fk0 + j)], [p["gfk"]],
                                   [(dkf, LANE, head)], [0], nj=2, gdt=[BF16])
    dqrot, dkrot, dv5 = attention("mla_attn_vjp", sv["qrot"], sv["krot"], sv["v5"], B=B, S=S, ntile=N_HEADS,
                                  hpt=1, dk=QK_DIM, do=dom)
    (dqraw,), (d_gq,) = rowwise_vjp("q_post_vjp", fn_qpost,
                                    [(sv["qraw"], LANE, head), (cos_t, LANE, 0), (sin_t, LANE, 0)], [p["gq"]],
                                    [(dqrot, LANE, head)], [0], nj=N_HEADS, gdt=[BF16])
    (dkn, dsmall_k), (d_gk,) = rowwise_vjp(
        "k_post_vjp", fn_kpost, [(sv["kn"], LANE, head), (proj, LANE, sm), (cos_t, LANE, 0), (sin_t, LANE, 0)],
        [p["gk"]], [(dkrot, LANE, head)], [0, 1], nj=N_HEADS, gdt=[BF16, F32])
    dsmall = rowwise("small_sum", fn_add5, [(dsmall_k, N_HEADS * LANE, 0), (dsmall_f, LANE, 0)], [], [(LANE, BF16)])[0]
    dcnq = mm(dqraw, p["WuqT"], "nn", name="q_up_dx", layer=l)
    d_wuq = mm(sv["cnq"], dqraw, "tn", name="q_up_dw")
    dcnkv = mm(dv5, p["WvT"], "nn", name="v_up_dx", layer=l, add=mm(dkn, p["WkT"], "nn", name="k_up_dx", layer=l))
    d_wk = mm(sv["cnkv"], dkn, "tn", name="k_up_dw")
    d_wv = mm(sv["cnkv"], dv5, "tn", name="v_up_dw")
    (dcq,), (d_gql,) = rowwise_vjp("latq_norm_vjp", fn_rms(Q_RANK), [(proj, Q_RANK, s["cq"] // Q_RANK)], [p["gql"]],
                                   [(dcnq, Q_RANK, 0)], [0], gdt=[BF16])
    (dckv,), (d_gkvl,) = rowwise_vjp("latkv_norm_vjp", fn_rms(KV_RANK), [(proj, KV_RANK, s["ckv"] // KV_RANK)],
                                     [p["gkvl"]], [(dcnkv, KV_RANK, 0)], [0], gdt=[BF16])
    dproj = jnp.concatenate([dg0, dg1, dg2, dckv, dfq, dfk, dfv.astype(BF16), du, dsmall,
                             jnp.zeros((M, LANE), BF16), dcq], axis=1)
    dh = mm(dproj, p["WpT"], "nn", name="in_proj_dx", layer=l)
    d_wp = mm(sv["h"], dproj, "tn", name="in_proj_dw", slot=slot("Wp"))
    (dxin,), (d_g1,) = rowwise_vjp("rms_attn_vjp", fn_rms_res(D), [(sv["x"], D, 0)], [p["g1"]],
                                   [(dh, D, 0), (dxmid, D, 0)], [0])
    grads = dict(g1=d_g1, Wp=d_wp, gql=d_gql, gkvl=d_gkvl, Wuq=d_wuq, Wk=d_wk, Wv=d_wv, gq=d_gq, gk=d_gk,
                 gfq=d_gfq, gfk=d_gfk, fbias=d_fbias, s5d=d_s5d, Wglu=d_wglu, bglu=d_bglu, Wb0=d_wb0, Wb1=d_wb1,
                 Wb2=d_wb2, Wout=d_wout, g2=d_g2, Wup=d_wup, convw=d_convw, Wdown=d_wdown)
    grads.update(dict(zip(S5_PARAM_NAMES, d_s5)))
    return dxin, grads


def local_step(x, positions, target, w):
    B, S, D = x.shape
    M = B * S
    P = with_transposes(prep_weights(w))
    L = P["g1"].shape[0]
    cos_t, sin_t = rope_tables(positions)
    xc, saved = x.reshape(M, D), []
    for l in range(L):
        xc, sv = layer_fwd(xc, layer_params(P, l), l, cos_t, sin_t, B, S)
        saved.append(sv)
    dxc, sq = loss_head(xc, target.reshape(M, D))
    grads, stacks = [None] * L, {"L": L}
    for l in reversed(range(L)):
        dxc, grads[l] = layer_bwd(dxc, layer_params(P, l), l, saved[l], cos_t, sin_t, B, S, stacks)
        stacks.update({k: grads[l][k] for k in STACKED})
    G = {k: (stacks[k] if k in STACKED else jnp.stack([g[k] for g in grads])) for k in grads[0]}
    return sq, dxc.reshape(B, S, D), unprep_grads(G, D)


MESH = pl.DeviceIdType.MESH
ANY = pl.BlockSpec(memory_space=pl.ANY)
N_CHIPS = 4
SHARDED = ("w_in", "w_uq", "w_ukv", "s5_w_glu", "w_branch", "w_out", "w_up", "ffn_conv_w", "w_down")
MINOR = ("w_branch", "w_up", "ffn_conv_w")
F32_TRAVEL = ("ffn_conv_w",)
WEIGHTS = ("attn_norm_g", "w_in", "q_lat_norm_g", "w_uq", "kv_lat_norm_g", "w_ukv", "mla_q_norm_g", "mla_k_norm_g",
           "fox_q_norm_g", "fox_k_norm_g", "fox_f_bias", "s5_lambda_re", "s5_lambda_im", "s5_b_re", "s5_b_im",
           "s5_c_re", "s5_c_im", "s5_d", "s5_log_step", "s5_w_glu", "s5_b_glu", "w_branch", "w_out", "ffn_norm_g",
           "w_up", "ffn_conv_w", "w_down")
SMALL = tuple(n for n in WEIGHTS if n not in SHARDED)
RS_BIG = tuple(n for n in SHARDED if n != "ffn_conv_w")
REDUCE_SMALL = SMALL + ("ffn_conv_w",)


def shard3(name, a):
    L = a.shape[0]
    if name in ("w_uq", "w_ukv"):
        return a.reshape(L, a.shape[1], -1)
    if name == "w_branch":
        return a.reshape(L, -1, a.shape[-1])
    return a


def full4(name, a):
    L = a.shape[0]
    if name == "w_in":
        return a
    if name in MINOR:
        return a.reshape(L, 1, -1, a.shape[-1])
    a = a.reshape(L, a.shape[1], -1)
    return a.reshape(L, N_CHIPS, a.shape[1] // N_CHIPS, a.shape[2])


def from_full4(name, a, ref_tail):
    L = a.shape[0]
    if name == "w_in":
        return a
    return a.reshape((L,) + tuple(ref_tail))


def _where():
    x, y, c = lax.axis_index("x"), lax.axis_index("y"), lax.axis_index("c")
    chips = [(1 - x, y), (x, 1 - y), (1 - x, 1 - y)]
    return (x, y, c), 2 * x + y, (x, y, 1 - c), chips, [2 * cx + cy for cx, cy in chips]


def _view(minor, ref4, layers, k):
    if minor:
        cs = ref4.shape[3] // N_CHIPS
        return ref4.at[layers, 0, :, pl.ds(pl.multiple_of(k * cs, LANE), cs)]
    return ref4.at[layers, k]


def _remote(src, dst, ssem, rsem, dev):
    return pltpu.make_async_remote_copy(src_ref=src, dst_ref=dst, send_sem=ssem, recv_sem=rsem,
                                        device_id=dev, device_id_type=MESH)


def gather_weights(shards, minor):
    n = len(shards)
    L = shards[0].shape[0]
    Lh = L // 2
    out_shape = []
    for a, mn in zip(shards, minor):
        _, r, cs = a.shape
        out_shape.append(jax.ShapeDtypeStruct((L, 1, r, N_CHIPS * cs) if mn else (L, N_CHIPS, r, cs), a.dtype))

    def body(*refs):
        w, g = refs[:n], refs[n:2 * n]
        send, recv = refs[2 * n:]
        (x, y, c), me, sib, chips, cidx = _where()
        mine, other, every = pl.ds(c * Lh, Lh), pl.ds((1 - c) * Lh, Lh), pl.ds(0, L)
        dst = lambda i, layers, k: _view(minor[i], g[i], layers, k)
        local = [_remote(w[i], dst(i, every, me), send.at[i, 6], recv.at[i, 6], sib) for i in range(n)]
        first = [_remote(w[i].at[mine], dst(i, mine, me), send.at[i, j], recv.at[i, j], (*chips[j], c))
                 for i in range(n) for j in range(3)]
        for cp in local + first:
            cp.start()
        passed = []
        for i in range(n):
            for j in range(3):
                blk = dst(i, mine, cidx[j])
                _remote(blk, blk, send.at[i, j], recv.at[i, j], (*chips[j], c)).wait_recv()
                fwd = _remote(blk, blk, send.at[i, 3 + j], recv.at[i, 3 + j], sib)
                fwd.start()
                passed.append(fwd)
        for i in range(n):
            for j in range(3):
                blk = dst(i, other, cidx[j])
                _remote(blk, blk, send.at[i, 3 + j], recv.at[i, 3 + j], sib).wait_recv()
        for cp in first + passed:
            cp.wait_send()
        for cp in local:
            cp.wait()

    return pl.pallas_call(
        body, name="gather_weights", in_specs=[ANY] * n, out_specs=[ANY] * n, out_shape=out_shape,
        scratch_shapes=[pltpu.SemaphoreType.DMA((n, 7)), pltpu.SemaphoreType.DMA((n, 7))],
    )(*shards)


def sibling_halves(grads):
    n = len(grads)
    L = grads[0].shape[0]
    Lh = L // 2
    half = [jax.ShapeDtypeStruct((Lh,) + a.shape[1:], a.dtype) for a in grads]

    def body(*refs):
        g, got = refs[:n], refs[n:2 * n]
        send, recv = refs[2 * n:]
        (x, y, c), me, sib, chips, cidx = _where()
        other = pl.ds((1 - c) * Lh, Lh)
        out = [_remote(g[i].at[other], got[i], send.at[i], recv.at[i], sib) for i in range(n)]
        for cp in out:
            cp.start()
        for cp in out:
            cp.wait()

    got = pl.pallas_call(
        body, name="grad_sibling_halves", in_specs=[ANY] * n, out_specs=[ANY] * n, out_shape=half,
        scratch_shapes=[pltpu.SemaphoreType.DMA((n,)), pltpu.SemaphoreType.DMA((n,))],
    )(*grads)
    return got


def scatter_chip_sums(travel, minor):
    n = len(travel)
    Lh = travel[0].shape[0]
    got_shape = []
    for t, mn in zip(travel, minor):
        r, cs = t.shape[2], (t.shape[3] // N_CHIPS if mn else t.shape[3])
        got_shape.append(jax.ShapeDtypeStruct((3, Lh, r, cs), t.dtype))

    def body(*refs):
        s16, got = refs[:n], refs[n:2 * n]
        send, recv = refs[2 * n:]
        (x, y, c), me, sib, chips, cidx = _where()
        every = pl.ds(0, Lh)
        out = [_remote(_view(minor[i], s16[i], every, cidx[j]), got[i].at[j], send.at[i, j], recv.at[i, j],
                       (*chips[j], c)) for i in range(n) for j in range(3)]
        for cp in out:
            cp.start()
        for cp in out:
            cp.wait()

    got = pl.pallas_call(
        body, name="grad_scatter", in_specs=[ANY] * n, out_specs=[ANY] * n, out_shape=got_shape,
        scratch_shapes=[pltpu.SemaphoreType.DMA((n, 3)), pltpu.SemaphoreType.DMA((n, 3))],
    )(*travel)
    return got


def share_halves(shards):
    n = len(shards)
    Lh = shards[0].shape[0] // 2

    def body(*refs):
        full = refs[n:2 * n]
        send, recv = refs[2 * n:]
        (x, y, c), me, sib, chips, cidx = _where()
        mine, other = pl.ds(c * Lh, Lh), pl.ds((1 - c) * Lh, Lh)
        out = [_remote(full[i].at[mine], full[i].at[mine], send.at[i], recv.at[i], sib) for i in range(n)]
        for cp in out:
            cp.start()
        for i in range(n):
            out[i].wait_send()
            _remote(full[i].at[other], full[i].at[other], send.at[i], recv.at[i], sib).wait_recv()

    return pl.pallas_call(
        body, name="grad_share_halves", in_specs=[ANY] * n, out_specs=[ANY] * n,
        out_shape=[jax.ShapeDtypeStruct(a.shape, a.dtype) for a in shards],
        input_output_aliases={i: i for i in range(n)},
        scratch_shapes=[pltpu.SemaphoreType.DMA((n,)), pltpu.SemaphoreType.DMA((n,))],
    )(*shards)


N_DEV = 8


def allreduce_small(v):
    R = v.shape[0]

    def body(x_ref, sum_ref, all_ref, send, recv, loc):
        (x, y, c), me, sib, chips, cidx = _where()

        def rows(px, py, pc):
            return all_ref.at[4 * px + 2 * py + pc]

        def copy(k, block, to, src=None):
            return _remote(rows(*block) if src is None else src, rows(*block), send.at[k], recv.at[k], to)

        mine = pltpu.make_async_copy(x_ref, rows(x, y, c), loc)
        mine.start()
        first = [copy(0, (x, y, c), sib, src=x_ref)]
        first += [copy(1 + j, (x, y, c), (*chip, c), src=x_ref) for j, chip in enumerate(chips)]
        for cp in first:
            cp.start()
        passed = [copy(4 + j, (*chip, c), sib) for j, chip in enumerate(chips)]
        for j, chip in enumerate(chips):
            copy(1 + j, (*chip, c), (x, y, c)).wait_recv()
            passed[j].start()
        copy(0, (x, y, 1 - c), (x, y, c)).wait_recv()
        for j, chip in enumerate(chips):
            copy(4 + j, (*chip, 1 - c), (x, y, c)).wait_recv()
        for cp in first + passed:
            cp.wait_send()
        mine.wait()
        acc = all_ref[0]
        for d in range(1, N_DEV):
            acc = acc + all_ref[d]
        sum_ref[...] = acc

    vm = pl.BlockSpec(memory_space=pltpu.VMEM)
    return pl.pallas_call(
        body, name="allreduce_small", in_specs=[vm], out_specs=[vm, vm],
        out_shape=[jax.ShapeDtypeStruct((R, LANE), F32), jax.ShapeDtypeStruct((N_DEV, R, LANE), F32)],
        scratch_shapes=[pltpu.SemaphoreType.DMA((7,)), pltpu.SemaphoreType.DMA((7,)), pltpu.SemaphoreType.DMA],
        compiler_params=pltpu.CompilerParams(vmem_limit_bytes=VMEM_LIMIT),
    )(v)[0]


EW_BLOCK_BYTES = 2 << 20


def _ew_rows(rows, cols):
    for tr in (1024, 512, 256, 128, 64, 32, 16, 8):
        if rows % tr == 0 and tr * cols * 4 <= EW_BLOCK_BYTES:
            return tr
    return rows


def add_pair(name, full, got, where, travel_dtype):
    R, C = got.shape
    tr = _ew_rows(R, C)
    nblk = R // tr

    def body(w_ref, a_ref, b_ref, s_ref, t_ref):
        s = a_ref[...] + b_ref[...]
        s_ref[...] = s
        t_ref[...] = s.astype(t_ref.dtype)

    blk = pl.BlockSpec((tr, C), lambda i, w: (i, 0))
    return pl.pallas_call(
        body, name=name,
        grid_spec=pltpu.PrefetchScalarGridSpec(
            num_scalar_prefetch=1, grid=(nblk,),
            in_specs=[pl.BlockSpec((tr, C), lambda i, w: (w[1] * nblk + i, 0)), blk], out_specs=[blk, blk]),
        out_shape=[jax.ShapeDtypeStruct((R, C), F32), jax.ShapeDtypeStruct((R, C), travel_dtype)],
        compiler_params=_cp("parallel"),
    )(where, full, got)


def add_four(name, sums, got, where, minor):
    Lh, _, r, C = sums.shape
    cs = got.shape[-1]
    tr = _ew_rows(r, cs)

    def body(w_ref, m_ref, g0, g1, g2, o_ref):
        o_ref[...] = ((m_ref[...] + g0[...].astype(F32)) + g1[...].astype(F32)) + g2[...].astype(F32)

    mine = (pl.BlockSpec((None, None, tr, cs), lambda l, i, w: (l, 0, i, w[0])) if minor
            else pl.BlockSpec((None, None, tr, cs), lambda l, i, w: (l, w[0], i, 0)))
    slot = lambda j: pl.BlockSpec((None, None, tr, cs), lambda l, i, w: (j, l, i, 0))
    return pl.pallas_call(
        body, name=name,
        grid_spec=pltpu.PrefetchScalarGridSpec(
            num_scalar_prefetch=1, grid=(Lh, r // tr), in_specs=[mine, slot(0), slot(1), slot(2)],
            out_specs=pl.BlockSpec((None, tr, cs), lambda l, i, w: (w[1] * Lh + l, i, 0))),
        out_shape=jax.ShapeDtypeStruct((2 * Lh, r, cs), F32), compiler_params=_cp("parallel", "parallel"),
    )(where, sums, got, got, got)


def reduce_scatter_grads(full_grads):
    names = list(RS_BIG)
    minor = [nm in MINOR for nm in names]
    where = jnp.stack([2 * lax.axis_index("x") + lax.axis_index("y"), lax.axis_index("c")]).astype(jnp.int32)
    g4 = [full4(nm, full_grads[nm]) for nm in names]
    got = sibling_halves(g4)
    sums, travel = [], []
    for nm, a, b in zip(names, g4, got):
        s, t = add_pair("chip_sum_" + nm, a.reshape(-1, a.shape[-1]), b.reshape(-1, b.shape[-1]), where,
                        F32 if nm in F32_TRAVEL else BF16)
        sums.append(s.reshape(b.shape))
        travel.append(t.reshape(b.shape))
    arrived = scatter_chip_sums(travel, minor)
    shards = [add_four("shard_sum_" + nm, s, b, where, mn) for nm, s, b, mn in zip(names, sums, arrived, minor)]
    return dict(zip(names, share_halves(shards)))


def pack_small(tree, extra=None, names=SMALL):
    parts = [tree[nm].reshape(-1) for nm in names]
    parts.append(jnp.zeros((1,), F32) if extra is None else extra.reshape(-1))
    blocks = []
    for p in parts:
        rows = _small_rows(p.shape[0])
        blocks.append(jnp.pad(p, (0, rows * LANE - p.shape[0])).reshape(rows, LANE))
    return jnp.concatenate(blocks, axis=0)


def _small_rows(size):
    return -(-size // (8 * LANE)) * 8


def unpack_small(packed, like, names=SMALL):
    out, at = {}, 0
    for nm in names:
        size = math.prod(like[nm].shape)
        rows = _small_rows(size)
        out[nm] = packed[at:at + rows].reshape(-1)[:size].reshape(like[nm].shape)
        at += rows
    return out, packed[at, 0]


def kernel(x, positions, attn_norm_g, w_in, q_lat_norm_g, w_uq, kv_lat_norm_g, w_ukv, mla_q_norm_g, mla_k_norm_g, fox_q_norm_g, fox_k_norm_g, fox_f_bias, s5_lambda_re, s5_lambda_im, s5_b_re, s5_b_im, s5_c_re, s5_c_im, s5_d, s5_log_step, s5_w_glu, s5_b_glu, w_branch, w_out, ffn_norm_g, w_up, ffn_conv_w, w_down, loss_target, m_attn_norm_g, m_w_in, m_q_lat_norm_g, m_w_uq, m_kv_lat_norm_g, m_w_ukv, m_mla_q_norm_g, m_mla_k_norm_g, m_fox_q_norm_g, m_fox_k_norm_g, m_fox_f_bias, m_s5_lambda_re, m_s5_lambda_im, m_s5_b_re, m_s5_b_im, m_s5_c_re, m_s5_c_im, m_s5_d, m_s5_log_step, m_s5_w_glu, m_s5_b_glu, m_w_branch, m_w_out, m_ffn_norm_g, m_w_up, m_ffn_conv_w, m_w_down, v_attn_norm_g, v_w_in, v_q_lat_norm_g, v_w_uq, v_kv_lat_norm_g, v_w_ukv, v_mla_q_norm_g, v_mla_k_norm_g, v_fox_q_norm_g, v_fox_k_norm_g, v_fox_f_bias, v_s5_lambda_re, v_s5_lambda_im, v_s5_b_re, v_s5_b_im, v_s5_c_re, v_s5_c_im, v_s5_d, v_s5_log_step, v_s5_w_glu, v_s5_b_glu, v_w_branch, v_w_out, v_ffn_norm_g, v_w_up, v_ffn_conv_w, v_w_down):
    given = dict(locals())
    w = {nm: given[nm] for nm in WEIGHTS}
    m = {nm: given["m_" + nm] for nm in WEIGHTS}
    v = {nm: given["v_" + nm] for nm in WEIGHTS}
    D = x.shape[-1]

    minor = [nm in MINOR for nm in SHARDED]
    shards = [shard3(nm, w[nm]).astype(F32 if nm in F32_TRAVEL else BF16) for nm in SHARDED]
    gathered = gather_weights(shards, minor)
    full = dict(w)
    for nm, g4 in zip(SHARDED, gathered):
        tail = list(w[nm].shape[1:])
        axis = (len(tail) - 1) if nm in MINOR or nm == "w_in" else 0
        tail[axis] *= N_CHIPS
        full[nm] = from_full4(nm, g4, tail)

    sq, grad_x, gw = local_step(x, positions, loss_target, full)

    big = reduce_scatter_grads(gw)
    total, sq_sum = unpack_small(allreduce_small(pack_small(gw, sq[0:1, 0:1], REDUCE_SMALL)), gw, REDUCE_SMALL)
    loss = 0.5 * sq_sum / D
    conv_cols = w["ffn_conv_w"].shape[-1]
    chip = 2 * lax.axis_index("x") + lax.axis_index("y")
    big["ffn_conv_w"] = lax.dynamic_slice_in_dim(total.pop("ffn_conv_w"), chip * conv_cols, conv_cols, axis=2)

    grads, delta, new_m, new_v = {}, {}, {}, {}
    for nm in SHARDED:
        g = big[nm].reshape(shard3(nm, w[nm]).shape)
        two = lambda a: shard3(nm, a).reshape(-1, g.shape[-1])
        d2, m2, v2 = adamw("adamw_" + nm, two(w[nm]), g.reshape(-1, g.shape[-1]), two(m[nm]), two(v[nm]))
        grads[nm] = g.reshape(w[nm].shape)
        delta[nm], new_m[nm], new_v[nm] = (a.reshape(w[nm].shape) for a in (d2, m2, v2))
    d2, m2, v2 = adamw("adamw_small", pack_small(w), pack_small(total), pack_small(m), pack_small(v))
    for tree, packed in ((delta, d2), (new_m, m2), (new_v, v2)):
        tree.update(unpack_small(packed, w)[0])
    grads.update(total)
    return (loss, grad_x, *[grads[nm] for nm in WEIGHTS], *[delta[nm] for nm in WEIGHTS],
            *[new_m[nm] for nm in WEIGHTS], *[new_v[nm] for nm in WEIGHTS])
```

```python
import functools
import math

import jax
import jax.numpy as jnp
from jax import lax
from jax.experimental import pallas as pl
from jax.experimental.pallas import tpu as pltpu

F32, BF16 = jnp.float32, jnp.bfloat16
NORM_EPS = 1e-6
NEG_INF = -1e30
ROPE_THETA = 10000.0
LANE = 128
N_HEADS = 4
NOPE, ROPE, QK_DIM, V_DIM = 64, 32, 96, 64
Q_RANK, KV_RANK = 384, 256
FOX_DIM = 64
S5_G, S5_H, S5_P = 16, 16, 64
S5_C = S5_G * S5_P
BW = 256
VMEM_LIMIT = 56 << 20
ADAM_LR, ADAM_B1, ADAM_B2, ADAM_EPS, ADAM_WD, ADAM_STEP = 0.001, 0.9, 0.999, 1e-08, 0.01, 10


def _pick(n, cands):
    for c in cands:
        if n % c == 0:
            return c
    return n


def _cp(*sem):
    return pltpu.CompilerParams(dimension_semantics=sem, vmem_limit_bytes=VMEM_LIMIT)


def _dg(a, b, ca, cb):
    return lax.dot_general(a.astype(BF16), b.astype(BF16), (((ca,), (cb,)), ((), ())),
                           preferred_element_type=F32)


@jax.custom_vjp
def dot_nn(a, b):
    return _dg(a, b, 1, 0)


@jax.custom_vjp
def dot_nt(a, b):
    return _dg(a, b, 1, 1)


@jax.custom_vjp
def dot_tn(a, b):
    return _dg(a, b, 0, 0)


dot_nn.defvjp(lambda a, b: (dot_nn(a, b), (a, b)),
              lambda r, g: (dot_nt(g, r[1]).astype(r[0].dtype), dot_tn(r[0], g).astype(r[1].dtype)))
dot_nt.defvjp(lambda a, b: (dot_nt(a, b), (a, b)),
              lambda r, g: (dot_nn(g, r[1]).astype(r[0].dtype), dot_tn(g, r[0]).astype(r[1].dtype)))
dot_tn.defvjp(lambda a, b: (dot_tn(a, b), (a, b)),
              lambda r, g: (dot_nt(r[1], g).astype(r[0].dtype), dot_nn(r[0], g).astype(r[1].dtype)))


def xdot(a, b):
    return jnp.dot(a, b, precision=lax.Precision.HIGHEST, preferred_element_type=F32)


MM_VMEM_BUDGET = 36 << 20
MM_STEP_S, MM_HBM_BPS, MM_VMEM_BPS = 0.4e-6, 2.5e12, 3e12


def _divisors(n, cands):
    return sorted({c for c in cands if n % c == 0} | {n}, reverse=True)


def _mm_tiles(M, K, N, ab, bb, ob, addb):
    best = None
    for tm in _divisors(M, (2048, 1024, 512, 256, 128)):
        for tn in _divisors(N, (2048, 1664, 1536, 1408, 1280, 1024, 768, 640, 512, 384, 256, 128)):
            for tk in _divisors(K, (4096, 2048, 1664, 1536, 1408, 1024, 768, 512, 384, 256, 128)):
                vmem = 2 * (tm * tk * ab + tk * tn * bb + tm * tn * (ob + addb)) + (tm * tn * 4 if tk != K else 0)
                if vmem > MM_VMEM_BUDGET:
                    continue
                nk = K // tk
                steps = (M // tm) * (N // tn) * nk
                traffic = M * K * ab * (N // tn) + K * N * bb * (M // tm) + M * N * (ob + addb)
                cost = steps * MM_STEP_S + traffic / MM_HBM_BPS + (M * N * 8 * nk / MM_VMEM_BPS if nk > 1 else 0)
                if best is None or cost < best[0]:
                    best = (cost, tm, tn, tk)
    assert best is not None, (M, K, N)
    return best[1:]


def mm(a, b, mode, *, name, add=None, out_dtype=F32, layer=None, slot=None):
    bk, bn = b.shape[-2:]
    if mode == "nn":
        (M, K), N = a.shape, bn
    else:
        (K, M), N = a.shape, bn
    assert bk == K, (name, a.shape, b.shape)
    isz = lambda x: jnp.dtype(x.dtype).itemsize
    tm, tn, tk = _mm_tiles(M, K, N, isz(a), isz(b), jnp.dtype(out_dtype).itemsize, 0 if add is None else isz(add))
    nk = K // tk
    ca = 1 if mode == "nn" else 0

    n_in = 2 + (add is not None) + (slot is not None and slot[2] is not None)

    def body(*refs):
        a_ref, b_ref = refs[:2]
        add_ref = refs[2] if add is not None else None
        o_ref = refs[n_in]

        def finish(r):
            if add is not None:
                r = r + add_ref[...].astype(F32)
            o_ref[...] = r.astype(out_dtype)

        part = _dg(a_ref[...], b_ref[...], ca, 0)
        if nk == 1:
            finish(part)
            return
        acc = refs[-1]
        kk = pl.program_id(2)

        @pl.when(kk == 0)
        def _():
            acc[...] = part

        @pl.when(kk > 0)
        def _():
            acc[...] += part

        @pl.when(kk == nk - 1)
        def _():
            finish(acc[...])

    a_spec = (pl.BlockSpec((tm, tk), lambda i, j, k: (i, k)) if mode == "nn"
              else pl.BlockSpec((tk, tm), lambda i, j, k: (k, i)))
    b_spec = (pl.BlockSpec((tk, tn), lambda i, j, k: (k, j)) if layer is None
              else pl.BlockSpec((None, tk, tn), lambda i, j, k: (layer, k, j)))
    in_specs, args = [a_spec, b_spec], [a, b]
    if add is not None:
        in_specs.append(pl.BlockSpec((tm, tn), lambda i, j, k: (i, j)))
        args.append(add)
    out_spec = pl.BlockSpec((tm, tn), lambda i, j, k: (i, j))
    out_shape = jax.ShapeDtypeStruct((M, N), out_dtype)
    aliases = {}
    if slot is not None:
        n_layers, l, buf = slot
        out_spec = pl.BlockSpec((None, tm, tn), lambda i, j, k: (l, i, j))
        out_shape = jax.ShapeDtypeStruct((n_layers, M, N), out_dtype)
        if buf is not None:
            aliases = {len(args): 0}
            in_specs.append(pl.BlockSpec(memory_space=pl.ANY))
            args.append(buf)
    return pl.pallas_call(
        body, name=name, grid=(M // tm, N // tn, nk),
        in_specs=in_specs, out_specs=out_spec, out_shape=out_shape, input_output_aliases=aliases,
        scratch_shapes=[pltpu.VMEM((tm, tn), F32)] if nk > 1 else [],
        compiler_params=_cp("parallel", "parallel", "arbitrary"),
    )(*args)


def _row_spec(tm, width, col):
    if callable(col):
        return pl.BlockSpec((tm, width), lambda i, j: (i, col(j)))
    return pl.BlockSpec((tm, width), lambda i, j: (i, col))


def _const_spec(c):
    return pl.BlockSpec(c.shape, lambda i, j: (0,) * c.ndim)


ROW_BLOCK_BYTES = 6 << 20


def _row_tile(M, widths, tm):
    if tm is None:
        tm = next((t for t in (1024, 512, 256) if t * sum(widths) * 4 <= ROW_BLOCK_BYTES), 128)
    return _pick(M, (tm, 256, 128, 64, 32, 16, 8))


def rowwise(name, fn, rows, consts, outs, *, tm=None, nj=1):
    M = rows[0][0].shape[0]
    tm = _row_tile(M, [w for _, w, _ in rows] + [w for w, _ in outs], tm)
    nr, nc = len(rows), len(consts)

    def body(*refs):
        vals = [r[...].astype(F32) for r in refs[:nr]] + [r[...] for r in refs[nr:nr + nc]]
        res = fn(*vals)
        for o_ref, r in zip(refs[nr + nc:], res):
            o_ref[...] = r.astype(o_ref.dtype)

    return pl.pallas_call(
        body, name=name, grid=(M // tm, nj),
        in_specs=[_row_spec(tm, w, c) for _, w, c in rows] + [_const_spec(c) for c in consts],
        out_specs=[pl.BlockSpec((tm, w), lambda i, j: (i, j)) for w, _ in outs],
        out_shape=[jax.ShapeDtypeStruct((M, nj * w), dt) for w, dt in outs],
        compiler_params=_cp("parallel", "parallel"),
    )(*[r[0] for r in rows], *consts)


def rowwise_vjp(name, fn, rows, consts, cts, diff, *, tm=None, nj=1, gdt=None):
    M = rows[0][0].shape[0]
    tm = _row_tile(M, [w for _, w, _ in rows] + [w for _, w, _ in cts] + [rows[p][1] for p in diff], tm)
    nr, nc, nt, nd = len(rows), len(consts), len(cts), len(diff)
    gdt = [F32] * nd if gdt is None else gdt

    def body(*refs):
        vals = [r[...].astype(F32) for r in refs[:nr + nc + nt]]
        rv, cv, tv = vals[:nr], vals[nr:nr + nc], vals[nr + nc:]
        grow, gconst = refs[nr + nc + nt:nr + nc + nt + nd], refs[nr + nc + nt + nd:]

        def f(*dargs):
            full = list(rv)
            for pos, val in zip(diff, dargs[:nd]):
                full[pos] = val
            return tuple(fn(*full, *dargs[nd:]))

        _, vjp = jax.vjp(f, *[rv[p] for p in diff], *cv)
        g = vjp(tuple(tv))
        for o_ref, gv in zip(grow, g[:nd]):
            o_ref[...] = gv.astype(o_ref.dtype)
        first = jnp.logical_and(pl.program_id(0) == 0, pl.program_id(1) == 0)
        for o_ref, gv in zip(gconst, g[nd:]):
            @pl.when(first)
            def _(o_ref=o_ref, gv=gv):
                o_ref[...] = gv

            @pl.when(jnp.logical_not(first))
            def _(o_ref=o_ref, gv=gv):
                o_ref[...] += gv

    out_specs = ([pl.BlockSpec((tm, rows[p][1]), lambda i, j: (i, j)) for p in diff]
                 + [_const_spec(c) for c in consts])
    out_shape = ([jax.ShapeDtypeStruct((M, nj * rows[p][1]), dt) for p, dt in zip(diff, gdt)]
                 + [jax.ShapeDtypeStruct(c.shape, F32) for c in consts])
    res = pl.pallas_call(
        body, name=name, grid=(M // tm, nj),
        in_specs=([_row_spec(tm, w, c) for _, w, c in rows] + [_const_spec(c) for c in consts]
                  + [_row_spec(tm, w, c) for _, w, c in cts]),
        out_specs=out_specs, out_shape=out_shape,
        compiler_params=_cp("arbitrary", "arbitrary"),
    )(*[r[0] for r in rows], *consts, *[t[0] for t in cts])
    return res[:nd], res[nd:]


def _lane(shape=(1, LANE)):
    return lax.broadcasted_iota(jnp.int32, shape, len(shape) - 1)


def _sigmoid(x):
    return 1.0 / (1.0 + jnp.exp(-x))


def _rms(x, g, n):
    return x * lax.rsqrt(jnp.sum(x * x, axis=-1, keepdims=True) * (1.0 / n) + NORM_EPS) * g


def fn_rms(n):
    return lambda x, g: (_rms(x, g, n),)


def _rope(x, cos_t, sin_t):
    i = lax.broadcasted_iota(jnp.int32, (LANE, LANE), 0)
    j = lax.broadcasted_iota(jnp.int32, (LANE, LANE), 1)
    half = ROPE // 2
    lo = jnp.logical_and(jnp.logical_and(j >= NOPE, j < NOPE + half), i == j + half)
    hi = jnp.logical_and(jnp.logical_and(j >= NOPE + half, j < NOPE + ROPE), i == j - half)
    perm = jnp.where(hi, 1.0, 0.0) - jnp.where(lo, 1.0, 0.0)
    return x * cos_t + xdot(x, perm) * sin_t


def fn_qpost(q, cos_t, sin_t, g):
    return (_rope(_rms(q, g, QK_DIM), cos_t, sin_t),)


def fn_kpost(kn, small, cos_t, sin_t, g):
    lane = _lane()
    rope_lanes = jnp.logical_and(lane >= NOPE, lane < NOPE + ROPE)
    kc = kn + jnp.where(rope_lanes, small, 0.0)
    return (_rope(_rms(kc, g, QK_DIM), cos_t, sin_t),)


def fn_foxnorm(x, g):
    first = _lane() < FOX_DIM
    sq = x * x
    s0 = jnp.sum(jnp.where(first, sq, 0.0), axis=-1, keepdims=True)
    s1 = jnp.sum(jnp.where(first, 0.0, sq), axis=-1, keepdims=True)
    r0 = lax.rsqrt(s0 * (1.0 / FOX_DIM) + NORM_EPS)
    r1 = lax.rsqrt(s1 * (1.0 / FOX_DIM) + NORM_EPS)
    return (x * jnp.where(first, r0, r1) * g,)


F_LANE0 = NOPE + ROPE


def fn_fgate(small, bias):
    z = small + bias
    lf = jnp.minimum(z, 0.0) - jnp.log(1.0 + jnp.exp(-jnp.abs(z)))
    lane = _lane()
    return (jnp.where(jnp.logical_and(lane >= F_LANE0, lane < F_LANE0 + N_HEADS), lf, 0.0),)


def _gelu(y):
    return 0.5 * y * (1.0 + jnp.tanh(math.sqrt(2.0 / math.pi) * (y + 0.044715 * (y * y * y))))


def fn_s5post(ypre, u, d, wglu, bglu):
    y = _gelu(ypre + d * u)
    return (y * _sigmoid(dot_nn(y, wglu) + bglu),)


def fn_merge(om, of, os_, g0, g1, g2, wb0, wb1, wb2):
    return (_sigmoid(g0) * dot_nn(om, wb0) + _sigmoid(g1) * dot_nn(of, wb1)
            + _sigmoid(g2) * dot_nn(os_, wb2),)


def fn_add5(a, b):
    return (a[:, 0:LANE] + a[:, LANE:2 * LANE] + a[:, 2 * LANE:3 * LANE] + a[:, 3 * LANE:4 * LANE] + b,)


def fn_addt(a, b):
    return (a + b,)


def fn_s5params(lre, lim, lstep, btr, bti, ctr, cti):
    C = S5_C
    grp = lax.broadcasted_iota(jnp.int32, (LANE, C), 1) >> 6
    expand = jnp.where(lax.broadcasted_iota(jnp.int32, (LANE, C), 0) == grp, 1.0, 0.0)
    lane = _lane()
    st = jnp.where(lane < S5_G, jnp.exp(lstep), 0.0)
    step = jnp.sum(xdot(jnp.broadcast_to(st, (8, LANE)), expand), axis=0, keepdims=True) * 0.125
    zr, zi = lre * step, lim * step
    er = jnp.exp(zr)
    lbr, lbi = er * jnp.cos(zi), er * jnp.sin(zi)
    den = lre * lre + lim * lim
    nr = lbr - 1.0
    cfr = (nr * lre + lbi * lim) / den
    cfi = (lbi * lre - nr * lim) / den
    bbr = cfr * btr - cfi * bti
    bbi = cfr * bti + cfi * btr
    rg = lax.broadcasted_iota(jnp.int32, (BW, C), 0) >> 4
    cg = lax.broadcasted_iota(jnp.int32, (BW, C), 1) >> 6
    mb = jnp.where(rg == cg, 1.0, 0.0)
    b_re = jnp.concatenate([bbr] * S5_G, axis=0) * mb
    b_im = jnp.concatenate([bbi] * S5_G, axis=0) * mb
    ecol = jnp.where(lax.broadcasted_iota(jnp.int32, (LANE, BW), 0)
                     == (lax.broadcasted_iota(jnp.int32, (LANE, BW), 1) & 15), 1.0, 0.0)
    mc = jnp.where((lax.broadcasted_iota(jnp.int32, (C, BW), 0) >> 6)
                   == (lax.broadcasted_iota(jnp.int32, (C, BW), 1) >> 4), 1.0, 0.0)
    c_top = xdot(ctr, ecol) * mc
    c_bot = -(xdot(cti, ecol) * mc)
    return lbr, lbi, b_re, b_im, c_top, c_bot


def s5_params(p):
    def body(lre, lim, ls, btr, bti, ctr, cti, lb_ref, bre_ref, bim_ref, ct_ref, cb_ref):
        lbr, lbi, b_re, b_im, c_top, c_bot = fn_s5params(
            lre[...], lim[...], ls[...], btr[...], bti[...], ctr[...], cti[...])
        lb_ref[0:1, :] = lbr
        lb_ref[1:2, :] = lbi
        bre_ref[...] = b_re.astype(BF16)
        bim_ref[...] = b_im.astype(BF16)
        ct_ref[...] = c_top.astype(BF16)
        cb_ref[...] = c_bot.astype(BF16)

    return pl.pallas_call(
        body, name="s5_params",
        out_shape=[jax.ShapeDtypeStruct((2, S5_C), F32), jax.ShapeDtypeStruct((BW, S5_C), BF16),
                   jax.ShapeDtypeStruct((BW, S5_C), BF16), jax.ShapeDtypeStruct((S5_C, BW), BF16),
                   jax.ShapeDtypeStruct((S5_C, BW), BF16)],
        compiler_params=pltpu.CompilerParams(vmem_limit_bytes=VMEM_LIMIT),
    )(p["lre"], p["lim"], p["lstep"], p["btr"], p["bti"], p["ctr"], p["cti"])


S5_PARAM_NAMES = ("lre", "lim", "lstep", "btr", "bti", "ctr", "cti")


def s5_params_vjp(p, dl_r, dl_i, db_re, db_im, dc_top, dc_bot):
    def body(lre, lim, ls, btr, bti, ctr, cti, dlr, dli, dbr, dbi, dct, dcb, *outs):
        args = [r[...] for r in (lre, lim, ls, btr, bti, ctr, cti)]
        _, vjp = jax.vjp(fn_s5params, *args)
        g = vjp((jnp.sum(dlr[...], axis=0, keepdims=True), jnp.sum(dli[...], axis=0, keepdims=True),
                 dbr[...], dbi[...], dct[...], dcb[...]))
        for o_ref, gv in zip(outs, g):
            o_ref[...] = gv

    return pl.pallas_call(
        body, name="s5_params_vjp",
        out_shape=[jax.ShapeDtypeStruct(p[n].shape, F32) for n in S5_PARAM_NAMES],
        compiler_params=pltpu.CompilerParams(vmem_limit_bytes=VMEM_LIMIT),
    )(*[p[n] for n in S5_PARAM_NAMES], dl_r, dl_i, db_re, db_im, dc_top, dc_bot)


def _attn_tile(q, k, v, cq, ckt, *, hpt, dk, q0, tile):
    tq, S = q.shape[0], k.shape[0]
    row = q0 + lax.broadcasted_iota(jnp.int32, (tq, S), 0)
    col = lax.broadcasted_iota(jnp.int32, (tq, S), 1)
    causal = row >= col
    lane = _lane()
    out = jnp.zeros((tq, LANE), F32)
    for h in range(hpt):
        if hpt > 1:
            mine = (lane >> int(math.log2(LANE // hpt))) == h
            qh = jnp.where(mine, q, 0.0)
        else:
            qh = q
        s = dot_nt(qh, k) * (dk ** -0.5)
        if cq is not None:
            head = tile * hpt + h
            cqh = jnp.sum(jnp.where(lane == F_LANE0 + head, cq, 0.0), axis=1, keepdims=True)
            sub = lax.broadcasted_iota(jnp.int32, (8, 1), 0)
            ckh = jnp.sum(jnp.where(sub == head, ckt, 0.0), axis=0, keepdims=True)
            s = s + (cqh - ckh)
        s = jnp.where(causal, s, NEG_INF)
        m = lax.stop_gradient(jnp.max(s, axis=-1, keepdims=True))
        e = jnp.exp(s - m)
        p = e / jnp.sum(e, axis=-1, keepdims=True)
        oh = dot_nn(p, v)
        out = out + (jnp.where(mine, oh, 0.0) if hpt > 1 else oh)
    return out


def attention(name, q, k, v, *, B, S, ntile, hpt, dk, qc=0, kc=0, vc=0, cum=None, ckt=None, do=None, tq=256):
    tq = _pick(S, (tq, 128))
    nq = S // tq
    M = B * S
    bias = cum is not None
    kw = dict(hpt=hpt, dk=dk)

    def load(refs, sk):
        q_ref, k_ref, v_ref = refs[:3]
        qv, kv, vv = q_ref[...].astype(F32), k_ref[0:sk, :].astype(F32), v_ref[0:sk, :].astype(F32)
        if bias:
            return qv, kv, vv, refs[3][...], refs[4][0, :, 0:sk]
        return qv, kv, vv, None, None

    nin = 5 if bias else 3

    def per_query_block(run):
        for g in range(nq):
            @pl.when(pl.program_id(2) == g)
            def _(g=g):
                run(g, (g + 1) * tq)

    def fwd_body(*refs):
        tile = pl.program_id(1)

        def run(g, sk):
            qv, kv, vv, cq, ck = load(refs, sk)
            o = _attn_tile(qv, kv, vv, cq, ck, q0=g * tq, tile=tile, **kw)
            refs[nin][...] = o.astype(refs[nin].dtype)

        per_query_block(run)

    def bwd_body(*refs):
        outs = refs[nin + 1:]
        tile = pl.program_id(1)

        @pl.when(pl.program_id(2) == 0)
        def _():
            for o_ref in (outs[1], outs[2]) + ((outs[4],) if bias else ()):
                o_ref[...] = jnp.zeros_like(o_ref)

        def run(g, sk):
            qv, kv, vv, cq, ck = load(refs, sk)
            dov = refs[nin][...].astype(F32)
            if bias:
                f = lambda a, b, c, d, e: _attn_tile(a, b, c, d, e, q0=g * tq, tile=tile, **kw)
                _, vjp = jax.vjp(f, qv, kv, vv, cq, ck)
            else:
                f = lambda a, b, c: _attn_tile(a, b, c, None, None, q0=g * tq, tile=tile, **kw)
                _, vjp = jax.vjp(f, qv, kv, vv)
            gr = vjp(dov)
            outs[0][...] = gr[0]
            outs[1][0:sk, :] += gr[1]
            outs[2][0:sk, :] += gr[2]
            if bias:
                outs[3][...] = gr[3]
                outs[4][0, :, 0:sk] += gr[4]

        per_query_block(run)

    qspec = lambda c: pl.BlockSpec((tq, LANE), lambda b, t, i: (b * nq + i, c + t))
    kspec = lambda c: pl.BlockSpec((S, LANE), lambda b, t, i: (b, c + t))
    in_specs, args = [qspec(qc), kspec(kc), kspec(vc)], [q, k, v]
    if bias:
        in_specs += [pl.BlockSpec((tq, LANE), lambda b, t, i: (b * nq + i, 0)),
                     pl.BlockSpec((1, 8, S), lambda b, t, i: (b, 0, 0))]
        args += [cum, ckt]
    if do is None:
        return pl.pallas_call(
            fwd_body, name=name, grid=(B, ntile, nq), in_specs=in_specs, out_specs=qspec(0),
            out_shape=jax.ShapeDtypeStruct((M, ntile * LANE), BF16),
            compiler_params=_cp("parallel", "parallel", "parallel"),
        )(*args)
    in_specs.append(qspec(0))
    args.append(do)
    out_specs = [qspec(0), kspec(0), kspec(0)]
    out_shape = [jax.ShapeDtypeStruct((M, ntile * LANE), F32)] * 3
    if bias:
        out_specs += [qspec(0), pl.BlockSpec((1, 8, S), lambda b, t, i: (b * ntile + t, 0, 0))]
        out_shape += [jax.ShapeDtypeStruct((M, ntile * LANE), F32),
                      jax.ShapeDtypeStruct((B * ntile, 8, S), F32)]
    return pl.pallas_call(
        bwd_body, name=name, grid=(B, ntile, nq), in_specs=in_specs, out_specs=out_specs,
        out_shape=out_shape, compiler_params=_cp("parallel", "parallel", "arbitrary"),
    )(*args)


def seq_cumsum(name, x, *, B, S, reverse):
    nb = S // LANE

    def body(x_ref, o_ref):
        r = lax.broadcasted_iota(jnp.int32, (LANE, LANE), 0)
        c = lax.broadcasted_iota(jnp.int32, (LANE, LANE), 1)
        tri = jnp.where((r <= c) if reverse else (r >= c), 1.0, 0.0)
        carry = jnp.zeros((1, LANE), F32)
        for blk in (range(nb - 1, -1, -1) if reverse else range(nb)):
            xb = x_ref[blk * LANE:(blk + 1) * LANE, :]
            o_ref[blk * LANE:(blk + 1) * LANE, :] = xdot(tri, xb) + carry
            carry = carry + jnp.sum(xb, axis=0, keepdims=True)

    return pl.pallas_call(
        body, name=name, grid=(B,), in_specs=[pl.BlockSpec((S, LANE), lambda b: (b, 0))],
        out_specs=pl.BlockSpec((S, LANE), lambda b: (b, 0)),
        out_shape=jax.ShapeDtypeStruct(x.shape, F32), compiler_params=_cp("parallel"),
    )(x)


SCAN_ROWS = 64


def _shift_rows(ref, r0, rows, d, up):
    if d % 8 == 0:
        return ref[pl.ds(r0 + d if up else r0 - d, rows), :]
    if up:
        win = ref[pl.ds(r0, rows + 8), :]
        return pltpu.roll(win, rows + 8 - d, 0)[0:rows, :]
    win = ref[pl.ds(r0 - 8, rows + 8), :]
    return pltpu.roll(win, d, 0)[8:rows + 8, :]


def s5_scan(name, x_re, x_im, lam, *, B, S, reverse, state=None):
    C = S5_C
    cw = LANE
    R = _pick(S, (SCAN_ROWS,))
    pad = max(S // 2, 8)
    nsteps = int(math.log2(S))
    assert 1 << nsteps == S
    base = 0 if reverse else pad
    with_grad = state is not None

    def body(*refs):
        if with_grad:
            xr, xi, lam_ref, sr, si, o_r, o_i, dl_r, dl_i, a_r, a_i, b_r, b_i = refs
        else:
            xr, xi, lam_ref, o_r, o_i, a_r, a_i, b_r, b_i = refs
        zero = jnp.zeros((pad, cw), F32)
        z0 = S if reverse else 0
        for buf in (a_r, a_i, b_r, b_i):
            buf[z0:z0 + pad, :] = zero
        a_r[base:base + S, :] = xr[...]
        a_i[base:base + S, :] = xi[...]
        mr = lam_ref[0:1, :]
        mi = -lam_ref[1:2, :] if reverse else lam_ref[1:2, :]
        src, dst = (a_r, a_i), (b_r, b_i)
        for step in range(nsteps):
            d = 1 << step
            last = step == nsteps - 1

            def chunk(c, _, src=src, dst=dst, d=d, last=last, mr=mr, mi=mi):
                r0 = pl.multiple_of(base + c * R, 8)
                pr = _shift_rows(src[0], r0, R, d, reverse)
                pi = _shift_rows(src[1], r0, R, d, reverse)
                nr = src[0][pl.ds(r0, R), :] + mr * pr - mi * pi
                ni = src[1][pl.ds(r0, R), :] + mr * pi + mi * pr
                if last:
                    o0 = pl.multiple_of(c * R, 8)
                    o_r[pl.ds(o0, R), :] = nr
                    o_i[pl.ds(o0, R), :] = ni
                else:
                    dst[0][pl.ds(r0, R), :] = nr
                    dst[1][pl.ds(r0, R), :] = ni
                return 0

            lax.fori_loop(0, S // R, chunk, 0)
            mr, mi = mr * mr - mi * mi, 2.0 * mr * mi
            src, dst = dst, src
        if with_grad:
            def fold(v):
                return jnp.sum(v.reshape(R // 8, 8, cw), axis=0)

            def accum(c, carry, first=False):
                r0 = 0 if first else pl.multiple_of(c * R, 8)
                gr, gi = o_r[pl.ds(r0, R), :], o_i[pl.ds(r0, R), :]
                if first:
                    keep = lax.broadcasted_iota(jnp.int32, (R, 1), 0) >= 1
                    pr = jnp.where(keep, pltpu.roll(sr[0:R, :], 1, 0), 0.0)
                    pi = jnp.where(keep, pltpu.roll(si[0:R, :], 1, 0), 0.0)
                else:
                    pr = _shift_rows(sr, r0, R, 1, False)
                    pi = _shift_rows(si, r0, R, 1, False)
                return (carry[0] + fold(gr * pr + gi * pi), carry[1] + fold(gi * pr - gr * pi))

            acc = accum(0, (jnp.zeros((8, cw), F32), jnp.zeros((8, cw), F32)), first=True)
            acc = lax.fori_loop(1, S // R, accum, acc)
            dl_r[...] = acc[0]
            dl_i[...] = acc[1]

    seq = pl.BlockSpec((S, cw), lambda b, j: (b, j))
    in_specs = [seq, seq, pl.BlockSpec((2, cw), lambda b, j: (0, j))]
    args = [x_re, x_im, lam]
    out_specs = [seq, seq]
    out_shape = [jax.ShapeDtypeStruct(x_re.shape, F32)] * 2
    if with_grad:
        in_specs += [seq, seq]
        args += list(state)
        out_specs += [pl.BlockSpec((8, cw), lambda b, j: (b, j))] * 2
        out_shape += [jax.ShapeDtypeStruct((B * 8, C), F32)] * 2
    return pl.pallas_call(
        body, name=name, grid=(B, C // cw), in_specs=in_specs, out_specs=out_specs, out_shape=out_shape,
        scratch_shapes=[pltpu.VMEM((S + pad, cw), F32)] * 4,
        compiler_params=_cp("parallel", "parallel"),
    )(*args)


CONV_CW = 256


def _conv_taps(ref, r0, rows, first):
    cur = ref[pl.ds(r0, rows), :]
    if first:
        row = lax.broadcasted_iota(jnp.int32, (rows, 1), 0)
        p1 = jnp.where(row >= 1, pltpu.roll(cur, 1, 0), 0.0)
        p2 = jnp.where(row >= 2, pltpu.roll(cur, 2, 0), 0.0)
    else:
        p1 = _shift_rows(ref, r0, rows, 1, False)
        p2 = _shift_rows(ref, r0, rows, 2, False)
    return cur, p1, p2


def _conv_apply(w_ref, taps):
    return w_ref[2:3, :] * taps[0] + w_ref[1:2, :] * taps[1] + w_ref[0:1, :] * taps[2]


def conv_gate_fwd(up, conv_w, *, B, S):
    M, F2 = up.shape
    F = F2 // 2
    cw = _pick(F, (CONV_CW, LANE))
    nf = F // cw
    R = _pick(S, (SCAN_ROWS,))

    def body(g_ref, v_ref, wg_ref, wv_ref, o_ref):
        def chunk(c, _, first=False):
            r0 = 0 if first else pl.multiple_of(c * R, 8)
            cg = _conv_apply(wg_ref, _conv_taps(g_ref, r0, R, first))
            cv = _conv_apply(wv_ref, _conv_taps(v_ref, r0, R, first))
            o_ref[pl.ds(r0, R), :] = (cg * _sigmoid(cg) * cv).astype(o_ref.dtype)
            return 0

        chunk(0, 0, first=True)
        lax.fori_loop(1, S // R, chunk, 0)

    seq = lambda off: pl.BlockSpec((S, cw), lambda b, j: (b, off + j))
    wsp = lambda off: pl.BlockSpec((3, cw), lambda b, j: (0, off + j))
    return pl.pallas_call(
        body, name="conv_gate", grid=(B, nf), in_specs=[seq(0), seq(nf), wsp(0), wsp(nf)],
        out_specs=seq(0), out_shape=jax.ShapeDtypeStruct((M, F), BF16),
        compiler_params=_cp("parallel", "parallel"),
    )(up, up, conv_w, conv_w)


def conv_gate_bwd(up, conv_w, dact, *, B, S):
    M, F2 = up.shape
    F = F2 // 2
    cw = _pick(F, (CONV_CW, LANE))
    nf = F // cw
    R = _pick(S, (SCAN_ROWS,))
    nchunk = S // R

    def body(s_ref, p_ref, ws_ref, wp_ref, da_ref, du_ref, dw_ref, dc_ref, silu_ref):
        is_gate = pl.program_id(1) == 0
        ex = pl.program_id(2)
        dc_ref[S:S + 8, :] = jnp.zeros((8, cw), F32)

        def fold(v):
            return jnp.sum(v.reshape(R // 8, 8, cw), axis=0)

        first_b = ex == 0

        def pass1(gate_step):
            def chunk(c, acc, first=False):
                r0 = 0 if first else pl.multiple_of(c * R, 8)
                taps = _conv_taps(s_ref, r0, R, first)
                da = da_ref[pl.ds(r0, R), :]
                if gate_step:
                    cp = _conv_apply(wp_ref, _conv_taps(p_ref, r0, R, first))
                    cs = _conv_apply(ws_ref, taps)
                    sg = _sigmoid(cs)
                    silu_ref[ex, pl.ds(r0, R), :] = cs * sg
                    dc = da * cp * (sg * (1.0 + cs * (1.0 - sg)))
                else:
                    dc = da * silu_ref[ex, pl.ds(r0, R), :]
                dc_ref[pl.ds(r0, R), :] = dc
                return (acc[0] + fold(dc * taps[2]), acc[1] + fold(dc * taps[1]), acc[2] + fold(dc * taps[0]))

            z = jnp.zeros((8, cw), F32)
            acc = chunk(0, (z, z, z), first=True)
            acc = lax.fori_loop(1, nchunk, chunk, acc)
            for tap in range(3):
                tot = jnp.sum(acc[tap], axis=0, keepdims=True)

                @pl.when(first_b)
                def _(tap=tap, tot=tot):
                    dw_ref[tap:tap + 1, :] = tot

                @pl.when(jnp.logical_not(first_b))
                def _(tap=tap, tot=tot):
                    dw_ref[tap:tap + 1, :] += tot

        @pl.when(is_gate)
        def _():
            pass1(True)

        @pl.when(jnp.logical_not(is_gate))
        def _():
            pass1(False)

        def pass2(c, _):
            r0 = pl.multiple_of(c * R, 16)
            n0 = dc_ref[pl.ds(r0, R), :]
            n1 = _shift_rows(dc_ref, r0, R, 1, True)
            n2 = _shift_rows(dc_ref, r0, R, 2, True)
            du = ws_ref[2:3, :] * n0 + ws_ref[1:2, :] * n1 + ws_ref[0:1, :] * n2
            du_ref[pl.ds(r0, R), :] = du.astype(du_ref.dtype)
            return 0

        lax.fori_loop(0, nchunk, pass2, 0)

    seq = lambda f: pl.BlockSpec((S, cw), lambda j, t, b: (b, f(j, t)))
    wsp = lambda f: pl.BlockSpec((3, cw), lambda j, t, b: (0, f(j, t)))
    same, value, act_col = (lambda j, t: j + t * nf), (lambda j, t: j + nf), (lambda j, t: j)
    return pl.pallas_call(
        body, name="conv_gate_vjp", grid=(nf, 2, B),
        in_specs=[seq(same), seq(value), wsp(same), wsp(value), seq(act_col)],
        out_specs=[seq(same), wsp(same)],
        out_shape=[jax.ShapeDtypeStruct((M, F2), BF16), jax.ShapeDtypeStruct((3, F2), F32)],
        scratch_shapes=[pltpu.VMEM((S + 8, cw), F32), pltpu.VMEM((B, S, cw), F32)],
        compiler_params=_cp("parallel", "arbitrary", "arbitrary"),
    )(up, up, conv_w, conv_w, dact)


def loss_head(y, target):
    M, D = y.shape
    tm = _pick(M, (256, 128, 64, 32, 16, 8))

    def body(y_ref, t_ref, dy_ref, l_ref):
        diff = y_ref[...] - t_ref[...]
        dy_ref[...] = diff * (1.0 / D)
        part = jnp.sum(jnp.sum(diff * diff, axis=1, keepdims=True), axis=0, keepdims=True)

        @pl.when(pl.program_id(0) == 0)
        def _():
            l_ref[...] = jnp.zeros_like(l_ref)

        l_ref[...] += part

    row = pl.BlockSpec((tm, D), lambda i: (i, 0))
    return pl.pallas_call(
        body, name="loss_head", grid=(M // tm,), in_specs=[row, row],
        out_specs=[row, pl.BlockSpec((8, LANE), lambda i: (0, 0))],
        out_shape=[jax.ShapeDtypeStruct((M, D), F32), jax.ShapeDtypeStruct((8, LANE), F32)],
        compiler_params=_cp("arbitrary"),
    )(y, target)


def adamw(name, w, g, m, v):
    R, C = w.shape
    tr = _pick(R, (256, 128, 64, 32, 16, 8))

    def body(w_ref, g_ref, m_ref, v_ref, d_ref, nm_ref, nv_ref):
        gv = g_ref[...]
        nm = ADAM_B1 * m_ref[...] + (1.0 - ADAM_B1) * gv
        nv = ADAM_B2 * v_ref[...] + (1.0 - ADAM_B2) * (gv * gv)
        m_hat = nm / (1.0 - ADAM_B1 ** ADAM_STEP)
        v_hat = nv / (1.0 - ADAM_B2 ** ADAM_STEP)
        d_ref[...] = -ADAM_LR * (m_hat / (jnp.sqrt(v_hat) + ADAM_EPS) + ADAM_WD * w_ref[...])
        nm_ref[...] = nm
        nv_ref[...] = nv

    blk = pl.BlockSpec((tr, C), lambda i: (i, 0))
    return pl.pallas_call(
        body, name=name, grid=(R // tr,), in_specs=[blk] * 4, out_specs=[blk] * 3,
        out_shape=[jax.ShapeDtypeStruct((R, C), F32)] * 3, compiler_params=_cp("parallel"),
    )(w, g, m, v)


def _seg(D):
    o = 3 * D
    return dict(ckv=o, fq=o + 256, fk=o + 512, fv=o + 768, u=o + 1024, small=o + 1280, cq=o + 1536, P=o + 1920)


def _pad_last(a, n):
    return jnp.pad(a, [(0, 0)] * (a.ndim - 1) + [(0, n - a.shape[-1])])


def _place(a, lo, n=LANE):
    return jnp.pad(a, [(0, 0)] * (a.ndim - 1) + [(lo, n - lo - a.shape[-1])])


def _in_segments(D):
    s = _seg(D)
    sm = s["small"]
    return ((0, 384, s["cq"]), (384, 640, s["ckv"]), (640, 672, sm + NOPE), (672, 1440, s["fq"]),
            (1440, 1444, sm + F_LANE0), (1444, 1700, s["u"]), (1700, 1700 + 3 * D, 0))


def _chip_pieces(win4, lo, hi):
    cw = win4.shape[-1]
    out = []
    for k in range(N_CHIPS):
        a, b = max(lo, k * cw), min(hi, (k + 1) * cw)
        if a < b:
            out.append(win4[:, k, :, a - k * cw:b - k * cw])
    return out


def prep_weights(w):
    L, D = w["attn_norm_g"].shape
    win = w["w_in"]
    z = lambda n: [jnp.zeros((L, D, n), win.dtype)]
    cols = lambda lo, hi: _chip_pieces(win, lo, hi)
    g0 = 1700
    wp = jnp.concatenate(
        cols(g0, g0 + 3 * D) + cols(384, 640) + cols(672, 1440) + cols(1444, 1700) + z(NOPE) + cols(640, 672)
        + cols(1440, 1444) + z(LANE - F_LANE0 - N_HEADS) + z(LANE) + cols(0, 384), axis=-1)
    wukv, wb = w["w_ukv"], w["w_branch"]
    row3 = lambda a: a[:, None, :]
    return dict(
        g1=row3(w["attn_norm_g"]), Wp=wp, gql=row3(w["q_lat_norm_g"]), gkvl=row3(w["kv_lat_norm_g"]),
        Wuq=_pad_last(w["w_uq"], LANE).reshape(L, Q_RANK, N_HEADS * LANE),
        Wk=_pad_last(wukv[..., :NOPE], LANE).reshape(L, KV_RANK, N_HEADS * LANE),
        Wv=_pad_last(wukv[..., NOPE:], LANE).reshape(L, KV_RANK, N_HEADS * LANE),
        gq=row3(_pad_last(w["mla_q_norm_g"], LANE)), gk=row3(_pad_last(w["mla_k_norm_g"], LANE)),
        gfq=row3(jnp.tile(w["fox_q_norm_g"], (1, 2))), gfk=row3(jnp.tile(w["fox_k_norm_g"], (1, 2))),
        fbias=row3(_place(w["fox_f_bias"], F_LANE0)),
        lre=w["s5_lambda_re"].reshape(L, 1, S5_C), lim=w["s5_lambda_im"].reshape(L, 1, S5_C),
        lstep=row3(_pad_last(w["s5_log_step"], LANE)),
        btr=jnp.transpose(w["s5_b_re"], (0, 3, 1, 2)).reshape(L, S5_H, S5_C),
        bti=jnp.transpose(w["s5_b_im"], (0, 3, 1, 2)).reshape(L, S5_H, S5_C),
        ctr=_pad_last(jnp.transpose(w["s5_c_re"], (0, 1, 3, 2)).reshape(L, S5_C, S5_H), LANE),
        cti=_pad_last(jnp.transpose(w["s5_c_im"], (0, 1, 3, 2)).reshape(L, S5_C, S5_H), LANE),
        s5d=w["s5_d"].reshape(L, 1, BW), Wglu=w["s5_w_glu"], bglu=row3(w["s5_b_glu"]),
        Wb0=jnp.pad(wb[:, 0].reshape(L, N_HEADS, V_DIM, D), ((0, 0), (0, 0), (0, LANE - V_DIM), (0, 0))
                    ).reshape(L, N_HEADS * LANE, D),
        Wb1=wb[:, 1], Wb2=wb[:, 2], Wout=w["w_out"], g2=row3(w["ffn_norm_g"]), Wup=w["w_up"],
        convw=w["ffn_conv_w"], Wdown=w["w_down"],
    )


BIG_KEYS = ("Wp", "Wuq", "Wk", "Wv", "Wout", "Wup", "Wdown")


def with_transposes(P):
    out = dict(P)
    for k in BIG_KEYS:
        out[k + "T"] = jnp.swapaxes(P[k], 1, 2)
    return out


def layer_params(P, l):
    return {k: (v if k in BIG_KEYS or k[:-1] in BIG_KEYS else v[l]) for k, v in P.items()}


def unprep_grads(G, D):
    L = G["g1"].shape[0]
    dwp = G["Wp"]
    segs = _in_segments(D)
    cw = segs[-1][1] // N_CHIPS
    chips = []
    for k in range(N_CHIPS):
        pieces = []
        for lo, hi, at in segs:
            a, b = max(lo, k * cw), min(hi, (k + 1) * cw)
            if a < b:
                pieces.append(dwp[..., at + a - lo:at + b - lo])
        chips.append(jnp.concatenate(pieces, axis=-1))
    w_in = jnp.stack(chips, axis=1)
    heads = lambda a, rows, keep: a.reshape(L, rows, N_HEADS, LANE)[..., :keep]
    wb0 = G["Wb0"].reshape(L, N_HEADS, LANE, D)[:, :, :V_DIM].reshape(L, BW, D)
    gf = lambda a: a[:, 0, :FOX_DIM] + a[:, 0, FOX_DIM:]
    return dict(
        attn_norm_g=G["g1"][:, 0], w_in=w_in, q_lat_norm_g=G["gql"][:, 0], w_uq=heads(G["Wuq"], Q_RANK, QK_DIM),
        kv_lat_norm_g=G["gkvl"][:, 0],
        w_ukv=jnp.concatenate([heads(G["Wk"], KV_RANK, NOPE), heads(G["Wv"], KV_RANK, V_DIM)], axis=-1),
        mla_q_norm_g=G["gq"][:, 0, :QK_DIM], mla_k_norm_g=G["gk"][:, 0, :QK_DIM],
        fox_q_norm_g=gf(G["gfq"]), fox_k_norm_g=gf(G["gfk"]),
        fox_f_bias=G["fbias"][:, 0, F_LANE0:F_LANE0 + N_HEADS],
        s5_lambda_re=G["lre"].reshape(L, S5_G, S5_P), s5_lambda_im=G["lim"].reshape(L, S5_G, S5_P),
        s5_b_re=jnp.transpose(G["btr"].reshape(L, S5_H, S5_G, S5_P), (0, 2, 3, 1)),
        s5_b_im=jnp.transpose(G["bti"].reshape(L, S5_H, S5_G, S5_P), (0, 2, 3, 1)),
        s5_c_re=jnp.transpose(G["ctr"][..., :S5_H].reshape(L, S5_G, S5_P, S5_H), (0, 1, 3, 2)),
        s5_c_im=jnp.transpose(G["cti"][..., :S5_H].reshape(L, S5_G, S5_P, S5_H), (0, 1, 3, 2)),
        s5_d=G["s5d"].reshape(L, S5_G, S5_H), s5_log_step=G["lstep"][:, 0, :S5_G],
        s5_w_glu=G["Wglu"], s5_b_glu=G["bglu"][:, 0],
        w_branch=jnp.stack([wb0, G["Wb1"], G["Wb2"]], axis=1), w_out=G["Wout"], ffn_norm_g=G["g2"][:, 0],
        w_up=G["Wup"], ffn_conv_w=G["convw"], w_down=G["Wdown"],
    )


def rope_tables(positions):
    inv_freq = ROPE_THETA ** (-jnp.arange(0, ROPE, 2, dtype=F32) / ROPE)
    ang = positions.astype(F32)[..., None] * inv_freq
    cos, sin = jnp.cos(ang), jnp.sin(ang)
    ones = jnp.ones(ang.shape[:-1] + (NOPE,), F32)
    zeros = jnp.zeros(ang.shape[:-1] + (LANE - NOPE - ROPE,), F32)
    cos_t = jnp.concatenate([ones, cos, cos, zeros], axis=-1)
    sin_t = jnp.concatenate([0.0 * ones, sin, sin, zeros], axis=-1)
    return cos_t.reshape(-1, LANE), sin_t.reshape(-1, LANE)


def fn_rms_res(n):
    return lambda x, g: (_rms(x, g, n), x)


def _s5_mats(p):
    return s5_params({k: p[k] for k in S5_PARAM_NAMES})


def layer_fwd(x, p, l, cos_t, sin_t, B, S):
    M, D = x.shape
    s = _seg(D)
    sm = s["small"] // LANE
    head = lambda j: j
    h = rowwise("rms_attn", fn_rms(D), [(x, D, 0)], [p["g1"]], [(D, BF16)])[0]
    proj = mm(h, p["Wp"], "nn", name="in_proj", layer=l)
    cnq = rowwise("latq_norm", fn_rms(Q_RANK), [(proj, Q_RANK, s["cq"] // Q_RANK)], [p["gql"]], [(Q_RANK, BF16)])[0]
    cnkv = rowwise("latkv_norm", fn_rms(KV_RANK), [(proj, KV_RANK, s["ckv"] // KV_RANK)], [p["gkvl"]],
                   [(KV_RANK, BF16)])[0]
    qraw = mm(cnq, p["Wuq"], "nn", name="q_up", layer=l)
    kn = mm(cnkv, p["Wk"], "nn", name="k_up", layer=l)
    v5 = mm(cnkv, p["Wv"], "nn", name="v_up", out_dtype=BF16, layer=l)
    qrot = rowwise("q_post", fn_qpost, [(qraw, LANE, head), (cos_t, LANE, 0), (sin_t, LANE, 0)], [p["gq"]],
                   [(LANE, BF16)], nj=N_HEADS)[0]
    krot = rowwise("k_post", fn_kpost, [(kn, LANE, head), (proj, LANE, sm), (cos_t, LANE, 0), (sin_t, LANE, 0)],
                   [p["gk"]], [(LANE, BF16)], nj=N_HEADS)[0]
    omla = attention("mla_attn", qrot, krot, v5, B=B, S=S, ntile=N_HEADS, hpt=1, dk=QK_DIM)
    fq0, fk0 = s["fq"] // LANE, s["fk"] // LANE
    qf = rowwise("foxq_norm", fn_foxnorm, [(proj, LANE, lambda j: fq0 + j)], [p["gfq"]], [(LANE, BF16)], nj=2)[0]
    kf = rowwise("foxk_norm", fn_foxnorm, [(proj, LANE, lambda j: fk0 + j)], [p["gfk"]], [(LANE, BF16)], nj=2)[0]
    lf = rowwise("fgate", fn_fgate, [(proj, LANE, sm)], [p["fbias"]], [(LANE, F32)])[0]
    cum = seq_cumsum("fox_cumsum", lf, B=B, S=S, reverse=False)
    ckt = _pad_rows8(jnp.transpose(cum.reshape(B, S, LANE)[:, :, F_LANE0:F_LANE0 + N_HEADS], (0, 2, 1)))
    ofox = attention("fox_attn", qf, kf, proj, B=B, S=S, ntile=2, hpt=2, dk=FOX_DIM, vc=s["fv"] // LANE,
                     cum=cum, ckt=ckt)
    lam, bre, bim, ctop, cbot = _s5_mats(p)
    u16 = proj[:, s["u"]:s["u"] + BW].astype(BF16)
    bur = mm(u16, bre, "nn", name="s5_bu_re")
    bui = mm(u16, bim, "nn", name="s5_bu_im")
    sr, si = s5_scan("s5_scan", bur, bui, lam, B=B, S=S, reverse=False)
    ypre = mm(si, cbot, "nn", name="s5_y_im", add=mm(sr, ctop, "nn", name="s5_y_re"))
    os5 = rowwise("s5_post", fn_s5post, [(ypre, BW, 0), (proj, BW, s["u"] // BW)],
                  [p["s5d"], p["Wglu"], p["bglu"]], [(BW, BF16)])[0]
    merged = rowwise("merge", fn_merge,
                     [(omla, N_HEADS * LANE, 0), (ofox, BW, 0), (os5, BW, 0), (proj, D, 0), (proj, D, 1), (proj, D, 2)],
                     [p["Wb0"], p["Wb1"], p["Wb2"]], [(D, BF16)])[0]
    xmid = mm(merged, p["Wout"], "nn", name="out_proj", add=x, layer=l)
    h2 = rowwise("rms_ffn", fn_rms(D), [(xmid, D, 0)], [p["g2"]], [(D, BF16)])[0]
    up = mm(h2, p["Wup"], "nn", name="ffn_up", layer=l)
    act = conv_gate_fwd(up, p["convw"], B=B, S=S)
    xout = mm(act, p["Wdown"], "nn", name="ffn_down", add=xmid, layer=l)
    saved = dict(x=x, h=h, proj=proj, cnq=cnq, cnkv=cnkv, qraw=qraw, kn=kn, v5=v5, qrot=qrot, krot=krot, qf=qf,
                 kf=kf, cum=cum, ckt=ckt, omla=omla, ofox=ofox, os5=os5, u16=u16, sr=sr, si=si, ypre=ypre,
                 merged=merged, xmid=xmid, h2=h2, up=up, act=act)
    return xout, saved


def _pad_rows8(a):
    return jnp.pad(a, ((0, 0), (0, 8 - a.shape[1]), (0, 0)))


STACKED = ("Wp", "Wout", "Wup", "Wdown")


def layer_bwd(dx, p, l, sv, cos_t, sin_t, B, S, stacks):
    M, D = dx.shape
    slot = lambda k: (stacks["L"], l, stacks.get(k))
    s = _seg(D)
    sm = s["small"] // LANE
    head = lambda j: j
    proj = sv["proj"]
    dact = mm(dx, p["WdownT"], "nn", name="ffn_down_dx", layer=l)
    d_wdown = mm(sv["act"], dx, "tn", name="ffn_down_dw", slot=slot("Wdown"))
    dup, d_convw = conv_gate_bwd(sv["up"], p["convw"], dact, B=B, S=S)
    dh2 = mm(dup, p["WupT"], "nn", name="ffn_up_dx", layer=l)
    d_wup = mm(sv["h2"], dup, "tn", name="ffn_up_dw", slot=slot("Wup"))
    (dxmid,), (d_g2,) = rowwise_vjp("rms_ffn_vjp", fn_rms_res(D), [(sv["xmid"], D, 0)], [p["g2"]],
                                    [(dh2, D, 0), (dx, D, 0)], [0])
    dmerged = mm(dxmid, p["WoutT"], "nn", name="out_proj_dx", layer=l)
    d_wout = mm(sv["merged"], dxmid, "tn", name="out_proj_dw", slot=slot("Wout"))
    (dom, dof, dos, dg0, dg1, dg2), (d_wb0, d_wb1, d_wb2) = rowwise_vjp(
        "merge_vjp", fn_merge,
        [(sv["omla"], N_HEADS * LANE, 0), (sv["ofox"], BW, 0), (sv["os5"], BW, 0), (proj, D, 0), (proj, D, 1),
         (proj, D, 2)], [p["Wb0"], p["Wb1"], p["Wb2"]], [(dmerged, D, 0)], [0, 1, 2, 3, 4, 5], tm=256,
        gdt=[BF16, BF16, F32, BF16, BF16, BF16])
    lam, bre, bim, ctop, cbot = _s5_mats(p)
    (dypre, du_a), (d_s5d, d_wglu, d_bglu) = rowwise_vjp(
        "s5_post_vjp", fn_s5post, [(sv["ypre"], BW, 0), (proj, BW, s["u"] // BW)],
        [p["s5d"], p["Wglu"], p["bglu"]], [(dos, BW, 0)], [0, 1], gdt=[BF16, F32])
    dsr = mm(dypre, ctop.T, "nn", name="s5_y_re_dx")
    dsi = mm(dypre, cbot.T, "nn", name="s5_y_im_dx")
    d_ctop = mm(sv["sr"], dypre, "tn", name="s5_y_re_dw")
    d_cbot = mm(sv["si"], dypre, "tn", name="s5_y_im_dw")
    gr, gi, dl_r, dl_i = s5_scan("s5_scan_vjp", dsr, dsi, lam, B=B, S=S, reverse=True, state=(sv["sr"], sv["si"]))
    du = mm(gi, bim.T, "nn", name="s5_bu_im_dx", out_dtype=BF16,
            add=mm(gr, bre.T, "nn", name="s5_bu_re_dx", add=du_a))
    d_bre = mm(sv["u16"], gr, "tn", name="s5_bu_re_dw")
    d_bim = mm(sv["u16"], gi, "tn", name="s5_bu_im_dw")
    d_s5 = s5_params_vjp({k: p[k] for k in S5_PARAM_NAMES}, dl_r, dl_i, d_bre, d_bim, d_ctop, d_cbot)
    dqf, dkf, dfv, dcq_t, dckt_t = attention("fox_attn_vjp", sv["qf"], sv["kf"], proj, B=B, S=S, ntile=2, hpt=2,
                                             dk=FOX_DIM, vc=s["fv"] // LANE, cum=sv["cum"], ckt=sv["ckt"], do=dof)
    dck = dckt_t.reshape(B, 2, 8, S)
    dck = jnp.transpose(dck[:, 0, :N_HEADS] + dck[:, 1, :N_HEADS], (0, 2, 1)).reshape(M, N_HEADS)
    dcum = rowwise("fox_dcum", lambda a, b: (a[:, 0:LANE] + a[:, LANE:2 * LANE] + b,),
                   [(dcq_t, 2 * LANE, 0), (_place(dck, F_LANE0), LANE, 0)], [], [(LANE, F32)])[0]
    dlf = seq_cumsum("fox_cumsum_vjp", dcum, B=B, S=S, reverse=True)
    (dsmall_f,), (d_fbias,) = rowwise_vjp("fgate_vjp", fn_fgate, [(proj, LANE, sm)], [p["fbias"]],
                                          [(dlf, LANE, 0)], [0])
    fq0, fk0 = s["fq"] // LANE, s["fk"] // LANE
    (dfq,), (d_gfq,) = rowwise_vjp("foxq_norm_vjp", fn_foxnorm, [(proj, LANE, lambda j: fq0 + j)], [p["gfq"]],
                                   [(dqf, LANE, head)], [0], nj=2, gdt=[BF16])
    (dfk,), (d_gfk,) = rowwise_vjp("foxk_norm_vjp", fn_foxnorm, [(proj, LANE, lambda j: fk0 + j)], [p["gfk"]],
                                   [(dkf, LANE, head)], [0], nj=2, gdt=[BF16])
    dqrot, dkrot, dv5 = attention("mla_attn_vjp", sv["qrot"], sv["krot"], sv["v5"], B=B, S=S, ntile=N_HEADS,
                                  hpt=1, dk=QK_DIM, do=dom)
    (dqraw,), (d_gq,) = rowwise_vjp("q_post_vjp", fn_qpost,
                                    [(sv["qraw"], LANE, head), (cos_t, LANE, 0), (sin_t, LANE, 0)], [p["gq"]],
                                    [(dqrot, LANE, head)], [0], nj=N_HEADS, gdt=[BF16])
    (dkn, dsmall_k), (d_gk,) = rowwise_vjp(
        "k_post_vjp", fn_kpost, [(sv["kn"], LANE, head), (proj, LANE, sm), (cos_t, LANE, 0), (sin_t, LANE, 0)],
        [p["gk"]], [(dkrot, LANE, head)], [0, 1], nj=N_HEADS, gdt=[BF16, F32])
    dsmall = rowwise("small_sum", fn_add5, [(dsmall_k, N_HEADS * LANE, 0), (dsmall_f, LANE, 0)], [], [(LANE, BF16)])[0]
    dcnq = mm(dqraw, p["WuqT"], "nn", name="q_up_dx", layer=l)
    d_wuq = mm(sv["cnq"], dqraw, "tn", name="q_up_dw")
    dcnkv = mm(dv5, p["WvT"], "nn", name="v_up_dx", layer=l, add=mm(dkn, p["WkT"], "nn", name="k_up_dx", layer=l))
    d_wk = mm(sv["cnkv"], dkn, "tn", name="k_up_dw")
    d_wv = mm(sv["cnkv"], dv5, "tn", name="v_up_dw")
    (dcq,), (d_gql,) = rowwise_vjp("latq_norm_vjp", fn_rms(Q_RANK), [(proj, Q_RANK, s["cq"] // Q_RANK)], [p["gql"]],
                                   [(dcnq, Q_RANK, 0)], [0], gdt=[BF16])
    (dckv,), (d_gkvl,) = rowwise_vjp("latkv_norm_vjp", fn_rms(KV_RANK), [(proj, KV_RANK, s["ckv"] // KV_RANK)],
                                     [p["gkvl"]], [(dcnkv, KV_RANK, 0)], [0], gdt=[BF16])
    dproj = jnp.concatenate([dg0, dg1, dg2, dckv, dfq, dfk, dfv.astype(BF16), du, dsmall,
                             jnp.zeros((M, LANE), BF16), dcq], axis=1)
    dh = mm(dproj, p["WpT"], "nn", name="in_proj_dx", layer=l)
    d_wp = mm(sv["h"], dproj, "tn", name="in_proj_dw", slot=slot("Wp"))
    (dxin,), (d_g1,) = rowwise_vjp("rms_attn_vjp", fn_rms_res(D), [(sv["x"], D, 0)], [p["g1"]],
                                   [(dh, D, 0), (dxmid, D, 0)], [0])
    grads = dict(g1=d_g1, Wp=d_wp, gql=d_gql, gkvl=d_gkvl, Wuq=d_wuq, Wk=d_wk, Wv=d_wv, gq=d_gq, gk=d_gk,
                 gfq=d_gfq, gfk=d_gfk, fbias=d_fbias, s5d=d_s5d, Wglu=d_wglu, bglu=d_bglu, Wb0=d_wb0, Wb1=d_wb1,
                 Wb2=d_wb2, Wout=d_wout, g2=d_g2, Wup=d_wup, convw=d_convw, Wdown=d_wdown)
    grads.update(dict(zip(S5_PARAM_NAMES, d_s5)))
    return dxin, grads


def local_step(x, positions, target, w):
    B, S, D = x.shape
    M = B * S
    P = with_transposes(prep_weights(w))
    L = P["g1"].shape[0]
    cos_t, sin_t = rope_tables(positions)
    xc, saved = x.reshape(M, D), []
    for l in range(L):
        xc, sv = layer_fwd(xc, layer_params(P, l), l, cos_t, sin_t, B, S)
        saved.append(sv)
    dxc, sq = loss_head(xc, target.reshape(M, D))
    grads, stacks = [None] * L, {"L": L}
    for l in reversed(range(L)):
        dxc, grads[l] = layer_bwd(dxc, layer_params(P, l), l, saved[l], cos_t, sin_t, B, S, stacks)
        stacks.update({k: grads[l][k] for k in STACKED})
    G = {k: (stacks[k] if k in STACKED else jnp.stack([g[k] for g in grads])) for k in grads[0]}
    return sq, dxc.reshape(B, S, D), unprep_grads(G, D)


MESH = pl.DeviceIdType.MESH
ANY = pl.BlockSpec(memory_space=pl.ANY)
N_CHIPS = 4
SHARDED = ("w_in", "w_uq", "w_ukv", "s5_w_glu", "w_branch", "w_out", "w_up", "ffn_conv_w", "w_down")
MINOR = ("w_branch", "w_up", "ffn_conv_w")
F32_TRAVEL = ("ffn_conv_w",)
WEIGHTS = ("attn_norm_g", "w_in", "q_lat_norm_g", "w_uq", "kv_lat_norm_g", "w_ukv", "mla_q_norm_g", "mla_k_norm_g",
           "fox_q_norm_g", "fox_k_norm_g", "fox_f_bias", "s5_lambda_re", "s5_lambda_im", "s5_b_re", "s5_b_im",
           "s5_c_re", "s5_c_im", "s5_d", "s5_log_step", "s5_w_glu", "s5_b_glu", "w_branch", "w_out", "ffn_norm_g",
           "w_up", "ffn_conv_w", "w_down")
SMALL = tuple(n for n in WEIGHTS if n not in SHARDED)
RS_BIG = tuple(n for n in SHARDED if n != "ffn_conv_w")
REDUCE_SMALL = SMALL + ("ffn_conv_w",)


def shard3(name, a):
    L = a.shape[0]
    if name in ("w_uq", "w_ukv"):
        return a.reshape(L, a.shape[1], -1)
    if name == "w_branch":
        return a.reshape(L, -1, a.shape[-1])
    return a


def full4(name, a):
    L = a.shape[0]
    if name == "w_in":
        return a
    if name in MINOR:
        return a.reshape(L, 1, -1, a.shape[-1])
    a = a.reshape(L, a.shape[1], -1)
    return a.reshape(L, N_CHIPS, a.shape[1] // N_CHIPS, a.shape[2])


def from_full4(name, a, ref_tail):
    L = a.shape[0]
    if name == "w_in":
        return a
    return a.reshape((L,) + tuple(ref_tail))


def _where():
    x, y, c = lax.axis_index("x"), lax.axis_index("y"), lax.axis_index("c")
    chips = [(1 - x, y), (x, 1 - y), (1 - x, 1 - y)]
    return (x, y, c), 2 * x + y, (x, y, 1 - c), chips, [2 * cx + cy for cx, cy in chips]


def _view(minor, ref4, layers, k):
    if minor:
        cs = ref4.shape[3] // N_CHIPS
        return ref4.at[layers, 0, :, pl.ds(pl.multiple_of(k * cs, LANE), cs)]
    return ref4.at[layers, k]


def _remote(src, dst, ssem, rsem, dev):
    return pltpu.make_async_remote_copy(src_ref=src, dst_ref=dst, send_sem=ssem, recv_sem=rsem,
                                        device_id=dev, device_id_type=MESH)


def gather_weights(shards, minor):
    n = len(shards)
    L = shards[0].shape[0]
    Lh = L // 2
    out_shape = []
    for a, mn in zip(shards, minor):
        _, r, cs = a.shape
        out_shape.append(jax.ShapeDtypeStruct((L, 1, r, N_CHIPS * cs) if mn else (L, N_CHIPS, r, cs), a.dtype))

    def body(*refs):
        w, g = refs[:n], refs[n:2 * n]
        send, recv = refs[2 * n:]
        (x, y, c), me, sib, chips, cidx = _where()
        mine, other, every = pl.ds(c * Lh, Lh), pl.ds((1 - c) * Lh, Lh), pl.ds(0, L)
        dst = lambda i, layers, k: _view(minor[i], g[i], layers, k)
        local = [_remote(w[i], dst(i, every, me), send.at[i, 6], recv.at[i, 6], sib) for i in range(n)]
        first = [_remote(w[i].at[mine], dst(i, mine, me), send.at[i, j], recv.at[i, j], (*chips[j], c))
                 for i in range(n) for j in range(3)]
        for cp in local + first:
            cp.start()
        passed = []
        for i in range(n):
            for j in range(3):
                blk = dst(i, mine, cidx[j])
                _remote(blk, blk, send.at[i, j], recv.at[i, j], (*chips[j], c)).wait_recv()
                fwd = _remote(blk, blk, send.at[i, 3 + j], recv.at[i, 3 + j], sib)
                fwd.start()
                passed.append(fwd)
        for i in range(n):
            for j in range(3):
                blk = dst(i, other, cidx[j])
                _remote(blk, blk, send.at[i, 3 + j], recv.at[i, 3 + j], sib).wait_recv()
        for cp in first + passed:
            cp.wait_send()
        for cp in local:
            cp.wait()

    return pl.pallas_call(
        body, name="gather_weights", in_specs=[ANY] * n, out_specs=[ANY] * n, out_shape=out_shape,
        scratch_shapes=[pltpu.SemaphoreType.DMA((n, 7)), pltpu.SemaphoreType.DMA((n, 7))],
    )(*shards)


def sibling_halves(grads):
    n = len(grads)
    L = grads[0].shape[0]
    Lh = L // 2
    half = [jax.ShapeDtypeStruct((Lh,) + a.shape[1:], a.dtype) for a in grads]

    def body(*refs):
        g, got = refs[:n], refs[n:2 * n]
        send, recv = refs[2 * n:]
        (x, y, c), me, sib, chips, cidx = _where()
        other = pl.ds((1 - c) * Lh, Lh)
        out = [_remote(g[i].at[other], got[i], send.at[i], recv.at[i], sib) for i in range(n)]
        for cp in out:
            cp.start()
        for cp in out:
            cp.wait()

    got = pl.pallas_call(
        body, name="grad_sibling_halves", in_specs=[ANY] * n, out_specs=[ANY] * n, out_shape=half,
        scratch_shapes=[pltpu.SemaphoreType.DMA((n,)), pltpu.SemaphoreType.DMA((n,))],
    )(*grads)
    return got


def scatter_chip_sums(travel, minor):
    n = len(travel)
    Lh = travel[0].shape[0]
    got_shape = []
    for t, mn in zip(travel, minor):
        r, cs = t.shape[2], (t.shape[3] // N_CHIPS if mn else t.shape[3])
        got_shape.append(jax.ShapeDtypeStruct((3, Lh, r, cs), t.dtype))

    def body(*refs):
        s16, got = refs[:n], refs[n:2 * n]
        send, recv = refs[2 * n:]
        (x, y, c), me, sib, chips, cidx = _where()
        every = pl.ds(0, Lh)
        out = [_remote(_view(minor[i], s16[i], every, cidx[j]), got[i].at[j], send.at[i, j], recv.at[i, j],
                       (*chips[j], c)) for i in range(n) for j in range(3)]
        for cp in out:
            cp.start()
        for cp in out:
            cp.wait()

    got = pl.pallas_call(
        body, name="grad_scatter", in_specs=[ANY] * n, out_specs=[ANY] * n, out_shape=got_shape,
        scratch_shapes=[pltpu.SemaphoreType.DMA((n, 3)), pltpu.SemaphoreType.DMA((n, 3))],
    )(*travel)
    return got


def share_halves(shards):
    n = len(shards)
    Lh = shards[0].shape[0] // 2

    def body(*refs):
        full = refs[n:2 * n]
        send, recv = refs[2 * n:]
        (x, y, c), me, sib, chips, cidx = _where()
        mine, other = pl.ds(c * Lh, Lh), pl.ds((1 - c) * Lh, Lh)
        out = [_remote(full[i].at[mine], full[i].at[mine], send.at[i], recv.at[i], sib) for i in range(n)]
        for cp in out:
            cp.start()
        for i in range(n):
            out[i].wait_send()
            _remote(full[i].at[other], full[i].at[other], send.at[i], recv.at[i], sib).wait_recv()

    return pl.pallas_call(
        body, name="grad_share_halves", in_specs=[ANY] * n, out_specs=[ANY] * n,
        out_shape=[jax.ShapeDtypeStruct(a.shape, a.dtype) for a in shards],
        input_output_aliases={i: i for i in range(n)},
        scratch_shapes=[pltpu.SemaphoreType.DMA((n,)), pltpu.SemaphoreType.DMA((n,))],
    )(*shards)


N_DEV = 8


def allreduce_small(v):
    R = v.shape[0]

    def body(x_ref, sum_ref, all_ref, send, recv, loc):
        (x, y, c), me, sib, chips, cidx = _where()

        def rows(px, py, pc):
            return all_ref.at[4 * px + 2 * py + pc]

        def copy(k, block, to, src=None):
            return _remote(rows(*block) if src is None else src, rows(*block), send.at[k], recv.at[k], to)

        mine = pltpu.make_async_copy(x_ref, rows(x, y, c), loc)
        mine.start()
        first = [copy(0, (x, y, c), sib, src=x_ref)]
        first += [copy(1 + j, (x, y, c), (*chip, c), src=x_ref) for j, chip in enumerate(chips)]
        for cp in first:
            cp.start()
        passed = [copy(4 + j, (*chip, c), sib) for j, chip in enumerate(chips)]
        for j, chip in enumerate(chips):
            copy(1 + j, (*chip, c), (x, y, c)).wait_recv()
            passed[j].start()
        copy(0, (x, y, 1 - c), (x, y, c)).wait_recv()
        for j, chip in enumerate(chips):
            copy(4 + j, (*chip, 1 - c), (x, y, c)).wait_recv()
        for cp in first + passed:
            cp.wait_send()
        mine.wait()
        acc = all_ref[0]
        for d in range(1, N_DEV):
            acc = acc + all_ref[d]
        sum_ref[...] = acc

    vm = pl.BlockSpec(memory_space=pltpu.VMEM)
    return pl.pallas_call(
        body, name="allreduce_small", in_specs=[vm], out_specs=[vm, vm],
        out_shape=[jax.ShapeDtypeStruct((R, LANE), F32), jax.ShapeDtypeStruct((N_DEV, R, LANE), F32)],
        scratch_shapes=[pltpu.SemaphoreType.DMA((7,)), pltpu.SemaphoreType.DMA((7,)), pltpu.SemaphoreType.DMA],
        compiler_params=pltpu.CompilerParams(vmem_limit_bytes=VMEM_LIMIT),
    )(v)[0]


EW_BLOCK_BYTES = 2 << 20


def _ew_rows(rows, cols):
    for tr in (1024, 512, 256, 128, 64, 32, 16, 8):
        if rows % tr == 0 and tr * cols * 4 <= EW_BLOCK_BYTES:
            return tr
    return rows


def add_pair(name, full, got, where, travel_dtype):
    R, C = got.shape
    tr = _ew_rows(R, C)
    nblk = R // tr

    def body(w_ref, a_ref, b_ref, s_ref, t_ref):
        s = a_ref[...] + b_ref[...]
        s_ref[...] = s
        t_ref[...] = s.astype(t_ref.dtype)

    blk = pl.BlockSpec((tr, C), lambda i, w: (i, 0))
    return pl.pallas_call(
        body, name=name,
        grid_spec=pltpu.PrefetchScalarGridSpec(
            num_scalar_prefetch=1, grid=(nblk,),
            in_specs=[pl.BlockSpec((tr, C), lambda i, w: (w[1] * nblk + i, 0)), blk], out_specs=[blk, blk]),
        out_shape=[jax.ShapeDtypeStruct((R, C), F32), jax.ShapeDtypeStruct((R, C), travel_dtype)],
        compiler_params=_cp("parallel"),
    )(where, full, got)


def add_four(name, sums, got, where, minor):
    Lh, _, r, C = sums.shape
    cs = got.shape[-1]
    tr = _ew_rows(r, cs)

    def body(w_ref, m_ref, g0, g1, g2, o_ref):
        o_ref[...] = ((m_ref[...] + g0[...].astype(F32)) + g1[...].astype(F32)) + g2[...].astype(F32)

    mine = (pl.BlockSpec((None, None, tr, cs), lambda l, i, w: (l, 0, i, w[0])) if minor
            else pl.BlockSpec((None, None, tr, cs), lambda l, i, w: (l, w[0], i, 0)))
    slot = lambda j: pl.BlockSpec((None, None, tr, cs), lambda l, i, w: (j, l, i, 0))
    return pl.pallas_call(
        body, name=name,
        grid_spec=pltpu.PrefetchScalarGridSpec(
            num_scalar_prefetch=1, grid=(Lh, r // tr), in_specs=[mine, slot(0), slot(1), slot(2)],
            out_specs=pl.BlockSpec((None, tr, cs), lambda l, i, w: (w[1] * Lh + l, i, 0))),
        out_shape=jax.ShapeDtypeStruct((2 * Lh, r, cs), F32), compiler_params=_cp("parallel", "parallel"),
    )(where, sums, got, got, got)


def reduce_scatter_grads(full_grads):
    names = list(RS_BIG)
    minor = [nm in MINOR for nm in names]
    where = jnp.stack([2 * lax.axis_index("x") + lax.axis_index("y"), lax.axis_index("c")]).astype(jnp.int32)
    g4 = [full4(nm, full_grads[nm]) for nm in names]
    got = sibling_halves(g4)
    sums, travel = [], []
    for nm, a, b in zip(names, g4, got):
        s, t = add_pair("chip_sum_" + nm, a.reshape(-1, a.shape[-1]), b.reshape(-1, b.shape[-1]), where,
                        F32 if nm in F32_TRAVEL else BF16)
        sums.append(s.reshape(b.shape))
        travel.append(t.reshape(b.shape))
    arrived = scatter_chip_sums(travel, minor)
    shards = [add_four("shard_sum_" + nm, s, b, where, mn) for nm, s, b, mn in zip(names, sums, arrived, minor)]
    return dict(zip(names, share_halves(shards)))


def pack_small(tree, extra=None, names=SMALL):
    parts = [tree[nm].reshape(-1) for nm in names]
    parts.append(jnp.zeros((1,), F32) if extra is None else extra.reshape(-1))
    blocks = []
    for p in parts:
        rows = _small_rows(p.shape[0])
        blocks.append(jnp.pad(p, (0, rows * LANE - p.shape[0])).reshape(rows, LANE))
    return jnp.concatenate(blocks, axis=0)


def _small_rows(size):
    return -(-size // (8 * LANE)) * 8


def unpack_small(packed, like, names=SMALL):
    out, at = {}, 0
    for nm in names:
        size = math.prod(like[nm].shape)
        rows = _small_rows(size)
        out[nm] = packed[at:at + rows].reshape(-1)[:size].reshape(like[nm].shape)
        at += rows
    return out, packed[at, 0]


def kernel(x, positions, attn_norm_g, w_in, q_lat_norm_g, w_uq, kv_lat_norm_g, w_ukv, mla_q_norm_g, mla_k_norm_g, fox_q_norm_g, fox_k_norm_g, fox_f_bias, s5_lambda_re, s5_lambda_im, s5_b_re, s5_b_im, s5_c_re, s5_c_im, s5_d, s5_log_step, s5_w_glu, s5_b_glu, w_branch, w_out, ffn_norm_g, w_up, ffn_conv_w, w_down, loss_target, m_attn_norm_g, m_w_in, m_q_lat_norm_g, m_w_uq, m_kv_lat_norm_g, m_w_ukv, m_mla_q_norm_g, m_mla_k_norm_g, m_fox_q_norm_g, m_fox_k_norm_g, m_fox_f_bias, m_s5_lambda_re, m_s5_lambda_im, m_s5_b_re, m_s5_b_im, m_s5_c_re, m_s5_c_im, m_s5_d, m_s5_log_step, m_s5_w_glu, m_s5_b_glu, m_w_branch, m_w_out, m_ffn_norm_g, m_w_up, m_ffn_conv_w, m_w_down, v_attn_norm_g, v_w_in, v_q_lat_norm_g, v_w_uq, v_kv_lat_norm_g, v_w_ukv, v_mla_q_norm_g, v_mla_k_norm_g, v_fox_q_norm_g, v_fox_k_norm_g, v_fox_f_bias, v_s5_lambda_re, v_s5_lambda_im, v_s5_b_re, v_s5_b_im, v_s5_c_re, v_s5_c_im, v_s5_d, v_s5_log_step, v_s5_w_glu, v_s5_b_glu, v_w_branch, v_w_out, v_ffn_norm_g, v_w_up, v_ffn_conv_w, v_w_down):
    given = dict(locals())
    w = {nm: given[nm] for nm in WEIGHTS}
    m = {nm: given["m_" + nm] for nm in WEIGHTS}
    v = {nm: given["v_" + nm] for nm in WEIGHTS}
    D = x.shape[-1]

    minor = [nm in MINOR for nm in SHARDED]
    shards = [shard3(nm, w[nm]).astype(F32 if nm in F32_TRAVEL else BF16) for nm in SHARDED]
    gathered = gather_weights(shards, minor)
    full = dict(w)
    for nm, g4 in zip(SHARDED, gathered):
        tail = list(w[nm].shape[1:])
        axis = (len(tail) - 1) if nm in MINOR or nm == "w_in" else 0
        tail[axis] *= N_CHIPS
        full[nm] = from_full4(nm, g4, tail)

    sq, grad_x, gw = local_step(x, positions, loss_target, full)

    big = reduce_scatter_grads(gw)
    total, sq_sum = unpack_small(allreduce_small(pack_small(gw, sq[0:1, 0:1], REDUCE_SMALL)), gw, REDUCE_SMALL)
    loss = 0.5 * sq_sum / D
    conv_cols = w["ffn_conv_w"].shape[-1]
    chip = 2 * lax.axis_index("x") + lax.axis_index("y")
    big["ffn_conv_w"] = lax.dynamic_slice_in_dim(total.pop("ffn_conv_w"), chip * conv_cols, conv_cols, axis=2)

    grads, delta, new_m, new_v = {}, {}, {}, {}
    for nm in SHARDED:
        g = big[nm].reshape(shard3(nm, w[nm]).shape)
        two = lambda a: shard3(nm, a).reshape(-1, g.shape[-1])
        d2, m2, v2 = adamw("adamw_" + nm, two(w[nm]), g.reshape(-1, g.shape[-1]), two(m[nm]), two(v[nm]))
        grads[nm] = g.reshape(w[nm].shape)
        delta[nm], new_m[nm], new_v[nm] = (a.reshape(w[nm].shape) for a in (d2, m2, v2))
    d2, m2, v2 = adamw("adamw_small", pack_small(w), pack_small(total), pack_small(m), pack_small(v))
    for tree, packed in ((delta, d2), (new_m, m2), (new_v, v2)):
        tree.update(unpack_small(packed, w)[0])
    grads.update(total)
    return (loss, grad_x, *[grads[nm] for nm in WEIGHTS], *[delta[nm] for nm in WEIGHTS],
            *[new_m[nm] for nm in WEIGHTS], *[new_v[nm] for nm in WEIGHTS])
```

```python
import functools
import math

import jax
import jax.numpy as jnp
from jax import lax
from jax.experimental import pallas as pl
from jax.experimental.pallas import tpu as pltpu

F32, BF16 = jnp.float32, jnp.bfloat16
NORM_EPS = 1e-6
NEG_INF = -1e30
ROPE_THETA = 10000.0
LANE = 128
N_HEADS = 4
NOPE, ROPE, QK_DIM, V_DIM = 64, 32, 96, 64
Q_RANK, KV_RANK = 384, 256
FOX_DIM = 64
S5_G, S5_H, S5_P = 16, 16, 64
S5_C = S5_G * S5_P
BW = 256
VMEM_LIMIT = 56 << 20
ADAM_LR, ADAM_B1, ADAM_B2, ADAM_EPS, ADAM_WD, ADAM_STEP = 0.001, 0.9, 0.999, 1e-08, 0.01, 10


def _pick(n, cands):
    for c in cands:
        if n % c == 0:
            return c
    return n


def _cp(*sem):
    return pltpu.CompilerParams(dimension_semantics=sem, vmem_limit_bytes=VMEM_LIMIT)


def _dg(a, b, ca, cb):
    return lax.dot_general(a.astype(BF16), b.astype(BF16), (((ca,), (cb,)), ((), ())),
                           preferred_element_type=F32)


@jax.custom_vjp
def dot_nn(a, b):
    return _dg(a, b, 1, 0)


@jax.custom_vjp
def dot_nt(a, b):
    return _dg(a, b, 1, 1)


@jax.custom_vjp
def dot_tn(a, b):
    return _dg(a, b, 0, 0)


dot_nn.defvjp(lambda a, b: (dot_nn(a, b), (a, b)),
              lambda r, g: (dot_nt(g, r[1]).astype(r[0].dtype), dot_tn(r[0], g).astype(r[1].dtype)))
dot_nt.defvjp(lambda a, b: (dot_nt(a, b), (a, b)),
              lambda r, g: (dot_nn(g, r[1]).astype(r[0].dtype), dot_tn(g, r[0]).astype(r[1].dtype)))
dot_tn.defvjp(lambda a, b: (dot_tn(a, b), (a, b)),
              lambda r, g: (dot_nt(r[1], g).astype(r[0].dtype), dot_nn(r[0], g).astype(r[1].dtype)))


def xdot(a, b):
    return jnp.dot(a, b, precision=lax.Precision.HIGHEST, preferred_element_type=F32)


MM_VMEM_BUDGET = 36 << 20
MM_STEP_S, MM_HBM_BPS, MM_VMEM_BPS = 0.4e-6, 2.5e12, 3e12


def _divisors(n, cands):
    return sorted({c for c in cands if n % c == 0} | {n}, reverse=True)


def _mm_tiles(M, K, N, ab, bb, ob, addb):
    best = None
    for tm in _divisors(M, (2048, 1024, 512, 256, 128)):
        for tn in _divisors(N, (2048, 1664, 1536, 1408, 1280, 1024, 768, 640, 512, 384, 256, 128)):
            for tk in _divisors(K, (4096, 2048, 1664, 1536, 1408, 1024, 768, 512, 384, 256, 128)):
                vmem = 2 * (tm * tk * ab + tk * tn * bb + tm * tn * (ob + addb)) + (tm * tn * 4 if tk != K else 0)
                if vmem > MM_VMEM_BUDGET:
                    continue
                nk = K // tk
                steps = (M // tm) * (N // tn) * nk
                traffic = M * K * ab * (N // tn) + K * N * bb * (M // tm) + M * N * (ob + addb)
                cost = steps * MM_STEP_S + traffic / MM_HBM_BPS + (M * N * 8 * nk / MM_VMEM_BPS if nk > 1 else 0)
                if best is None or cost < best[0]:
                    best = (cost, tm, tn, tk)
    assert best is not None, (M, K, N)
    return best[1:]


def mm(a, b, mode, *, name, add=None, out_dtype=F32, layer=None, slot=None):
    bk, bn = b.shape[-2:]
    if mode == "nn":
        (M, K), N = a.shape, bn
    else:
        (K, M), N = a.shape, bn
    assert bk == K, (name, a.shape, b.shape)
    isz = lambda x: jnp.dtype(x.dtype).itemsize
    tm, tn, tk = _mm_tiles(M, K, N, isz(a), isz(b), jnp.dtype(out_dtype).itemsize, 0 if add is None else isz(add))
    nk = K // tk
    ca = 1 if mode == "nn" else 0

    n_in = 2 + (add is not None) + (slot is not None and slot[2] is not None)

    def body(*refs):
        a_ref, b_ref = refs[:2]
        add_ref = refs[2] if add is not None else None
        o_ref = refs[n_in]

        def finish(r):
            if add is not None:
                r = r + add_ref[...].astype(F32)
            o_ref[...] = r.astype(out_dtype)

        part = _dg(a_ref[...], b_ref[...], ca, 0)
        if nk == 1:
            finish(part)
            return
        acc = refs[-1]
        kk = pl.program_id(2)

        @pl.when(kk == 0)
        def _():
            acc[...] = part

        @pl.when(kk > 0)
        def _():
            acc[...] += part

        @pl.when(kk == nk - 1)
        def _():
            finish(acc[...])

    a_spec = (pl.BlockSpec((tm, tk), lambda i, j, k: (i, k)) if mode == "nn"
              else pl.BlockSpec((tk, tm), lambda i, j, k: (k, i)))
    b_spec = (pl.BlockSpec((tk, tn), lambda i, j, k: (k, j)) if layer is None
              else pl.BlockSpec((None, tk, tn), lambda i, j, k: (layer, k, j)))
    in_specs, args = [a_spec, b_spec], [a, b]
    if add is not None:
        in_specs.append(pl.BlockSpec((tm, tn), lambda i, j, k: (i, j)))
        args.append(add)
    out_spec = pl.BlockSpec((tm, tn), lambda i, j, k: (i, j))
    out_shape = jax.ShapeDtypeStruct((M, N), out_dtype)
    aliases = {}
    if slot is not None:
        n_layers, l, buf = slot
        out_spec = pl.BlockSpec((None, tm, tn), lambda i, j, k: (l, i, j))
        out_shape = jax.ShapeDtypeStruct((n_layers, M, N), out_dtype)
        if buf is not None:
            aliases = {len(args): 0}
            in_specs.append(pl.BlockSpec(memory_space=pl.ANY))
            args.append(buf)
    return pl.pallas_call(
        body, name=name, grid=(M // tm, N // tn, nk),
        in_specs=in_specs, out_specs=out_spec, out_shape=out_shape, input_output_aliases=aliases,
        scratch_shapes=[pltpu.VMEM((tm, tn), F32)] if nk > 1 else [],
        compiler_params=_cp("parallel", "parallel", "arbitrary"),
    )(*args)


def _row_spec(tm, width, col):
    if callable(col):
        return pl.BlockSpec((tm, width), lambda i, j: (i, col(j)))
    return pl.BlockSpec((tm, width), lambda i, j: (i, col))


def _const_spec(c):
    return pl.BlockSpec(c.shape, lambda i, j: (0,) * c.ndim)


ROW_BLOCK_BYTES = 6 << 20


def _row_tile(M, widths, tm):
    if tm is None:
        tm = next((t for t in (1024, 512, 256) if t * sum(widths) * 4 <= ROW_BLOCK_BYTES), 128)
    return _pick(M, (tm, 256, 128, 64, 32, 16, 8))


def rowwise(name, fn, rows, consts, outs, *, tm=None, nj=1):
    M = rows[0][0].shape[0]
    tm = _row_tile(M, [w for _, w, _ in rows] + [w for w, _ in outs], tm)
    nr, nc = len(rows), len(consts)

    def body(*refs):
        vals = [r[...].astype(F32) for r in refs[:nr]] + [r[...] for r in refs[nr:nr + nc]]
        res = fn(*vals)
        for o_ref, r in zip(refs[nr + nc:], res):
            o_ref[...] = r.astype(o_ref.dtype)

    return pl.pallas_call(
        body, name=name, grid=(M // tm, nj),
        in_specs=[_row_spec(tm, w, c) for _, w, c in rows] + [_const_spec(c) for c in consts],
        out_specs=[pl.BlockSpec((tm, w), lambda i, j: (i, j)) for w, _ in outs],
        out_shape=[jax.ShapeDtypeStruct((M, nj * w), dt) for w, dt in outs],
        compiler_params=_cp("parallel", "parallel"),
    )(*[r[0] for r in rows], *consts)


def rowwise_vjp(name, fn, rows, consts, cts, diff, *, tm=None, nj=1, gdt=None):
    M = rows[0][0].shape[0]
    tm = _row_tile(M, [w for _, w, _ in rows] + [w for _, w, _ in cts] + [rows[p][1] for p in diff], tm)
    nr, nc, nt, nd = len(rows), len(consts), len(cts), len(diff)
    gdt = [F32] * nd if gdt is None else gdt

    def body(*refs):
        vals = [r[...].astype(F32) for r in refs[:nr + nc + nt]]
        rv, cv, tv = vals[:nr], vals[nr:nr + nc], vals[nr + nc:]
        grow, gconst = refs[nr + nc + nt:nr + nc + nt + nd], refs[nr + nc + nt + nd:]

        def f(*dargs):
            full = list(rv)
            for pos, val in zip(diff, dargs[:nd]):
                full[pos] = val
            return tuple(fn(*full, *dargs[nd:]))

        _, vjp = jax.vjp(f, *[rv[p] for p in diff], *cv)
        g = vjp(tuple(tv))
        for o_ref, gv in zip(grow, g[:nd]):
            o_ref[...] = gv.astype(o_ref.dtype)
        first = jnp.logical_and(pl.program_id(0) == 0, pl.program_id(1) == 0)
        for o_ref, gv in zip(gconst, g[nd:]):
            @pl.when(first)
            def _(o_ref=o_ref, gv=gv):
                o_ref[...] = gv

            @pl.when(jnp.logical_not(first))
            def _(o_ref=o_ref, gv=gv):
                o_ref[...] += gv

    out_specs = ([pl.BlockSpec((tm, rows[p][1]), lambda i, j: (i, j)) for p in diff]
                 + [_const_spec(c) for c in consts])
    out_shape = ([jax.ShapeDtypeStruct((M, nj * rows[p][1]), dt) for p, dt in zip(diff, gdt)]
                 + [jax.ShapeDtypeStruct(c.shape, F32) for c in consts])
    res = pl.pallas_call(
        body, name=name, grid=(M // tm, nj),
        in_specs=([_row_spec(tm, w, c) for _, w, c in rows] + [_const_spec(c) for c in consts]
                  + [_row_spec(tm, w, c) for _, w, c in cts]),
        out_specs=out_specs, out_shape=out_shape,
        compiler_params=_cp("arbitrary", "arbitrary"),
    )(*[r[0] for r in rows], *consts, *[t[0] for t in cts])
    return res[:nd], res[nd:]


def _lane(shape=(1, LANE)):
    return lax.broadcasted_iota(jnp.int32, shape, len(shape) - 1)


def _sigmoid(x):
    return 1.0 / (1.0 + jnp.exp(-x))


def _rms(x, g, n):
    return x * lax.rsqrt(jnp.sum(x * x, axis=-1, keepdims=True) * (1.0 / n) + NORM_EPS) * g


def fn_rms(n):
    return lambda x, g: (_rms(x, g, n),)


def _rope(x, cos_t, sin_t):
    i = lax.broadcasted_iota(jnp.int32, (LANE, LANE), 0)
    j = lax.broadcasted_iota(jnp.int32, (LANE, LANE), 1)
    half = ROPE // 2
    lo = jnp.logical_and(jnp.logical_and(j >= NOPE, j < NOPE + half), i == j + half)
    hi = jnp.logical_and(jnp.logical_and(j >= NOPE + half, j < NOPE + ROPE), i == j - half)
    perm = jnp.where(hi, 1.0, 0.0) - jnp.where(lo, 1.0, 0.0)
    return x * cos_t + xdot(x, perm) * sin_t


def fn_qpost(q, cos_t, sin_t, g):
    return (_rope(_rms(q, g, QK_DIM), cos_t, sin_t),)


def fn_kpost(kn, small, cos_t, sin_t, g):
    lane = _lane()
    rope_lanes = jnp.logical_and(lane >= NOPE, lane < NOPE + ROPE)
    kc = kn + jnp.where(rope_lanes, small, 0.0)
    return (_rope(_rms(kc, g, QK_DIM), cos_t, sin_t),)


def fn_foxnorm(x, g):
    first = _lane() < FOX_DIM
    sq = x * x
    s0 = jnp.sum(jnp.where(first, sq, 0.0), axis=-1, keepdims=True)
    s1 = jnp.sum(jnp.where(first, 0.0, sq), axis=-1, keepdims=True)
    r0 = lax.rsqrt(s0 * (1.0 / FOX_DIM) + NORM_EPS)
    r1 = lax.rsqrt(s1 * (1.0 / FOX_DIM) + NORM_EPS)
    return (x * jnp.where(first, r0, r1) * g,)


F_LANE0 = NOPE + ROPE


def fn_fgate(small, bias):
    z = small + bias
    lf = jnp.minimum(z, 0.0) - jnp.log(1.0 + jnp.exp(-jnp.abs(z)))
    lane = _lane()
    return (jnp.where(jnp.logical_and(lane >= F_LANE0, lane < F_LANE0 + N_HEADS), lf, 0.0),)


def _gelu(y):
    return 0.5 * y * (1.0 + jnp.tanh(math.sqrt(2.0 / math.pi) * (y + 0.044715 * (y * y * y))))


def fn_s5post(ypre, u, d, wglu, bglu):
    y = _gelu(ypre + d * u)
    return (y * _sigmoid(dot_nn(y, wglu) + bglu),)


def fn_merge(om, of, os_, g0, g1, g2, wb0, wb1, wb2):
    return (_sigmoid(g0) * dot_nn(om, wb0) + _sigmoid(g1) * dot_nn(of, wb1)
            + _sigmoid(g2) * dot_nn(os_, wb2),)


def fn_add5(a, b):
    return (a[:, 0:LANE] + a[:, LANE:2 * LANE] + a[:, 2 * LANE:3 * LANE] + a[:, 3 * LANE:4 * LANE] + b,)


def fn_addt(a, b):
    return (a + b,)


def fn_s5params(lre, lim, lstep, btr, bti, ctr, cti):
    C = S5_C
    grp = lax.broadcasted_iota(jnp.int32, (LANE, C), 1) >> 6
    expand = jnp.where(lax.broadcasted_iota(jnp.int32, (LANE, C), 0) == grp, 1.0, 0.0)
    lane = _lane()
    st = jnp.where(lane < S5_G, jnp.exp(lstep), 0.0)
    step = jnp.sum(xdot(jnp.broadcast_to(st, (8, LANE)), expand), axis=0, keepdims=True) * 0.125
    zr, zi = lre * step, lim * step
    er = jnp.exp(zr)
    lbr, lbi = er * jnp.cos(zi), er * jnp.sin(zi)
    den = lre * lre + lim * lim
    nr = lbr - 1.0
    cfr = (nr * lre + lbi * lim) / den
    cfi = (lbi * lre - nr * lim) / den
    bbr = cfr * btr - cfi * bti
    bbi = cfr * bti + cfi * btr
    rg = lax.broadcasted_iota(jnp.int32, (BW, C), 0) >> 4
    cg = lax.broadcasted_iota(jnp.int32, (BW, C), 1) >> 6
    mb = jnp.where(rg == cg, 1.0, 0.0)
    b_re = jnp.concatenate([bbr] * S5_G, axis=0) * mb
    b_im = jnp.concatenate([bbi] * S5_G, axis=0) * mb
    ecol = jnp.where(lax.broadcasted_iota(jnp.int32, (LANE, BW), 0)
                     == (lax.broadcasted_iota(jnp.int32, (LANE, BW), 1) & 15), 1.0, 0.0)
    mc = jnp.where((lax.broadcasted_iota(jnp.int32, (C, BW), 0) >> 6)
                   == (lax.broadcasted_iota(jnp.int32, (C, BW), 1) >> 4), 1.0, 0.0)
    c_top = xdot(ctr, ecol) * mc
    c_bot = -(xdot(cti, ecol) * mc)
    return lbr, lbi, b_re, b_im, c_top, c_bot


def s5_params(p):
    def body(lre, lim, ls, btr, bti, ctr, cti, lb_ref, bre_ref, bim_ref, ct_ref, cb_ref):
        lbr, lbi, b_re, b_im, c_top, c_bot = fn_s5params(
            lre[...], lim[...], ls[...], btr[...], bti[...], ctr[...], cti[...])
        lb_ref[0:1, :] = lbr
        lb_ref[1:2, :] = lbi
        bre_ref[...] = b_re.astype(BF16)
        bim_ref[...] = b_im.astype(BF16)
        ct_ref[...] = c_top.astype(BF16)
        cb_ref[...] = c_bot.astype(BF16)

    return pl.pallas_call(
        body, name="s5_params",
        out_shape=[jax.ShapeDtypeStruct((2, S5_C), F32), jax.ShapeDtypeStruct((BW, S5_C), BF16),
                   jax.ShapeDtypeStruct((BW, S5_C), BF16), jax.ShapeDtypeStruct((S5_C, BW), BF16),
                   jax.ShapeDtypeStruct((S5_C, BW), BF16)],
        compiler_params=pltpu.CompilerParams(vmem_limit_bytes=VMEM_LIMIT),
    )(p["lre"], p["lim"], p["lstep"], p["btr"], p["bti"], p["ctr"], p["cti"])


S5_PARAM_NAMES = ("lre", "lim", "lstep", "btr", "bti", "ctr", "cti")


def s5_params_vjp(p, dl_r, dl_i, db_re, db_im, dc_top, dc_bot):
    def body(lre, lim, ls, btr, bti, ctr, cti, dlr, dli, dbr, dbi, dct, dcb, *outs):
        args = [r[...] for r in (lre, lim, ls, btr, bti, ctr, cti)]
        _, vjp = jax.vjp(fn_s5params, *args)
        g = vjp((jnp.sum(dlr[...], axis=0, keepdims=True), jnp.sum(dli[...], axis=0, keepdims=True),
                 dbr[...], dbi[...], dct[...], dcb[...]))
        for o_ref, gv in zip(outs, g):
            o_ref[...] = gv

    return pl.pallas_call(
        body, name="s5_params_vjp",
        out_shape=[jax.ShapeDtypeStruct(p[n].shape, F32) for n in S5_PARAM_NAMES],
        compiler_params=pltpu.CompilerParams(vmem_limit_bytes=VMEM_LIMIT),
    )(*[p[n] for n in S5_PARAM_NAMES], dl_r, dl_i, db_re, db_im, dc_top, dc_bot)


def _attn_tile(q, k, v, cq, ckt, *, hpt, dk, q0, tile):
    tq, S = q.shape[0], k.shape[0]
    row = q0 + lax.broadcasted_iota(jnp.int32, (tq, S), 0)
    col = lax.broadcasted_iota(jnp.int32, (tq, S), 1)
    causal = row >= col
    lane = _lane()
    out = jnp.zeros((tq, LANE), F32)
    for h in range(hpt):
        if hpt > 1:
            mine = (lane >> int(math.log2(LANE // hpt))) == h
            qh = jnp.where(mine, q, 0.0)
        else:
            qh = q
        s = dot_nt(qh, k) * (dk ** -0.5)
        if cq is not None:
            head = tile * hpt + h
            cqh = jnp.sum(jnp.where(lane == F_LANE0 + head, cq, 0.0), axis=1, keepdims=True)
            sub = lax.broadcasted_iota(jnp.int32, (8, 1), 0)
            ckh = jnp.sum(jnp.where(sub == head, ckt, 0.0), axis=0, keepdims=True)
            s = s + (cqh - ckh)
        s = jnp.where(causal, s, NEG_INF)
        m = lax.stop_gradient(jnp.max(s, axis=-1, keepdims=True))
        e = jnp.exp(s - m)
        p = e / jnp.sum(e, axis=-1, keepdims=True)
        oh = dot_nn(p, v)
        out = out + (jnp.where(mine, oh, 0.0) if hpt > 1 else oh)
    return out


def attention(name, q, k, v, *, B, S, ntile, hpt, dk, qc=0, kc=0, vc=0, cum=None, ckt=None, do=None, tq=256):
    tq = _pick(S, (tq, 128))
    nq = S // tq
    M = B * S
    bias = cum is not None
    kw = dict(hpt=hpt, dk=dk)

    def load(refs, sk):
        q_ref, k_ref, v_ref = refs[:3]
        qv, kv, vv = q_ref[...].astype(F32), k_ref[0:sk, :].astype(F32), v_ref[0:sk, :].astype(F32)
        if bias:
            return qv, kv, vv, refs[3][...], refs[4][0, :, 0:sk]
        return qv, kv, vv, None, None

    nin = 5 if bias else 3

    def per_query_block(run):
        for g in range(nq):
            @pl.when(pl.program_id(2) == g)
            def _(g=g):
                run(g, (g + 1) * tq)

    def fwd_body(*refs):
        tile = pl.program_id(1)

        def run(g, sk):
            qv, kv, vv, cq, ck = load(refs, sk)
            o = _attn_tile(qv, kv, vv, cq, ck, q0=g * tq, tile=tile, **kw)
            refs[nin][...] = o.astype(refs[nin].dtype)

        per_query_block(run)

    def bwd_body(*refs):
        outs = refs[nin + 1:]
        tile = pl.program_id(1)

        @pl.when(pl.program_id(2) == 0)
        def _():
            for o_ref in (outs[1], outs[2]) + ((outs[4],) if bias else ()):
                o_ref[...] = jnp.zeros_like(o_ref)

        def run(g, sk):
            qv, kv, vv, cq, ck = load(refs, sk)
            dov = refs[nin][...].astype(F32)
            if bias:
                f = lambda a, b, c, d, e: _attn_tile(a, b, c, d, e, q0=g * tq, tile=tile, **kw)
                _, vjp = jax.vjp(f, qv, kv, vv, cq, ck)
            else:
                f = lambda a, b, c: _attn_tile(a, b, c, None, None, q0=g * tq, tile=tile, **kw)
                _, vjp = jax.vjp(f, qv, kv, vv)
            gr = vjp(dov)
            outs[0][...] = gr[0]
            outs[1][0:sk, :] += gr[1]
            outs[2][0:sk, :] += gr[2]
            if bias:
                outs[3][...] = gr[3]
                outs[4][0, :, 0:sk] += gr[4]

        per_query_block(run)

    qspec = lambda c: pl.BlockSpec((tq, LANE), lambda b, t, i: (b * nq + i, c + t))
    kspec = lambda c: pl.BlockSpec((S, LANE), lambda b, t, i: (b, c + t))
    in_specs, args = [qspec(qc), kspec(kc), kspec(vc)], [q, k, v]
    if bias:
        in_specs += [pl.BlockSpec((tq, LANE), lambda b, t, i: (b * nq + i, 0)),
                     pl.BlockSpec((1, 8, S), lambda b, t, i: (b, 0, 0))]
        args += [cum, ckt]
    if do is None:
        return pl.pallas_call(
            fwd_body, name=name, grid=(B, ntile, nq), in_specs=in_specs, out_specs=qspec(0),
            out_shape=jax.ShapeDtypeStruct((M, ntile * LANE), BF16),
            compiler_params=_cp("parallel", "parallel", "parallel"),
        )(*args)
    in_specs.append(qspec(0))
    args.append(do)
    out_specs = [qspec(0), kspec(0), kspec(0)]
    out_shape = [jax.ShapeDtypeStruct((M, ntile * LANE), F32)] * 3
    if bias:
        out_specs += [qspec(0), pl.BlockSpec((1, 8, S), lambda b, t, i: (b * ntile + t, 0, 0))]
        out_shape += [jax.ShapeDtypeStruct((M, ntile * LANE), F32),
                      jax.ShapeDtypeStruct((B * ntile, 8, S), F32)]
    return pl.pallas_call(
        bwd_body, name=name, grid=(B, ntile, nq), in_specs=in_specs, out_specs=out_specs,
        out_shape=out_shape, compiler_params=_cp("parallel", "parallel", "arbitrary"),
    )(*args)


def seq_cumsum(name, x, *, B, S, reverse):
    nb = S // LANE

    def body(x_ref, o_ref):
        r = lax.broadcasted_iota(jnp.int32, (LANE, LANE), 0)
        c = lax.broadcasted_iota(jnp.int32, (LANE, LANE), 1)
        tri = jnp.where((r <= c) if reverse else (r >= c), 1.0, 0.0)
        carry = jnp.zeros((1, LANE), F32)
        for blk in (range(nb - 1, -1, -1) if reverse else range(nb)):
            xb = x_ref[blk * LANE:(blk + 1) * LANE, :]
            o_ref[blk * LANE:(blk + 1) * LANE, :] = xdot(tri, xb) + carry
            carry = carry + jnp.sum(xb, axis=0, keepdims=True)

    return pl.pallas_call(
        body, name=name, grid=(B,), in_specs=[pl.BlockSpec((S, LANE), lambda b: (b, 0))],
        out_specs=pl.BlockSpec((S, LANE), lambda b: (b, 0)),
        out_shape=jax.ShapeDtypeStruct(x.shape, F32), compiler_params=_cp("parallel"),
    )(x)


SCAN_ROWS = 64


def _shift_rows(ref, r0, rows, d, up):
    if d % 8 == 0:
        return ref[pl.ds(r0 + d if up else r0 - d, rows), :]
    pad = -(-d // 8) * 8
    if up:
        win = ref[pl.ds(r0, rows + pad), :]
        return pltpu.roll(win, rows + pad - d, 0)[0:rows, :]
    win = ref[pl.ds(r0 - pad, rows + pad), :]
    return pltpu.roll(win, d, 0)[pad:rows + pad, :]


def _cmul(ar, ai, br, bi):
    return ar * br - ai * bi, ar * bi + ai * br


def s5_scan(name, x_re, x_im, lam, *, B, S, reverse, state=None):
    C = S5_C
    cw = LANE
    R = _pick(S, (SCAN_ROWS,))
    pad = S
    log2s = int(math.log2(S))
    assert 1 << log2s == S
    passes = [(4 ** k, 3) for k in range(log2s // 2)] + ([(S // 2, 1)] if log2s % 2 else [])
    base = 0 if reverse else pad
    with_grad = state is not None

    def body(*refs):
        if with_grad:
            xr, xi, lam_ref, sr, si, o_r, o_i, dl_r, dl_i, a_r, a_i, b_r, b_i = refs
        else:
            xr, xi, lam_ref, o_r, o_i, a_r, a_i, b_r, b_i = refs
        zero = jnp.zeros((pad, cw), F32)
        z0 = S if reverse else 0
        for buf in (a_r, a_i, b_r, b_i):
            buf[z0:z0 + pad, :] = zero
        a_r[base:base + S, :] = xr[...]
        a_i[base:base + S, :] = xi[...]
        mr = lam_ref[0:1, :]
        mi = -lam_ref[1:2, :] if reverse else lam_ref[1:2, :]
        src, dst = (a_r, a_i), (b_r, b_i)
        for step, (d, terms) in enumerate(passes):
            last = step == len(passes) - 1
            powers = [(mr, mi)]
            for _ in range(terms - 1):
                powers.append(_cmul(*powers[-1], mr, mi))

            def chunk(c, _, src=src, dst=dst, d=d, last=last, powers=powers):
                r0 = pl.multiple_of(base + c * R, 8)
                nr, ni = src[0][pl.ds(r0, R), :], src[1][pl.ds(r0, R), :]
                for j, (pr_, pi_) in enumerate(powers):
                    xr_ = _shift_rows(src[0], r0, R, (j + 1) * d, reverse)
                    xi_ = _shift_rows(src[1], r0, R, (j + 1) * d, reverse)
                    nr = nr + pr_ * xr_ - pi_ * xi_
                    ni = ni + pr_ * xi_ + pi_ * xr_
                if last:
                    o0 = pl.multiple_of(c * R, 8)
                    o_r[pl.ds(o0, R), :] = nr
                    o_i[pl.ds(o0, R), :] = ni
                else:
                    dst[0][pl.ds(r0, R), :] = nr
                    dst[1][pl.ds(r0, R), :] = ni
                return 0

            lax.fori_loop(0, S // R, chunk, 0)
            mr, mi = _cmul(*powers[-1], *powers[0]) if terms == 3 else _cmul(mr, mi, mr, mi)
            src, dst = dst, src
        if with_grad:
            def fold(v):
                return jnp.sum(v.reshape(R // 8, 8, cw), axis=0)

            def accum(c, carry, first=False):
                r0 = 0 if first else pl.multiple_of(c * R, 8)
                gr, gi = o_r[pl.ds(r0, R), :], o_i[pl.ds(r0, R), :]
                if first:
                    keep = lax.broadcasted_iota(jnp.int32, (R, 1), 0) >= 1
                    pr = jnp.where(keep, pltpu.roll(sr[0:R, :], 1, 0), 0.0)
                    pi = jnp.where(keep, pltpu.roll(si[0:R, :], 1, 0), 0.0)
                else:
                    pr = _shift_rows(sr, r0, R, 1, False)
                    pi = _shift_rows(si, r0, R, 1, False)
                return (carry[0] + fold(gr * pr + gi * pi), carry[1] + fold(gi * pr - gr * pi))

            acc = accum(0, (jnp.zeros((8, cw), F32), jnp.zeros((8, cw), F32)), first=True)
            acc = lax.fori_loop(1, S // R, accum, acc)
            dl_r[...] = acc[0]
            dl_i[...] = acc[1]

    seq = pl.BlockSpec((S, cw), lambda b, j: (b, j))
    in_specs = [seq, seq, pl.BlockSpec((2, cw), lambda b, j: (0, j))]
    args = [x_re, x_im, lam]
    out_specs = [seq, seq]
    out_shape = [jax.ShapeDtypeStruct(x_re.shape, F32)] * 2
    if with_grad:
        in_specs += [seq, seq]
        args += list(state)
        out_specs += [pl.BlockSpec((8, cw), lambda b, j: (b, j))] * 2
        out_shape += [jax.ShapeDtypeStruct((B * 8, C), F32)] * 2
    return pl.pallas_call(
        body, name=name, grid=(B, C // cw), in_specs=in_specs, out_specs=out_specs, out_shape=out_shape,
        scratch_shapes=[pltpu.VMEM((S + pad, cw), F32)] * 4,
        compiler_params=_cp("parallel", "parallel"),
    )(*args)


CONV_CW = 256


def _conv_taps(ref, r0, rows, first):
    cur = ref[pl.ds(r0, rows), :]
    if first:
        row = lax.broadcasted_iota(jnp.int32, (rows, 1), 0)
        p1 = jnp.where(row >= 1, pltpu.roll(cur, 1, 0), 0.0)
        p2 = jnp.where(row >= 2, pltpu.roll(cur, 2, 0), 0.0)
    else:
        p1 = _shift_rows(ref, r0, rows, 1, False)
        p2 = _shift_rows(ref, r0, rows, 2, False)
    return cur, p1, p2


def _conv_apply(w_ref, taps):
    return w_ref[2:3, :] * taps[0] + w_ref[1:2, :] * taps[1] + w_ref[0:1, :] * taps[2]


def conv_gate_fwd(up, conv_w, *, B, S):
    M, F2 = up.shape
    F = F2 // 2
    cw = _pick(F, (CONV_CW, LANE))
    nf = F // cw
    R = _pick(S, (SCAN_ROWS,))

    def body(g_ref, v_ref, wg_ref, wv_ref, o_ref):
        def chunk(c, _, first=False):
            r0 = 0 if first else pl.multiple_of(c * R, 8)
            cg = _conv_apply(wg_ref, _conv_taps(g_ref, r0, R, first))
            cv = _conv_apply(wv_ref, _conv_taps(v_ref, r0, R, first))
            o_ref[pl.ds(r0, R), :] = (cg * _sigmoid(cg) * cv).astype(o_ref.dtype)
            return 0

        chunk(0, 0, first=True)
        lax.fori_loop(1, S // R, chunk, 0)

    seq = lambda off: pl.BlockSpec((S, cw), lambda b, j: (b, off + j))
    wsp = lambda off: pl.BlockSpec((3, cw), lambda b, j: (0, off + j))
    return pl.pallas_call(
        body, name="conv_gate", grid=(B, nf), in_specs=[seq(0), seq(nf), wsp(0), wsp(nf)],
        out_specs=seq(0), out_shape=jax.ShapeDtypeStruct((M, F), BF16),
        compiler_params=_cp("parallel", "parallel"),
    )(up, up, conv_w, conv_w)


def conv_gate_bwd(up, conv_w, dact, *, B, S):
    M, F2 = up.shape
    F = F2 // 2
    cw = _pick(F, (CONV_CW, LANE))
    nf = F // cw
    R = _pick(S, (SCAN_ROWS,))
    nchunk = S // R

    def body(s_ref, p_ref, ws_ref, wp_ref, da_ref, du_ref, dw_ref, dc_ref, silu_ref):
        is_gate = pl.program_id(1) == 0
        ex = pl.program_id(2)
        dc_ref[S:S + 8, :] = jnp.zeros((8, cw), F32)

        def fold(v):
            return jnp.sum(v.reshape(R // 8, 8, cw), axis=0)

        first_b = ex == 0

        def pass1(gate_step):
            def chunk(c, acc, first=False):
                r0 = 0 if first else pl.multiple_of(c * R, 8)
                taps = _conv_taps(s_ref, r0, R, first)
                da = da_ref[pl.ds(r0, R), :]
                if gate_step:
                    cp = _conv_apply(wp_ref, _conv_taps(p_ref, r0, R, first))
                    cs = _conv_apply(ws_ref, taps)
                    sg = _sigmoid(cs)
                    silu_ref[ex, pl.ds(r0, R), :] = cs * sg
                    dc = da * cp * (sg * (1.0 + cs * (1.0 - sg)))
                else:
                    dc = da * silu_ref[ex, pl.ds(r0, R), :]
                dc_ref[pl.ds(r0, R), :] = dc
                return (acc[0] + fold(dc * taps[2]), acc[1] + fold(dc * taps[1]), acc[2] + fold(dc * taps[0]))

            z = jnp.zeros((8, cw), F32)
            acc = chunk(0, (z, z, z), first=True)
            acc = lax.fori_loop(1, nchunk, chunk, acc)
            for tap in range(3):
                tot = jnp.sum(acc[tap], axis=0, keepdims=True)

                @pl.when(first_b)
                def _(tap=tap, tot=tot):
                    dw_ref[tap:tap + 1, :] = tot

                @pl.when(jnp.logical_not(first_b))
                def _(tap=tap, tot=tot):
                    dw_ref[tap:tap + 1, :] += tot

        @pl.when(is_gate)
        def _():
            pass1(True)

        @pl.when(jnp.logical_not(is_gate))
        def _():
            pass1(False)

        def pass2(c, _):
            r0 = pl.multiple_of(c * R, 16)
            n0 = dc_ref[pl.ds(r0, R), :]
            n1 = _shift_rows(dc_ref, r0, R, 1, True)
            n2 = _shift_rows(dc_ref, r0, R, 2, True)
            du = ws_ref[2:3, :] * n0 + ws_ref[1:2, :] * n1 + ws_ref[0:1, :] * n2
            du_ref[pl.ds(r0, R), :] = du.astype(du_ref.dtype)
            return 0

        lax.fori_loop(0, nchunk, pass2, 0)

    seq = lambda f: pl.BlockSpec((S, cw), lambda j, t, b: (b, f(j, t)))
    wsp = lambda f: pl.BlockSpec((3, cw), lambda j, t, b: (0, f(j, t)))
    same, value, act_col = (lambda j, t: j + t * nf), (lambda j, t: j + nf), (lambda j, t: j)
    return pl.pallas_call(
        body, name="conv_gate_vjp", grid=(nf, 2, B),
        in_specs=[seq(same), seq(value), wsp(same), wsp(value), seq(act_col)],
        out_specs=[seq(same), wsp(same)],
        out_shape=[jax.ShapeDtypeStruct((M, F2), BF16), jax.ShapeDtypeStruct((3, F2), F32)],
        scratch_shapes=[pltpu.VMEM((S + 8, cw), F32), pltpu.VMEM((B, S, cw), F32)],
        compiler_params=_cp("parallel", "arbitrary", "arbitrary"),
    )(up, up, conv_w, conv_w, dact)


def loss_head(y, target):
    M, D = y.shape
    tm = _pick(M, (256, 128, 64, 32, 16, 8))

    def body(y_ref, t_ref, dy_ref, l_ref):
        diff = y_ref[...] - t_ref[...]
        dy_ref[...] = diff * (1.0 / D)
        part = jnp.sum(jnp.sum(diff * diff, axis=1, keepdims=True), axis=0, keepdims=True)

        @pl.when(pl.program_id(0) == 0)
        def _():
            l_ref[...] = jnp.zeros_like(l_ref)

        l_ref[...] += part

    row = pl.BlockSpec((tm, D), lambda i: (i, 0))
    return pl.pallas_call(
        body, name="loss_head", grid=(M // tm,), in_specs=[row, row],
        out_specs=[row, pl.BlockSpec((8, LANE), lambda i: (0, 0))],
        out_shape=[jax.ShapeDtypeStruct((M, D), F32), jax.ShapeDtypeStruct((8, LANE), F32)],
        compiler_params=_cp("arbitrary"),
    )(y, target)


def adamw(name, w, g, m, v):
    R, C = w.shape
    tr = _pick(R, (256, 128, 64, 32, 16, 8))

    def body(w_ref, g_ref, m_ref, v_ref, d_ref, nm_ref, nv_ref, go_ref):
        gv = g_ref[...]
        go_ref[...] = gv
        nm = ADAM_B1 * m_ref[...] + (1.0 - ADAM_B1) * gv
        nv = ADAM_B2 * v_ref[...] + (1.0 - ADAM_B2) * (gv * gv)
        m_hat = nm / (1.0 - ADAM_B1 ** ADAM_STEP)
        v_hat = nv / (1.0 - ADAM_B2 ** ADAM_STEP)
        d_ref[...] = -ADAM_LR * (m_hat / (jnp.sqrt(v_hat) + ADAM_EPS) + ADAM_WD * w_ref[...])
        nm_ref[...] = nm
        nv_ref[...] = nv

    blk = pl.BlockSpec((tr, C), lambda i: (i, 0))
    return pl.pallas_call(
        body, name=name, grid=(R // tr,), in_specs=[blk] * 4, out_specs=[blk] * 4,
        out_shape=[jax.ShapeDtypeStruct((R, C), F32)] * 4, compiler_params=_cp("parallel"),
    )(w, g, m, v)


def _seg(D):
    o = 3 * D
    return dict(ckv=o, fq=o + 256, fk=o + 512, fv=o + 768, u=o + 1024, small=o + 1280, cq=o + 1536, P=o + 1920)


def _pad_last(a, n):
    return jnp.pad(a, [(0, 0)] * (a.ndim - 1) + [(0, n - a.shape[-1])])


def _place(a, lo, n=LANE):
    return jnp.pad(a, [(0, 0)] * (a.ndim - 1) + [(lo, n - lo - a.shape[-1])])


def _in_segments(D):
    s = _seg(D)
    sm = s["small"]
    return ((0, 384, s["cq"]), (384, 640, s["ckv"]), (640, 672, sm + NOPE), (672, 1440, s["fq"]),
            (1440, 1444, sm + F_LANE0), (1444, 1700, s["u"]), (1700, 1700 + 3 * D, 0))


def _chip_pieces(win4, lo, hi):
    cw = win4.shape[-1]
    out = []
    for k in range(N_CHIPS):
        a, b = max(lo, k * cw), min(hi, (k + 1) * cw)
        if a < b:
            out.append(win4[:, k, :, a - k * cw:b - k * cw])
    return out


def prep_weights(w):
    L, D = w["attn_norm_g"].shape
    win = w["w_in"]
    z = lambda n: [jnp.zeros((L, D, n), win.dtype)]
    cols = lambda lo, hi: _chip_pieces(win, lo, hi)
    g0 = 1700
    wp = jnp.concatenate(
        cols(g0, g0 + 3 * D) + cols(384, 640) + cols(672, 1440) + cols(1444, 1700) + z(NOPE) + cols(640, 672)
        + cols(1440, 1444) + z(LANE - F_LANE0 - N_HEADS) + z(LANE) + cols(0, 384), axis=-1)
    wukv, wb = w["w_ukv"], w["w_branch"]
    row3 = lambda a: a[:, None, :]
    return dict(
        g1=row3(w["attn_norm_g"]), Wp=wp, gql=row3(w["q_lat_norm_g"]), gkvl=row3(w["kv_lat_norm_g"]),
        Wuq=_pad_last(w["w_uq"], LANE).reshape(L, Q_RANK, N_HEADS * LANE),
        Wk=_pad_last(wukv[..., :NOPE], LANE).reshape(L, KV_RANK, N_HEADS * LANE),
        Wv=_pad_last(wukv[..., NOPE:], LANE).reshape(L, KV_RANK, N_HEADS * LANE),
        gq=row3(_pad_last(w["mla_q_norm_g"], LANE)), gk=row3(_pad_last(w["mla_k_norm_g"], LANE)),
        gfq=row3(jnp.tile(w["fox_q_norm_g"], (1, 2))), gfk=row3(jnp.tile(w["fox_k_norm_g"], (1, 2))),
        fbias=row3(_place(w["fox_f_bias"], F_LANE0)),
        lre=w["s5_lambda_re"].reshape(L, 1, S5_C), lim=w["s5_lambda_im"].reshape(L, 1, S5_C),
        lstep=row3(_pad_last(w["s5_log_step"], LANE)),
        btr=jnp.transpose(w["s5_b_re"], (0, 3, 1, 2)).reshape(L, S5_H, S5_C),
        bti=jnp.transpose(w["s5_b_im"], (0, 3, 1, 2)).reshape(L, S5_H, S5_C),
        ctr=_pad_last(jnp.transpose(w["s5_c_re"], (0, 1, 3, 2)).reshape(L, S5_C, S5_H), LANE),
        cti=_pad_last(jnp.transpose(w["s5_c_im"], (0, 1, 3, 2)).reshape(L, S5_C, S5_H), LANE),
        s5d=w["s5_d"].reshape(L, 1, BW), Wglu=w["s5_w_glu"], bglu=row3(w["s5_b_glu"]),
        Wb0=jnp.pad(wb[:, 0].reshape(L, N_HEADS, V_DIM, D), ((0, 0), (0, 0), (0, LANE - V_DIM), (0, 0))
                    ).reshape(L, N_HEADS * LANE, D),
        Wb1=wb[:, 1], Wb2=wb[:, 2], Wout=w["w_out"], g2=row3(w["ffn_norm_g"]), Wup=w["w_up"],
        convw=w["ffn_conv_w"], Wdown=w["w_down"],
    )


BIG_KEYS = ("Wp", "Wuq", "Wk", "Wv", "Wout", "Wup", "Wdown")


def with_transposes(P):
    out = dict(P)
    for k in BIG_KEYS:
        out[k + "T"] = jnp.swapaxes(P[k], 1, 2)
    return out


def layer_params(P, l):
    return {k: (v if k in BIG_KEYS or k[:-1] in BIG_KEYS else v[l]) for k, v in P.items()}


def unprep_grads(G, D):
    L = G["g1"].shape[0]
    dwp = G["Wp"]
    segs = _in_segments(D)
    cw = segs[-1][1] // N_CHIPS
    chips = []
    for k in range(N_CHIPS):
        pieces = []
        for lo, hi, at in segs:
            a, b = max(lo, k * cw), min(hi, (k + 1) * cw)
            if a < b:
                pieces.append(dwp[..., at + a - lo:at + b - lo])
        chips.append(jnp.concatenate(pieces, axis=-1))
    w_in = jnp.stack(chips, axis=1)
    heads = lambda a, rows, keep: a.reshape(L, rows, N_HEADS, LANE)[..., :keep]
    wb0 = G["Wb0"].reshape(L, N_HEADS, LANE, D)[:, :, :V_DIM].reshape(L, BW, D)
    gf = lambda a: a[:, 0, :FOX_DIM] + a[:, 0, FOX_DIM:]
    return dict(
        attn_norm_g=G["g1"][:, 0], w_in=w_in, q_lat_norm_g=G["gql"][:, 0], w_uq=heads(G["Wuq"], Q_RANK, QK_DIM),
        kv_lat_norm_g=G["gkvl"][:, 0],
        w_ukv=jnp.concatenate([heads(G["Wk"], KV_RANK, NOPE), heads(G["Wv"], KV_RANK, V_DIM)], axis=-1),
        mla_q_norm_g=G["gq"][:, 0, :QK_DIM], mla_k_norm_g=G["gk"][:, 0, :QK_DIM],
        fox_q_norm_g=gf(G["gfq"]), fox_k_norm_g=gf(G["gfk"]),
        fox_f_bias=G["fbias"][:, 0, F_LANE0:F_LANE0 + N_HEADS],
        s5_lambda_re=G["lre"].reshape(L, S5_G, S5_P), s5_lambda_im=G["lim"].reshape(L, S5_G, S5_P),
        s5_b_re=jnp.transpose(G["btr"].reshape(L, S5_H, S5_G, S5_P), (0, 2, 3, 1)),
        s5_b_im=jnp.transpose(G["bti"].reshape(L, S5_H, S5_G, S5_P), (0, 2, 3, 1)),
        s5_c_re=jnp.transpose(G["ctr"][..., :S5_H].reshape(L, S5_G, S5_P, S5_H), (0, 1, 3, 2)),
        s5_c_im=jnp.transpose(G["cti"][..., :S5_H].reshape(L, S5_G, S5_P, S5_H), (0, 1, 3, 2)),
        s5_d=G["s5d"].reshape(L, S5_G, S5_H), s5_log_step=G["lstep"][:, 0, :S5_G],
        s5_w_glu=G["Wglu"], s5_b_glu=G["bglu"][:, 0],
        w_branch=jnp.stack([wb0, G["Wb1"], G["Wb2"]], axis=1), w_out=G["Wout"], ffn_norm_g=G["g2"][:, 0],
        w_up=G["Wup"], ffn_conv_w=G["convw"], w_down=G["Wdown"],
    )


def rope_tables(positions):
    inv_freq = ROPE_THETA ** (-jnp.arange(0, ROPE, 2, dtype=F32) / ROPE)
    ang = positions.astype(F32)[..., None] * inv_freq
    cos, sin = jnp.cos(ang), jnp.sin(ang)
    ones = jnp.ones(ang.shape[:-1] + (NOPE,), F32)
    zeros = jnp.zeros(ang.shape[:-1] + (LANE - NOPE - ROPE,), F32)
    cos_t = jnp.concatenate([ones, cos, cos, zeros], axis=-1)
    sin_t = jnp.concatenate([0.0 * ones, sin, sin, zeros], axis=-1)
    return cos_t.reshape(-1, LANE), sin_t.reshape(-1, LANE)


def fn_rms_res(n):
    return lambda x, g: (_rms(x, g, n), x)


def _s5_mats(p):
    return s5_params({k: p[k] for k in S5_PARAM_NAMES})


def layer_fwd(x, p, l, cos_t, sin_t, B, S):
    M, D = x.shape
    s = _seg(D)
    sm = s["small"] // LANE
    head = lambda j: j
    h = rowwise("rms_attn", fn_rms(D), [(x, D, 0)], [p["g1"]], [(D, BF16)])[0]
    proj = mm(h, p["Wp"], "nn", name="in_proj", layer=l)
    cnq = rowwise("latq_norm", fn_rms(Q_RANK), [(proj, Q_RANK, s["cq"] // Q_RANK)], [p["gql"]], [(Q_RANK, BF16)])[0]
    cnkv = rowwise("latkv_norm", fn_rms(KV_RANK), [(proj, KV_RANK, s["ckv"] // KV_RANK)], [p["gkvl"]],
                   [(KV_RANK, BF16)])[0]
    qraw = mm(cnq, p["Wuq"], "nn", name="q_up", layer=l)
    kn = mm(cnkv, p["Wk"], "nn", name="k_up", layer=l)
    v5 = mm(cnkv, p["Wv"], "nn", name="v_up", out_dtype=BF16, layer=l)
    qrot = rowwise("q_post", fn_qpost, [(qraw, LANE, head), (cos_t, LANE, 0), (sin_t, LANE, 0)], [p["gq"]],
                   [(LANE, BF16)], nj=N_HEADS)[0]
    krot = rowwise("k_post", fn_kpost, [(kn, LANE, head), (proj, LANE, sm), (cos_t, LANE, 0), (sin_t, LANE, 0)],
                   [p["gk"]], [(LANE, BF16)], nj=N_HEADS)[0]
    omla = attention("mla_attn", qrot, krot, v5, B=B, S=S, ntile=N_HEADS, hpt=1, dk=QK_DIM)
    fq0, fk0 = s["fq"] // LANE, s["fk"] // LANE
    qf = rowwise("foxq_norm", fn_foxnorm, [(proj, LANE, lambda j: fq0 + j)], [p["gfq"]], [(LANE, BF16)], nj=2)[0]
    kf = rowwise("foxk_norm", fn_foxnorm, [(proj, LANE, lambda j: fk0 + j)], [p["gfk"]], [(LANE, BF16)], nj=2)[0]
    lf = rowwise("fgate", fn_fgate, [(proj, LANE, sm)], [p["fbias"]], [(LANE, F32)])[0]
    cum = seq_cumsum("fox_cumsum", lf, B=B, S=S, reverse=False)
    ckt = _pad_rows8(jnp.transpose(cum.reshape(B, S, LANE)[:, :, F_LANE0:F_LANE0 + N_HEADS], (0, 2, 1)))
    ofox = attention("fox_attn", qf, kf, proj, B=B, S=S, ntile=2, hpt=2, dk=FOX_DIM, vc=s["fv"] // LANE,
                     cum=cum, ckt=ckt)
    lam, bre, bim, ctop, cbot = _s5_mats(p)
    u16 = proj[:, s["u"]:s["u"] + BW].astype(BF16)
    bur = mm(u16, bre, "nn", name="s5_bu_re")
    bui = mm(u16, bim, "nn", name="s5_bu_im")
    sr, si = s5_scan("s5_scan", bur, bui, lam, B=B, S=S, reverse=False)
    ypre = mm(si, cbot, "nn", name="s5_y_im", add=mm(sr, ctop, "nn", name="s5_y_re"))
    os5 = rowwise("s5_post", fn_s5post, [(ypre, BW, 0), (proj, BW, s["u"] // BW)],
                  [p["s5d"], p["Wglu"], p["bglu"]], [(BW, BF16)])[0]
    merged = rowwise("merge", fn_merge,
                     [(omla, N_HEADS * LANE, 0), (ofox, BW, 0), (os5, BW, 0), (proj, D, 0), (proj, D, 1), (proj, D, 2)],
                     [p["Wb0"], p["Wb1"], p["Wb2"]], [(D, BF16)])[0]
    xmid = mm(merged, p["Wout"], "nn", name="out_proj", add=x, layer=l)
    h2 = rowwise("rms_ffn", fn_rms(D), [(xmid, D, 0)], [p["g2"]], [(D, BF16)])[0]
    up = mm(h2, p["Wup"], "nn", name="ffn_up", layer=l)
    act = conv_gate_fwd(up, p["convw"], B=B, S=S)
    xout = mm(act, p["Wdown"], "nn", name="ffn_down", add=xmid, layer=l)
    saved = dict(x=x, h=h, proj=proj, cnq=cnq, cnkv=cnkv, qraw=qraw, kn=kn, v5=v5, qrot=qrot, krot=krot, qf=qf,
                 kf=kf, cum=cum, ckt=ckt, omla=omla, ofox=ofox, os5=os5, u16=u16, sr=sr, si=si, ypre=ypre,
                 merged=merged, xmid=xmid, h2=h2, up=up, act=act)
    return xout, saved


def _pad_rows8(a):
    return jnp.pad(a, ((0, 0), (0, 8 - a.shape[1]), (0, 0)))


STACKED = ("Wp", "Wout", "Wup", "Wdown")


def layer_bwd(dx, p, l, sv, cos_t, sin_t, B, S, stacks):
    M, D = dx.shape
    slot = lambda k: (stacks["L"], l, stacks.get(k))
    s = _seg(D)
    sm = s["small"] // LANE
    head = lambda j: j
    proj = sv["proj"]
    dact = mm(dx, p["WdownT"], "nn", name="ffn_down_dx", layer=l)
    d_wdown = mm(sv["act"], dx, "tn", name="ffn_down_dw", slot=slot("Wdown"))
    dup, d_convw = conv_gate_bwd(sv["up"], p["convw"], dact, B=B, S=S)
    dh2 = mm(dup, p["WupT"], "nn", name="ffn_up_dx", layer=l)
    d_wup = mm(sv["h2"], dup, "tn", name="ffn_up_dw", slot=slot("Wup"))
    (dxmid,), (d_g2,) = rowwise_vjp("rms_ffn_vjp", fn_rms_res(D), [(sv["xmid"], D, 0)], [p["g2"]],
                                    [(dh2, D, 0), (dx, D, 0)], [0])
    dmerged = mm(dxmid, p["WoutT"], "nn", name="out_proj_dx", layer=l)
    d_wout = mm(sv["merged"], dxmid, "tn", name="out_proj_dw", slot=slot("Wout"))
    (dom, dof, dos, dg0, dg1, dg2), (d_wb0, d_wb1, d_wb2) = rowwise_vjp(
        "merge_vjp", fn_merge,
        [(sv["omla"], N_HEADS * LANE, 0), (sv["ofox"], BW, 0), (sv["os5"], BW, 0), (proj, D, 0), (proj, D, 1),
         (proj, D, 2)], [p["Wb0"], p["Wb1"], p["Wb2"]], [(dmerged, D, 0)], [0, 1, 2, 3, 4, 5], tm=256,
        gdt=[BF16, BF16, F32, BF16, BF16, BF16])
    lam, bre, bim, ctop, cbot = _s5_mats(p)
    (dypre, du_a), (d_s5d, d_wglu, d_bglu) = rowwise_vjp(
        "s5_post_vjp", fn_s5post, [(sv["ypre"], BW, 0), (proj, BW, s["u"] // BW)],
        [p["s5d"], p["Wglu"], p["bglu"]], [(dos, BW, 0)], [0, 1], gdt=[BF16, F32])
    dsr = mm(dypre, ctop.T, "nn", name="s5_y_re_dx")
    dsi = mm(dypre, cbot.T, "nn", name="s5_y_im_dx")
    d_ctop = mm(sv["sr"], dypre, "tn", name="s5_y_re_dw")
    d_cbot = mm(sv["si"], dypre, "tn", name="s5_y_im_dw")
    gr, gi, dl_r, dl_i = s5_scan("s5_scan_vjp", dsr, dsi, lam, B=B, S=S, reverse=True, state=(sv["sr"], sv["si"]))
    du = mm(gi, bim.T, "nn", name="s5_bu_im_dx", out_dtype=BF16,
            add=mm(gr, bre.T, "nn", name="s5_bu_re_dx", add=du_a))
    d_bre = mm(sv["u16"], gr, "tn", name="s5_bu_re_dw")
    d_bim = mm(sv["u16"], gi, "tn", name="s5_bu_im_dw")
    d_s5 = s5_params_vjp({k: p[k] for k in S5_PARAM_NAMES}, dl_r, dl_i, d_bre, d_bim, d_ctop, d_cbot)
    dqf, dkf, dfv, dcq_t, dckt_t = attention("fox_attn_vjp", sv["qf"], sv["kf"], proj, B=B, S=S, ntile=2, hpt=2,
                                             dk=FOX_DIM, vc=s["fv"] // LANE, cum=sv["cum"], ckt=sv["ckt"], do=dof)
    dck = dckt_t.reshape(B, 2, 8, S)
    dck = jnp.transpose(dck[:, 0, :N_HEADS] + dck[:, 1, :N_HEADS], (0, 2, 1)).reshape(M, N_HEADS)
    dcum = rowwise("fox_dcum", lambda a, b: (a[:, 0:LANE] + a[:, LANE:2 * LANE] + b,),
                   [(dcq_t, 2 * LANE, 0), (_place(dck, F_LANE0), LANE, 0)], [], [(LANE, F32)])[0]
    dlf = seq_cumsum("fox_cumsum_vjp", dcum, B=B, S=S, reverse=True)
    (dsmall_f,), (d_fbias,) = rowwise_vjp("fgate_vjp", fn_fgate, [(proj, LANE, sm)], [p["fbias"]],
                                          [(dlf, LANE, 0)], [0])
    fq0, fk0 = s["fq"] // LANE, s["fk"] // LANE
    (dfq,), (d_gfq,) = rowwise_vjp("foxq_norm_vjp", fn_foxnorm, [(proj, LANE, lambda j: fq0 + j)], [p["gfq"]],
                                   [(dqf, LANE, head)], [0], nj=2, gdt=[BF16])
    (dfk,), (d_gfk,) = rowwise_vjp("foxk_norm_vjp", fn_foxnorm, [(proj, LANE, lambda j: fk0 + j)], [p["gfk"]],
                                   [(dkf, LANE, head)], [0], nj=2, gdt=[BF16])
    dqrot, dkrot, dv5 = attention("mla_attn_vjp", sv["qrot"], sv["krot"], sv["v5"], B=B, S=S, ntile=N_HEADS,
                                  hpt=1, dk=QK_DIM, do=dom)
    (dqraw,), (d_gq,) = rowwise_vjp("q_post_vjp", fn_qpost,
                                    [(sv["qraw"], LANE, head), (cos_t, LANE, 0), (sin_t, LANE, 0)], [p["gq"]],
                                    [(dqrot, LANE, head)], [0], nj=N_HEADS, gdt=[BF16])
    (dkn, dsmall_k), (d_gk,) = rowwise_vjp(
        "k_post_vjp", fn_kpost, [(sv["kn"], LANE, head), (proj, LANE, sm), (cos_t, LANE, 0), (sin_t, LANE, 0)],
        [p["gk"]], [(dkrot, LANE, head)], [0, 1], nj=N_HEADS, gdt=[BF16, F32])
    dsmall = rowwise("small_sum", fn_add5, [(dsmall_k, N_HEADS * LANE, 0), (dsmall_f, LANE, 0)], [], [(LANE, BF16)])[0]
    dcnq = mm(dqraw, p["WuqT"], "nn", name="q_up_dx", layer=l)
    d_wuq = mm(sv["cnq"], dqraw, "tn", name="q_up_dw")
    dcnkv = mm(dv5, p["WvT"], "nn", name="v_up_dx", layer=l, add=mm(dkn, p["WkT"], "nn", name="k_up_dx", layer=l))
    d_wk = mm(sv["cnkv"], dkn, "tn", name="k_up_dw")
    d_wv = mm(sv["cnkv"], dv5, "tn", name="v_up_dw")
    (dcq,), (d_gql,) = rowwise_vjp("latq_norm_vjp", fn_rms(Q_RANK), [(proj, Q_RANK, s["cq"] // Q_RANK)], [p["gql"]],
                                   [(dcnq, Q_RANK, 0)], [0], gdt=[BF16])
    (dckv,), (d_gkvl,) = rowwise_vjp("latkv_norm_vjp", fn_rms(KV_RANK), [(proj, KV_RANK, s["ckv"] // KV_RANK)],
                                     [p["gkvl"]], [(dcnkv, KV_RANK, 0)], [0], gdt=[BF16])
    dproj = jnp.concatenate([dg0, dg1, dg2, dckv, dfq, dfk, dfv.astype(BF16), du, dsmall,
                             jnp.zeros((M, LANE), BF16), dcq], axis=1)
    dh = mm(dproj, p["WpT"], "nn", name="in_proj_dx", layer=l)
    d_wp = mm(sv["h"], dproj, "tn", name="in_proj_dw", slot=slot("Wp"))
    (dxin,), (d_g1,) = rowwise_vjp("rms_attn_vjp", fn_rms_res(D), [(sv["x"], D, 0)], [p["g1"]],
                                   [(dh, D, 0), (dxmid, D, 0)], [0])
    grads = dict(g1=d_g1, Wp=d_wp, gql=d_gql, gkvl=d_gkvl, Wuq=d_wuq, Wk=d_wk, Wv=d_wv, gq=d_gq, gk=d_gk,
                 gfq=d_gfq, gfk=d_gfk, fbias=d_fbias, s5d=d_s5d, Wglu=d_wglu, bglu=d_bglu, Wb0=d_wb0, Wb1=d_wb1,
                 Wb2=d_wb2, Wout=d_wout, g2=d_g2, Wup=d_wup, convw=d_convw, Wdown=d_wdown)
    grads.update(dict(zip(S5_PARAM_NAMES, d_s5)))
    return dxin, grads


def local_step(x, positions, target, w):
    B, S, D = x.shape
    M = B * S
    P = with_transposes(prep_weights(w))
    L = P["g1"].shape[0]
    cos_t, sin_t = rope_tables(positions)
    xc, saved = x.reshape(M, D), []
    for l in range(L):
        xc, sv = layer_fwd(xc, layer_params(P, l), l, cos_t, sin_t, B, S)
        saved.append(sv)
    dxc, sq = loss_head(xc, target.reshape(M, D))
    grads, stacks = [None] * L, {"L": L}
    for l in reversed(range(L)):
        dxc, grads[l] = layer_bwd(dxc, layer_params(P, l), l, saved[l], cos_t, sin_t, B, S, stacks)
        stacks.update({k: grads[l][k] for k in STACKED})
    G = {k: (stacks[k] if k in STACKED else jnp.stack([g[k] for g in grads])) for k in grads[0]}
    return sq, dxc.reshape(B, S, D), unprep_grads(G, D)


MESH = pl.DeviceIdType.MESH
ANY = pl.BlockSpec(memory_space=pl.ANY)
N_CHIPS = 4
SHARDED = ("w_in", "w_uq", "w_ukv", "s5_w_glu", "w_branch", "w_out", "w_up", "ffn_conv_w", "w_down")
MINOR = ("w_branch", "w_up", "ffn_conv_w")
F32_TRAVEL = ("ffn_conv_w",)
WEIGHTS = ("attn_norm_g", "w_in", "q_lat_norm_g", "w_uq", "kv_lat_norm_g", "w_ukv", "mla_q_norm_g", "mla_k_norm_g",
           "fox_q_norm_g", "fox_k_norm_g", "fox_f_bias", "s5_lambda_re", "s5_lambda_im", "s5_b_re", "s5_b_im",
           "s5_c_re", "s5_c_im", "s5_d", "s5_log_step", "s5_w_glu", "s5_b_glu", "w_branch", "w_out", "ffn_norm_g",
           "w_up", "ffn_conv_w", "w_down")
SMALL = tuple(n for n in WEIGHTS if n not in SHARDED)
RS_BIG = tuple(n for n in SHARDED if n != "ffn_conv_w")
REDUCE_SMALL = SMALL + ("ffn_conv_w",)


def shard3(name, a):
    L = a.shape[0]
    if name in ("w_uq", "w_ukv"):
        return a.reshape(L, a.shape[1], -1)
    if name == "w_branch":
        return a.reshape(L, -1, a.shape[-1])
    return a


def full4(name, a):
    L = a.shape[0]
    if name == "w_in":
        return a
    if name in MINOR:
        return a.reshape(L, 1, -1, a.shape[-1])
    a = a.reshape(L, a.shape[1], -1)
    return a.reshape(L, N_CHIPS, a.shape[1] // N_CHIPS, a.shape[2])


def from_full4(name, a, ref_tail):
    L = a.shape[0]
    if name == "w_in":
        return a
    return a.reshape((L,) + tuple(ref_tail))


def _where():
    x, y, c = lax.axis_index("x"), lax.axis_index("y"), lax.axis_index("c")
    chips = [(1 - x, y), (x, 1 - y), (1 - x, 1 - y)]
    return (x, y, c), 2 * x + y, (x, y, 1 - c), chips, [2 * cx + cy for cx, cy in chips]


def _view(minor, ref4, layers, k):
    if minor:
        cs = ref4.shape[3] // N_CHIPS
        return ref4.at[layers, 0, :, pl.ds(pl.multiple_of(k * cs, LANE), cs)]
    return ref4.at[layers, k]


def _remote(src, dst, ssem, rsem, dev):
    return pltpu.make_async_remote_copy(src_ref=src, dst_ref=dst, send_sem=ssem, recv_sem=rsem,
                                        device_id=dev, device_id_type=MESH)


def gather_weights(shards, minor):
    n = len(shards)
    L = shards[0].shape[0]
    Lh = L // 2
    out_shape = []
    for a, mn in zip(shards, minor):
        _, r, cs = a.shape
        out_shape.append(jax.ShapeDtypeStruct((L, 1, r, N_CHIPS * cs) if mn else (L, N_CHIPS, r, cs), a.dtype))

    def body(*refs):
        w, g = refs[:n], refs[n:2 * n]
        send, recv = refs[2 * n:]
        (x, y, c), me, sib, chips, cidx = _where()
        mine, other, every = pl.ds(c * Lh, Lh), pl.ds((1 - c) * Lh, Lh), pl.ds(0, L)
        dst = lambda i, layers, k: _view(minor[i], g[i], layers, k)
        local = [_remote(w[i], dst(i, every, me), send.at[i, 6], recv.at[i, 6], sib) for i in range(n)]
        first = [_remote(w[i].at[mine], dst(i, mine, me), send.at[i, j], recv.at[i, j], (*chips[j], c))
                 for i in range(n) for j in range(3)]
        for cp in local + first:
            cp.start()
        passed = []
        for i in range(n):
            for j in range(3):
                blk = dst(i, mine, cidx[j])
                _remote(blk, blk, send.at[i, j], recv.at[i, j], (*chips[j], c)).wait_recv()
                fwd = _remote(blk, blk, send.at[i, 3 + j], recv.at[i, 3 + j], sib)
                fwd.start()
                passed.append(fwd)
        for i in range(n):
            for j in range(3):
                blk = dst(i, other, cidx[j])
                _remote(blk, blk, send.at[i, 3 + j], recv.at[i, 3 + j], sib).wait_recv()
        for cp in first + passed:
            cp.wait_send()
        for cp in local:
            cp.wait()

    return pl.pallas_call(
        body, name="gather_weights", in_specs=[ANY] * n, out_specs=[ANY] * n, out_shape=out_shape,
        scratch_shapes=[pltpu.SemaphoreType.DMA((n, 7)), pltpu.SemaphoreType.DMA((n, 7))],
    )(*shards)


def sibling_halves(grads):
    n = len(grads)
    L = grads[0].shape[0]
    Lh = L // 2
    half = [jax.ShapeDtypeStruct((Lh,) + a.shape[1:], a.dtype) for a in grads]

    def body(*refs):
        g, got = refs[:n], refs[n:2 * n]
        send, recv = refs[2 * n:]
        (x, y, c), me, sib, chips, cidx = _where()
        other = pl.ds((1 - c) * Lh, Lh)
        out = [_remote(g[i].at[other], got[i], send.at[i], recv.at[i], sib) for i in range(n)]
        for cp in out:
            cp.start()
        for cp in out:
            cp.wait()

    got = pl.pallas_call(
        body, name="grad_sibling_halves", in_specs=[ANY] * n, out_specs=[ANY] * n, out_shape=half,
        scratch_shapes=[pltpu.SemaphoreType.DMA((n,)), pltpu.SemaphoreType.DMA((n,))],
    )(*grads)
    return got


def scatter_chip_sums(travel, minor):
    n = len(travel)
    Lh = travel[0].shape[0]
    got_shape = []
    for t, mn in zip(travel, minor):
        r, cs = t.shape[2], (t.shape[3] // N_CHIPS if mn else t.shape[3])
        got_shape.append(jax.ShapeDtypeStruct((3, Lh, r, cs), t.dtype))

    def body(*refs):
        s16, got = refs[:n], refs[n:2 * n]
        send, recv = refs[2 * n:]
        (x, y, c), me, sib, chips, cidx = _where()
        every = pl.ds(0, Lh)
        out = [_remote(_view(minor[i], s16[i], every, cidx[j]), got[i].at[j], send.at[i, j], recv.at[i, j],
                       (*chips[j], c)) for i in range(n) for j in range(3)]
        for cp in out:
            cp.start()
        for cp in out:
            cp.wait()

    got = pl.pallas_call(
        body, name="grad_scatter", in_specs=[ANY] * n, out_specs=[ANY] * n, out_shape=got_shape,
        scratch_shapes=[pltpu.SemaphoreType.DMA((n, 3)), pltpu.SemaphoreType.DMA((n, 3))],
    )(*travel)
    return got


def share_halves(shards):
    n = len(shards)
    Lh = shards[0].shape[0] // 2

    def body(*refs):
        full = refs[n:2 * n]
        send, recv = refs[2 * n:]
        (x, y, c), me, sib, chips, cidx = _where()
        mine, other = pl.ds(c * Lh, Lh), pl.ds((1 - c) * Lh, Lh)
        out = [_remote(full[i].at[mine], full[i].at[mine], send.at[i], recv.at[i], sib) for i in range(n)]
        for cp in out:
            cp.start()
        for i in range(n):
            out[i].wait_send()
            _remote(full[i].at[other], full[i].at[other], send.at[i], recv.at[i], sib).wait_recv()

    return pl.pallas_call(
        body, name="grad_share_halves", in_specs=[ANY] * n, out_specs=[ANY] * n,
        out_shape=[jax.ShapeDtypeStruct(a.shape, a.dtype) for a in shards],
        input_output_aliases={i: i for i in range(n)},
        scratch_shapes=[pltpu.SemaphoreType.DMA((n,)), pltpu.SemaphoreType.DMA((n,))],
    )(*shards)


N_DEV = 8


def allreduce_small(v):
    R = v.shape[0]

    def body(x_ref, sum_ref, all_ref, send, recv, loc):
        (x, y, c), me, sib, chips, cidx = _where()

        def rows(px, py, pc):
            return all_ref.at[4 * px + 2 * py + pc]

        def copy(k, block, to, src=None):
            return _remote(rows(*block) if src is None else src, rows(*block), send.at[k], recv.at[k], to)

        mine = pltpu.make_async_copy(x_ref, rows(x, y, c), loc)
        mine.start()
        first = [copy(0, (x, y, c), sib, src=x_ref)]
        first += [copy(1 + j, (x, y, c), (*chip, c), src=x_ref) for j, chip in enumerate(chips)]
        for cp in first:
            cp.start()
        passed = [copy(4 + j, (*chip, c), sib) for j, chip in enumerate(chips)]
        for j, chip in enumerate(chips):
            copy(1 + j, (*chip, c), (x, y, c)).wait_recv()
            passed[j].start()
        copy(0, (x, y, 1 - c), (x, y, c)).wait_recv()
        for j, chip in enumerate(chips):
            copy(4 + j, (*chip, 1 - c), (x, y, c)).wait_recv()
        for cp in first + passed:
            cp.wait_send()
        mine.wait()
        acc = all_ref[0]
        for d in range(1, N_DEV):
            acc = acc + all_ref[d]
        sum_ref[...] = acc

    vm = pl.BlockSpec(memory_space=pltpu.VMEM)
    return pl.pallas_call(
        body, name="allreduce_small", in_specs=[vm], out_specs=[vm, vm],
        out_shape=[jax.ShapeDtypeStruct((R, LANE), F32), jax.ShapeDtypeStruct((N_DEV, R, LANE), F32)],
        scratch_shapes=[pltpu.SemaphoreType.DMA((7,)), pltpu.SemaphoreType.DMA((7,)), pltpu.SemaphoreType.DMA],
        compiler_params=pltpu.CompilerParams(vmem_limit_bytes=VMEM_LIMIT),
    )(v)[0]


EW_BLOCK_BYTES = 2 << 20


def _ew_rows(rows, cols):
    for tr in (1024, 512, 256, 128, 64, 32, 16, 8):
        if rows % tr == 0 and tr * cols * 4 <= EW_BLOCK_BYTES:
            return tr
    return rows


def add_pair(name, full, got, where, travel_dtype):
    R, C = got.shape
    tr = _ew_rows(R, C)
    nblk = R // tr

    def body(w_ref, a_ref, b_ref, s_ref, t_ref):
        s = a_ref[...] + b_ref[...]
        s_ref[...] = s
        t_ref[...] = s.astype(t_ref.dtype)

    blk = pl.BlockSpec((tr, C), lambda i, w: (i, 0))
    return pl.pallas_call(
        body, name=name,
        grid_spec=pltpu.PrefetchScalarGridSpec(
            num_scalar_prefetch=1, grid=(nblk,),
            in_specs=[pl.BlockSpec((tr, C), lambda i, w: (w[1] * nblk + i, 0)), blk], out_specs=[blk, blk]),
        out_shape=[jax.ShapeDtypeStruct((R, C), F32), jax.ShapeDtypeStruct((R, C), travel_dtype)],
        compiler_params=_cp("parallel"),
    )(where, full, got)


def add_four(name, sums, got, where, minor):
    Lh, _, r, C = sums.shape
    cs = got.shape[-1]
    tr = _ew_rows(r, cs)

    def body(w_ref, m_ref, g0, g1, g2, o_ref):
        o_ref[...] = ((m_ref[...] + g0[...].astype(F32)) + g1[...].astype(F32)) + g2[...].astype(F32)

    mine = (pl.BlockSpec((None, None, tr, cs), lambda l, i, w: (l, 0, i, w[0])) if minor
            else pl.BlockSpec((None, None, tr, cs), lambda l, i, w: (l, w[0], i, 0)))
    slot = lambda j: pl.BlockSpec((None, None, tr, cs), lambda l, i, w: (j, l, i, 0))
    return pl.pallas_call(
        body, name=name,
        grid_spec=pltpu.PrefetchScalarGridSpec(
            num_scalar_prefetch=1, grid=(Lh, r // tr), in_specs=[mine, slot(0), slot(1), slot(2)],
            out_specs=pl.BlockSpec((None, tr, cs), lambda l, i, w: (w[1] * Lh + l, i, 0))),
        out_shape=jax.ShapeDtypeStruct((2 * Lh, r, cs), F32), compiler_params=_cp("parallel", "parallel"),
    )(where, sums, got, got, got)


def reduce_scatter_grads(full_grads):
    names = list(RS_BIG)
    minor = [nm in MINOR for nm in names]
    where = jnp.stack([2 * lax.axis_index("x") + lax.axis_index("y"), lax.axis_index("c")]).astype(jnp.int32)
    g4 = [full4(nm, full_grads[nm]) for nm in names]
    got = sibling_halves(g4)
    sums, travel = [], []
    for nm, a, b in zip(names, g4, got):
        s, t = add_pair("chip_sum_" + nm, a.reshape(-1, a.shape[-1]), b.reshape(-1, b.shape[-1]), where,
                        F32 if nm in F32_TRAVEL else BF16)
        sums.append(s.reshape(b.shape))
        travel.append(t.reshape(b.shape))
    arrived = scatter_chip_sums(travel, minor)
    shards = [add_four("shard_sum_" + nm, s, b, where, mn) for nm, s, b, mn in zip(names, sums, arrived, minor)]
    return dict(zip(names, share_halves(shards)))


def pack_small(tree, extra=None, names=SMALL):
    parts = [tree[nm].reshape(-1) for nm in names]
    parts.append(jnp.zeros((1,), F32) if extra is None else extra.reshape(-1))
    blocks = []
    for p in parts:
        rows = _small_rows(p.shape[0])
        blocks.append(jnp.pad(p, (0, rows * LANE - p.shape[0])).reshape(rows, LANE))
    return jnp.concatenate(blocks, axis=0)


def _small_rows(size):
    return -(-size // (8 * LANE)) * 8


def unpack_small(packed, like, names=SMALL):
    out, at = {}, 0
    for nm in names:
        size = math.prod(like[nm].shape)
        rows = _small_rows(size)
        out[nm] = packed[at:at + rows].reshape(-1)[:size].reshape(like[nm].shape)
        at += rows
    return out, packed[at, 0]


def kernel(x, positions, attn_norm_g, w_in, q_lat_norm_g, w_uq, kv_lat_norm_g, w_ukv, mla_q_norm_g, mla_k_norm_g, fox_q_norm_g, fox_k_norm_g, fox_f_bias, s5_lambda_re, s5_lambda_im, s5_b_re, s5_b_im, s5_c_re, s5_c_im, s5_d, s5_log_step, s5_w_glu, s5_b_glu, w_branch, w_out, ffn_norm_g, w_up, ffn_conv_w, w_down, loss_target, m_attn_norm_g, m_w_in, m_q_lat_norm_g, m_w_uq, m_kv_lat_norm_g, m_w_ukv, m_mla_q_norm_g, m_mla_k_norm_g, m_fox_q_norm_g, m_fox_k_norm_g, m_fox_f_bias, m_s5_lambda_re, m_s5_lambda_im, m_s5_b_re, m_s5_b_im, m_s5_c_re, m_s5_c_im, m_s5_d, m_s5_log_step, m_s5_w_glu, m_s5_b_glu, m_w_branch, m_w_out, m_ffn_norm_g, m_w_up, m_ffn_conv_w, m_w_down, v_attn_norm_g, v_w_in, v_q_lat_norm_g, v_w_uq, v_kv_lat_norm_g, v_w_ukv, v_mla_q_norm_g, v_mla_k_norm_g, v_fox_q_norm_g, v_fox_k_norm_g, v_fox_f_bias, v_s5_lambda_re, v_s5_lambda_im, v_s5_b_re, v_s5_b_im, v_s5_c_re, v_s5_c_im, v_s5_d, v_s5_log_step, v_s5_w_glu, v_s5_b_glu, v_w_branch, v_w_out, v_ffn_norm_g, v_w_up, v_ffn_conv_w, v_w_down):
    given = dict(locals())
    w = {nm: given[nm] for nm in WEIGHTS}
    m = {nm: given["m_" + nm] for nm in WEIGHTS}
    v = {nm: given["v_" + nm] for nm in WEIGHTS}
    D = x.shape[-1]

    minor = [nm in MINOR for nm in SHARDED]
    shards = [shard3(nm, w[nm]).astype(F32 if nm in F32_TRAVEL else BF16) for nm in SHARDED]
    gathered = gather_weights(shards, minor)
    full = dict(w)
    for nm, g4 in zip(SHARDED, gathered):
        tail = list(w[nm].shape[1:])
        axis = (len(tail) - 1) if nm in MINOR or nm == "w_in" else 0
        tail[axis] *= N_CHIPS
        full[nm] = from_full4(nm, g4, tail)

    sq, grad_x, gw = local_step(x, positions, loss_target, full)

    big = reduce_scatter_grads(gw)
    total, sq_sum = unpack_small(allreduce_small(pack_small(gw, sq[0:1, 0:1], REDUCE_SMALL)), gw, REDUCE_SMALL)
    loss = 0.5 * sq_sum / D
    conv_cols = w["ffn_conv_w"].shape[-1]
    chip = 2 * lax.axis_index("x") + lax.axis_index("y")
    big["ffn_conv_w"] = lax.dynamic_slice_in_dim(total.pop("ffn_conv_w"), chip * conv_cols, conv_cols, axis=2)

    grads, delta, new_m, new_v = {}, {}, {}, {}
    for nm in SHARDED:
        g = big[nm].reshape(shard3(nm, w[nm]).shape)
        two = lambda a: shard3(nm, a).reshape(-1, g.shape[-1])
        d2, m2, v2, g2 = adamw("adamw_" + nm, two(w[nm]), g.reshape(-1, g.shape[-1]), two(m[nm]), two(v[nm]))
        grads[nm], delta[nm], new_m[nm], new_v[nm] = (a.reshape(w[nm].shape) for a in (g2, d2, m2, v2))
    d2, m2, v2, _ = adamw("adamw_small", pack_small(w), pack_small(total), pack_small(m), pack_small(v))
    for tree, packed in ((delta, d2), (new_m, m2), (new_v, v2)):
        tree.update(unpack_small(packed, w)[0])
    grads.update(total)
    return (loss, grad_x, *[grads[nm] for nm in WEIGHTS], *[delta[nm] for nm in WEIGHTS],
            *[new_m[nm] for nm in WEIGHTS], *[new_v[nm] for nm in WEIGHTS])
```

```python
import functools
import math

import jax
import jax.numpy as jnp
from jax import lax
from jax.experimental import pallas as pl
from jax.experimental.pallas import tpu as pltpu

F32, BF16 = jnp.float32, jnp.bfloat16
NORM_EPS = 1e-6
NEG_INF = -1e30
ROPE_THETA = 10000.0
LANE = 128
N_HEADS = 4
NOPE, ROPE, QK_DIM, V_DIM = 64, 32, 96, 64
Q_RANK, KV_RANK = 384, 256
FOX_DIM = 64
S5_G, S5_H, S5_P = 16, 16, 64
S5_C = S5_G * S5_P
BW = 256
VMEM_LIMIT = 56 << 20
ADAM_LR, ADAM_B1, ADAM_B2, ADAM_EPS, ADAM_WD, ADAM_STEP = 0.001, 0.9, 0.999, 1e-08, 0.01, 10


def _pick(n, cands):
    for c in cands:
        if n % c == 0:
            return c
    return n


def _cp(*sem):
    return pltpu.CompilerParams(dimension_semantics=sem, vmem_limit_bytes=VMEM_LIMIT)


def _dg(a, b, ca, cb):
    return lax.dot_general(a.astype(BF16), b.astype(BF16), (((ca,), (cb,)), ((), ())),
                           preferred_element_type=F32)


@jax.custom_vjp
def dot_nn(a, b):
    return _dg(a, b, 1, 0)


@jax.custom_vjp
def dot_nt(a, b):
    return _dg(a, b, 1, 1)


@jax.custom_vjp
def dot_tn(a, b):
    return _dg(a, b, 0, 0)


dot_nn.defvjp(lambda a, b: (dot_nn(a, b), (a, b)),
              lambda r, g: (dot_nt(g, r[1]).astype(r[0].dtype), dot_tn(r[0], g).astype(r[1].dtype)))
dot_nt.defvjp(lambda a, b: (dot_nt(a, b), (a, b)),
              lambda r, g: (dot_nn(g, r[1]).astype(r[0].dtype), dot_tn(g, r[0]).astype(r[1].dtype)))
dot_tn.defvjp(lambda a, b: (dot_tn(a, b), (a, b)),
              lambda r, g: (dot_nt(r[1], g).astype(r[0].dtype), dot_nn(r[0], g).astype(r[1].dtype)))


def xdot(a, b):
    return jnp.dot(a, b, precision=lax.Precision.HIGHEST, preferred_element_type=F32)


MM_VMEM_BUDGET = 36 << 20
MM_STEP_S, MM_HBM_BPS, MM_VMEM_BPS = 0.4e-6, 2.5e12, 3e12


def _divisors(n, cands):
    return sorted({c for c in cands if n % c == 0} | {n}, reverse=True)


def _mm_tiles(M, K, N, ab, bb, ob, addb):
    best = None
    for tm in _divisors(M, (2048, 1024, 512, 256, 128)):
        for tn in _divisors(N, (2048, 1664, 1536, 1408, 1280, 1024, 768, 640, 512, 384, 256, 128)):
            for tk in _divisors(K, (4096, 2048, 1664, 1536, 1408, 1024, 768, 512, 384, 256, 128)):
                vmem = 2 * (tm * tk * ab + tk * tn * bb + tm * tn * (ob + addb)) + (tm * tn * 4 if tk != K else 0)
                if vmem > MM_VMEM_BUDGET:
                    continue
                nk = K // tk
                steps = (M // tm) * (N // tn) * nk
                traffic = M * K * ab * (N // tn) + K * N * bb * (M // tm) + M * N * (ob + addb)
                cost = steps * MM_STEP_S + traffic / MM_HBM_BPS + (M * N * 8 * nk / MM_VMEM_BPS if nk > 1 else 0)
                if best is None or cost < best[0]:
                    best = (cost, tm, tn, tk)
    assert best is not None, (M, K, N)
    return best[1:]


def mm(a, b, mode, *, name, add=None, out_dtype=F32, layer=None, slot=None):
    bk, bn = b.shape[-2:]
    if mode == "nn":
        (M, K), N = a.shape, bn
    else:
        (K, M), N = a.shape, bn
    assert bk == K, (name, a.shape, b.shape)
    isz = lambda x: jnp.dtype(x.dtype).itemsize
    tm, tn, tk = _mm_tiles(M, K, N, isz(a), isz(b), jnp.dtype(out_dtype).itemsize, 0 if add is None else isz(add))
    nk = K // tk
    ca = 1 if mode == "nn" else 0

    n_in = 2 + (add is not None) + (slot is not None and slot[2] is not None)

    def body(*refs):
        a_ref, b_ref = refs[:2]
        add_ref = refs[2] if add is not None else None
        o_ref = refs[n_in]

        def finish(r):
            if add is not None:
                r = r + add_ref[...].astype(F32)
            o_ref[...] = r.astype(out_dtype)

        part = _dg(a_ref[...], b_ref[...], ca, 0)
        if nk == 1:
            finish(part)
            return
        acc = refs[-1]
        kk = pl.program_id(2)

        @pl.when(kk == 0)
        def _():
            acc[...] = part

        @pl.when(kk > 0)
        def _():
            acc[...] += part

        @pl.when(kk == nk - 1)
        def _():
            finish(acc[...])

    a_spec = (pl.BlockSpec((tm, tk), lambda i, j, k: (i, k)) if mode == "nn"
              else pl.BlockSpec((tk, tm), lambda i, j, k: (k, i)))
    b_spec = (pl.BlockSpec((tk, tn), lambda i, j, k: (k, j)) if layer is None
              else pl.BlockSpec((None, tk, tn), lambda i, j, k: (layer, k, j)))
    in_specs, args = [a_spec, b_spec], [a, b]
    if add is not None:
        in_specs.append(pl.BlockSpec((tm, tn), lambda i, j, k: (i, j)))
        args.append(add)
    out_spec = pl.BlockSpec((tm, tn), lambda i, j, k: (i, j))
    out_shape = jax.ShapeDtypeStruct((M, N), out_dtype)
    aliases = {}
    if slot is not None:
        n_layers, l, buf = slot
        out_spec = pl.BlockSpec((None, tm, tn), lambda i, j, k: (l, i, j))
        out_shape = jax.ShapeDtypeStruct((n_layers, M, N), out_dtype)
        if buf is not None:
            aliases = {len(args): 0}
            in_specs.append(pl.BlockSpec(memory_space=pl.ANY))
            args.append(buf)
    return pl.pallas_call(
        body, name=name, grid=(M // tm, N // tn, nk),
        in_specs=in_specs, out_specs=out_spec, out_shape=out_shape, input_output_aliases=aliases,
        scratch_shapes=[pltpu.VMEM((tm, tn), F32)] if nk > 1 else [],
        compiler_params=_cp("parallel", "parallel", "arbitrary"),
    )(*args)


def _row_spec(tm, width, col):
    if callable(col):
        return pl.BlockSpec((tm, width), lambda i, j: (i, col(j)))
    return pl.BlockSpec((tm, width), lambda i, j: (i, col))


def _const_spec(c):
    return pl.BlockSpec(c.shape, lambda i, j: (0,) * c.ndim)


ROW_BLOCK_BYTES = 6 << 20


def _row_tile(M, widths, tm):
    if tm is None:
        tm = next((t for t in (1024, 512, 256) if t * sum(widths) * 4 <= ROW_BLOCK_BYTES), 128)
    return _pick(M, (tm, 256, 128, 64, 32, 16, 8))


def rowwise(name, fn, rows, consts, outs, *, tm=None, nj=1):
    M = rows[0][0].shape[0]
    tm = _row_tile(M, [w for _, w, _ in rows] + [w for w, _ in outs], tm)
    nr, nc = len(rows), len(consts)

    def body(*refs):
        vals = [r[...].astype(F32) for r in refs[:nr]] + [r[...] for r in refs[nr:nr + nc]]
        res = fn(*vals)
        for o_ref, r in zip(refs[nr + nc:], res):
            o_ref[...] = r.astype(o_ref.dtype)

    return pl.pallas_call(
        body, name=name, grid=(M // tm, nj),
        in_specs=[_row_spec(tm, w, c) for _, w, c in rows] + [_const_spec(c) for c in consts],
        out_specs=[pl.BlockSpec((tm, w), lambda i, j: (i, j)) for w, _ in outs],
        out_shape=[jax.ShapeDtypeStruct((M, nj * w), dt) for w, dt in outs],
        compiler_params=_cp("parallel", "parallel"),
    )(*[r[0] for r in rows], *consts)


def rowwise_vjp(name, fn, rows, consts, cts, diff, *, tm=None, nj=1, gdt=None):
    M = rows[0][0].shape[0]
    tm = _row_tile(M, [w for _, w, _ in rows] + [w for _, w, _ in cts] + [rows[p][1] for p in diff], tm)
    nr, nc, nt, nd = len(rows), len(consts), len(cts), len(diff)
    gdt = [F32] * nd if gdt is None else gdt

    def body(*refs):
        vals = [r[...].astype(F32) for r in refs[:nr + nc + nt]]
        rv, cv, tv = vals[:nr], vals[nr:nr + nc], vals[nr + nc:]
        grow, gconst = refs[nr + nc + nt:nr + nc + nt + nd], refs[nr + nc + nt + nd:]

        def f(*dargs):
            full = list(rv)
            for pos, val in zip(diff, dargs[:nd]):
                full[pos] = val
            return tuple(fn(*full, *dargs[nd:]))

        _, vjp = jax.vjp(f, *[rv[p] for p in diff], *cv)
        g = vjp(tuple(tv))
        for o_ref, gv in zip(grow, g[:nd]):
            o_ref[...] = gv.astype(o_ref.dtype)
        first = jnp.logical_and(pl.program_id(0) == 0, pl.program_id(1) == 0)
        for o_ref, gv in zip(gconst, g[nd:]):
            @pl.when(first)
            def _(o_ref=o_ref, gv=gv):
                o_ref[...] = gv

            @pl.when(jnp.logical_not(first))
            def _(o_ref=o_ref, gv=gv):
                o_ref[...] += gv

    out_specs = ([pl.BlockSpec((tm, rows[p][1]), lambda i, j: (i, j)) for p in diff]
                 + [_const_spec(c) for c in consts])
    out_shape = ([jax.ShapeDtypeStruct((M, nj * rows[p][1]), dt) for p, dt in zip(diff, gdt)]
                 + [jax.ShapeDtypeStruct(c.shape, F32) for c in consts])
    res = pl.pallas_call(
        body, name=name, grid=(M // tm, nj),
        in_specs=([_row_spec(tm, w, c) for _, w, c in rows] + [_const_spec(c) for c in consts]
                  + [_row_spec(tm, w, c) for _, w, c in cts]),
        out_specs=out_specs, out_shape=out_shape,
        compiler_params=_cp("arbitrary", "arbitrary"),
    )(*[r[0] for r in rows], *consts, *[t[0] for t in cts])
    return res[:nd], res[nd:]


def _lane(shape=(1, LANE)):
    return lax.broadcasted_iota(jnp.int32, shape, len(shape) - 1)


def _sigmoid(x):
    return 1.0 / (1.0 + jnp.exp(-x))


def _rms(x, g, n):
    return x * lax.rsqrt(jnp.sum(x * x, axis=-1, keepdims=True) * (1.0 / n) + NORM_EPS) * g


def fn_rms(n):
    return lambda x, g: (_rms(x, g, n),)


def _rope(x, cos_t, sin_t):
    i = lax.broadcasted_iota(jnp.int32, (LANE, LANE), 0)
    j = lax.broadcasted_iota(jnp.int32, (LANE, LANE), 1)
    half = ROPE // 2
    lo = jnp.logical_and(jnp.logical_and(j >= NOPE, j < NOPE + half), i == j + half)
    hi = jnp.logical_and(jnp.logical_and(j >= NOPE + half, j < NOPE + ROPE), i == j - half)
    perm = jnp.where(hi, 1.0, 0.0) - jnp.where(lo, 1.0, 0.0)
    return x * cos_t + xdot(x, perm) * sin_t


def fn_qpost(q, cos_t, sin_t, g):
    return (_rope(_rms(q, g, QK_DIM), cos_t, sin_t),)


def fn_kpost(kn, small, cos_t, sin_t, g):
    lane = _lane()
    rope_lanes = jnp.logical_and(lane >= NOPE, lane < NOPE + ROPE)
    kc = kn + jnp.where(rope_lanes, small, 0.0)
    return (_rope(_rms(kc, g, QK_DIM), cos_t, sin_t),)


def fn_foxnorm(x, g):
    first = _lane() < FOX_DIM
    sq = x * x
    s0 = jnp.sum(jnp.where(first, sq, 0.0), axis=-1, keepdims=True)
    s1 = jnp.sum(jnp.where(first, 0.0, sq), axis=-1, keepdims=True)
    r0 = lax.rsqrt(s0 * (1.0 / FOX_DIM) + NORM_EPS)
    r1 = lax.rsqrt(s1 * (1.0 / FOX_DIM) + NORM_EPS)
    return (x * jnp.where(first, r0, r1) * g,)


F_LANE0 = NOPE + ROPE


def fn_fgate(small, bias):
    z = small + bias
    lf = jnp.minimum(z, 0.0) - jnp.log(1.0 + jnp.exp(-jnp.abs(z)))
    lane = _lane()
    return (jnp.where(jnp.logical_and(lane >= F_LANE0, lane < F_LANE0 + N_HEADS), lf, 0.0),)


def _gelu(y):
    return 0.5 * y * (1.0 + jnp.tanh(math.sqrt(2.0 / math.pi) * (y + 0.044715 * (y * y * y))))


def fn_s5post(ypre, u, d, wglu, bglu):
    y = _gelu(ypre + d * u)
    return (y * _sigmoid(dot_nn(y, wglu) + bglu),)


def fn_merge(om, of, os_, g0, g1, g2, wb0, wb1, wb2):
    return (_sigmoid(g0) * dot_nn(om, wb0) + _sigmoid(g1) * dot_nn(of, wb1)
            + _sigmoid(g2) * dot_nn(os_, wb2),)


def fn_add5(a, b):
    return (a[:, 0:LANE] + a[:, LANE:2 * LANE] + a[:, 2 * LANE:3 * LANE] + a[:, 3 * LANE:4 * LANE] + b,)


def fn_addt(a, b):
    return (a + b,)


def fn_s5params(lre, lim, lstep, btr, bti, ctr, cti):
    C = S5_C
    grp = lax.broadcasted_iota(jnp.int32, (LANE, C), 1) >> 6
    expand = jnp.where(lax.broadcasted_iota(jnp.int32, (LANE, C), 0) == grp, 1.0, 0.0)
    lane = _lane()
    st = jnp.where(lane < S5_G, jnp.exp(lstep), 0.0)
    step = jnp.sum(xdot(jnp.broadcast_to(st, (8, LANE)), expand), axis=0, keepdims=True) * 0.125
    zr, zi = lre * step, lim * step
    er = jnp.exp(zr)
    lbr, lbi = er * jnp.cos(zi), er * jnp.sin(zi)
    den = lre * lre + lim * lim
    nr = lbr - 1.0
    cfr = (nr * lre + lbi * lim) / den
    cfi = (lbi * lre - nr * lim) / den
    bbr = cfr * btr - cfi * bti
    bbi = cfr * bti + cfi * btr
    rg = lax.broadcasted_iota(jnp.int32, (BW, C), 0) >> 4
    cg = lax.broadcasted_iota(jnp.int32, (BW, C), 1) >> 6
    mb = jnp.where(rg == cg, 1.0, 0.0)
    b_re = jnp.concatenate([bbr] * S5_G, axis=0) * mb
    b_im = jnp.concatenate([bbi] * S5_G, axis=0) * mb
    ecol = jnp.where(lax.broadcasted_iota(jnp.int32, (LANE, BW), 0)
                     == (lax.broadcasted_iota(jnp.int32, (LANE, BW), 1) & 15), 1.0, 0.0)
    mc = jnp.where((lax.broadcasted_iota(jnp.int32, (C, BW), 0) >> 6)
                   == (lax.broadcasted_iota(jnp.int32, (C, BW), 1) >> 4), 1.0, 0.0)
    c_top = xdot(ctr, ecol) * mc
    c_bot = -(xdot(cti, ecol) * mc)
    return lbr, lbi, b_re, b_im, c_top, c_bot


def s5_params(p):
    def body(lre, lim, ls, btr, bti, ctr, cti, lb_ref, bre_ref, bim_ref, ct_ref, cb_ref):
        lbr, lbi, b_re, b_im, c_top, c_bot = fn_s5params(
            lre[...], lim[...], ls[...], btr[...], bti[...], ctr[...], cti[...])
        lb_ref[0:1, :] = lbr
        lb_ref[1:2, :] = lbi
        bre_ref[...] = b_re.astype(BF16)
        bim_ref[...] = b_im.astype(BF16)
        ct_ref[...] = c_top.astype(BF16)
        cb_ref[...] = c_bot.astype(BF16)

    return pl.pallas_call(
        body, name="s5_params",
        out_shape=[jax.ShapeDtypeStruct((2, S5_C), F32), jax.ShapeDtypeStruct((BW, S5_C), BF16),
                   jax.ShapeDtypeStruct((BW, S5_C), BF16), jax.ShapeDtypeStruct((S5_C, BW), BF16),
                   jax.ShapeDtypeStruct((S5_C, BW), BF16)],
        compiler_params=pltpu.CompilerParams(vmem_limit_bytes=VMEM_LIMIT),
    )(p["lre"], p["lim"], p["lstep"], p["btr"], p["bti"], p["ctr"], p["cti"])


S5_PARAM_NAMES = ("lre", "lim", "lstep", "btr", "bti", "ctr", "cti")


def s5_params_vjp(p, dl_r, dl_i, db_re, db_im, dc_top, dc_bot):
    def body(lre, lim, ls, btr, bti, ctr, cti, dlr, dli, dbr, dbi, dct, dcb, *outs):
        args = [r[...] for r in (lre, lim, ls, btr, bti, ctr, cti)]
        _, vjp = jax.vjp(fn_s5params, *args)
        g = vjp((jnp.sum(dlr[...], axis=0, keepdims=True), jnp.sum(dli[...], axis=0, keepdims=True),
                 dbr[...], dbi[...], dct[...], dcb[...]))
        for o_ref, gv in zip(outs, g):
            o_ref[...] = gv

    return pl.pallas_call(
        body, name="s5_params_vjp",
        out_shape=[jax.ShapeDtypeStruct(p[n].shape, F32) for n in S5_PARAM_NAMES],
        compiler_params=pltpu.CompilerParams(vmem_limit_bytes=VMEM_LIMIT),
    )(*[p[n] for n in S5_PARAM_NAMES], dl_r, dl_i, db_re, db_im, dc_top, dc_bot)


def _attn_tile(q, k, v, cq, ckt, *, hpt, dk, q0, tile):
    tq, S = q.shape[0], k.shape[0]
    row = q0 + lax.broadcasted_iota(jnp.int32, (tq, S), 0)
    col = lax.broadcasted_iota(jnp.int32, (tq, S), 1)
    causal = row >= col
    lane = _lane()
    out = jnp.zeros((tq, LANE), F32)
    for h in range(hpt):
        if hpt > 1:
            mine = (lane >> int(math.log2(LANE // hpt))) == h
            qh = jnp.where(mine, q, 0.0)
        else:
            qh = q
        s = dot_nt(qh, k) * (dk ** -0.5)
        if cq is not None:
            head = tile * hpt + h
            cqh = jnp.sum(jnp.where(lane == F_LANE0 + head, cq, 0.0), axis=1, keepdims=True)
            sub = lax.broadcasted_iota(jnp.int32, (8, 1), 0)
            ckh = jnp.sum(jnp.where(sub == head, ckt, 0.0), axis=0, keepdims=True)
            s = s + (cqh - ckh)
        s = jnp.where(causal, s, NEG_INF)
        m = lax.stop_gradient(jnp.max(s, axis=-1, keepdims=True))
        e = jnp.exp(s - m)
        p = e / jnp.sum(e, axis=-1, keepdims=True)
        oh = dot_nn(p, v)
        out = out + (jnp.where(mine, oh, 0.0) if hpt > 1 else oh)
    return out


def attention(name, q, k, v, *, B, S, ntile, hpt, dk, qc=0, kc=0, vc=0, cum=None, ckt=None, do=None, tq=256):
    tq = _pick(S, (tq, 128))
    nq = S // tq
    M = B * S
    bias = cum is not None
    kw = dict(hpt=hpt, dk=dk)

    def load(refs, sk):
        q_ref, k_ref, v_ref = refs[:3]
        qv, kv, vv = q_ref[...].astype(F32), k_ref[0:sk, :].astype(F32), v_ref[0:sk, :].astype(F32)
        if bias:
            return qv, kv, vv, refs[3][...], refs[4][0, :, 0:sk]
        return qv, kv, vv, None, None

    nin = 5 if bias else 3

    def per_query_block(run):
        for g in range(nq):
            @pl.when(pl.program_id(2) == g)
            def _(g=g):
                run(g, (g + 1) * tq)

    def fwd_body(*refs):
        tile = pl.program_id(1)

        def run(g, sk):
            qv, kv, vv, cq, ck = load(refs, sk)
            o = _attn_tile(qv, kv, vv, cq, ck, q0=g * tq, tile=tile, **kw)
            refs[nin][...] = o.astype(refs[nin].dtype)

        per_query_block(run)

    def bwd_body(*refs):
        outs = refs[nin + 1:]
        tile = pl.program_id(1)

        @pl.when(pl.program_id(2) == 0)
        def _():
            for o_ref in (outs[1], outs[2]) + ((outs[4],) if bias else ()):
                o_ref[...] = jnp.zeros_like(o_ref)

        def run(g, sk):
            qv, kv, vv, cq, ck = load(refs, sk)
            dov = refs[nin][...].astype(F32)
            if bias:
                f = lambda a, b, c, d, e: _attn_tile(a, b, c, d, e, q0=g * tq, tile=tile, **kw)
                _, vjp = jax.vjp(f, qv, kv, vv, cq, ck)
            else:
                f = lambda a, b, c: _attn_tile(a, b, c, None, None, q0=g * tq, tile=tile, **kw)
                _, vjp = jax.vjp(f, qv, kv, vv)
            gr = vjp(dov)
            outs[0][...] = gr[0]
            outs[1][0:sk, :] += gr[1]
            outs[2][0:sk, :] += gr[2]
            if bias:
                outs[3][...] = gr[3]
                outs[4][0, :, 0:sk] += gr[4]

        per_query_block(run)

    qspec = lambda c: pl.BlockSpec((tq, LANE), lambda b, t, i: (b * nq + i, c + t))
    kspec = lambda c: pl.BlockSpec((S, LANE), lambda b, t, i: (b, c + t))
    in_specs, args = [qspec(qc), kspec(kc), kspec(vc)], [q, k, v]
    if bias:
        in_specs += [pl.BlockSpec((tq, LANE), lambda b, t, i: (b * nq + i, 0)),
                     pl.BlockSpec((1, 8, S), lambda b, t, i: (b, 0, 0))]
        args += [cum, ckt]
    if do is None:
        return pl.pallas_call(
            fwd_body, name=name, grid=(B, ntile, nq), in_specs=in_specs, out_specs=qspec(0),
            out_shape=jax.ShapeDtypeStruct((M, ntile * LANE), BF16),
            compiler_params=_cp("parallel", "parallel", "parallel"),
        )(*args)
    in_specs.append(qspec(0))
    args.append(do)
    out_specs = [qspec(0), kspec(0), kspec(0)]
    out_shape = [jax.ShapeDtypeStruct((M, ntile * LANE), F32)] * 3
    if bias:
        out_specs += [qspec(0), pl.BlockSpec((1, 8, S), lambda b, t, i: (b * ntile + t, 0, 0))]
        out_shape += [jax.ShapeDtypeStruct((M, ntile * LANE), F32),
                      jax.ShapeDtypeStruct((B * ntile, 8, S), F32)]
    return pl.pallas_call(
        bwd_body, name=name, grid=(B, ntile, nq), in_specs=in_specs, out_specs=out_specs,
        out_shape=out_shape, compiler_params=_cp("parallel", "parallel", "arbitrary"),
    )(*args)


def seq_cumsum(name, x, *, B, S, reverse):
    nb = S // LANE

    def body(x_ref, o_ref):
        r = lax.broadcasted_iota(jnp.int32, (LANE, LANE), 0)
        c = lax.broadcasted_iota(jnp.int32, (LANE, LANE), 1)
        tri = jnp.where((r <= c) if reverse else (r >= c), 1.0, 0.0)
        carry = jnp.zeros((1, LANE), F32)
        for blk in (range(nb - 1, -1, -1) if reverse else range(nb)):
            xb = x_ref[blk * LANE:(blk + 1) * LANE, :]
            o_ref[blk * LANE:(blk + 1) * LANE, :] = xdot(tri, xb) + carry
            carry = carry + jnp.sum(xb, axis=0, keepdims=True)

    return pl.pallas_call(
        body, name=name, grid=(B,), in_specs=[pl.BlockSpec((S, LANE), lambda b: (b, 0))],
        out_specs=pl.BlockSpec((S, LANE), lambda b: (b, 0)),
        out_shape=jax.ShapeDtypeStruct(x.shape, F32), compiler_params=_cp("parallel"),
    )(x)


SCAN_ROWS = 64


def _shift_rows(ref, r0, rows, d, up):
    if d % 8 == 0:
        return ref[pl.ds(r0 + d if up else r0 - d, rows), :]
    pad = -(-d // 8) * 8
    if up:
        win = ref[pl.ds(r0, rows + pad), :]
        return pltpu.roll(win, rows + pad - d, 0)[0:rows, :]
    win = ref[pl.ds(r0 - pad, rows + pad), :]
    return pltpu.roll(win, d, 0)[pad:rows + pad, :]


def _cmul(ar, ai, br, bi):
    return ar * br - ai * bi, ar * bi + ai * br


def s5_scan(name, x_re, x_im, lam, *, B, S, reverse, state=None):
    C = S5_C
    cw = LANE
    R = _pick(S, (SCAN_ROWS,))
    pad = max(S // 2, 8)
    log2s = int(math.log2(S))
    assert 1 << log2s == S
    base = 0 if reverse else pad
    with_grad = state is not None

    def body(*refs):
        if with_grad:
            xr, xi, lam_ref, sr, si, o_r, o_i, dl_r, dl_i, a_r, a_i, b_r, b_i = refs
        else:
            xr, xi, lam_ref, o_r, o_i, a_r, a_i, b_r, b_i = refs
        zero = jnp.zeros((pad, cw), F32)
        z0 = S if reverse else 0
        for buf in (a_r, a_i, b_r, b_i):
            buf[z0:z0 + pad, :] = zero
        a_r[base:base + S, :] = xr[...]
        a_i[base:base + S, :] = xi[...]
        mr = lam_ref[0:1, :]
        mi = -lam_ref[1:2, :] if reverse else lam_ref[1:2, :]
        src, dst = (a_r, a_i), (b_r, b_i)
        for step in range(log2s):
            d = 1 << step
            last = step == log2s - 1

            def chunk(c, _, src=src, dst=dst, d=d, last=last, mr=mr, mi=mi):
                r0 = pl.multiple_of(base + c * R, 8)
                xr_ = _shift_rows(src[0], r0, R, d, reverse)
                xi_ = _shift_rows(src[1], r0, R, d, reverse)
                nr = src[0][pl.ds(r0, R), :] + mr * xr_ - mi * xi_
                ni = src[1][pl.ds(r0, R), :] + mr * xi_ + mi * xr_
                if last:
                    o0 = pl.multiple_of(c * R, 8)
                    o_r[pl.ds(o0, R), :] = nr
                    o_i[pl.ds(o0, R), :] = ni
                else:
                    dst[0][pl.ds(r0, R), :] = nr
                    dst[1][pl.ds(r0, R), :] = ni
                return 0

            lax.fori_loop(0, S // R, chunk, 0)
            mr, mi = _cmul(mr, mi, mr, mi)
            src, dst = dst, src
        if with_grad:
            def fold(v):
                return jnp.sum(v.reshape(R // 8, 8, cw), axis=0)

            def accum(c, carry, first=False):
                r0 = 0 if first else pl.multiple_of(c * R, 8)
                gr, gi = o_r[pl.ds(r0, R), :], o_i[pl.ds(r0, R), :]
                if first:
                    keep = lax.broadcasted_iota(jnp.int32, (R, 1), 0) >= 1
                    pr = jnp.where(keep, pltpu.roll(sr[0:R, :], 1, 0), 0.0)
                    pi = jnp.where(keep, pltpu.roll(si[0:R, :], 1, 0), 0.0)
                else:
                    pr = _shift_rows(sr, r0, R, 1, False)
                    pi = _shift_rows(si, r0, R, 1, False)
                return (carry[0] + fold(gr * pr + gi * pi), carry[1] + fold(gi * pr - gr * pi))

            acc = accum(0, (jnp.zeros((8, cw), F32), jnp.zeros((8, cw), F32)), first=True)
            acc = lax.fori_loop(1, S // R, accum, acc)
            dl_r[...] = acc[0]
            dl_i[...] = acc[1]

    seq = pl.BlockSpec((S, cw), lambda b, j: (b, j))
    in_specs = [seq, seq, pl.BlockSpec((2, cw), lambda b, j: (0, j))]
    args = [x_re, x_im, lam]
    out_specs = [seq, seq]
    out_shape = [jax.ShapeDtypeStruct(x_re.shape, F32)] * 2
    if with_grad:
        in_specs += [seq, seq]
        args += list(state)
        out_specs += [pl.BlockSpec((8, cw), lambda b, j: (b, j))] * 2
        out_shape += [jax.ShapeDtypeStruct((B * 8, C), F32)] * 2
    return pl.pallas_call(
        body, name=name, grid=(B, C // cw), in_specs=in_specs, out_specs=out_specs, out_shape=out_shape,
        scratch_shapes=[pltpu.VMEM((S + pad, cw), F32)] * 4,
        compiler_params=_cp("parallel", "parallel"),
    )(*args)


CONV_CW = 256


def _conv_taps(ref, r0, rows, first):
    cur = ref[pl.ds(r0, rows), :]
    if first:
        row = lax.broadcasted_iota(jnp.int32, (rows, 1), 0)
        p1 = jnp.where(row >= 1, pltpu.roll(cur, 1, 0), 0.0)
        p2 = jnp.where(row >= 2, pltpu.roll(cur, 2, 0), 0.0)
    else:
        p1 = _shift_rows(ref, r0, rows, 1, False)
        p2 = _shift_rows(ref, r0, rows, 2, False)
    return cur, p1, p2


def _conv_apply(w_ref, taps):
    return w_ref[2:3, :] * taps[0] + w_ref[1:2, :] * taps[1] + w_ref[0:1, :] * taps[2]


def conv_gate_fwd(up, conv_w, *, B, S):
    M, F2 = up.shape
    F = F2 // 2
    cw = _pick(F, (CONV_CW, LANE))
    nf = F // cw
    R = _pick(S, (SCAN_ROWS,))

    def body(g_ref, v_ref, wg_ref, wv_ref, o_ref):
        def chunk(c, _, first=False):
            r0 = 0 if first else pl.multiple_of(c * R, 8)
            cg = _conv_apply(wg_ref, _conv_taps(g_ref, r0, R, first))
            cv = _conv_apply(wv_ref, _conv_taps(v_ref, r0, R, first))
            o_ref[pl.ds(r0, R), :] = (cg * _sigmoid(cg) * cv).astype(o_ref.dtype)
            return 0

        chunk(0, 0, first=True)
        lax.fori_loop(1, S // R, chunk, 0)

    seq = lambda off: pl.BlockSpec((S, cw), lambda b, j: (b, off + j))
    wsp = lambda off: pl.BlockSpec((3, cw), lambda b, j: (0, off + j))
    return pl.pallas_call(
        body, name="conv_gate", grid=(B, nf), in_specs=[seq(0), seq(nf), wsp(0), wsp(nf)],
        out_specs=seq(0), out_shape=jax.ShapeDtypeStruct((M, F), BF16),
        compiler_params=_cp("parallel", "parallel"),
    )(up, up, conv_w, conv_w)


def conv_gate_bwd(up, conv_w, dact, *, B, S):
    M, F2 = up.shape
    F = F2 // 2
    cw = _pick(F, (CONV_CW, LANE))
    nf = F // cw
    R = _pick(S, (SCAN_ROWS,))
    nchunk = S // R

    def body(s_ref, p_ref, ws_ref, wp_ref, da_ref, du_ref, dw_ref, dc_ref, silu_ref):
        is_gate = pl.program_id(1) == 0
        ex = pl.program_id(2)
        dc_ref[S:S + 8, :] = jnp.zeros((8, cw), F32)

        def fold(v):
            return jnp.sum(v.reshape(R // 8, 8, cw), axis=0)

        first_b = ex == 0

        def pass1(gate_step):
            def chunk(c, acc, first=False):
                r0 = 0 if first else pl.multiple_of(c * R, 8)
                taps = _conv_taps(s_ref, r0, R, first)
                da = da_ref[pl.ds(r0, R), :]
                if gate_step:
                    cp = _conv_apply(wp_ref, _conv_taps(p_ref, r0, R, first))
                    cs = _conv_apply(ws_ref, taps)
                    sg = _sigmoid(cs)
                    silu_ref[ex, pl.ds(r0, R), :] = cs * sg
                    dc = da * cp * (sg * (1.0 + cs * (1.0 - sg)))
                else:
                    dc = da * silu_ref[ex, pl.ds(r0, R), :]
                dc_ref[pl.ds(r0, R), :] = dc
                return (acc[0] + fold(dc * taps[2]), acc[1] + fold(dc * taps[1]), acc[2] + fold(dc * taps[0]))

            z = jnp.zeros((8, cw), F32)
            acc = chunk(0, (z, z, z), first=True)
            acc = lax.fori_loop(1, nchunk, chunk, acc)
            for tap in range(3):
                tot = jnp.sum(acc[tap], axis=0, keepdims=True)

                @pl.when(first_b)
                def _(tap=tap, tot=tot):
                    dw_ref[tap:tap + 1, :] = tot

                @pl.when(jnp.logical_not(first_b))
                def _(tap=tap, tot=tot):
                    dw_ref[tap:tap + 1, :] += tot

        @pl.when(is_gate)
        def _():
            pass1(True)

        @pl.when(jnp.logical_not(is_gate))
        def _():
            pass1(False)

        def pass2(c, _):
            r0 = pl.multiple_of(c * R, 16)
            n0 = dc_ref[pl.ds(r0, R), :]
            n1 = _shift_rows(dc_ref, r0, R, 1, True)
            n2 = _shift_rows(dc_ref, r0, R, 2, True)
            du = ws_ref[2:3, :] * n0 + ws_ref[1:2, :] * n1 + ws_ref[0:1, :] * n2
            du_ref[pl.ds(r0, R), :] = du.astype(du_ref.dtype)
            return 0

        lax.fori_loop(0, nchunk, pass2, 0)

    seq = lambda f: pl.BlockSpec((S, cw), lambda j, t, b: (b, f(j, t)))
    wsp = lambda f: pl.BlockSpec((3, cw), lambda j, t, b: (0, f(j, t)))
    same, value, act_col = (lambda j, t: j + t * nf), (lambda j, t: j + nf), (lambda j, t: j)
    return pl.pallas_call(
        body, name="conv_gate_vjp", grid=(nf, 2, B),
        in_specs=[seq(same), seq(value), wsp(same), wsp(value), seq(act_col)],
        out_specs=[seq(same), wsp(same)],
        out_shape=[jax.ShapeDtypeStruct((M, F2), BF16), jax.ShapeDtypeStruct((3, F2), F32)],
        scratch_shapes=[pltpu.VMEM((S + 8, cw), F32), pltpu.VMEM((B, S, cw), F32)],
        compiler_params=_cp("parallel", "arbitrary", "arbitrary"),
    )(up, up, conv_w, conv_w, dact)


def loss_head(y, target):
    M, D = y.shape
    tm = _pick(M, (256, 128, 64, 32, 16, 8))

    def body(y_ref, t_ref, dy_ref, l_ref):
        diff = y_ref[...] - t_ref[...]
        dy_ref[...] = diff * (1.0 / D)
        part = jnp.sum(jnp.sum(diff * diff, axis=1, keepdims=True), axis=0, keepdims=True)

        @pl.when(pl.program_id(0) == 0)
        def _():
            l_ref[...] = jnp.zeros_like(l_ref)

        l_ref[...] += part

    row = pl.BlockSpec((tm, D), lambda i: (i, 0))
    return pl.pallas_call(
        body, name="loss_head", grid=(M // tm,), in_specs=[row, row],
        out_specs=[row, pl.BlockSpec((8, LANE), lambda i: (0, 0))],
        out_shape=[jax.ShapeDtypeStruct((M, D), F32), jax.ShapeDtypeStruct((8, LANE), F32)],
        compiler_params=_cp("arbitrary"),
    )(y, target)


def adamw(name, w, g, m, v):
    R, C = w.shape
    tr = _pick(R, (256, 128, 64, 32, 16, 8))

    def body(w_ref, g_ref, m_ref, v_ref, d_ref, nm_ref, nv_ref, go_ref):
        gv = g_ref[...]
        go_ref[...] = gv
        nm = ADAM_B1 * m_ref[...] + (1.0 - ADAM_B1) * gv
        nv = ADAM_B2 * v_ref[...] + (1.0 - ADAM_B2) * (gv * gv)
        m_hat = nm / (1.0 - ADAM_B1 ** ADAM_STEP)
        v_hat = nv / (1.0 - ADAM_B2 ** ADAM_STEP)
        d_ref[...] = -ADAM_LR * (m_hat / (jnp.sqrt(v_hat) + ADAM_EPS) + ADAM_WD * w_ref[...])
        nm_ref[...] = nm
        nv_ref[...] = nv

    blk = pl.BlockSpec((tr, C), lambda i: (i, 0))
    return pl.pallas_call(
        body, name=name, grid=(R // tr,), in_specs=[blk] * 4, out_specs=[blk] * 4,
        out_shape=[jax.ShapeDtypeStruct((R, C), F32)] * 4, compiler_params=_cp("parallel"),
    )(w, g, m, v)


def _seg(D):
    o = 3 * D
    return dict(ckv=o, fq=o + 256, fk=o + 512, fv=o + 768, u=o + 1024, small=o + 1280, cq=o + 1536, P=o + 1920)


def _pad_last(a, n):
    return jnp.pad(a, [(0, 0)] * (a.ndim - 1) + [(0, n - a.shape[-1])])


def _place(a, lo, n=LANE):
    return jnp.pad(a, [(0, 0)] * (a.ndim - 1) + [(lo, n - lo - a.shape[-1])])


def _in_segments(D):
    s = _seg(D)
    sm = s["small"]
    return ((0, 384, s["cq"]), (384, 640, s["ckv"]), (640, 672, sm + NOPE), (672, 1440, s["fq"]),
            (1440, 1444, sm + F_LANE0), (1444, 1700, s["u"]), (1700, 1700 + 3 * D, 0))


def _chip_pieces(win4, lo, hi):
    cw = win4.shape[-1]
    out = []
    for k in range(N_CHIPS):
        a, b = max(lo, k * cw), min(hi, (k + 1) * cw)
        if a < b:
            out.append(win4[:, k, :, a - k * cw:b - k * cw])
    return out


def prep_weights(w):
    L, D = w["attn_norm_g"].shape
    win = w["w_in"]
    z = lambda n: [jnp.zeros((L, D, n), win.dtype)]
    cols = lambda lo, hi: _chip_pieces(win, lo, hi)
    g0 = 1700
    wp = jnp.concatenate(
        cols(g0, g0 + 3 * D) + cols(384, 640) + cols(672, 1440) + cols(1444, 1700) + z(NOPE) + cols(640, 672)
        + cols(1440, 1444) + z(LANE - F_LANE0 - N_HEADS) + z(LANE) + cols(0, 384), axis=-1)
    wukv, wb = w["w_ukv"], w["w_branch"]
    row3 = lambda a: a[:, None, :]
    return dict(
        g1=row3(w["attn_norm_g"]), Wp=wp, gql=row3(w["q_lat_norm_g"]), gkvl=row3(w["kv_lat_norm_g"]),
        Wuq=_pad_last(w["w_uq"], LANE).reshape(L, Q_RANK, N_HEADS * LANE),
        Wk=_pad_last(wukv[..., :NOPE], LANE).reshape(L, KV_RANK, N_HEADS * LANE),
        Wv=_pad_last(wukv[..., NOPE:], LANE).reshape(L, KV_RANK, N_HEADS * LANE),
        gq=row3(_pad_last(w["mla_q_norm_g"], LANE)), gk=row3(_pad_last(w["mla_k_norm_g"], LANE)),
        gfq=row3(jnp.tile(w["fox_q_norm_g"], (1, 2))), gfk=row3(jnp.tile(w["fox_k_norm_g"], (1, 2))),
        fbias=row3(_place(w["fox_f_bias"], F_LANE0)),
        lre=w["s5_lambda_re"].reshape(L, 1, S5_C), lim=w["s5_lambda_im"].reshape(L, 1, S5_C),
        lstep=row3(_pad_last(w["s5_log_step"], LANE)),
        btr=jnp.transpose(w["s5_b_re"], (0, 3, 1, 2)).reshape(L, S5_H, S5_C),
        bti=jnp.transpose(w["s5_b_im"], (0, 3, 1, 2)).reshape(L, S5_H, S5_C),
        ctr=_pad_last(jnp.transpose(w["s5_c_re"], (0, 1, 3, 2)).reshape(L, S5_C, S5_H), LANE),
        cti=_pad_last(jnp.transpose(w["s5_c_im"], (0, 1, 3, 2)).reshape(L, S5_C, S5_H), LANE),
        s5d=w["s5_d"].reshape(L, 1, BW), Wglu=w["s5_w_glu"], bglu=row3(w["s5_b_glu"]),
        Wb0=jnp.pad(wb[:, 0].reshape(L, N_HEADS, V_DIM, D), ((0, 0), (0, 0), (0, LANE - V_DIM), (0, 0))
                    ).reshape(L, N_HEADS * LANE, D),
        Wb1=wb[:, 1], Wb2=wb[:, 2], Wout=w["w_out"], g2=row3(w["ffn_norm_g"]), Wup=w["w_up"],
        convw=w["ffn_conv_w"], Wdown=w["w_down"],
    )


BIG_KEYS = ("Wp", "Wuq", "Wk", "Wv", "Wout", "Wup", "Wdown")


def with_transposes(P):
    out = dict(P)
    for k in BIG_KEYS:
        out[k + "T"] = jnp.swapaxes(P[k], 1, 2)
    return out


def layer_params(P, l):
    return {k: (v if k in BIG_KEYS or k[:-1] in BIG_KEYS else v[l]) for k, v in P.items()}


def unprep_grads(G, D):
    L = G["g1"].shape[0]
    dwp = G["Wp"]
    segs = _in_segments(D)
    cw = segs[-1][1] // N_CHIPS
    chips = []
    for k in range(N_CHIPS):
        pieces = []
        for lo, hi, at in segs:
            a, b = max(lo, k * cw), min(hi, (k + 1) * cw)
            if a < b:
                pieces.append(dwp[..., at + a - lo:at + b - lo])
        chips.append(jnp.concatenate(pieces, axis=-1))
    w_in = jnp.stack(chips, axis=1)
    heads = lambda a, rows, keep: a.reshape(L, rows, N_HEADS, LANE)[..., :keep]
    wb0 = G["Wb0"].reshape(L, N_HEADS, LANE, D)[:, :, :V_DIM].reshape(L, BW, D)
    gf = lambda a: a[:, 0, :FOX_DIM] + a[:, 0, FOX_DIM:]
    return dict(
        attn_norm_g=G["g1"][:, 0], w_in=w_in, q_lat_norm_g=G["gql"][:, 0], w_uq=heads(G["Wuq"], Q_RANK, QK_DIM),
        kv_lat_norm_g=G["gkvl"][:, 0],
        w_ukv=jnp.concatenate([heads(G["Wk"], KV_RANK, NOPE), heads(G["Wv"], KV_RANK, V_DIM)], axis=-1),
        mla_q_norm_g=G["gq"][:, 0, :QK_DIM], mla_k_norm_g=G["gk"][:, 0, :QK_DIM],
        fox_q_norm_g=gf(G["gfq"]), fox_k_norm_g=gf(G["gfk"]),
        fox_f_bias=G["fbias"][:, 0, F_LANE0:F_LANE0 + N_HEADS],
        s5_lambda_re=G["lre"].reshape(L, S5_G, S5_P), s5_lambda_im=G["lim"].reshape(L, S5_G, S5_P),
        s5_b_re=jnp.transpose(G["btr"].reshape(L, S5_H, S5_G, S5_P), (0, 2, 3, 1)),
        s5_b_im=jnp.transpose(G["bti"].reshape(L, S5_H, S5_G, S5_P), (0, 2, 3, 1)),
        s5_c_re=jnp.transpose(G["ctr"][..., :S5_H].reshape(L, S5_G, S5_P, S5_H), (0, 1, 3, 2)),
        s5_c_im=jnp.transpose(G["cti"][..., :S5_H].reshape(L, S5_G, S5_P, S5_H), (0, 1, 3, 2)),
        s5_d=G["s5d"].reshape(L, S5_G, S5_H), s5_log_step=G["lstep"][:, 0, :S5_G],
        s5_w_glu=G["Wglu"], s5_b_glu=G["bglu"][:, 0],
        w_branch=jnp.stack([wb0, G["Wb1"], G["Wb2"]], axis=1), w_out=G["Wout"], ffn_norm_g=G["g2"][:, 0],
        w_up=G["Wup"], ffn_conv_w=G["convw"], w_down=G["Wdown"],
    )


def rope_tables(positions):
    inv_freq = ROPE_THETA ** (-jnp.arange(0, ROPE, 2, dtype=F32) / ROPE)
    ang = positions.astype(F32)[..., None] * inv_freq
    cos, sin = jnp.cos(ang), jnp.sin(ang)
    ones = jnp.ones(ang.shape[:-1] + (NOPE,), F32)
    zeros = jnp.zeros(ang.shape[:-1] + (LANE - NOPE - ROPE,), F32)
    cos_t = jnp.concatenate([ones, cos, cos, zeros], axis=-1)
    sin_t = jnp.concatenate([0.0 * ones, sin, sin, zeros], axis=-1)
    return cos_t.reshape(-1, LANE), sin_t.reshape(-1, LANE)


def fn_rms_res(n):
    return lambda x, g: (_rms(x, g, n), x)


def _s5_mats(p):
    return s5_params({k: p[k] for k in S5_PARAM_NAMES})


def layer_fwd(x, p, l, cos_t, sin_t, B, S):
    M, D = x.shape
    s = _seg(D)
    sm = s["small"] // LANE
    head = lambda j: j
    h = rowwise("rms_attn", fn_rms(D), [(x, D, 0)], [p["g1"]], [(D, BF16)])[0]
    proj = mm(h, p["Wp"], "nn", name="in_proj", layer=l)
    cnq = rowwise("latq_norm", fn_rms(Q_RANK), [(proj, Q_RANK, s["cq"] // Q_RANK)], [p["gql"]], [(Q_RANK, BF16)])[0]
    cnkv = rowwise("latkv_norm", fn_rms(KV_RANK), [(proj, KV_RANK, s["ckv"] // KV_RANK)], [p["gkvl"]],
                   [(KV_RANK, BF16)])[0]
    qraw = mm(cnq, p["Wuq"], "nn", name="q_up", layer=l)
    kn = mm(cnkv, p["Wk"], "nn", name="k_up", layer=l)
    v5 = mm(cnkv, p["Wv"], "nn", name="v_up", out_dtype=BF16, layer=l)
    qrot = rowwise("q_post", fn_qpost, [(qraw, LANE, head), (cos_t, LANE, 0), (sin_t, LANE, 0)], [p["gq"]],
                   [(LANE, BF16)], nj=N_HEADS)[0]
    krot = rowwise("k_post", fn_kpost, [(kn, LANE, head), (proj, LANE, sm), (cos_t, LANE, 0), (sin_t, LANE, 0)],
                   [p["gk"]], [(LANE, BF16)], nj=N_HEADS)[0]
    omla = attention("mla_attn", qrot, krot, v5, B=B, S=S, ntile=N_HEADS, hpt=1, dk=QK_DIM)
    fq0, fk0 = s["fq"] // LANE, s["fk"] // LANE
    qf = rowwise("foxq_norm", fn_foxnorm, [(proj, LANE, lambda j: fq0 + j)], [p["gfq"]], [(LANE, BF16)], nj=2)[0]
    kf = rowwise("foxk_norm", fn_foxnorm, [(proj, LANE, lambda j: fk0 + j)], [p["gfk"]], [(LANE, BF16)], nj=2)[0]
    lf = rowwise("fgate", fn_fgate, [(proj, LANE, sm)], [p["fbias"]], [(LANE, F32)])[0]
    cum = seq_cumsum("fox_cumsum", lf, B=B, S=S, reverse=False)
    ckt = _pad_rows8(jnp.transpose(cum.reshape(B, S, LANE)[:, :, F_LANE0:F_LANE0 + N_HEADS], (0, 2, 1)))
    ofox = attention("fox_attn", qf, kf, proj, B=B, S=S, ntile=2, hpt=2, dk=FOX_DIM, vc=s["fv"] // LANE,
                     cum=cum, ckt=ckt)
    lam, bre, bim, ctop, cbot = _s5_mats(p)
    u16 = proj[:, s["u"]:s["u"] + BW].astype(BF16)
    bur = mm(u16, bre, "nn", name="s5_bu_re")
    bui = mm(u16, bim, "nn", name="s5_bu_im")
    sr, si = s5_scan("s5_scan", bur, bui, lam, B=B, S=S, reverse=False)
    ypre = mm(si, cbot, "nn", name="s5_y_im", add=mm(sr, ctop, "nn", name="s5_y_re"))
    os5 = rowwise("s5_post", fn_s5post, [(ypre, BW, 0), (proj, BW, s["u"] // BW)],
                  [p["s5d"], p["Wglu"], p["bglu"]], [(BW, BF16)])[0]
    merged = rowwise("merge", fn_merge,
                     [(omla, N_HEADS * LANE, 0), (ofox, BW, 0), (os5, BW, 0), (proj, D, 0), (proj, D, 1), (proj, D, 2)],
                     [p["Wb0"], p["Wb1"], p["Wb2"]], [(D, BF16)])[0]
    xmid = mm(merged, p["Wout"], "nn", name="out_proj", add=x, layer=l)
    h2 = rowwise("rms_ffn", fn_rms(D), [(xmid, D, 0)], [p["g2"]], [(D, BF16)])[0]
    up = mm(h2, p["Wup"], "nn", name="ffn_up", layer=l)
    act = conv_gate_fwd(up, p["convw"], B=B, S=S)
    xout = mm(act, p["Wdown"], "nn", name="ffn_down", add=xmid, layer=l)
    saved = dict(x=x, h=h, proj=proj, cnq=cnq, cnkv=cnkv, qraw=qraw, kn=kn, v5=v5, qrot=qrot, krot=krot, qf=qf,
                 kf=kf, cum=cum, ckt=ckt, omla=omla, ofox=ofox, os5=os5, u16=u16, sr=sr, si=si, ypre=ypre,
                 merged=merged, xmid=xmid, h2=h2, up=up, act=act)
    return xout, saved


def _pad_rows8(a):
    return jnp.pad(a, ((0, 0), (0, 8 - a.shape[1]), (0, 0)))


STACKED = ("Wp", "Wout", "Wup", "Wdown")


def layer_bwd(dx, p, l, sv, cos_t, sin_t, B, S, stacks):
    M, D = dx.shape
    slot = lambda k: (stacks["L"], l, stacks.get(k))
    s = _seg(D)
    sm = s["small"] // LANE
    head = lambda j: j
    proj = sv["proj"]
    dact = mm(dx, p["WdownT"], "nn", name="ffn_down_dx", layer=l)
    d_wdown = mm(sv["act"], dx, "tn", name="ffn_down_dw", slot=slot("Wdown"))
    dup, d_convw = conv_gate_bwd(sv["up"], p["convw"], dact, B=B, S=S)
    dh2 = mm(dup, p["WupT"], "nn", name="ffn_up_dx", layer=l)
    d_wup = mm(sv["h2"], dup, "tn", name="ffn_up_dw", slot=slot("Wup"))
    (dxmid,), (d_g2,) = rowwise_vjp("rms_ffn_vjp", fn_rms_res(D), [(sv["xmid"], D, 0)], [p["g2"]],
                                    [(dh2, D, 0), (dx, D, 0)], [0])
    dmerged = mm(dxmid, p["WoutT"], "nn", name="out_proj_dx", layer=l)
    d_wout = mm(sv["merged"], dxmid, "tn", name="out_proj_dw", slot=slot("Wout"))
    (dom, dof, dos, dg0, dg1, dg2), (d_wb0, d_wb1, d_wb2) = rowwise_vjp(
        "merge_vjp", fn_merge,
        [(sv["omla"], N_HEADS * LANE, 0), (sv["ofox"], BW, 0), (sv["os5"], BW, 0), (proj, D, 0), (proj, D, 1),
         (proj, D, 2)], [p["Wb0"], p["Wb1"], p["Wb2"]], [(dmerged, D, 0)], [0, 1, 2, 3, 4, 5], tm=256,
        gdt=[BF16, BF16, F32, BF16, BF16, BF16])
    lam, bre, bim, ctop, cbot = _s5_mats(p)
    (dypre, du_a), (d_s5d, d_wglu, d_bglu) = rowwise_vjp(
        "s5_post_vjp", fn_s5post, [(sv["ypre"], BW, 0), (proj, BW, s["u"] // BW)],
        [p["s5d"], p["Wglu"], p["bglu"]], [(dos, BW, 0)], [0, 1], gdt=[BF16, F32])
    dsr = mm(dypre, ctop.T, "nn", name="s5_y_re_dx")
    dsi = mm(dypre, cbot.T, "nn", name="s5_y_im_dx")
    d_ctop = mm(sv["sr"], dypre, "tn", name="s5_y_re_dw")
    d_cbot = mm(sv["si"], dypre, "tn", name="s5_y_im_dw")
    gr, gi, dl_r, dl_i = s5_scan("s5_scan_vjp", dsr, dsi, lam, B=B, S=S, reverse=True, state=(sv["sr"], sv["si"]))
    du = mm(gi, bim.T, "nn", name="s5_bu_im_dx", out_dtype=BF16,
            add=mm(gr, bre.T, "nn", name="s5_bu_re_dx", add=du_a))
    d_bre = mm(sv["u16"], gr, "tn", name="s5_bu_re_dw")
    d_bim = mm(sv["u16"], gi, "tn", name="s5_bu_im_dw")
    d_s5 = s5_params_vjp({k: p[k] for k in S5_PARAM_NAMES}, dl_r, dl_i, d_bre, d_bim, d_ctop, d_cbot)
    dqf, dkf, dfv, dcq_t, dckt_t = attention("fox_attn_vjp", sv["qf"], sv["kf"], proj, B=B, S=S, ntile=2, hpt=2,
                                             dk=FOX_DIM, vc=s["fv"] // LANE, cum=sv["cum"], ckt=sv["ckt"], do=dof)
    dck = dckt_t.reshape(B, 2, 8, S)
    dck = jnp.transpose(dck[:, 0, :N_HEADS] + dck[:, 1, :N_HEADS], (0, 2, 1)).reshape(M, N_HEADS)
    dcum = rowwise("fox_dcum", lambda a, b: (a[:, 0:LANE] + a[:, LANE:2 * LANE] + b,),
                   [(dcq_t, 2 * LANE, 0), (_place(dck, F_LANE0), LANE, 0)], [], [(LANE, F32)])[0]
    dlf = seq_cumsum("fox_cumsum_vjp", dcum, B=B, S=S, reverse=True)
    (dsmall_f,), (d_fbias,) = rowwise_vjp("fgate_vjp", fn_fgate, [(proj, LANE, sm)], [p["fbias"]],
                                          [(dlf, LANE, 0)], [0])
    fq0, fk0 = s["fq"] // LANE, s["fk"] // LANE
    (dfq,), (d_gfq,) = rowwise_vjp("foxq_norm_vjp", fn_foxnorm, [(proj, LANE, lambda j: fq0 + j)], [p["gfq"]],
                                   [(dqf, LANE, head)], [0], nj=2, gdt=[BF16])
    (dfk,), (d_gfk,) = rowwise_vjp("foxk_norm_vjp", fn_foxnorm, [(proj, LANE, lambda j: fk0 + j)], [p["gfk"]],
                                   [(dkf, LANE, head)], [0], nj=2, gdt=[BF16])
    dqrot, dkrot, dv5 = attention("mla_attn_vjp", sv["qrot"], sv["krot"], sv["v5"], B=B, S=S, ntile=N_HEADS,
                                  hpt=1, dk=QK_DIM, do=dom)
    (dqraw,), (d_gq,) = rowwise_vjp("q_post_vjp", fn_qpost,
                                    [(sv["qraw"], LANE, head), (cos_t, LANE, 0), (sin_t, LANE, 0)], [p["gq"]],
                                    [(dqrot, LANE, head)], [0], nj=N_HEADS, gdt=[BF16])
    (dkn, dsmall_k), (d_gk,) = rowwise_vjp(
        "k_post_vjp", fn_kpost, [(sv["kn"], LANE, head), (proj, LANE, sm), (cos_t, LANE, 0), (sin_t, LANE, 0)],
        [p["gk"]], [(dkrot, LANE, head)], [0, 1], nj=N_HEADS, gdt=[BF16, F32])
    dsmall = rowwise("small_sum", fn_add5, [(dsmall_k, N_HEADS * LANE, 0), (dsmall_f, LANE, 0)], [], [(LANE, BF16)])[0]
    dcnq = mm(dqraw, p["WuqT"], "nn", name="q_up_dx", layer=l)
    d_wuq = mm(sv["cnq"], dqraw, "tn", name="q_up_dw")
    dcnkv = mm(dv5, p["WvT"], "nn", name="v_up_dx", layer=l, add=mm(dkn, p["WkT"], "nn", name="k_up_dx", layer=l))
    d_wk = mm(sv["cnkv"], dkn, "tn", name="k_up_dw")
    d_wv = mm(sv["cnkv"], dv5, "tn", name="v_up_dw")
    (dcq,), (d_gql,) = rowwise_vjp("latq_norm_vjp", fn_rms(Q_RANK), [(proj, Q_RANK, s["cq"] // Q_RANK)], [p["gql"]],
                                   [(dcnq, Q_RANK, 0)], [0], gdt=[BF16])
    (dckv,), (d_gkvl,) = rowwise_vjp("latkv_norm_vjp", fn_rms(KV_RANK), [(proj, KV_RANK, s["ckv"] // KV_RANK)],
                                     [p["gkvl"]], [(dcnkv, KV_RANK, 0)], [0], gdt=[BF16])
    dproj = jnp.concatenate([dg0, dg1, dg2, dckv, dfq, dfk, dfv.astype(BF16), du, dsmall,
                             jnp.zeros((M, LANE), BF16), dcq], axis=1)
    dh = mm(dproj, p["WpT"], "nn", name="in_proj_dx", layer=l)
    d_wp = mm(sv["h"], dproj, "tn", name="in_proj_dw", slot=slot("Wp"))
    (dxin,), (d_g1,) = rowwise_vjp("rms_attn_vjp", fn_rms_res(D), [(sv["x"], D, 0)], [p["g1"]],
                                   [(dh, D, 0), (dxmid, D, 0)], [0])
    grads = dict(g1=d_g1, Wp=d_wp, gql=d_gql, gkvl=d_gkvl, Wuq=d_wuq, Wk=d_wk, Wv=d_wv, gq=d_gq, gk=d_gk,
                 gfq=d_gfq, gfk=d_gfk, fbias=d_fbias, s5d=d_s5d, Wglu=d_wglu, bglu=d_bglu, Wb0=d_wb0, Wb1=d_wb1,
                 Wb2=d_wb2, Wout=d_wout, g2=d_g2, Wup=d_wup, convw=d_convw, Wdown=d_wdown)
    grads.update(dict(zip(S5_PARAM_NAMES, d_s5)))
    return dxin, grads


def local_step(x, positions, target, w):
    B, S, D = x.shape
    M = B * S
    P = with_transposes(prep_weights(w))
    L = P["g1"].shape[0]
    cos_t, sin_t = rope_tables(positions)
    xc, saved = x.reshape(M, D), []
    for l in range(L):
        xc, sv = layer_fwd(xc, layer_params(P, l), l, cos_t, sin_t, B, S)
        saved.append(sv)
    dxc, sq = loss_head(xc, target.reshape(M, D))
    grads, stacks = [None] * L, {"L": L}
    for l in reversed(range(L)):
        dxc, grads[l] = layer_bwd(dxc, layer_params(P, l), l, saved[l], cos_t, sin_t, B, S, stacks)
        stacks.update({k: grads[l][k] for k in STACKED})
    G = {k: (stacks[k] if k in STACKED else jnp.stack([g[k] for g in grads])) for k in grads[0]}
    return sq, dxc.reshape(B, S, D), unprep_grads(G, D)


MESH = pl.DeviceIdType.MESH
ANY = pl.BlockSpec(memory_space=pl.ANY)
N_CHIPS = 4
SHARDED = ("w_in", "w_uq", "w_ukv", "s5_w_glu", "w_branch", "w_out", "w_up", "ffn_conv_w", "w_down")
MINOR = ("w_branch", "w_up", "ffn_conv_w")
F32_TRAVEL = ("ffn_conv_w",)
WEIGHTS = ("attn_norm_g", "w_in", "q_lat_norm_g", "w_uq", "kv_lat_norm_g", "w_ukv", "mla_q_norm_g", "mla_k_norm_g",
           "fox_q_norm_g", "fox_k_norm_g", "fox_f_bias", "s5_lambda_re", "s5_lambda_im", "s5_b_re", "s5_b_im",
           "s5_c_re", "s5_c_im", "s5_d", "s5_log_step", "s5_w_glu", "s5_b_glu", "w_branch", "w_out", "ffn_norm_g",
           "w_up", "ffn_conv_w", "w_down")
SMALL = tuple(n for n in WEIGHTS if n not in SHARDED)
RS_BIG = tuple(n for n in SHARDED if n != "ffn_conv_w")
REDUCE_SMALL = SMALL + ("ffn_conv_w",)


def shard3(name, a):
    L = a.shape[0]
    if name in ("w_uq", "w_ukv"):
        return a.reshape(L, a.shape[1], -1)
    if name == "w_branch":
        return a.reshape(L, -1, a.shape[-1])
    return a


def full4(name, a):
    L = a.shape[0]
    if name == "w_in":
        return a
    if name in MINOR:
        return a.reshape(L, 1, -1, a.shape[-1])
    a = a.reshape(L, a.shape[1], -1)
    return a.reshape(L, N_CHIPS, a.shape[1] // N_CHIPS, a.shape[2])


def from_full4(name, a, ref_tail):
    L = a.shape[0]
    if name == "w_in":
        return a
    return a.reshape((L,) + tuple(ref_tail))


def _where():
    x, y, c = lax.axis_index("x"), lax.axis_index("y"), lax.axis_index("c")
    chips = [(1 - x, y), (x, 1 - y), (1 - x, 1 - y)]
    return (x, y, c), 2 * x + y, (x, y, 1 - c), chips, [2 * cx + cy for cx, cy in chips]


def _view(minor, ref4, layers, k):
    if minor:
        cs = ref4.shape[3] // N_CHIPS
        return ref4.at[layers, 0, :, pl.ds(pl.multiple_of(k * cs, LANE), cs)]
    return ref4.at[layers, k]


def _remote(src, dst, ssem, rsem, dev):
    return pltpu.make_async_remote_copy(src_ref=src, dst_ref=dst, send_sem=ssem, recv_sem=rsem,
                                        device_id=dev, device_id_type=MESH)


def gather_weights(shards, minor):
    n = len(shards)
    L = shards[0].shape[0]
    Lh = L // 2
    out_shape = []
    for a, mn in zip(shards, minor):
        _, r, cs = a.shape
        out_shape.append(jax.ShapeDtypeStruct((L, 1, r, N_CHIPS * cs) if mn else (L, N_CHIPS, r, cs), a.dtype))

    def body(*refs):
        w, g = refs[:n], refs[n:2 * n]
        send, recv = refs[2 * n:]
        (x, y, c), me, sib, chips, cidx = _where()
        mine, other, every = pl.ds(c * Lh, Lh), pl.ds((1 - c) * Lh, Lh), pl.ds(0, L)
        dst = lambda i, layers, k: _view(minor[i], g[i], layers, k)
        local = [_remote(w[i], dst(i, every, me), send.at[i, 6], recv.at[i, 6], sib) for i in range(n)]
        first = [_remote(w[i].at[mine], dst(i, mine, me), send.at[i, j], recv.at[i, j], (*chips[j], c))
                 for i in range(n) for j in range(3)]
        for cp in local + first:
            cp.start()
        passed = []
        for i in range(n):
            for j in range(3):
                blk = dst(i, mine, cidx[j])
                _remote(blk, blk, send.at[i, j], recv.at[i, j], (*chips[j], c)).wait_recv()
                fwd = _remote(blk, blk, send.at[i, 3 + j], recv.at[i, 3 + j], sib)
                fwd.start()
                passed.append(fwd)
        for i in range(n):
            for j in range(3):
                blk = dst(i, other, cidx[j])
                _remote(blk, blk, send.at[i, 3 + j], recv.at[i, 3 + j], sib).wait_recv()
        for cp in first + passed:
            cp.wait_send()
        for cp in local:
            cp.wait()

    return pl.pallas_call(
        body, name="gather_weights", in_specs=[ANY] * n, out_specs=[ANY] * n, out_shape=out_shape,
        scratch_shapes=[pltpu.SemaphoreType.DMA((n, 7)), pltpu.SemaphoreType.DMA((n, 7))],
    )(*shards)


def sibling_halves(grads):
    n = len(grads)
    L = grads[0].shape[0]
    Lh = L // 2
    half = [jax.ShapeDtypeStruct((Lh,) + a.shape[1:], a.dtype) for a in grads]

    def body(*refs):
        g, got = refs[:n], refs[n:2 * n]
        send, recv = refs[2 * n:]
        (x, y, c), me, sib, chips, cidx = _where()
        other = pl.ds((1 - c) * Lh, Lh)
        out = [_remote(g[i].at[other], got[i], send.at[i], recv.at[i], sib) for i in range(n)]
        for cp in out:
            cp.start()
        for cp in out:
            cp.wait()

    got = pl.pallas_call(
        body, name="grad_sibling_halves", in_specs=[ANY] * n, out_specs=[ANY] * n, out_shape=half,
        scratch_shapes=[pltpu.SemaphoreType.DMA((n,)), pltpu.SemaphoreType.DMA((n,))],
    )(*grads)
    return got


def scatter_chip_sums(travel, minor):
    n = len(travel)
    Lh = travel[0].shape[0]
    got_shape = []
    for t, mn in zip(travel, minor):
        r, cs = t.shape[2], (t.shape[3] // N_CHIPS if mn else t.shape[3])
        got_shape.append(jax.ShapeDtypeStruct((3, Lh, r, cs), t.dtype))

    def body(*refs):
        s16, got = refs[:n], refs[n:2 * n]
        send, recv = refs[2 * n:]
        (x, y, c), me, sib, chips, cidx = _where()
        every = pl.ds(0, Lh)
        out = [_remote(_view(minor[i], s16[i], every, cidx[j]), got[i].at[j], send.at[i, j], recv.at[i, j],
                       (*chips[j], c)) for i in range(n) for j in range(3)]
        for cp in out:
            cp.start()
        for cp in out:
            cp.wait()

    got = pl.pallas_call(
        body, name="grad_scatter", in_specs=[ANY] * n, out_specs=[ANY] * n, out_shape=got_shape,
        scratch_shapes=[pltpu.SemaphoreType.DMA((n, 3)), pltpu.SemaphoreType.DMA((n, 3))],
    )(*travel)
    return got


def share_halves(shards):
    n = len(shards)
    Lh = shards[0].shape[0] // 2

    def body(*refs):
        full = refs[n:2 * n]
        send, recv = refs[2 * n:]
        (x, y, c), me, sib, chips, cidx = _where()
        mine, other = pl.ds(c * Lh, Lh), pl.ds((1 - c) * Lh, Lh)
        out = [_remote(full[i].at[mine], full[i].at[mine], send.at[i], recv.at[i], sib) for i in range(n)]
        for cp in out:
            cp.start()
        for i in range(n):
            out[i].wait_send()
            _remote(full[i].at[other], full[i].at[other], send.at[i], recv.at[i], sib).wait_recv()

    return pl.pallas_call(
        body, name="grad_share_halves", in_specs=[ANY] * n, out_specs=[ANY] * n,
        out_shape=[jax.ShapeDtypeStruct(a.shape, a.dtype) for a in shards],
        input_output_aliases={i: i for i in range(n)},
        scratch_shapes=[pltpu.SemaphoreType.DMA((n,)), pltpu.SemaphoreType.DMA((n,))],
    )(*shards)


N_DEV = 8


def allreduce_small(v):
    R = v.shape[0]

    def body(x_ref, sum_ref, all_ref, send, recv, loc):
        (x, y, c), me, sib, chips, cidx = _where()

        def rows(px, py, pc):
            return all_ref.at[4 * px + 2 * py + pc]

        def copy(k, block, to, src=None):
            return _remote(rows(*block) if src is None else src, rows(*block), send.at[k], recv.at[k], to)

        mine = pltpu.make_async_copy(x_ref, rows(x, y, c), loc)
        mine.start()
        first = [copy(0, (x, y, c), sib, src=x_ref)]
        first += [copy(1 + j, (x, y, c), (*chip, c), src=x_ref) for j, chip in enumerate(chips)]
        for cp in first:
            cp.start()
        passed = [copy(4 + j, (*chip, c), sib) for j, chip in enumerate(chips)]
        for j, chip in enumerate(chips):
            copy(1 + j, (*chip, c), (x, y, c)).wait_recv()
            passed[j].start()
        copy(0, (x, y, 1 - c), (x, y, c)).wait_recv()
        for j, chip in enumerate(chips):
            copy(4 + j, (*chip, 1 - c), (x, y, c)).wait_recv()
        for cp in first + passed:
            cp.wait_send()
        mine.wait()
        acc = all_ref[0]
        for d in range(1, N_DEV):
            acc = acc + all_ref[d]
        sum_ref[...] = acc

    vm = pl.BlockSpec(memory_space=pltpu.VMEM)
    return pl.pallas_call(
        body, name="allreduce_small", in_specs=[vm], out_specs=[vm, vm],
        out_shape=[jax.ShapeDtypeStruct((R, LANE), F32), jax.ShapeDtypeStruct((N_DEV, R, LANE), F32)],
        scratch_shapes=[pltpu.SemaphoreType.DMA((7,)), pltpu.SemaphoreType.DMA((7,)), pltpu.SemaphoreType.DMA],
        compiler_params=pltpu.CompilerParams(vmem_limit_bytes=VMEM_LIMIT),
    )(v)[0]


EW_BLOCK_BYTES = 2 << 20


def _ew_rows(rows, cols):
    for tr in (1024, 512, 256, 128, 64, 32, 16, 8):
        if rows % tr == 0 and tr * cols * 4 <= EW_BLOCK_BYTES:
            return tr
    return rows


def add_pair(name, full, got, where, travel_dtype):
    R, C = got.shape
    tr = _ew_rows(R, C)
    nblk = R // tr

    def body(w_ref, a_ref, b_ref, s_ref, t_ref):
        s = a_ref[...] + b_ref[...]
        s_ref[...] = s
        t_ref[...] = s.astype(t_ref.dtype)

    blk = pl.BlockSpec((tr, C), lambda i, w: (i, 0))
    return pl.pallas_call(
        body, name=name,
        grid_spec=pltpu.PrefetchScalarGridSpec(
            num_scalar_prefetch=1, grid=(nblk,),
            in_specs=[pl.BlockSpec((tr, C), lambda i, w: (w[1] * nblk + i, 0)), blk], out_specs=[blk, blk]),
        out_shape=[jax.ShapeDtypeStruct((R, C), F32), jax.ShapeDtypeStruct((R, C), travel_dtype)],
        compiler_params=_cp("parallel"),
    )(where, full, got)


def add_four(name, sums, got, where, minor):
    Lh, _, r, C = sums.shape
    cs = got.shape[-1]
    tr = _ew_rows(r, cs)

    def body(w_ref, m_ref, g0, g1, g2, o_ref):
        o_ref[...] = ((m_ref[...] + g0[...].astype(F32)) + g1[...].astype(F32)) + g2[...].astype(F32)

    mine = (pl.BlockSpec((None, None, tr, cs), lambda l, i, w: (l, 0, i, w[0])) if minor
            else pl.BlockSpec((None, None, tr, cs), lambda l, i, w: (l, w[0], i, 0)))
    slot = lambda j: pl.BlockSpec((None, None, tr, cs), lambda l, i, w: (j, l, i, 0))
    return pl.pallas_call(
        body, name=name,
        grid_spec=pltpu.PrefetchScalarGridSpec(
            num_scalar_prefetch=1, grid=(Lh, r // tr), in_specs=[mine, slot(0), slot(1), slot(2)],
            out_specs=pl.BlockSpec((None, tr, cs), lambda l, i, w: (w[1] * Lh + l, i, 0))),
        out_shape=jax.ShapeDtypeStruct((2 * Lh, r, cs), F32), compiler_params=_cp("parallel", "parallel"),
    )(where, sums, got, got, got)


def reduce_scatter_grads(full_grads):
    names = list(RS_BIG)
    minor = [nm in MINOR for nm in names]
    where = jnp.stack([2 * lax.axis_index("x") + lax.axis_index("y"), lax.axis_index("c")]).astype(jnp.int32)
    g4 = [full4(nm, full_grads[nm]) for nm in names]
    got = sibling_halves(g4)
    sums, travel = [], []
    for nm, a, b in zip(names, g4, got):
        s, t = add_pair("chip_sum_" + nm, a.reshape(-1, a.shape[-1]), b.reshape(-1, b.shape[-1]), where,
                        F32 if nm in F32_TRAVEL else BF16)
        sums.append(s.reshape(b.shape))
        travel.append(t.reshape(b.shape))
    arrived = scatter_chip_sums(travel, minor)
    shards = [add_four("shard_sum_" + nm, s, b, where, mn) for nm, s, b, mn in zip(names, sums, arrived, minor)]
    return dict(zip(names, share_halves(shards)))


def pack_small(tree, extra=None, names=SMALL):
    parts = [tree[nm].reshape(-1) for nm in names]
    parts.append(jnp.zeros((1,), F32) if extra is None else extra.reshape(-1))
    blocks = []
    for p in parts:
        rows = _small_rows(p.shape[0])
        blocks.append(jnp.pad(p, (0, rows * LANE - p.shape[0])).reshape(rows, LANE))
    return jnp.concatenate(blocks, axis=0)


def _small_rows(size):
    return -(-size // (8 * LANE)) * 8


def unpack_small(packed, like, names=SMALL):
    out, at = {}, 0
    for nm in names:
        size = math.prod(like[nm].shape)
        rows = _small_rows(size)
        out[nm] = packed[at:at + rows].reshape(-1)[:size].reshape(like[nm].shape)
        at += rows
    return out, packed[at, 0]


def kernel(x, positions, attn_norm_g, w_in, q_lat_norm_g, w_uq, kv_lat_norm_g, w_ukv, mla_q_norm_g, mla_k_norm_g, fox_q_norm_g, fox_k_norm_g, fox_f_bias, s5_lambda_re, s5_lambda_im, s5_b_re, s5_b_im, s5_c_re, s5_c_im, s5_d, s5_log_step, s5_w_glu, s5_b_glu, w_branch, w_out, ffn_norm_g, w_up, ffn_conv_w, w_down, loss_target, m_attn_norm_g, m_w_in, m_q_lat_norm_g, m_w_uq, m_kv_lat_norm_g, m_w_ukv, m_mla_q_norm_g, m_mla_k_norm_g, m_fox_q_norm_g, m_fox_k_norm_g, m_fox_f_bias, m_s5_lambda_re, m_s5_lambda_im, m_s5_b_re, m_s5_b_im, m_s5_c_re, m_s5_c_im, m_s5_d, m_s5_log_step, m_s5_w_glu, m_s5_b_glu, m_w_branch, m_w_out, m_ffn_norm_g, m_w_up, m_ffn_conv_w, m_w_down, v_attn_norm_g, v_w_in, v_q_lat_norm_g, v_w_uq, v_kv_lat_norm_g, v_w_ukv, v_mla_q_norm_g, v_mla_k_norm_g, v_fox_q_norm_g, v_fox_k_norm_g, v_fox_f_bias, v_s5_lambda_re, v_s5_lambda_im, v_s5_b_re, v_s5_b_im, v_s5_c_re, v_s5_c_im, v_s5_d, v_s5_log_step, v_s5_w_glu, v_s5_b_glu, v_w_branch, v_w_out, v_ffn_norm_g, v_w_up, v_ffn_conv_w, v_w_down):
    given = dict(locals())
    w = {nm: given[nm] for nm in WEIGHTS}
    m = {nm: given["m_" + nm] for nm in WEIGHTS}
    v = {nm: given["v_" + nm] for nm in WEIGHTS}
    D = x.shape[-1]

    minor = [nm in MINOR for nm in SHARDED]
    shards = [shard3(nm, w[nm]).astype(F32 if nm in F32_TRAVEL else BF16) for nm in SHARDED]
    gathered = gather_weights(shards, minor)
    full = dict(w)
    for nm, g4 in zip(SHARDED, gathered):
        tail = list(w[nm].shape[1:])
        axis = (len(tail) - 1) if nm in MINOR or nm == "w_in" else 0
        tail[axis] *= N_CHIPS
        full[nm] = from_full4(nm, g4, tail)

    sq, grad_x, gw = local_step(x, positions, loss_target, full)

    big = reduce_scatter_grads(gw)
    total, sq_sum = unpack_small(allreduce_small(pack_small(gw, sq[0:1, 0:1], REDUCE_SMALL)), gw, REDUCE_SMALL)
    loss = 0.5 * sq_sum / D
    conv_cols = w["ffn_conv_w"].shape[-1]
    chip = 2 * lax.axis_index("x") + lax.axis_index("y")
    big["ffn_conv_w"] = lax.dynamic_slice_in_dim(total.pop("ffn_conv_w"), chip * conv_cols, conv_cols, axis=2)

    grads, delta, new_m, new_v = {}, {}, {}, {}
    for nm in SHARDED:
        g = big[nm].reshape(shard3(nm, w[nm]).shape)
        two = lambda a: shard3(nm, a).reshape(-1, g.shape[-1])
        d2, m2, v2, g2 = adamw("adamw_" + nm, two(w[nm]), g.reshape(-1, g.shape[-1]), two(m[nm]), two(v[nm]))
        grads[nm], delta[nm], new_m[nm], new_v[nm] = (a.reshape(w[nm].shape) for a in (g2, d2, m2, v2))
    d2, m2, v2, _ = adamw("adamw_small", pack_small(w), pack_small(total), pack_small(m), pack_small(v))
    for tree, packed in ((delta, d2), (new_m, m2), (new_v, v2)):
        tree.update(unpack_small(packed, w)[0])
    grads.update(total)
    return (loss, grad_x, *[grads[nm] for nm in WEIGHTS], *[delta[nm] for nm in WEIGHTS],
            *[new_m[nm] for nm in WEIGHTS], *[new_v[nm] for nm in WEIGHTS])
```

```python
import functools
import math

import jax
import jax.numpy as jnp
from jax import lax
from jax.experimental import pallas as pl
from jax.experimental.pallas import tpu as pltpu

F32, BF16 = jnp.float32, jnp.bfloat16
NORM_EPS = 1e-6
NEG_INF = -1e30
ROPE_THETA = 10000.0
LANE = 128
N_HEADS = 4
NOPE, ROPE, QK_DIM, V_DIM = 64, 32, 96, 64
Q_RANK, KV_RANK = 384, 256
FOX_DIM = 64
S5_G, S5_H, S5_P = 16, 16, 64
S5_C = S5_G * S5_P
BW = 256
VMEM_LIMIT = 56 << 20
ADAM_LR, ADAM_B1, ADAM_B2, ADAM_EPS, ADAM_WD, ADAM_STEP = 0.001, 0.9, 0.999, 1e-08, 0.01, 10


def _pick(n, cands):
    for c in cands:
        if n % c == 0:
            return c
    return n


def _cp(*sem):
    return pltpu.CompilerParams(dimension_semantics=sem, vmem_limit_bytes=VMEM_LIMIT)


def _dg(a, b, ca, cb):
    return lax.dot_general(a.astype(BF16), b.astype(BF16), (((ca,), (cb,)), ((), ())),
                           preferred_element_type=F32)


@jax.custom_vjp
def dot_nn(a, b):
    return _dg(a, b, 1, 0)


@jax.custom_vjp
def dot_nt(a, b):
    return _dg(a, b, 1, 1)


@jax.custom_vjp
def dot_tn(a, b):
    return _dg(a, b, 0, 0)


dot_nn.defvjp(lambda a, b: (dot_nn(a, b), (a, b)),
              lambda r, g: (dot_nt(g, r[1]).astype(r[0].dtype), dot_tn(r[0], g).astype(r[1].dtype)))
dot_nt.defvjp(lambda a, b: (dot_nt(a, b), (a, b)),
              lambda r, g: (dot_nn(g, r[1]).astype(r[0].dtype), dot_tn(g, r[0]).astype(r[1].dtype)))
dot_tn.defvjp(lambda a, b: (dot_tn(a, b), (a, b)),
              lambda r, g: (dot_nt(r[1], g).astype(r[0].dtype), dot_nn(r[0], g).astype(r[1].dtype)))


def xdot(a, b):
    return jnp.dot(a, b, precision=lax.Precision.HIGHEST, preferred_element_type=F32)


MM_VMEM_BUDGET = 42 << 20
MM_STEP_S, MM_HBM_BPS, MM_VMEM_BPS = 0.4e-6, 2.5e12, 3e12


def _divisors(n, cands):
    return sorted({c for c in cands if n % c == 0} | {n}, reverse=True)


def _mm_tiles(M, K, N, ab, bb, ob, addb):
    best = None
    for tm in _divisors(M, (2048, 1024, 512, 256, 128)):
        for tn in _divisors(N, (2048, 1664, 1536, 1408, 1280, 1024, 768, 640, 512, 384, 256, 128)):
            for tk in _divisors(K, (4096, 2048, 1664, 1536, 1408, 1024, 768, 512, 384, 256, 128)):
                vmem = 2 * (tm * tk * ab + tk * tn * bb + tm * tn * (ob + addb)) + (tm * tn * 4 if tk != K else 0)
                if vmem > MM_VMEM_BUDGET:
                    continue
                nk = K // tk
                steps = (M // tm) * (N // tn) * nk
                traffic = M * K * ab * (N // tn) + K * N * bb * (M // tm) + M * N * (ob + addb)
                cost = steps * MM_STEP_S + traffic / MM_HBM_BPS + (M * N * 8 * nk / MM_VMEM_BPS if nk > 1 else 0)
                if best is None or cost < best[0]:
                    best = (cost, tm, tn, tk)
    assert best is not None, (M, K, N)
    return best[1:]


def mm(a, b, mode, *, name, add=None, out_dtype=F32, layer=None, slot=None):
    bk, bn = b.shape[-2:]
    if mode == "nn":
        (M, K), N = a.shape, bn
    else:
        (K, M), N = a.shape, bn
    assert bk == K, (name, a.shape, b.shape)
    isz = lambda x: jnp.dtype(x.dtype).itemsize
    tm, tn, tk = _mm_tiles(M, K, N, isz(a), isz(b), jnp.dtype(out_dtype).itemsize, 0 if add is None else isz(add))
    nk = K // tk
    ca = 1 if mode == "nn" else 0

    n_in = 2 + (add is not None) + (slot is not None and slot[2] is not None)

    def body(*refs):
        a_ref, b_ref = refs[:2]
        add_ref = refs[2] if add is not None else None
        o_ref = refs[n_in]

        def finish(r):
            if add is not None:
                r = r + add_ref[...].astype(F32)
            o_ref[...] = r.astype(out_dtype)

        part = _dg(a_ref[...], b_ref[...], ca, 0)
        if nk == 1:
            finish(part)
            return
        acc = refs[-1]
        kk = pl.program_id(2)

        @pl.when(kk == 0)
        def _():
            acc[...] = part

        @pl.when(kk > 0)
        def _():
            acc[...] += part

        @pl.when(kk == nk - 1)
        def _():
            finish(acc[...])

    a_spec = (pl.BlockSpec((tm, tk), lambda i, j, k: (i, k)) if mode == "nn"
              else pl.BlockSpec((tk, tm), lambda i, j, k: (k, i)))
    b_spec = (pl.BlockSpec((tk, tn), lambda i, j, k: (k, j)) if layer is None
              else pl.BlockSpec((None, tk, tn), lambda i, j, k: (layer, k, j)))
    in_specs, args = [a_spec, b_spec], [a, b]
    if add is not None:
        in_specs.append(pl.BlockSpec((tm, tn), lambda i, j, k: (i, j)))
        args.append(add)
    out_spec = pl.BlockSpec((tm, tn), lambda i, j, k: (i, j))
    out_shape = jax.ShapeDtypeStruct((M, N), out_dtype)
    aliases = {}
    if slot is not None:
        n_layers, l, buf = slot
        out_spec = pl.BlockSpec((None, tm, tn), lambda i, j, k: (l, i, j))
        out_shape = jax.ShapeDtypeStruct((n_layers, M, N), out_dtype)
        if buf is not None:
            aliases = {len(args): 0}
            in_specs.append(pl.BlockSpec(memory_space=pl.ANY))
            args.append(buf)
    return pl.pallas_call(
        body, name=name, grid=(M // tm, N // tn, nk),
        in_specs=in_specs, out_specs=out_spec, out_shape=out_shape, input_output_aliases=aliases,
        scratch_shapes=[pltpu.VMEM((tm, tn), F32)] if nk > 1 else [],
        compiler_params=_cp("parallel", "parallel", "arbitrary"),
    )(*args)


def _row_spec(tm, width, col):
    if callable(col):
        return pl.BlockSpec((tm, width), lambda i, j: (i, col(j)))
    return pl.BlockSpec((tm, width), lambda i, j: (i, col))


def _const_spec(c):
    return pl.BlockSpec(c.shape, lambda i, j: (0,) * c.ndim)


ROW_BLOCK_BYTES = 6 << 20


def _row_tile(M, widths, tm):
    if tm is None:
        tm = next((t for t in (1024, 512, 256) if t * sum(widths) * 4 <= ROW_BLOCK_BYTES), 128)
    return _pick(M, (tm, 256, 128, 64, 32, 16, 8))


def rowwise(name, fn, rows, consts, outs, *, tm=None, nj=1):
    M = rows[0][0].shape[0]
    tm = _row_tile(M, [w for _, w, _ in rows] + [w for w, _ in outs], tm)
    nr, nc = len(rows), len(consts)

    def body(*refs):
        vals = [r[...].astype(F32) for r in refs[:nr]] + [r[...] for r in refs[nr:nr + nc]]
        res = fn(*vals)
        for o_ref, r in zip(refs[nr + nc:], res):
            o_ref[...] = r.astype(o_ref.dtype)

    return pl.pallas_call(
        body, name=name, grid=(M // tm, nj),
        in_specs=[_row_spec(tm, w, c) for _, w, c in rows] + [_const_spec(c) for c in consts],
        out_specs=[pl.BlockSpec((tm, w), lambda i, j: (i, j)) for w, _ in outs],
        out_shape=[jax.ShapeDtypeStruct((M, nj * w), dt) for w, dt in outs],
        compiler_params=_cp("parallel", "parallel"),
    )(*[r[0] for r in rows], *consts)


def rowwise_vjp(name, fn, rows, consts, cts, diff, *, tm=None, nj=1, gdt=None):
    M = rows[0][0].shape[0]
    tm = _row_tile(M, [w for _, w, _ in rows] + [w for _, w, _ in cts] + [rows[p][1] for p in diff], tm)
    nr, nc, nt, nd = len(rows), len(consts), len(cts), len(diff)
    gdt = [F32] * nd if gdt is None else gdt

    def body(*refs):
        vals = [r[...].astype(F32) for r in refs[:nr + nc + nt]]
        rv, cv, tv = vals[:nr], vals[nr:nr + nc], vals[nr + nc:]
        grow, gconst = refs[nr + nc + nt:nr + nc + nt + nd], refs[nr + nc + nt + nd:]

        def f(*dargs):
            full = list(rv)
            for pos, val in zip(diff, dargs[:nd]):
                full[pos] = val
            return tuple(fn(*full, *dargs[nd:]))

        _, vjp = jax.vjp(f, *[rv[p] for p in diff], *cv)
        g = vjp(tuple(tv))
        for o_ref, gv in zip(grow, g[:nd]):
            o_ref[...] = gv.astype(o_ref.dtype)
        first = jnp.logical_and(pl.program_id(0) == 0, pl.program_id(1) == 0)
        for o_ref, gv in zip(gconst, g[nd:]):
            @pl.when(first)
            def _(o_ref=o_ref, gv=gv):
                o_ref[...] = gv

            @pl.when(jnp.logical_not(first))
            def _(o_ref=o_ref, gv=gv):
                o_ref[...] += gv

    out_specs = ([pl.BlockSpec((tm, rows[p][1]), lambda i, j: (i, j)) for p in diff]
                 + [_const_spec(c) for c in consts])
    out_shape = ([jax.ShapeDtypeStruct((M, nj * rows[p][1]), dt) for p, dt in zip(diff, gdt)]
                 + [jax.ShapeDtypeStruct(c.shape, F32) for c in consts])
    res = pl.pallas_call(
        body, name=name, grid=(M // tm, nj),
        in_specs=([_row_spec(tm, w, c) for _, w, c in rows] + [_const_spec(c) for c in consts]
                  + [_row_spec(tm, w, c) for _, w, c in cts]),
        out_specs=out_specs, out_shape=out_shape,
        compiler_params=_cp("arbitrary", "arbitrary"),
    )(*[r[0] for r in rows], *consts, *[t[0] for t in cts])
    return res[:nd], res[nd:]


def _lane(shape=(1, LANE)):
    return lax.broadcasted_iota(jnp.int32, shape, len(shape) - 1)


def _sigmoid(x):
    return 1.0 / (1.0 + jnp.exp(-x))


def _rms(x, g, n):
    return x * lax.rsqrt(jnp.sum(x * x, axis=-1, keepdims=True) * (1.0 / n) + NORM_EPS) * g


def fn_rms(n):
    return lambda x, g: (_rms(x, g, n),)


def _rope(x, cos_t, sin_t):
    i = lax.broadcasted_iota(jnp.int32, (LANE, LANE), 0)
    j = lax.broadcasted_iota(jnp.int32, (LANE, LANE), 1)
    half = ROPE // 2
    lo = jnp.logical_and(jnp.logical_and(j >= NOPE, j < NOPE + half), i == j + half)
    hi = jnp.logical_and(jnp.logical_and(j >= NOPE + half, j < NOPE + ROPE), i == j - half)
    perm = jnp.where(hi, 1.0, 0.0) - jnp.where(lo, 1.0, 0.0)
    return x * cos_t + xdot(x, perm) * sin_t


def fn_qpost(q, cos_t, sin_t, g):
    return (_rope(_rms(q, g, QK_DIM), cos_t, sin_t),)


def fn_kpost(kn, small, cos_t, sin_t, g):
    lane = _lane()
    rope_lanes = jnp.logical_and(lane >= NOPE, lane < NOPE + ROPE)
    kc = kn + jnp.where(rope_lanes, small, 0.0)
    return (_rope(_rms(kc, g, QK_DIM), cos_t, sin_t),)


def fn_foxnorm(x, g):
    first = _lane() < FOX_DIM
    sq = x * x
    s0 = jnp.sum(jnp.where(first, sq, 0.0), axis=-1, keepdims=True)
    s1 = jnp.sum(jnp.where(first, 0.0, sq), axis=-1, keepdims=True)
    r0 = lax.rsqrt(s0 * (1.0 / FOX_DIM) + NORM_EPS)
    r1 = lax.rsqrt(s1 * (1.0 / FOX_DIM) + NORM_EPS)
    return (x * jnp.where(first, r0, r1) * g,)


F_LANE0 = NOPE + ROPE


def fn_fgate(small, bias):
    z = small + bias
    lf = jnp.minimum(z, 0.0) - jnp.log(1.0 + jnp.exp(-jnp.abs(z)))
    lane = _lane()
    return (jnp.where(jnp.logical_and(lane >= F_LANE0, lane < F_LANE0 + N_HEADS), lf, 0.0),)


def _gelu(y):
    return 0.5 * y * (1.0 + jnp.tanh(math.sqrt(2.0 / math.pi) * (y + 0.044715 * (y * y * y))))


def fn_s5post(ypre, u, d, wglu, bglu):
    y = _gelu(ypre + d * u)
    return (y * _sigmoid(dot_nn(y, wglu) + bglu),)


def fn_merge(om, of, os_, g0, g1, g2, wb0, wb1, wb2):
    return (_sigmoid(g0) * dot_nn(om, wb0) + _sigmoid(g1) * dot_nn(of, wb1)
            + _sigmoid(g2) * dot_nn(os_, wb2),)


def fn_add5(a, b):
    return (a[:, 0:LANE] + a[:, LANE:2 * LANE] + a[:, 2 * LANE:3 * LANE] + a[:, 3 * LANE:4 * LANE] + b,)


def fn_addt(a, b):
    return (a + b,)


def fn_s5params(lre, lim, lstep, btr, bti, ctr, cti):
    C = S5_C
    grp = lax.broadcasted_iota(jnp.int32, (LANE, C), 1) >> 6
    expand = jnp.where(lax.broadcasted_iota(jnp.int32, (LANE, C), 0) == grp, 1.0, 0.0)
    lane = _lane()
    st = jnp.where(lane < S5_G, jnp.exp(lstep), 0.0)
    step = jnp.sum(xdot(jnp.broadcast_to(st, (8, LANE)), expand), axis=0, keepdims=True) * 0.125
    zr, zi = lre * step, lim * step
    er = jnp.exp(zr)
    lbr, lbi = er * jnp.cos(zi), er * jnp.sin(zi)
    den = lre * lre + lim * lim
    nr = lbr - 1.0
    cfr = (nr * lre + lbi * lim) / den
    cfi = (lbi * lre - nr * lim) / den
    bbr = cfr * btr - cfi * bti
    bbi = cfr * bti + cfi * btr
    rg = lax.broadcasted_iota(jnp.int32, (BW, C), 0) >> 4
    cg = lax.broadcasted_iota(jnp.int32, (BW, C), 1) >> 6
    mb = jnp.where(rg == cg, 1.0, 0.0)
    b_re = jnp.concatenate([bbr] * S5_G, axis=0) * mb
    b_im = jnp.concatenate([bbi] * S5_G, axis=0) * mb
    ecol = jnp.where(lax.broadcasted_iota(jnp.int32, (LANE, BW), 0)
                     == (lax.broadcasted_iota(jnp.int32, (LANE, BW), 1) & 15), 1.0, 0.0)
    mc = jnp.where((lax.broadcasted_iota(jnp.int32, (C, BW), 0) >> 6)
                   == (lax.broadcasted_iota(jnp.int32, (C, BW), 1) >> 4), 1.0, 0.0)
    c_top = xdot(ctr, ecol) * mc
    c_bot = -(xdot(cti, ecol) * mc)
    return lbr, lbi, b_re, b_im, c_top, c_bot


def s5_params(p):
    def body(lre, lim, ls, btr, bti, ctr, cti, lb_ref, bre_ref, bim_ref, ct_ref, cb_ref):
        lbr, lbi, b_re, b_im, c_top, c_bot = fn_s5params(
            lre[...], lim[...], ls[...], btr[...], bti[...], ctr[...], cti[...])
        lb_ref[0:1, :] = lbr
        lb_ref[1:2, :] = lbi
        bre_ref[...] = b_re.astype(BF16)
        bim_ref[...] = b_im.astype(BF16)
        ct_ref[...] = c_top.astype(BF16)
        cb_ref[...] = c_bot.astype(BF16)

    return pl.pallas_call(
        body, name="s5_params",
        out_shape=[jax.ShapeDtypeStruct((2, S5_C), F32), jax.ShapeDtypeStruct((BW, S5_C), BF16),
                   jax.ShapeDtypeStruct((BW, S5_C), BF16), jax.ShapeDtypeStruct((S5_C, BW), BF16),
                   jax.ShapeDtypeStruct((S5_C, BW), BF16)],
        compiler_params=pltpu.CompilerParams(vmem_limit_bytes=VMEM_LIMIT),
    )(p["lre"], p["lim"], p["lstep"], p["btr"], p["bti"], p["ctr"], p["cti"])


S5_PARAM_NAMES = ("lre", "lim", "lstep", "btr", "bti", "ctr", "cti")


def s5_params_vjp(p, dl_r, dl_i, db_re, db_im, dc_top, dc_bot):
    def body(lre, lim, ls, btr, bti, ctr, cti, dlr, dli, dbr, dbi, dct, dcb, *outs):
        args = [r[...] for r in (lre, lim, ls, btr, bti, ctr, cti)]
        _, vjp = jax.vjp(fn_s5params, *args)
        g = vjp((jnp.sum(dlr[...], axis=0, keepdims=True), jnp.sum(dli[...], axis=0, keepdims=True),
                 dbr[...], dbi[...], dct[...], dcb[...]))
        for o_ref, gv in zip(outs, g):
            o_ref[...] = gv

    return pl.pallas_call(
        body, name="s5_params_vjp",
        out_shape=[jax.ShapeDtypeStruct(p[n].shape, F32) for n in S5_PARAM_NAMES],
        compiler_params=pltpu.CompilerParams(vmem_limit_bytes=VMEM_LIMIT),
    )(*[p[n] for n in S5_PARAM_NAMES], dl_r, dl_i, db_re, db_im, dc_top, dc_bot)


def _attn_tile(q, k, v, cq, ckt, *, hpt, dk, q0, tile):
    tq, S = q.shape[0], k.shape[0]
    row = q0 + lax.broadcasted_iota(jnp.int32, (tq, S), 0)
    col = lax.broadcasted_iota(jnp.int32, (tq, S), 1)
    causal = row >= col
    lane = _lane()
    out = jnp.zeros((tq, LANE), F32)
    for h in range(hpt):
        if hpt > 1:
            mine = (lane >> int(math.log2(LANE // hpt))) == h
            qh = jnp.where(mine, q, 0.0)
        else:
            qh = q
        s = dot_nt(qh, k) * (dk ** -0.5)
        if cq is not None:
            head = tile * hpt + h
            cqh = jnp.sum(jnp.where(lane == F_LANE0 + head, cq, 0.0), axis=1, keepdims=True)
            sub = lax.broadcasted_iota(jnp.int32, (8, 1), 0)
            ckh = jnp.sum(jnp.where(sub == head, ckt, 0.0), axis=0, keepdims=True)
            s = s + (cqh - ckh)
        s = jnp.where(causal, s, NEG_INF)
        m = lax.stop_gradient(jnp.max(s, axis=-1, keepdims=True))
        e = jnp.exp(s - m)
        p = e / jnp.sum(e, axis=-1, keepdims=True)
        oh = dot_nn(p, v)
        out = out + (jnp.where(mine, oh, 0.0) if hpt > 1 else oh)
    return out


def attention(name, q, k, v, *, B, S, ntile, hpt, dk, qc=0, kc=0, vc=0, cum=None, ckt=None, do=None, tq=256):
    tq = _pick(S, (tq, 128))
    nq = S // tq
    M = B * S
    bias = cum is not None
    kw = dict(hpt=hpt, dk=dk)

    def load(refs, sk):
        q_ref, k_ref, v_ref = refs[:3]
        qv, kv, vv = q_ref[...].astype(F32), k_ref[0:sk, :].astype(F32), v_ref[0:sk, :].astype(F32)
        if bias:
            return qv, kv, vv, refs[3][...], refs[4][0, :, 0:sk]
        return qv, kv, vv, None, None

    nin = 5 if bias else 3

    def per_query_block(run):
        for g in range(nq):
            @pl.when(pl.program_id(2) == g)
            def _(g=g):
                run(g, (g + 1) * tq)

    def fwd_body(*refs):
        tile = pl.program_id(1)

        def run(g, sk):
            qv, kv, vv, cq, ck = load(refs, sk)
            o = _attn_tile(qv, kv, vv, cq, ck, q0=g * tq, tile=tile, **kw)
            refs[nin][...] = o.astype(refs[nin].dtype)

        per_query_block(run)

    def bwd_body(*refs):
        outs = refs[nin + 1:]
        tile = pl.program_id(1)

        @pl.when(pl.program_id(2) == 0)
        def _():
            for o_ref in (outs[1], outs[2]) + ((outs[4],) if bias else ()):
                o_ref[...] = jnp.zeros_like(o_ref)

        def run(g, sk):
            qv, kv, vv, cq, ck = load(refs, sk)
            dov = refs[nin][...].astype(F32)
            if bias:
                f = lambda a, b, c, d, e: _attn_tile(a, b, c, d, e, q0=g * tq, tile=tile, **kw)
                _, vjp = jax.vjp(f, qv, kv, vv, cq, ck)
            else:
                f = lambda a, b, c: _attn_tile(a, b, c, None, None, q0=g * tq, tile=tile, **kw)
                _, vjp = jax.vjp(f, qv, kv, vv)
            gr = vjp(dov)
            outs[0][...] = gr[0]
            outs[1][0:sk, :] += gr[1]
            outs[2][0:sk, :] += gr[2]
            if bias:
                outs[3][...] = gr[3]
                outs[4][0, :, 0:sk] += gr[4]

        per_query_block(run)

    qspec = lambda c: pl.BlockSpec((tq, LANE), lambda b, t, i: (b * nq + i, c + t))
    kspec = lambda c: pl.BlockSpec((S, LANE), lambda b, t, i: (b, c + t))
    in_specs, args = [qspec(qc), kspec(kc), kspec(vc)], [q, k, v]
    if bias:
        in_specs += [pl.BlockSpec((tq, LANE), lambda b, t, i: (b * nq + i, 0)),
                     pl.BlockSpec((1, 8, S), lambda b, t, i: (b, 0, 0))]
        args += [cum, ckt]
    if do is None:
        return pl.pallas_call(
            fwd_body, name=name, grid=(B, ntile, nq), in_specs=in_specs, out_specs=qspec(0),
            out_shape=jax.ShapeDtypeStruct((M, ntile * LANE), BF16),
            compiler_params=_cp("parallel", "parallel", "parallel"),
        )(*args)
    in_specs.append(qspec(0))
    args.append(do)
    out_specs = [qspec(0), kspec(0), kspec(0)]
    out_shape = [jax.ShapeDtypeStruct((M, ntile * LANE), F32)] * 3
    if bias:
        out_specs += [qspec(0), pl.BlockSpec((1, 8, S), lambda b, t, i: (b * ntile + t, 0, 0))]
        out_shape += [jax.ShapeDtypeStruct((M, ntile * LANE), F32),
                      jax.ShapeDtypeStruct((B * ntile, 8, S), F32)]
    return pl.pallas_call(
        bwd_body, name=name, grid=(B, ntile, nq), in_specs=in_specs, out_specs=out_specs,
        out_shape=out_shape, compiler_params=_cp("parallel", "parallel", "arbitrary"),
    )(*args)


def seq_cumsum(name, x, *, B, S, reverse):
    nb = S // LANE

    def body(x_ref, o_ref):
        r = lax.broadcasted_iota(jnp.int32, (LANE, LANE), 0)
        c = lax.broadcasted_iota(jnp.int32, (LANE, LANE), 1)
        tri = jnp.where((r <= c) if reverse else (r >= c), 1.0, 0.0)
        carry = jnp.zeros((1, LANE), F32)
        for blk in (range(nb - 1, -1, -1) if reverse else range(nb)):
            xb = x_ref[blk * LANE:(blk + 1) * LANE, :]
            o_ref[blk * LANE:(blk + 1) * LANE, :] = xdot(tri, xb) + carry
            carry = carry + jnp.sum(xb, axis=0, keepdims=True)

    return pl.pallas_call(
        body, name=name, grid=(B,), in_specs=[pl.BlockSpec((S, LANE), lambda b: (b, 0))],
        out_specs=pl.BlockSpec((S, LANE), lambda b: (b, 0)),
        out_shape=jax.ShapeDtypeStruct(x.shape, F32), compiler_params=_cp("parallel"),
    )(x)


SCAN_ROWS = 64


def _shift_rows(ref, r0, rows, d, up):
    if d % 8 == 0:
        return ref[pl.ds(r0 + d if up else r0 - d, rows), :]
    pad = -(-d // 8) * 8
    if up:
        win = ref[pl.ds(r0, rows + pad), :]
        return pltpu.roll(win, rows + pad - d, 0)[0:rows, :]
    win = ref[pl.ds(r0 - pad, rows + pad), :]
    return pltpu.roll(win, d, 0)[pad:rows + pad, :]


def _cmul(ar, ai, br, bi):
    return ar * br - ai * bi, ar * bi + ai * br


def s5_scan(name, x_re, x_im, lam, *, B, S, reverse, state=None):
    C = S5_C
    cw = LANE
    R = _pick(S, (SCAN_ROWS,))
    pad = max(S // 2, 8)
    log2s = int(math.log2(S))
    assert 1 << log2s == S
    base = 0 if reverse else pad
    with_grad = state is not None

    def body(*refs):
        if with_grad:
            xr, xi, lam_ref, sr, si, o_r, o_i, dl_r, dl_i, a_r, a_i, b_r, b_i = refs
        else:
            xr, xi, lam_ref, o_r, o_i, a_r, a_i, b_r, b_i = refs
        zero = jnp.zeros((pad, cw), F32)
        z0 = S if reverse else 0
        for buf in (a_r, a_i, b_r, b_i):
            buf[z0:z0 + pad, :] = zero
        a_r[base:base + S, :] = xr[...]
        a_i[base:base + S, :] = xi[...]
        mr = lam_ref[0:1, :]
        mi = -lam_ref[1:2, :] if reverse else lam_ref[1:2, :]
        src, dst = (a_r, a_i), (b_r, b_i)
        for step in range(log2s):
            d = 1 << step
            last = step == log2s - 1

            def chunk(c, _, src=src, dst=dst, d=d, last=last, mr=mr, mi=mi):
                r0 = pl.multiple_of(base + c * R, 8)
                xr_ = _shift_rows(src[0], r0, R, d, reverse)
                xi_ = _shift_rows(src[1], r0, R, d, reverse)
                nr = src[0][pl.ds(r0, R), :] + mr * xr_ - mi * xi_
                ni = src[1][pl.ds(r0, R), :] + mr * xi_ + mi * xr_
                if last:
                    o0 = pl.multiple_of(c * R, 8)
                    o_r[pl.ds(o0, R), :] = nr
                    o_i[pl.ds(o0, R), :] = ni
                else:
                    dst[0][pl.ds(r0, R), :] = nr
                    dst[1][pl.ds(r0, R), :] = ni
                return 0

            lax.fori_loop(0, S // R, chunk, 0)
            mr, mi = _cmul(mr, mi, mr, mi)
            src, dst = dst, src
        if with_grad:
            def fold(v):
                return jnp.sum(v.reshape(R // 8, 8, cw), axis=0)

            def accum(c, carry, first=False):
                r0 = 0 if first else pl.multiple_of(c * R, 8)
                gr, gi = o_r[pl.ds(r0, R), :], o_i[pl.ds(r0, R), :]
                if first:
                    keep = lax.broadcasted_iota(jnp.int32, (R, 1), 0) >= 1
                    pr = jnp.where(keep, pltpu.roll(sr[0:R, :], 1, 0), 0.0)
                    pi = jnp.where(keep, pltpu.roll(si[0:R, :], 1, 0), 0.0)
                else:
                    pr = _shift_rows(sr, r0, R, 1, False)
                    pi = _shift_rows(si, r0, R, 1, False)
                return (carry[0] + fold(gr * pr + gi * pi), carry[1] + fold(gi * pr - gr * pi))

            acc = accum(0, (jnp.zeros((8, cw), F32), jnp.zeros((8, cw), F32)), first=True)
            acc = lax.fori_loop(1, S // R, accum, acc)
            dl_r[...] = acc[0]
            dl_i[...] = acc[1]

    seq = pl.BlockSpec((S, cw), lambda b, j: (b, j))
    in_specs = [seq, seq, pl.BlockSpec((2, cw), lambda b, j: (0, j))]
    args = [x_re, x_im, lam]
    out_specs = [seq, seq]
    out_shape = [jax.ShapeDtypeStruct(x_re.shape, F32)] * 2
    if with_grad:
        in_specs += [seq, seq]
        args += list(state)
        out_specs += [pl.BlockSpec((8, cw), lambda b, j: (b, j))] * 2
        out_shape += [jax.ShapeDtypeStruct((B * 8, C), F32)] * 2
    return pl.pallas_call(
        body, name=name, grid=(B, C // cw), in_specs=in_specs, out_specs=out_specs, out_shape=out_shape,
        scratch_shapes=[pltpu.VMEM((S + pad, cw), F32)] * 4,
        compiler_params=_cp("parallel", "parallel"),
    )(*args)


CONV_CW = 256


def _conv_taps(ref, r0, rows, first):
    cur = ref[pl.ds(r0, rows), :]
    if first:
        row = lax.broadcasted_iota(jnp.int32, (rows, 1), 0)
        p1 = jnp.where(row >= 1, pltpu.roll(cur, 1, 0), 0.0)
        p2 = jnp.where(row >= 2, pltpu.roll(cur, 2, 0), 0.0)
    else:
        p1 = _shift_rows(ref, r0, rows, 1, False)
        p2 = _shift_rows(ref, r0, rows, 2, False)
    return cur, p1, p2


def _conv_apply(w_ref, taps):
    return w_ref[2:3, :] * taps[0] + w_ref[1:2, :] * taps[1] + w_ref[0:1, :] * taps[2]


def conv_gate_fwd(up, conv_w, *, B, S):
    M, F2 = up.shape
    F = F2 // 2
    cw = _pick(F, (CONV_CW, LANE))
    nf = F // cw
    R = _pick(S, (SCAN_ROWS,))

    def body(g_ref, v_ref, wg_ref, wv_ref, o_ref):
        def chunk(c, _, first=False):
            r0 = 0 if first else pl.multiple_of(c * R, 8)
            cg = _conv_apply(wg_ref, _conv_taps(g_ref, r0, R, first))
            cv = _conv_apply(wv_ref, _conv_taps(v_ref, r0, R, first))
            o_ref[pl.ds(r0, R), :] = (cg * _sigmoid(cg) * cv).astype(o_ref.dtype)
            return 0

        chunk(0, 0, first=True)
        lax.fori_loop(1, S // R, chunk, 0)

    seq = lambda off: pl.BlockSpec((S, cw), lambda b, j: (b, off + j))
    wsp = lambda off: pl.BlockSpec((3, cw), lambda b, j: (0, off + j))
    return pl.pallas_call(
        body, name="conv_gate", grid=(B, nf), in_specs=[seq(0), seq(nf), wsp(0), wsp(nf)],
        out_specs=seq(0), out_shape=jax.ShapeDtypeStruct((M, F), BF16),
        compiler_params=_cp("parallel", "parallel"),
    )(up, up, conv_w, conv_w)


def conv_gate_bwd(up, conv_w, dact, *, B, S):
    M, F2 = up.shape
    F = F2 // 2
    cw = _pick(F, (CONV_CW, LANE))
    nf = F // cw
    R = _pick(S, (SCAN_ROWS,))
    nchunk = S // R

    def body(s_ref, p_ref, ws_ref, wp_ref, da_ref, du_ref, dw_ref, dc_ref, silu_ref):
        is_gate = pl.program_id(1) == 0
        ex = pl.program_id(2)
        dc_ref[S:S + 8, :] = jnp.zeros((8, cw), F32)

        def fold(v):
            return jnp.sum(v.reshape(R // 8, 8, cw), axis=0)

        first_b = ex == 0

        def pass1(gate_step):
            def chunk(c, acc, first=False):
                r0 = 0 if first else pl.multiple_of(c * R, 8)
                taps = _conv_taps(s_ref, r0, R, first)
                da = da_ref[pl.ds(r0, R), :]
                if gate_step:
                    cp = _conv_apply(wp_ref, _conv_taps(p_ref, r0, R, first))
                    cs = _conv_apply(ws_ref, taps)
                    sg = _sigmoid(cs)
                    silu_ref[ex, pl.ds(r0, R), :] = cs * sg
                    dc = da * cp * (sg * (1.0 + cs * (1.0 - sg)))
                else:
                    dc = da * silu_ref[ex, pl.ds(r0, R), :]
                dc_ref[pl.ds(r0, R), :] = dc
                return (acc[0] + fold(dc * taps[2]), acc[1] + fold(dc * taps[1]), acc[2] + fold(dc * taps[0]))

            z = jnp.zeros((8, cw), F32)
            acc = chunk(0, (z, z, z), first=True)
            acc = lax.fori_loop(1, nchunk, chunk, acc)
            for tap in range(3):
                tot = jnp.sum(acc[tap], axis=0, keepdims=True)

                @pl.when(first_b)
                def _(tap=tap, tot=tot):
                    dw_ref[tap:tap + 1, :] = tot

                @pl.when(jnp.logical_not(first_b))
                def _(tap=tap, tot=tot):
                    dw_ref[tap:tap + 1, :] += tot

        @pl.when(is_gate)
        def _():
            pass1(True)

        @pl.when(jnp.logical_not(is_gate))
        def _():
            pass1(False)

        def pass2(c, _):
            r0 = pl.multiple_of(c * R, 16)
            n0 = dc_ref[pl.ds(r0, R), :]
            n1 = _shift_rows(dc_ref, r0, R, 1, True)
            n2 = _shift_rows(dc_ref, r0, R, 2, True)
            du = ws_ref[2:3, :] * n0 + ws_ref[1:2, :] * n1 + ws_ref[0:1, :] * n2
            du_ref[pl.ds(r0, R), :] = du.astype(du_ref.dtype)
            return 0

        lax.fori_loop(0, nchunk, pass2, 0)

    seq = lambda f: pl.BlockSpec((S, cw), lambda j, t, b: (b, f(j, t)))
    wsp = lambda f: pl.BlockSpec((3, cw), lambda j, t, b: (0, f(j, t)))
    same, value, act_col = (lambda j, t: j + t * nf), (lambda j, t: j + nf), (lambda j, t: j)
    return pl.pallas_call(
        body, name="conv_gate_vjp", grid=(nf, 2, B),
        in_specs=[seq(same), seq(value), wsp(same), wsp(value), seq(act_col)],
        out_specs=[seq(same), wsp(same)],
        out_shape=[jax.ShapeDtypeStruct((M, F2), BF16), jax.ShapeDtypeStruct((3, F2), F32)],
        scratch_shapes=[pltpu.VMEM((S + 8, cw), F32), pltpu.VMEM((B, S, cw), F32)],
        compiler_params=_cp("parallel", "arbitrary", "arbitrary"),
    )(up, up, conv_w, conv_w, dact)


def loss_head(y, target):
    M, D = y.shape
    tm = _pick(M, (256, 128, 64, 32, 16, 8))

    def body(y_ref, t_ref, dy_ref, l_ref):
        diff = y_ref[...] - t_ref[...]
        dy_ref[...] = diff * (1.0 / D)
        part = jnp.sum(jnp.sum(diff * diff, axis=1, keepdims=True), axis=0, keepdims=True)

        @pl.when(pl.program_id(0) == 0)
        def _():
            l_ref[...] = jnp.zeros_like(l_ref)

        l_ref[...] += part

    row = pl.BlockSpec((tm, D), lambda i: (i, 0))
    return pl.pallas_call(
        body, name="loss_head", grid=(M // tm,), in_specs=[row, row],
        out_specs=[row, pl.BlockSpec((8, LANE), lambda i: (0, 0))],
        out_shape=[jax.ShapeDtypeStruct((M, D), F32), jax.ShapeDtypeStruct((8, LANE), F32)],
        compiler_params=_cp("arbitrary"),
    )(y, target)


def adamw(name, w, g, m, v):
    R, C = w.shape
    tr = _pick(R, (256, 128, 64, 32, 16, 8))

    def body(w_ref, g_ref, m_ref, v_ref, d_ref, nm_ref, nv_ref, go_ref):
        gv = g_ref[...]
        go_ref[...] = gv
        nm = ADAM_B1 * m_ref[...] + (1.0 - ADAM_B1) * gv
        nv = ADAM_B2 * v_ref[...] + (1.0 - ADAM_B2) * (gv * gv)
        m_hat = nm / (1.0 - ADAM_B1 ** ADAM_STEP)
        v_hat = nv / (1.0 - ADAM_B2 ** ADAM_STEP)
        d_ref[...] = -ADAM_LR * (m_hat / (jnp.sqrt(v_hat) + ADAM_EPS) + ADAM_WD * w_ref[...])
        nm_ref[...] = nm
        nv_ref[...] = nv

    blk = pl.BlockSpec((tr, C), lambda i: (i, 0))
    return pl.pallas_call(
        body, name=name, grid=(R // tr,), in_specs=[blk] * 4, out_specs=[blk] * 4,
        out_shape=[jax.ShapeDtypeStruct((R, C), F32)] * 4, compiler_params=_cp("parallel"),
    )(w, g, m, v)


def _seg(D):
    o = 3 * D
    return dict(ckv=o, fq=o + 256, fk=o + 512, fv=o + 768, u=o + 1024, small=o + 1280, cq=o + 1536, P=o + 1920)


def _pad_last(a, n):
    return jnp.pad(a, [(0, 0)] * (a.ndim - 1) + [(0, n - a.shape[-1])])


def _place(a, lo, n=LANE):
    return jnp.pad(a, [(0, 0)] * (a.ndim - 1) + [(lo, n - lo - a.shape[-1])])


def _in_segments(D):
    s = _seg(D)
    sm = s["small"]
    return ((0, 384, s["cq"]), (384, 640, s["ckv"]), (640, 672, sm + NOPE), (672, 1440, s["fq"]),
            (1440, 1444, sm + F_LANE0), (1444, 1700, s["u"]), (1700, 1700 + 3 * D, 0))


def _chip_pieces(win4, lo, hi):
    cw = win4.shape[-1]
    out = []
    for k in range(N_CHIPS):
        a, b = max(lo, k * cw), min(hi, (k + 1) * cw)
        if a < b:
            out.append(win4[:, k, :, a - k * cw:b - k * cw])
    return out


def prep_weights(w):
    L, D = w["attn_norm_g"].shape
    win = w["w_in"]
    z = lambda n: [jnp.zeros((L, D, n), win.dtype)]
    cols = lambda lo, hi: _chip_pieces(win, lo, hi)
    g0 = 1700
    wp = jnp.concatenate(
        cols(g0, g0 + 3 * D) + cols(384, 640) + cols(672, 1440) + cols(1444, 1700) + z(NOPE) + cols(640, 672)
        + cols(1440, 1444) + z(LANE - F_LANE0 - N_HEADS) + z(LANE) + cols(0, 384), axis=-1)
    wukv, wb = w["w_ukv"], w["w_branch"]
    row3 = lambda a: a[:, None, :]
    return dict(
        g1=row3(w["attn_norm_g"]), Wp=wp, gql=row3(w["q_lat_norm_g"]), gkvl=row3(w["kv_lat_norm_g"]),
        Wuq=_pad_last(w["w_uq"], LANE).reshape(L, Q_RANK, N_HEADS * LANE),
        Wk=_pad_last(wukv[..., :NOPE], LANE).reshape(L, KV_RANK, N_HEADS * LANE),
        Wv=_pad_last(wukv[..., NOPE:], LANE).reshape(L, KV_RANK, N_HEADS * LANE),
        gq=row3(_pad_last(w["mla_q_norm_g"], LANE)), gk=row3(_pad_last(w["mla_k_norm_g"], LANE)),
        gfq=row3(jnp.tile(w["fox_q_norm_g"], (1, 2))), gfk=row3(jnp.tile(w["fox_k_norm_g"], (1, 2))),
        fbias=row3(_place(w["fox_f_bias"], F_LANE0)),
        lre=w["s5_lambda_re"].reshape(L, 1, S5_C), lim=w["s5_lambda_im"].reshape(L, 1, S5_C),
        lstep=row3(_pad_last(w["s5_log_step"], LANE)),
        btr=jnp.transpose(w["s5_b_re"], (0, 3, 1, 2)).reshape(L, S5_H, S5_C),
        bti=jnp.transpose(w["s5_b_im"], (0, 3, 1, 2)).reshape(L, S5_H, S5_C),
        ctr=_pad_last(jnp.transpose(w["s5_c_re"], (0, 1, 3, 2)).reshape(L, S5_C, S5_H), LANE),
        cti=_pad_last(jnp.transpose(w["s5_c_im"], (0, 1, 3, 2)).reshape(L, S5_C, S5_H), LANE),
        s5d=w["s5_d"].reshape(L, 1, BW), Wglu=w["s5_w_glu"], bglu=row3(w["s5_b_glu"]),
        Wb0=jnp.pad(wb[:, 0].reshape(L, N_HEADS, V_DIM, D), ((0, 0), (0, 0), (0, LANE - V_DIM), (0, 0))
                    ).reshape(L, N_HEADS * LANE, D),
        Wb1=wb[:, 1], Wb2=wb[:, 2], Wout=w["w_out"], g2=row3(w["ffn_norm_g"]), Wup=w["w_up"],
        convw=w["ffn_conv_w"], Wdown=w["w_down"],
    )


BIG_KEYS = ("Wp", "Wuq", "Wk", "Wv", "Wout", "Wup", "Wdown")


def with_transposes(P):
    out = dict(P)
    for k in BIG_KEYS:
        out[k + "T"] = jnp.swapaxes(P[k], 1, 2)
    return out


def layer_params(P, l):
    return {k: (v if k in BIG_KEYS or k[:-1] in BIG_KEYS else v[l]) for k, v in P.items()}


def unprep_grads(G, D):
    L = G["g1"].shape[0]
    dwp = G["Wp"]
    segs = _in_segments(D)
    cw = segs[-1][1] // N_CHIPS
    chips = []
    for k in range(N_CHIPS):
        pieces = []
        for lo, hi, at in segs:
            a, b = max(lo, k * cw), min(hi, (k + 1) * cw)
            if a < b:
                pieces.append(dwp[..., at + a - lo:at + b - lo])
        chips.append(jnp.concatenate(pieces, axis=-1))
    w_in = jnp.stack(chips, axis=1)
    heads = lambda a, rows, keep: a.reshape(L, rows, N_HEADS, LANE)[..., :keep]
    wb0 = G["Wb0"].reshape(L, N_HEADS, LANE, D)[:, :, :V_DIM].reshape(L, BW, D)
    gf = lambda a: a[:, 0, :FOX_DIM] + a[:, 0, FOX_DIM:]
    return dict(
        attn_norm_g=G["g1"][:, 0], w_in=w_in, q_lat_norm_g=G["gql"][:, 0], w_uq=heads(G["Wuq"], Q_RANK, QK_DIM),
        kv_lat_norm_g=G["gkvl"][:, 0],
        w_ukv=jnp.concatenate([heads(G["Wk"], KV_RANK, NOPE), heads(G["Wv"], KV_RANK, V_DIM)], axis=-1),
        mla_q_norm_g=G["gq"][:, 0, :QK_DIM], mla_k_norm_g=G["gk"][:, 0, :QK_DIM],
        fox_q_norm_g=gf(G["gfq"]), fox_k_norm_g=gf(G["gfk"]),
        fox_f_bias=G["fbias"][:, 0, F_LANE0:F_LANE0 + N_HEADS],
        s5_lambda_re=G["lre"].reshape(L, S5_G, S5_P), s5_lambda_im=G["lim"].reshape(L, S5_G, S5_P),
        s5_b_re=jnp.transpose(G["btr"].reshape(L, S5_H, S5_G, S5_P), (0, 2, 3, 1)),
        s5_b_im=jnp.transpose(G["bti"].reshape(L, S5_H, S5_G, S5_P), (0, 2, 3, 1)),
        s5_c_re=jnp.transpose(G["ctr"][..., :S5_H].reshape(L, S5_G, S5_P, S5_H), (0, 1, 3, 2)),
        s5_c_im=jnp.transpose(G["cti"][..., :S5_H].reshape(L, S5_G, S5_P, S5_H), (0, 1, 3, 2)),
        s5_d=G["s5d"].reshape(L, S5_G, S5_H), s5_log_step=G["lstep"][:, 0, :S5_G],
        s5_w_glu=G["Wglu"], s5_b_glu=G["bglu"][:, 0],
        w_branch=jnp.stack([wb0, G["Wb1"], G["Wb2"]], axis=1), w_out=G["Wout"], ffn_norm_g=G["g2"][:, 0],
        w_up=G["Wup"], ffn_conv_w=G["convw"], w_down=G["Wdown"],
    )


def rope_tables(positions):
    inv_freq = ROPE_THETA ** (-jnp.arange(0, ROPE, 2, dtype=F32) / ROPE)
    ang = positions.astype(F32)[..., None] * inv_freq
    cos, sin = jnp.cos(ang), jnp.sin(ang)
    ones = jnp.ones(ang.shape[:-1] + (NOPE,), F32)
    zeros = jnp.zeros(ang.shape[:-1] + (LANE - NOPE - ROPE,), F32)
    cos_t = jnp.concatenate([ones, cos, cos, zeros], axis=-1)
    sin_t = jnp.concatenate([0.0 * ones, sin, sin, zeros], axis=-1)
    return cos_t.reshape(-1, LANE), sin_t.reshape(-1, LANE)


def fn_rms_res(n):
    return lambda x, g: (_rms(x, g, n), x)


def _s5_mats(p):
    return s5_params({k: p[k] for k in S5_PARAM_NAMES})


def layer_fwd(x, p, l, cos_t, sin_t, B, S):
    M, D = x.shape
    s = _seg(D)
    sm = s["small"] // LANE
    head = lambda j: j
    h = rowwise("rms_attn", fn_rms(D), [(x, D, 0)], [p["g1"]], [(D, BF16)])[0]
    proj = mm(h, p["Wp"], "nn", name="in_proj", layer=l)
    cnq = rowwise("latq_norm", fn_rms(Q_RANK), [(proj, Q_RANK, s["cq"] // Q_RANK)], [p["gql"]], [(Q_RANK, BF16)])[0]
    cnkv = rowwise("latkv_norm", fn_rms(KV_RANK), [(proj, KV_RANK, s["ckv"] // KV_RANK)], [p["gkvl"]],
                   [(KV_RANK, BF16)])[0]
    qraw = mm(cnq, p["Wuq"], "nn", name="q_up", layer=l)
    kn = mm(cnkv, p["Wk"], "nn", name="k_up", layer=l)
    v5 = mm(cnkv, p["Wv"], "nn", name="v_up", out_dtype=BF16, layer=l)
    qrot = rowwise("q_post", fn_qpost, [(qraw, LANE, head), (cos_t, LANE, 0), (sin_t, LANE, 0)], [p["gq"]],
                   [(LANE, BF16)], nj=N_HEADS)[0]
    krot = rowwise("k_post", fn_kpost, [(kn, LANE, head), (proj, LANE, sm), (cos_t, LANE, 0), (sin_t, LANE, 0)],
                   [p["gk"]], [(LANE, BF16)], nj=N_HEADS)[0]
    omla = attention("mla_attn", qrot, krot, v5, B=B, S=S, ntile=N_HEADS, hpt=1, dk=QK_DIM)
    fq0, fk0 = s["fq"] // LANE, s["fk"] // LANE
    qf = rowwise("foxq_norm", fn_foxnorm, [(proj, LANE, lambda j: fq0 + j)], [p["gfq"]], [(LANE, BF16)], nj=2)[0]
    kf = rowwise("foxk_norm", fn_foxnorm, [(proj, LANE, lambda j: fk0 + j)], [p["gfk"]], [(LANE, BF16)], nj=2)[0]
    lf = rowwise("fgate", fn_fgate, [(proj, LANE, sm)], [p["fbias"]], [(LANE, F32)])[0]
    cum = seq_cumsum("fox_cumsum", lf, B=B, S=S, reverse=False)
    ckt = _pad_rows8(jnp.transpose(cum.reshape(B, S, LANE)[:, :, F_LANE0:F_LANE0 + N_HEADS], (0, 2, 1)))
    ofox = attention("fox_attn", qf, kf, proj, B=B, S=S, ntile=2, hpt=2, dk=FOX_DIM, vc=s["fv"] // LANE,
                     cum=cum, ckt=ckt)
    lam, bre, bim, ctop, cbot = _s5_mats(p)
    u16 = proj[:, s["u"]:s["u"] + BW].astype(BF16)
    bur = mm(u16, bre, "nn", name="s5_bu_re")
    bui = mm(u16, bim, "nn", name="s5_bu_im")
    sr, si = s5_scan("s5_scan", bur, bui, lam, B=B, S=S, reverse=False)
    ypre = mm(si, cbot, "nn", name="s5_y_im", add=mm(sr, ctop, "nn", name="s5_y_re"))
    os5 = rowwise("s5_post", fn_s5post, [(ypre, BW, 0), (proj, BW, s["u"] // BW)],
                  [p["s5d"], p["Wglu"], p["bglu"]], [(BW, BF16)])[0]
    merged = rowwise("merge", fn_merge,
                     [(omla, N_HEADS * LANE, 0), (ofox, BW, 0), (os5, BW, 0), (proj, D, 0), (proj, D, 1), (proj, D, 2)],
                     [p["Wb0"], p["Wb1"], p["Wb2"]], [(D, BF16)])[0]
    xmid = mm(merged, p["Wout"], "nn", name="out_proj", add=x, layer=l)
    h2 = rowwise("rms_ffn", fn_rms(D), [(xmid, D, 0)], [p["g2"]], [(D, BF16)])[0]
    up = mm(h2, p["Wup"], "nn", name="ffn_up", layer=l)
    act = conv_gate_fwd(up, p["convw"], B=B, S=S)
    xout = mm(act, p["Wdown"], "nn", name="ffn_down", add=xmid, layer=l)
    saved = dict(x=x, h=h, proj=proj, cnq=cnq, cnkv=cnkv, qraw=qraw, kn=kn, v5=v5, qrot=qrot, krot=krot, qf=qf,
                 kf=kf, cum=cum, ckt=ckt, omla=omla, ofox=ofox, os5=os5, u16=u16, sr=sr, si=si, ypre=ypre,
                 merged=merged, xmid=xmid, h2=h2, up=up, act=act)
    return xout, saved


def _pad_rows8(a):
    return jnp.pad(a, ((0, 0), (0, 8 - a.shape[1]), (0, 0)))


STACKED = ("Wp", "Wout", "Wup", "Wdown")


def layer_bwd(dx, p, l, sv, cos_t, sin_t, B, S, stacks):
    M, D = dx.shape
    slot = lambda k: (stacks["L"], l, stacks.get(k))
    s = _seg(D)
    sm = s["small"] // LANE
    head = lambda j: j
    proj = sv["proj"]
    dact = mm(dx, p["WdownT"], "nn", name="ffn_down_dx", layer=l)
    d_wdown = mm(sv["act"], dx, "tn", name="ffn_down_dw", slot=slot("Wdown"))
    dup, d_convw = conv_gate_bwd(sv["up"], p["convw"], dact, B=B, S=S)
    dh2 = mm(dup, p["WupT"], "nn", name="ffn_up_dx", layer=l)
    d_wup = mm(sv["h2"], dup, "tn", name="ffn_up_dw", slot=slot("Wup"))
    (dxmid,), (d_g2,) = rowwise_vjp("rms_ffn_vjp", fn_rms_res(D), [(sv["xmid"], D, 0)], [p["g2"]],
                                    [(dh2, D, 0), (dx, D, 0)], [0])
    dmerged = mm(dxmid, p["WoutT"], "nn", name="out_proj_dx", layer=l)
    d_wout = mm(sv["merged"], dxmid, "tn", name="out_proj_dw", slot=slot("Wout"))
    (dom, dof, dos, dg0, dg1, dg2), (d_wb0, d_wb1, d_wb2) = rowwise_vjp(
        "merge_vjp", fn_merge,
        [(sv["omla"], N_HEADS * LANE, 0), (sv["ofox"], BW, 0), (sv["os5"], BW, 0), (proj, D, 0), (proj, D, 1),
         (proj, D, 2)], [p["Wb0"], p["Wb1"], p["Wb2"]], [(dmerged, D, 0)], [0, 1, 2, 3, 4, 5], tm=256,
        gdt=[BF16, BF16, F32, BF16, BF16, BF16])
    lam, bre, bim, ctop, cbot = _s5_mats(p)
    (dypre, du_a), (d_s5d, d_wglu, d_bglu) = rowwise_vjp(
        "s5_post_vjp", fn_s5post, [(sv["ypre"], BW, 0), (proj, BW, s["u"] // BW)],
        [p["s5d"], p["Wglu"], p["bglu"]], [(dos, BW, 0)], [0, 1], gdt=[BF16, F32])
    dsr = mm(dypre, ctop.T, "nn", name="s5_y_re_dx")
    dsi = mm(dypre, cbot.T, "nn", name="s5_y_im_dx")
    d_ctop = mm(sv["sr"], dypre, "tn", name="s5_y_re_dw")
    d_cbot = mm(sv["si"], dypre, "tn", name="s5_y_im_dw")
    gr, gi, dl_r, dl_i = s5_scan("s5_scan_vjp", dsr, dsi, lam, B=B, S=S, reverse=True, state=(sv["sr"], sv["si"]))
    du = mm(gi, bim.T, "nn", name="s5_bu_im_dx", out_dtype=BF16,
            add=mm(gr, bre.T, "nn", name="s5_bu_re_dx", add=du_a))
    d_bre = mm(sv["u16"], gr, "tn", name="s5_bu_re_dw")
    d_bim = mm(sv["u16"], gi, "tn", name="s5_bu_im_dw")
    d_s5 = s5_params_vjp({k: p[k] for k in S5_PARAM_NAMES}, dl_r, dl_i, d_bre, d_bim, d_ctop, d_cbot)
    dqf, dkf, dfv, dcq_t, dckt_t = attention("fox_attn_vjp", sv["qf"], sv["kf"], proj, B=B, S=S, ntile=2, hpt=2,
                                             dk=FOX_DIM, vc=s["fv"] // LANE, cum=sv["cum"], ckt=sv["ckt"], do=dof)
    dck = dckt_t.reshape(B, 2, 8, S)
    dck = jnp.transpose(dck[:, 0, :N_HEADS] + dck[:, 1, :N_HEADS], (0, 2, 1)).reshape(M, N_HEADS)
    dcum = rowwise("fox_dcum", lambda a, b: (a[:, 0:LANE] + a[:, LANE:2 * LANE] + b,),
                   [(dcq_t, 2 * LANE, 0), (_place(dck, F_LANE0), LANE, 0)], [], [(LANE, F32)])[0]
    dlf = seq_cumsum("fox_cumsum_vjp", dcum, B=B, S=S, reverse=True)
    (dsmall_f,), (d_fbias,) = rowwise_vjp("fgate_vjp", fn_fgate, [(proj, LANE, sm)], [p["fbias"]],
                                          [(dlf, LANE, 0)], [0])
    fq0, fk0 = s["fq"] // LANE, s["fk"] // LANE
    (dfq,), (d_gfq,) = rowwise_vjp("foxq_norm_vjp", fn_foxnorm, [(proj, LANE, lambda j: fq0 + j)], [p["gfq"]],
                                   [(dqf, LANE, head)], [0], nj=2, gdt=[BF16])
    (dfk,), (d_gfk,) = rowwise_vjp("foxk_norm_vjp", fn_foxnorm, [(proj, LANE, lambda j: fk0 + j)], [p["gfk"]],
                                   [(dkf, LANE, head)], [0], nj=2, gdt=[BF16])
    dqrot, dkrot, dv5 = attention("mla_attn_vjp", sv["qrot"], sv["krot"], sv["v5"], B=B, S=S, ntile=N_HEADS,
                                  hpt=1, dk=QK_DIM, do=dom)
    (dqraw,), (d_gq,) = rowwise_vjp("q_post_vjp", fn_qpost,
                                    [(sv["qraw"], LANE, head), (cos_t, LANE, 0), (sin_t, LANE, 0)], [p["gq"]],
                                    [(dqrot, LANE, head)], [0], nj=N_HEADS, gdt=[BF16])
    (dkn, dsmall_k), (d_gk,) = rowwise_vjp(
        "k_post_vjp", fn_kpost, [(sv["kn"], LANE, head), (proj, LANE, sm), (cos_t, LANE, 0), (sin_t, LANE, 0)],
        [p["gk"]], [(dkrot, LANE, head)], [0, 1], nj=N_HEADS, gdt=[BF16, F32])
    dsmall = rowwise("small_sum", fn_add5, [(dsmall_k, N_HEADS * LANE, 0), (dsmall_f, LANE, 0)], [], [(LANE, BF16)])[0]
    dcnq = mm(dqraw, p["WuqT"], "nn", name="q_up_dx", layer=l)
    d_wuq = mm(sv["cnq"], dqraw, "tn", name="q_up_dw")
    dcnkv = mm(dv5, p["WvT"], "nn", name="v_up_dx", layer=l, add=mm(dkn, p["WkT"], "nn", name="k_up_dx", layer=l))
    d_wk = mm(sv["cnkv"], dkn, "tn", name="k_up_dw")
    d_wv = mm(sv["cnkv"], dv5, "tn", name="v_up_dw")
    (dcq,), (d_gql,) = rowwise_vjp("latq_norm_vjp", fn_rms(Q_RANK), [(proj, Q_RANK, s["cq"] // Q_RANK)], [p["gql"]],
                                   [(dcnq, Q_RANK, 0)], [0], gdt=[BF16])
    (dckv,), (d_gkvl,) = rowwise_vjp("latkv_norm_vjp", fn_rms(KV_RANK), [(proj, KV_RANK, s["ckv"] // KV_RANK)],
                                     [p["gkvl"]], [(dcnkv, KV_RANK, 0)], [0], gdt=[BF16])
    dproj = jnp.concatenate([dg0, dg1, dg2, dckv, dfq, dfk, dfv.astype(BF16), du, dsmall,
                             jnp.zeros((M, LANE), BF16), dcq], axis=1)
    dh = mm(dproj, p["WpT"], "nn", name="in_proj_dx", layer=l)
    d_wp = mm(sv["h"], dproj, "tn", name="in_proj_dw", slot=slot("Wp"))
    (dxin,), (d_g1,) = rowwise_vjp("rms_attn_vjp", fn_rms_res(D), [(sv["x"], D, 0)], [p["g1"]],
                                   [(dh, D, 0), (dxmid, D, 0)], [0])
    grads = dict(g1=d_g1, Wp=d_wp, gql=d_gql, gkvl=d_gkvl, Wuq=d_wuq, Wk=d_wk, Wv=d_wv, gq=d_gq, gk=d_gk,
                 gfq=d_gfq, gfk=d_gfk, fbias=d_fbias, s5d=d_s5d, Wglu=d_wglu, bglu=d_bglu, Wb0=d_wb0, Wb1=d_wb1,
                 Wb2=d_wb2, Wout=d_wout, g2=d_g2, Wup=d_wup, convw=d_convw, Wdown=d_wdown)
    grads.update(dict(zip(S5_PARAM_NAMES, d_s5)))
    return dxin, grads


def local_step(x, positions, target, w):
    B, S, D = x.shape
    M = B * S
    P = with_transposes(prep_weights(w))
    L = P["g1"].shape[0]
    cos_t, sin_t = rope_tables(positions)
    xc, saved = x.reshape(M, D), []
    for l in range(L):
        xc, sv = layer_fwd(xc, layer_params(P, l), l, cos_t, sin_t, B, S)
        saved.append(sv)
    dxc, sq = loss_head(xc, target.reshape(M, D))
    grads, stacks = [None] * L, {"L": L}
    for l in reversed(range(L)):
        dxc, grads[l] = layer_bwd(dxc, layer_params(P, l), l, saved[l], cos_t, sin_t, B, S, stacks)
        stacks.update({k: grads[l][k] for k in STACKED})
    G = {k: (stacks[k] if k in STACKED else jnp.stack([g[k] for g in grads])) for k in grads[0]}
    return sq, dxc.reshape(B, S, D), unprep_grads(G, D)


MESH = pl.DeviceIdType.MESH
ANY = pl.BlockSpec(memory_space=pl.ANY)
N_CHIPS = 4
SHARDED = ("w_in", "w_uq", "w_ukv", "s5_w_glu", "w_branch", "w_out", "w_up", "ffn_conv_w", "w_down")
MINOR = ("w_branch", "w_up", "ffn_conv_w")
F32_TRAVEL = ("ffn_conv_w",)
WEIGHTS = ("attn_norm_g", "w_in", "q_lat_norm_g", "w_uq", "kv_lat_norm_g", "w_ukv", "mla_q_norm_g", "mla_k_norm_g",
           "fox_q_norm_g", "fox_k_norm_g", "fox_f_bias", "s5_lambda_re", "s5_lambda_im", "s5_b_re", "s5_b_im",
           "s5_c_re", "s5_c_im", "s5_d", "s5_log_step", "s5_w_glu", "s5_b_glu", "w_branch", "w_out", "ffn_norm_g",
           "w_up", "ffn_conv_w", "w_down")
SMALL = tuple(n for n in WEIGHTS if n not in SHARDED)
RS_BIG = tuple(n for n in SHARDED if n != "ffn_conv_w")
REDUCE_SMALL = SMALL + ("ffn_conv_w",)


def shard3(name, a):
    L = a.shape[0]
    if name in ("w_uq", "w_ukv"):
        return a.reshape(L, a.shape[1], -1)
    if name == "w_branch":
        return a.reshape(L, -1, a.shape[-1])
    return a


def full4(name, a):
    L = a.shape[0]
    if name == "w_in":
        return a
    if name in MINOR:
        return a.reshape(L, 1, -1, a.shape[-1])
    a = a.reshape(L, a.shape[1], -1)
    return a.reshape(L, N_CHIPS, a.shape[1] // N_CHIPS, a.shape[2])


def from_full4(name, a, ref_tail):
    L = a.shape[0]
    if name == "w_in":
        return a
    return a.reshape((L,) + tuple(ref_tail))


def _where():
    x, y, c = lax.axis_index("x"), lax.axis_index("y"), lax.axis_index("c")
    chips = [(1 - x, y), (x, 1 - y), (1 - x, 1 - y)]
    return (x, y, c), 2 * x + y, (x, y, 1 - c), chips, [2 * cx + cy for cx, cy in chips]


def _view(minor, ref4, layers, k):
    if minor:
        cs = ref4.shape[3] // N_CHIPS
        return ref4.at[layers, 0, :, pl.ds(pl.multiple_of(k * cs, LANE), cs)]
    return ref4.at[layers, k]


def _remote(src, dst, ssem, rsem, dev):
    return pltpu.make_async_remote_copy(src_ref=src, dst_ref=dst, send_sem=ssem, recv_sem=rsem,
                                        device_id=dev, device_id_type=MESH)


def gather_weights(shards, minor):
    n = len(shards)
    L = shards[0].shape[0]
    Lh = L // 2
    out_shape = []
    for a, mn in zip(shards, minor):
        _, r, cs = a.shape
        out_shape.append(jax.ShapeDtypeStruct((L, 1, r, N_CHIPS * cs) if mn else (L, N_CHIPS, r, cs), a.dtype))

    def body(*refs):
        w, g = refs[:n], refs[n:2 * n]
        send, recv = refs[2 * n:]
        (x, y, c), me, sib, chips, cidx = _where()
        mine, other, every = pl.ds(c * Lh, Lh), pl.ds((1 - c) * Lh, Lh), pl.ds(0, L)
        dst = lambda i, layers, k: _view(minor[i], g[i], layers, k)
        local = [_remote(w[i], dst(i, every, me), send.at[i, 6], recv.at[i, 6], sib) for i in range(n)]
        first = [_remote(w[i].at[mine], dst(i, mine, me), send.at[i, j], recv.at[i, j], (*chips[j], c))
                 for i in range(n) for j in range(3)]
        for cp in local + first:
            cp.start()
        passed = []
        for i in range(n):
            for j in range(3):
                blk = dst(i, mine, cidx[j])
                _remote(blk, blk, send.at[i, j], recv.at[i, j], (*chips[j], c)).wait_recv()
                fwd = _remote(blk, blk, send.at[i, 3 + j], recv.at[i, 3 + j], sib)
                fwd.start()
                passed.append(fwd)
        for i in range(n):
            for j in range(3):
                blk = dst(i, other, cidx[j])
                _remote(blk, blk, send.at[i, 3 + j], recv.at[i, 3 + j], sib).wait_recv()
        for cp in first + passed:
            cp.wait_send()
        for cp in local:
            cp.wait()

    return pl.pallas_call(
        body, name="gather_weights", in_specs=[ANY] * n, out_specs=[ANY] * n, out_shape=out_shape,
        scratch_shapes=[pltpu.SemaphoreType.DMA((n, 7)), pltpu.SemaphoreType.DMA((n, 7))],
    )(*shards)


def sibling_halves(grads):
    n = len(grads)
    L = grads[0].shape[0]
    Lh = L // 2
    half = [jax.ShapeDtypeStruct((Lh,) + a.shape[1:], a.dtype) for a in grads]

    def body(*refs):
        g, got = refs[:n], refs[n:2 * n]
        send, recv = refs[2 * n:]
        (x, y, c), me, sib, chips, cidx = _where()
        other = pl.ds((1 - c) * Lh, Lh)
        out = [_remote(g[i].at[other], got[i], send.at[i], recv.at[i], sib) for i in range(n)]
        for cp in out:
            cp.start()
        for cp in out:
            cp.wait()

    got = pl.pallas_call(
        body, name="grad_sibling_halves", in_specs=[ANY] * n, out_specs=[ANY] * n, out_shape=half,
        scratch_shapes=[pltpu.SemaphoreType.DMA((n,)), pltpu.SemaphoreType.DMA((n,))],
    )(*grads)
    return got


def scatter_chip_sums(travel, minor):
    n = len(travel)
    Lh = travel[0].shape[0]
    got_shape = []
    for t, mn in zip(travel, minor):
        r, cs = t.shape[2], (t.shape[3] // N_CHIPS if mn else t.shape[3])
        got_shape.append(jax.ShapeDtypeStruct((3, Lh, r, cs), t.dtype))

    def body(*refs):
        s16, got = refs[:n], refs[n:2 * n]
        send, recv = refs[2 * n:]
        (x, y, c), me, sib, chips, cidx = _where()
        every = pl.ds(0, Lh)
        out = [_remote(_view(minor[i], s16[i], every, cidx[j]), got[i].at[j], send.at[i, j], recv.at[i, j],
                       (*chips[j], c)) for i in range(n) for j in range(3)]
        for cp in out:
            cp.start()
        for cp in out:
            cp.wait()

    got = pl.pallas_call(
        body, name="grad_scatter", in_specs=[ANY] * n, out_specs=[ANY] * n, out_shape=got_shape,
        scratch_shapes=[pltpu.SemaphoreType.DMA((n, 3)), pltpu.SemaphoreType.DMA((n, 3))],
    )(*travel)
    return got


def share_halves(shards):
    n = len(shards)
    Lh = shards[0].shape[0] // 2

    def body(*refs):
        full = refs[n:2 * n]
        send, recv = refs[2 * n:]
        (x, y, c), me, sib, chips, cidx = _where()
        mine, other = pl.ds(c * Lh, Lh), pl.ds((1 - c) * Lh, Lh)
        out = [_remote(full[i].at[mine], full[i].at[mine], send.at[i], recv.at[i], sib) for i in range(n)]
        for cp in out:
            cp.start()
        for i in range(n):
            out[i].wait_send()
            _remote(full[i].at[other], full[i].at[other], send.at[i], recv.at[i], sib).wait_recv()

    return pl.pallas_call(
        body, name="grad_share_halves", in_specs=[ANY] * n, out_specs=[ANY] * n,
        out_shape=[jax.ShapeDtypeStruct(a.shape, a.dtype) for a in shards],
        input_output_aliases={i: i for i in range(n)},
        scratch_shapes=[pltpu.SemaphoreType.DMA((n,)), pltpu.SemaphoreType.DMA((n,))],
    )(*shards)


N_DEV = 8


def allreduce_small(v):
    R = v.shape[0]

    def body(x_ref, sum_ref, all_ref, send, recv, loc):
        (x, y, c), me, sib, chips, cidx = _where()

        def rows(px, py, pc):
            return all_ref.at[4 * px + 2 * py + pc]

        def copy(k, block, to, src=None):
            return _remote(rows(*block) if src is None else src, rows(*block), send.at[k], recv.at[k], to)

        mine = pltpu.make_async_copy(x_ref, rows(x, y, c), loc)
        mine.start()
        first = [copy(0, (x, y, c), sib, src=x_ref)]
        first += [copy(1 + j, (x, y, c), (*chip, c), src=x_ref) for j, chip in enumerate(chips)]
        for cp in first:
            cp.start()
        passed = [copy(4 + j, (*chip, c), sib) for j, chip in enumerate(chips)]
        for j, chip in enumerate(chips):
            copy(1 + j, (*chip, c), (x, y, c)).wait_recv()
            passed[j].start()
        copy(0, (x, y, 1 - c), (x, y, c)).wait_recv()
        for j, chip in enumerate(chips):
            copy(4 + j, (*chip, 1 - c), (x, y, c)).wait_recv()
        for cp in first + passed:
            cp.wait_send()
        mine.wait()
        acc = all_ref[0]
        for d in range(1, N_DEV):
            acc = acc + all_ref[d]
        sum_ref[...] = acc

    vm = pl.BlockSpec(memory_space=pltpu.VMEM)
    return pl.pallas_call(
        body, name="allreduce_small", in_specs=[vm], out_specs=[vm, vm],
        out_shape=[jax.ShapeDtypeStruct((R, LANE), F32), jax.ShapeDtypeStruct((N_DEV, R, LANE), F32)],
        scratch_shapes=[pltpu.SemaphoreType.DMA((7,)), pltpu.SemaphoreType.DMA((7,)), pltpu.SemaphoreType.DMA],
        compiler_params=pltpu.CompilerParams(vmem_limit_bytes=VMEM_LIMIT),
    )(v)[0]


EW_BLOCK_BYTES = 2 << 20


def _ew_rows(rows, cols):
    for tr in (1024, 512, 256, 128, 64, 32, 16, 8):
        if rows % tr == 0 and tr * cols * 4 <= EW_BLOCK_BYTES:
            return tr
    return rows


def add_pair(name, full, got, where, travel_dtype):
    R, C = got.shape
    tr = _ew_rows(R, C)
    nblk = R // tr

    def body(w_ref, a_ref, b_ref, s_ref, t_ref):
        s = a_ref[...] + b_ref[...]
        s_ref[...] = s
        t_ref[...] = s.astype(t_ref.dtype)

    blk = pl.BlockSpec((tr, C), lambda i, w: (i, 0))
    return pl.pallas_call(
        body, name=name,
        grid_spec=pltpu.PrefetchScalarGridSpec(
            num_scalar_prefetch=1, grid=(nblk,),
            in_specs=[pl.BlockSpec((tr, C), lambda i, w: (w[1] * nblk + i, 0)), blk], out_specs=[blk, blk]),
        out_shape=[jax.ShapeDtypeStruct((R, C), F32), jax.ShapeDtypeStruct((R, C), travel_dtype)],
        compiler_params=_cp("parallel"),
    )(where, full, got)


def add_four(name, sums, got, where, minor):
    Lh, _, r, C = sums.shape
    cs = got.shape[-1]
    tr = _ew_rows(r, cs)

    def body(w_ref, m_ref, g0, g1, g2, o_ref):
        o_ref[...] = ((m_ref[...] + g0[...].astype(F32)) + g1[...].astype(F32)) + g2[...].astype(F32)

    mine = (pl.BlockSpec((None, None, tr, cs), lambda l, i, w: (l, 0, i, w[0])) if minor
            else pl.BlockSpec((None, None, tr, cs), lambda l, i, w: (l, w[0], i, 0)))
    slot = lambda j: pl.BlockSpec((None, None, tr, cs), lambda l, i, w: (j, l, i, 0))
    return pl.pallas_call(
        body, name=name,
        grid_spec=pltpu.PrefetchScalarGridSpec(
            num_scalar_prefetch=1, grid=(Lh, r // tr), in_specs=[mine, slot(0), slot(1), slot(2)],
            out_specs=pl.BlockSpec((None, tr, cs), lambda l, i, w: (w[1] * Lh + l, i, 0))),
        out_shape=jax.ShapeDtypeStruct((2 * Lh, r, cs), F32), compiler_params=_cp("parallel", "parallel"),
    )(where, sums, got, got, got)


def reduce_scatter_grads(full_grads):
    names = list(RS_BIG)
    minor = [nm in MINOR for nm in names]
    where = jnp.stack([2 * lax.axis_index("x") + lax.axis_index("y"), lax.axis_index("c")]).astype(jnp.int32)
    g4 = [full4(nm, full_grads[nm]) for nm in names]
    got = sibling_halves(g4)
    sums, travel = [], []
    for nm, a, b in zip(names, g4, got):
        s, t = add_pair("chip_sum_" + nm, a.reshape(-1, a.shape[-1]), b.reshape(-1, b.shape[-1]), where,
                        F32 if nm in F32_TRAVEL else BF16)
        sums.append(s.reshape(b.shape))
        travel.append(t.reshape(b.shape))
    arrived = scatter_chip_sums(travel, minor)
    shards = [add_four("shard_sum_" + nm, s, b, where, mn) for nm, s, b, mn in zip(names, sums, arrived, minor)]
    return dict(zip(names, share_halves(shards)))


def pack_small(tree, extra=None, names=SMALL):
    parts = [tree[nm].reshape(-1) for nm in names]
    parts.append(jnp.zeros((1,), F32) if extra is None else extra.reshape(-1))
    blocks = []
    for p in parts:
        rows = _small_rows(p.shape[0])
        blocks.append(jnp.pad(p, (0, rows * LANE - p.shape[0])).reshape(rows, LANE))
    return jnp.concatenate(blocks, axis=0)


def _small_rows(size):
    return -(-size // (8 * LANE)) * 8


def unpack_small(packed, like, names=SMALL):
    out, at = {}, 0
    for nm in names:
        size = math.prod(like[nm].shape)
        rows = _small_rows(size)
        out[nm] = packed[at:at + rows].reshape(-1)[:size].reshape(like[nm].shape)
        at += rows
    return out, packed[at, 0]


def kernel(x, positions, attn_norm_g, w_in, q_lat_norm_g, w_uq, kv_lat_norm_g, w_ukv, mla_q_norm_g, mla_k_norm_g, fox_q_norm_g, fox_k_norm_g, fox_f_bias, s5_lambda_re, s5_lambda_im, s5_b_re, s5_b_im, s5_c_re, s5_c_im, s5_d, s5_log_step, s5_w_glu, s5_b_glu, w_branch, w_out, ffn_norm_g, w_up, ffn_conv_w, w_down, loss_target, m_attn_norm_g, m_w_in, m_q_lat_norm_g, m_w_uq, m_kv_lat_norm_g, m_w_ukv, m_mla_q_norm_g, m_mla_k_norm_g, m_fox_q_norm_g, m_fox_k_norm_g, m_fox_f_bias, m_s5_lambda_re, m_s5_lambda_im, m_s5_b_re, m_s5_b_im, m_s5_c_re, m_s5_c_im, m_s5_d, m_s5_log_step, m_s5_w_glu, m_s5_b_glu, m_w_branch, m_w_out, m_ffn_norm_g, m_w_up, m_ffn_conv_w, m_w_down, v_attn_norm_g, v_w_in, v_q_lat_norm_g, v_w_uq, v_kv_lat_norm_g, v_w_ukv, v_mla_q_norm_g, v_mla_k_norm_g, v_fox_q_norm_g, v_fox_k_norm_g, v_fox_f_bias, v_s5_lambda_re, v_s5_lambda_im, v_s5_b_re, v_s5_b_im, v_s5_c_re, v_s5_c_im, v_s5_d, v_s5_log_step, v_s5_w_glu, v_s5_b_glu, v_w_branch, v_w_out, v_ffn_norm_g, v_w_up, v_ffn_conv_w, v_w_down):
    given = dict(locals())
    w = {nm: given[nm] for nm in WEIGHTS}
    m = {nm: given["m_" + nm] for nm in WEIGHTS}
    v = {nm: given["v_" + nm] for nm in WEIGHTS}
    D = x.shape[-1]

    minor = [nm in MINOR for nm in SHARDED]
    shards = [shard3(nm, w[nm]).astype(F32 if nm in F32_TRAVEL else BF16) for nm in SHARDED]
    gathered = gather_weights(shards, minor)
    full = dict(w)
    for nm, g4 in zip(SHARDED, gathered):
        tail = list(w[nm].shape[1:])
        axis = (len(tail) - 1) if nm in MINOR or nm == "w_in" else 0
        tail[axis] *= N_CHIPS
        full[nm] = from_full4(nm, g4, tail)

    sq, grad_x, gw = local_step(x, positions, loss_target, full)

    big = reduce_scatter_grads(gw)
    total, sq_sum = unpack_small(allreduce_small(pack_small(gw, sq[0:1, 0:1], REDUCE_SMALL)), gw, REDUCE_SMALL)
    loss = 0.5 * sq_sum / D
    conv_cols = w["ffn_conv_w"].shape[-1]
    chip = 2 * lax.axis_index("x") + lax.axis_index("y")
    big["ffn_conv_w"] = lax.dynamic_slice_in_dim(total.pop("ffn_conv_w"), chip * conv_cols, conv_cols, axis=2)

    grads, delta, new_m, new_v = {}, {}, {}, {}
    for nm in SHARDED:
        g = big[nm].reshape(shard3(nm, w[nm]).shape)
        two = lambda a: shard3(nm, a).reshape(-1, g.shape[-1])
        d2, m2, v2, g2 = adamw("adamw_" + nm, two(w[nm]), g.reshape(-1, g.shape[-1]), two(m[nm]), two(v[nm]))
        grads[nm], delta[nm], new_m[nm], new_v[nm] = (a.reshape(w[nm].shape) for a in (g2, d2, m2, v2))
    d2, m2, v2, _ = adamw("adamw_small", pack_small(w), pack_small(total), pack_small(m), pack_small(v))
    for tree, packed in ((delta, d2), (new_m, m2), (new_v, v2)):
        tree.update(unpack_small(packed, w)[0])
    grads.update(total)
    return (loss, grad_x, *[grads[nm] for nm in WEIGHTS], *[delta[nm] for nm in WEIGHTS],
            *[new_m[nm] for nm in WEIGHTS], *[new_v[nm] for nm in WEIGHTS])
```

```python
import functools
import math

import jax
import jax.numpy as jnp
from jax import lax
from jax.experimental import pallas as pl
from jax.experimental.pallas import tpu as pltpu

F32, BF16 = jnp.float32, jnp.bfloat16
NORM_EPS = 1e-6
NEG_INF = -1e30
ROPE_THETA = 10000.0
LANE = 128
N_HEADS = 4
NOPE, ROPE, QK_DIM, V_DIM = 64, 32, 96, 64
Q_RANK, KV_RANK = 384, 256
FOX_DIM = 64
S5_G, S5_H, S5_P = 16, 16, 64
S5_C = S5_G * S5_P
BW = 256
VMEM_LIMIT = 56 << 20
ADAM_LR, ADAM_B1, ADAM_B2, ADAM_EPS, ADAM_WD, ADAM_STEP = 0.001, 0.9, 0.999, 1e-08, 0.01, 10


def _pick(n, cands):
    for c in cands:
        if n % c == 0:
            return c
    return n


def _cp(*sem):
    return pltpu.CompilerParams(dimension_semantics=sem, vmem_limit_bytes=VMEM_LIMIT)


def _dg(a, b, ca, cb):
    return lax.dot_general(a.astype(BF16), b.astype(BF16), (((ca,), (cb,)), ((), ())),
                           preferred_element_type=F32)


@jax.custom_vjp
def dot_nn(a, b):
    return _dg(a, b, 1, 0)


@jax.custom_vjp
def dot_nt(a, b):
    return _dg(a, b, 1, 1)


@jax.custom_vjp
def dot_tn(a, b):
    return _dg(a, b, 0, 0)


dot_nn.defvjp(lambda a, b: (dot_nn(a, b), (a, b)),
              lambda r, g: (dot_nt(g, r[1]).astype(r[0].dtype), dot_tn(r[0], g).astype(r[1].dtype)))
dot_nt.defvjp(lambda a, b: (dot_nt(a, b), (a, b)),
              lambda r, g: (dot_nn(g, r[1]).astype(r[0].dtype), dot_tn(g, r[0]).astype(r[1].dtype)))
dot_tn.defvjp(lambda a, b: (dot_tn(a, b), (a, b)),
              lambda r, g: (dot_nt(r[1], g).astype(r[0].dtype), dot_nn(r[0], g).astype(r[1].dtype)))


def xdot(a, b):
    return jnp.dot(a, b, precision=lax.Precision.HIGHEST, preferred_element_type=F32)


MM_VMEM_BUDGET = 42 << 20
MM_STEP_S, MM_HBM_BPS, MM_VMEM_BPS = 0.4e-6, 2.5e12, 3e12


def _divisors(n, cands):
    return sorted({c for c in cands if n % c == 0} | {n}, reverse=True)


def _mm_tiles(M, K, N, ab, bb, ob, addb):
    best = None
    for tm in _divisors(M, (2048, 1024, 512, 256, 128)):
        for tn in _divisors(N, (2048, 1664, 1536, 1408, 1280, 1024, 768, 640, 512, 384, 256, 128)):
            for tk in _divisors(K, (4096, 2048, 1664, 1536, 1408, 1024, 768, 512, 384, 256, 128)):
                vmem = 2 * (tm * tk * ab + tk * tn * bb + tm * tn * (ob + addb)) + (tm * tn * 4 if tk != K else 0)
                if vmem > MM_VMEM_BUDGET:
                    continue
                nk = K // tk
                steps = (M // tm) * (N // tn) * nk
                traffic = M * K * ab * (N // tn) + K * N * bb * (M // tm) + M * N * (ob + addb)
                cost = steps * MM_STEP_S + traffic / MM_HBM_BPS + (M * N * 8 * nk / MM_VMEM_BPS if nk > 1 else 0)
                if best is None or cost < best[0]:
                    best = (cost, tm, tn, tk)
    assert best is not None, (M, K, N)
    return best[1:]


def mm(a, b, mode, *, name, add=None, out_dtype=F32, layer=None, slot=None):
    bk, bn = b.shape[-2:]
    if mode == "nn":
        (M, K), N = a.shape, bn
    else:
        (K, M), N = a.shape, bn
    assert bk == K, (name, a.shape, b.shape)
    isz = lambda x: jnp.dtype(x.dtype).itemsize
    tm, tn, tk = _mm_tiles(M, K, N, isz(a), isz(b), jnp.dtype(out_dtype).itemsize, 0 if add is None else isz(add))
    nk = K // tk
    ca = 1 if mode == "nn" else 0

    n_in = 2 + (add is not None) + (slot is not None and slot[2] is not None)

    def body(*refs):
        a_ref, b_ref = refs[:2]
        add_ref = refs[2] if add is not None else None
        o_ref = refs[n_in]

        def finish(r):
            if add is not None:
                r = r + add_ref[...].astype(F32)
            o_ref[...] = r.astype(out_dtype)

        part = _dg(a_ref[...], b_ref[...], ca, 0)
        if nk == 1:
            finish(part)
            return
        acc = refs[-1]
        kk = pl.program_id(2)

        @pl.when(kk == 0)
        def _():
            acc[...] = part

        @pl.when(kk > 0)
        def _():
            acc[...] += part

        @pl.when(kk == nk - 1)
        def _():
            finish(acc[...])

    a_spec = (pl.BlockSpec((tm, tk), lambda i, j, k: (i, k)) if mode == "nn"
              else pl.BlockSpec((tk, tm), lambda i, j, k: (k, i)))
    b_spec = (pl.BlockSpec((tk, tn), lambda i, j, k: (k, j)) if layer is None
              else pl.BlockSpec((None, tk, tn), lambda i, j, k: (layer, k, j)))
    in_specs, args = [a_spec, b_spec], [a, b]
    if add is not None:
        in_specs.append(pl.BlockSpec((tm, tn), lambda i, j, k: (i, j)))
        args.append(add)
    out_spec = pl.BlockSpec((tm, tn), lambda i, j, k: (i, j))
    out_shape = jax.ShapeDtypeStruct((M, N), out_dtype)
    aliases = {}
    if slot is not None:
        n_layers, l, buf = slot
        out_spec = pl.BlockSpec((None, tm, tn), lambda i, j, k: (l, i, j))
        out_shape = jax.ShapeDtypeStruct((n_layers, M, N), out_dtype)
        if buf is not None:
            aliases = {len(args): 0}
            in_specs.append(pl.BlockSpec(memory_space=pl.ANY))
            args.append(buf)
    return pl.pallas_call(
        body, name=name, grid=(M // tm, N // tn, nk),
        in_specs=in_specs, out_specs=out_spec, out_shape=out_shape, input_output_aliases=aliases,
        scratch_shapes=[pltpu.VMEM((tm, tn), F32)] if nk > 1 else [],
        compiler_params=_cp("parallel", "parallel", "arbitrary"),
    )(*args)


def _row_spec(tm, width, col):
    if callable(col):
        return pl.BlockSpec((tm, width), lambda i, j: (i, col(j)))
    return pl.BlockSpec((tm, width), lambda i, j: (i, col))


def _const_spec(c):
    return pl.BlockSpec(c.shape, lambda i, j: (0,) * c.ndim)


ROW_BLOCK_BYTES = 6 << 20


def _row_tile(M, widths, tm):
    if tm is None:
        tm = next((t for t in (1024, 512, 256) if t * sum(widths) * 4 <= ROW_BLOCK_BYTES), 128)
    return _pick(M, (tm, 256, 128, 64, 32, 16, 8))


def rowwise(name, fn, rows, consts, outs, *, tm=None, nj=1):
    M = rows[0][0].shape[0]
    tm = _row_tile(M, [w for _, w, _ in rows] + [w for w, _ in outs], tm)
    nr, nc = len(rows), len(consts)

    def body(*refs):
        vals = [r[...].astype(F32) for r in refs[:nr]] + [r[...] for r in refs[nr:nr + nc]]
        res = fn(*vals)
        for o_ref, r in zip(refs[nr + nc:], res):
            o_ref[...] = r.astype(o_ref.dtype)

    return pl.pallas_call(
        body, name=name, grid=(M // tm, nj),
        in_specs=[_row_spec(tm, w, c) for _, w, c in rows] + [_const_spec(c) for c in consts],
        out_specs=[pl.BlockSpec((tm, w), lambda i, j: (i, j)) for w, _ in outs],
        out_shape=[jax.ShapeDtypeStruct((M, nj * w), dt) for w, dt in outs],
        compiler_params=_cp("parallel", "parallel"),
    )(*[r[0] for r in rows], *consts)


def rowwise_vjp(name, fn, rows, consts, cts, diff, *, tm=None, nj=1, gdt=None):
    M = rows[0][0].shape[0]
    tm = _row_tile(M, [w for _, w, _ in rows] + [w for _, w, _ in cts] + [rows[p][1] for p in diff], tm)
    nr, nc, nt, nd = len(rows), len(consts), len(cts), len(diff)
    gdt = [F32] * nd if gdt is None else gdt

    def body(*refs):
        vals = [r[...].astype(F32) for r in refs[:nr + nc + nt]]
        rv, cv, tv = vals[:nr], vals[nr:nr + nc], vals[nr + nc:]
        grow, gconst = refs[nr + nc + nt:nr + nc + nt + nd], refs[nr + nc + nt + nd:]

        def f(*dargs):
            full = list(rv)
            for pos, val in zip(diff, dargs[:nd]):
                full[pos] = val
            return tuple(fn(*full, *dargs[nd:]))

        _, vjp = jax.vjp(f, *[rv[p] for p in diff], *cv)
        g = vjp(tuple(tv))
        for o_ref, gv in zip(grow, g[:nd]):
            o_ref[...] = gv.astype(o_ref.dtype)
        first = jnp.logical_and(pl.program_id(0) == 0, pl.program_id(1) == 0)
        for o_ref, gv in zip(gconst, g[nd:]):
            @pl.when(first)
            def _(o_ref=o_ref, gv=gv):
                o_ref[...] = gv

            @pl.when(jnp.logical_not(first))
            def _(o_ref=o_ref, gv=gv):
                o_ref[...] += gv

    out_specs = ([pl.BlockSpec((tm, rows[p][1]), lambda i, j: (i, j)) for p in diff]
                 + [_const_spec(c) for c in consts])
    out_shape = ([jax.ShapeDtypeStruct((M, nj * rows[p][1]), dt) for p, dt in zip(diff, gdt)]
                 + [jax.ShapeDtypeStruct(c.shape, F32) for c in consts])
    res = pl.pallas_call(
        body, name=name, grid=(M // tm, nj),
        in_specs=([_row_spec(tm, w, c) for _, w, c in rows] + [_const_spec(c) for c in consts]
                  + [_row_spec(tm, w, c) for _, w, c in cts]),
        out_specs=out_specs, out_shape=out_shape,
        compiler_params=_cp("arbitrary", "arbitrary"),
    )(*[r[0] for r in rows], *consts, *[t[0] for t in cts])
    return res[:nd], res[nd:]


def _lane(shape=(1, LANE)):
    return lax.broadcasted_iota(jnp.int32, shape, len(shape) - 1)


def _sigmoid(x):
    return 1.0 / (1.0 + jnp.exp(-x))


def _rms(x, g, n):
    return x * lax.rsqrt(jnp.sum(x * x, axis=-1, keepdims=True) * (1.0 / n) + NORM_EPS) * g


def fn_rms(n):
    return lambda x, g: (_rms(x, g, n),)


def _rope(x, cos_t, sin_t):
    i = lax.broadcasted_iota(jnp.int32, (LANE, LANE), 0)
    j = lax.broadcasted_iota(jnp.int32, (LANE, LANE), 1)
    half = ROPE // 2
    lo = jnp.logical_and(jnp.logical_and(j >= NOPE, j < NOPE + half), i == j + half)
    hi = jnp.logical_and(jnp.logical_and(j >= NOPE + half, j < NOPE + ROPE), i == j - half)
    perm = jnp.where(hi, 1.0, 0.0) - jnp.where(lo, 1.0, 0.0)
    return x * cos_t + xdot(x, perm) * sin_t


def fn_qpost(q, cos_t, sin_t, g):
    return (_rope(_rms(q, g, QK_DIM), cos_t, sin_t),)


def fn_kpost(kn, small, cos_t, sin_t, g):
    lane = _lane()
    rope_lanes = jnp.logical_and(lane >= NOPE, lane < NOPE + ROPE)
    kc = kn + jnp.where(rope_lanes, small, 0.0)
    return (_rope(_rms(kc, g, QK_DIM), cos_t, sin_t),)


def fn_foxnorm(x, g):
    first = _lane() < FOX_DIM
    sq = x * x
    s0 = jnp.sum(jnp.where(first, sq, 0.0), axis=-1, keepdims=True)
    s1 = jnp.sum(jnp.where(first, 0.0, sq), axis=-1, keepdims=True)
    r0 = lax.rsqrt(s0 * (1.0 / FOX_DIM) + NORM_EPS)
    r1 = lax.rsqrt(s1 * (1.0 / FOX_DIM) + NORM_EPS)
    return (x * jnp.where(first, r0, r1) * g,)


F_LANE0 = NOPE + ROPE


def fn_fgate(small, bias):
    z = small + bias
    lf = jnp.minimum(z, 0.0) - jnp.log(1.0 + jnp.exp(-jnp.abs(z)))
    lane = _lane()
    return (jnp.where(jnp.logical_and(lane >= F_LANE0, lane < F_LANE0 + N_HEADS), lf, 0.0),)


def _gelu(y):
    return 0.5 * y * (1.0 + jnp.tanh(math.sqrt(2.0 / math.pi) * (y + 0.044715 * (y * y * y))))


def fn_s5post(ypre, u, d, wglu, bglu):
    y = _gelu(ypre + d * u)
    return (y * _sigmoid(dot_nn(y, wglu) + bglu),)


def fn_merge(om, of, os_, g0, g1, g2, wb0, wb1, wb2):
    return (_sigmoid(g0) * dot_nn(om, wb0) + _sigmoid(g1) * dot_nn(of, wb1)
            + _sigmoid(g2) * dot_nn(os_, wb2),)


def fn_add5(a, b):
    return (a[:, 0:LANE] + a[:, LANE:2 * LANE] + a[:, 2 * LANE:3 * LANE] + a[:, 3 * LANE:4 * LANE] + b,)


def fn_addt(a, b):
    return (a + b,)


def fn_s5params(lre, lim, lstep, btr, bti, ctr, cti):
    C = S5_C
    grp = lax.broadcasted_iota(jnp.int32, (LANE, C), 1) >> 6
    expand = jnp.where(lax.broadcasted_iota(jnp.int32, (LANE, C), 0) == grp, 1.0, 0.0)
    lane = _lane()
    st = jnp.where(lane < S5_G, jnp.exp(lstep), 0.0)
    step = jnp.sum(xdot(jnp.broadcast_to(st, (8, LANE)), expand), axis=0, keepdims=True) * 0.125
    zr, zi = lre * step, lim * step
    er = jnp.exp(zr)
    lbr, lbi = er * jnp.cos(zi), er * jnp.sin(zi)
    den = lre * lre + lim * lim
    nr = lbr - 1.0
    cfr = (nr * lre + lbi * lim) / den
    cfi = (lbi * lre - nr * lim) / den
    bbr = cfr * btr - cfi * bti
    bbi = cfr * bti + cfi * btr
    rg = lax.broadcasted_iota(jnp.int32, (BW, C), 0) >> 4
    cg = lax.broadcasted_iota(jnp.int32, (BW, C), 1) >> 6
    mb = jnp.where(rg == cg, 1.0, 0.0)
    b_re = jnp.concatenate([bbr] * S5_G, axis=0) * mb
    b_im = jnp.concatenate([bbi] * S5_G, axis=0) * mb
    ecol = jnp.where(lax.broadcasted_iota(jnp.int32, (LANE, BW), 0)
                     == (lax.broadcasted_iota(jnp.int32, (LANE, BW), 1) & 15), 1.0, 0.0)
    mc = jnp.where((lax.broadcasted_iota(jnp.int32, (C, BW), 0) >> 6)
                   == (lax.broadcasted_iota(jnp.int32, (C, BW), 1) >> 4), 1.0, 0.0)
    c_top = xdot(ctr, ecol) * mc
    c_bot = -(xdot(cti, ecol) * mc)
    return lbr, lbi, b_re, b_im, c_top, c_bot


def s5_params(p):
    def body(lre, lim, ls, btr, bti, ctr, cti, lb_ref, bre_ref, bim_ref, ct_ref, cb_ref):
        lbr, lbi, b_re, b_im, c_top, c_bot = fn_s5params(
            lre[...], lim[...], ls[...], btr[...], bti[...], ctr[...], cti[...])
        lb_ref[0:1, :] = lbr
        lb_ref[1:2, :] = lbi
        bre_ref[...] = b_re.astype(BF16)
        bim_ref[...] = b_im.astype(BF16)
        ct_ref[...] = c_top.astype(BF16)
        cb_ref[...] = c_bot.astype(BF16)

    return pl.pallas_call(
        body, name="s5_params",
        out_shape=[jax.ShapeDtypeStruct((2, S5_C), F32), jax.ShapeDtypeStruct((BW, S5_C), BF16),
                   jax.ShapeDtypeStruct((BW, S5_C), BF16), jax.ShapeDtypeStruct((S5_C, BW), BF16),
                   jax.ShapeDtypeStruct((S5_C, BW), BF16)],
        compiler_params=pltpu.CompilerParams(vmem_limit_bytes=VMEM_LIMIT),
    )(p["lre"], p["lim"], p["lstep"], p["btr"], p["bti"], p["ctr"], p["cti"])


S5_PARAM_NAMES = ("lre", "lim", "lstep", "btr", "bti", "ctr", "cti")


def s5_params_vjp(p, dl_r, dl_i, db_re, db_im, dc_top, dc_bot):
    def body(lre, lim, ls, btr, bti, ctr, cti, dlr, dli, dbr, dbi, dct, dcb, *outs):
        args = [r[...] for r in (lre, lim, ls, btr, bti, ctr, cti)]
        _, vjp = jax.vjp(fn_s5params, *args)
        g = vjp((jnp.sum(dlr[...], axis=0, keepdims=True), jnp.sum(dli[...], axis=0, keepdims=True),
                 dbr[...], dbi[...], dct[...], dcb[...]))
        for o_ref, gv in zip(outs, g):
            o_ref[...] = gv

    return pl.pallas_call(
        body, name="s5_params_vjp",
        out_shape=[jax.ShapeDtypeStruct(p[n].shape, F32) for n in S5_PARAM_NAMES],
        compiler_params=pltpu.CompilerParams(vmem_limit_bytes=VMEM_LIMIT),
    )(*[p[n] for n in S5_PARAM_NAMES], dl_r, dl_i, db_re, db_im, dc_top, dc_bot)


def _attn_tile(q, k, v, cq, ckt, *, hpt, dk, q0, tile):
    tq, S = q.shape[0], k.shape[0]
    row = q0 + lax.broadcasted_iota(jnp.int32, (tq, S), 0)
    col = lax.broadcasted_iota(jnp.int32, (tq, S), 1)
    causal = row >= col
    lane = _lane()
    out = jnp.zeros((tq, LANE), F32)
    for h in range(hpt):
        if hpt > 1:
            mine = (lane >> int(math.log2(LANE // hpt))) == h
            qh = jnp.where(mine, q, 0.0)
        else:
            qh = q
        s = dot_nt(qh, k) * (dk ** -0.5)
        if cq is not None:
            head = tile * hpt + h
            cqh = jnp.sum(jnp.where(lane == F_LANE0 + head, cq, 0.0), axis=1, keepdims=True)
            sub = lax.broadcasted_iota(jnp.int32, (8, 1), 0)
            ckh = jnp.sum(jnp.where(sub == head, ckt, 0.0), axis=0, keepdims=True)
            s = s + (cqh - ckh)
        s = jnp.where(causal, s, NEG_INF)
        m = lax.stop_gradient(jnp.max(s, axis=-1, keepdims=True))
        e = jnp.exp(s - m)
        p = e / jnp.sum(e, axis=-1, keepdims=True)
        oh = dot_nn(p, v)
        out = out + (jnp.where(mine, oh, 0.0) if hpt > 1 else oh)
    return out


def attention(name, q, k, v, *, B, S, ntile, hpt, dk, qc=0, kc=0, vc=0, cum=None, ckt=None, do=None, tq=256):
    tq = _pick(S, (tq, 128))
    nq = S // tq
    M = B * S
    bias = cum is not None
    kw = dict(hpt=hpt, dk=dk)

    def load(refs, sk):
        q_ref, k_ref, v_ref = refs[:3]
        qv, kv, vv = q_ref[...].astype(F32), k_ref[0:sk, :].astype(F32), v_ref[0:sk, :].astype(F32)
        if bias:
            return qv, kv, vv, refs[3][...], refs[4][0, :, 0:sk]
        return qv, kv, vv, None, None

    nin = 5 if bias else 3

    def per_query_block(run):
        for g in range(nq):
            @pl.when(pl.program_id(2) == g)
            def _(g=g):
                run(g, (g + 1) * tq)

    def fwd_body(*refs):
        tile = pl.program_id(1)

        def run(g, sk):
            qv, kv, vv, cq, ck = load(refs, sk)
            o = _attn_tile(qv, kv, vv, cq, ck, q0=g * tq, tile=tile, **kw)
            refs[nin][...] = o.astype(refs[nin].dtype)

        per_query_block(run)

    def bwd_body(*refs):
        outs = refs[nin + 1:]
        tile = pl.program_id(1)

        @pl.when(pl.program_id(2) == 0)
        def _():
            for o_ref in (outs[1], outs[2]) + ((outs[4],) if bias else ()):
                o_ref[...] = jnp.zeros_like(o_ref)

        def run(g, sk):
            qv, kv, vv, cq, ck = load(refs, sk)
            dov = refs[nin][...].astype(F32)
            if bias:
                f = lambda a, b, c, d, e: _attn_tile(a, b, c, d, e, q0=g * tq, tile=tile, **kw)
                _, vjp = jax.vjp(f, qv, kv, vv, cq, ck)
            else:
                f = lambda a, b, c: _attn_tile(a, b, c, None, None, q0=g * tq, tile=tile, **kw)
                _, vjp = jax.vjp(f, qv, kv, vv)
            gr = vjp(dov)
            outs[0][...] = gr[0]
            outs[1][0:sk, :] += gr[1]
            outs[2][0:sk, :] += gr[2]
            if bias:
                outs[3][...] = gr[3]
                outs[4][0, :, 0:sk] += gr[4]

        per_query_block(run)

    qspec = lambda c: pl.BlockSpec((tq, LANE), lambda b, t, i: (b * nq + i, c + t))
    kspec = lambda c: pl.BlockSpec((S, LANE), lambda b, t, i: (b, c + t))
    in_specs, args = [qspec(qc), kspec(kc), kspec(vc)], [q, k, v]
    if bias:
        in_specs += [pl.BlockSpec((tq, LANE), lambda b, t, i: (b * nq + i, 0)),
                     pl.BlockSpec((1, 8, S), lambda b, t, i: (b, 0, 0))]
        args += [cum, ckt]
    if do is None:
        return pl.pallas_call(
            fwd_body, name=name, grid=(B, ntile, nq), in_specs=in_specs, out_specs=qspec(0),
            out_shape=jax.ShapeDtypeStruct((M, ntile * LANE), BF16),
            compiler_params=_cp("parallel", "parallel", "parallel"),
        )(*args)
    in_specs.append(qspec(0))
    args.append(do)
    out_specs = [qspec(0), kspec(0), kspec(0)]
    out_shape = [jax.ShapeDtypeStruct((M, ntile * LANE), F32)] * 3
    if bias:
        out_specs += [qspec(0), pl.BlockSpec((1, 8, S), lambda b, t, i: (b * ntile + t, 0, 0))]
        out_shape += [jax.ShapeDtypeStruct((M, ntile * LANE), F32),
                      jax.ShapeDtypeStruct((B * ntile, 8, S), F32)]
    return pl.pallas_call(
        bwd_body, name=name, grid=(B, ntile, nq), in_specs=in_specs, out_specs=out_specs,
        out_shape=out_shape, compiler_params=_cp("parallel", "parallel", "arbitrary"),
    )(*args)


def seq_cumsum(name, x, *, B, S, reverse):
    nb = S // LANE

    def body(x_ref, o_ref):
        r = lax.broadcasted_iota(jnp.int32, (LANE, LANE), 0)
        c = lax.broadcasted_iota(jnp.int32, (LANE, LANE), 1)
        tri = jnp.where((r <= c) if reverse else (r >= c), 1.0, 0.0)
        carry = jnp.zeros((1, LANE), F32)
        for blk in (range(nb - 1, -1, -1) if reverse else range(nb)):
            xb = x_ref[blk * LANE:(blk + 1) * LANE, :]
            o_ref[blk * LANE:(blk + 1) * LANE, :] = xdot(tri, xb) + carry
            carry = carry + jnp.sum(xb, axis=0, keepdims=True)

    return pl.pallas_call(
        body, name=name, grid=(B,), in_specs=[pl.BlockSpec((S, LANE), lambda b: (b, 0))],
        out_specs=pl.BlockSpec((S, LANE), lambda b: (b, 0)),
        out_shape=jax.ShapeDtypeStruct(x.shape, F32), compiler_params=_cp("parallel"),
    )(x)


SCAN_ROWS = 64


def _shift_rows(ref, r0, rows, d, up):
    if d % 8 == 0:
        return ref[pl.ds(r0 + d if up else r0 - d, rows), :]
    pad = -(-d // 8) * 8
    if up:
        win = ref[pl.ds(r0, rows + pad), :]
        return pltpu.roll(win, rows + pad - d, 0)[0:rows, :]
    win = ref[pl.ds(r0 - pad, rows + pad), :]
    return pltpu.roll(win, d, 0)[pad:rows + pad, :]


def _cmul(ar, ai, br, bi):
    return ar * br - ai * bi, ar * bi + ai * br


def s5_scan(name, x_re, x_im, lam, *, B, S, reverse, state=None):
    C = S5_C
    cw = LANE
    R = _pick(S, (SCAN_ROWS,))
    pad = max(S // 2, 8)
    log2s = int(math.log2(S))
    assert 1 << log2s == S
    base = 0 if reverse else pad
    with_grad = state is not None

    def body(*refs):
        if with_grad:
            xr, xi, lam_ref, sr, si, o_r, o_i, dl_r, dl_i, a_r, a_i, b_r, b_i = refs
        else:
            xr, xi, lam_ref, o_r, o_i, a_r, a_i, b_r, b_i = refs
        zero = jnp.zeros((pad, cw), F32)
        z0 = S if reverse else 0
        for buf in (a_r, a_i, b_r, b_i):
            buf[z0:z0 + pad, :] = zero
        mr = lam_ref[0:1, :]
        mi = -lam_ref[1:2, :] if reverse else lam_ref[1:2, :]
        nchunks = S // R
        edge = nchunks - 1 if reverse else 0

        def first_pass(c, _, at_edge=False):
            r_in = edge * R if at_edge else pl.multiple_of(c * R, 8)
            cr, ci = xr[pl.ds(r_in, R), :], xi[pl.ds(r_in, R), :]
            if at_edge:
                row = lax.broadcasted_iota(jnp.int32, (R, 1), 0)
                keep = (row < R - 1) if reverse else (row >= 1)
                shift = R - 1 if reverse else 1
                pr = jnp.where(keep, pltpu.roll(cr, shift, 0), 0.0)
                pi = jnp.where(keep, pltpu.roll(ci, shift, 0), 0.0)
            else:
                pr = _shift_rows(xr, r_in, R, 1, reverse)
                pi = _shift_rows(xi, r_in, R, 1, reverse)
            w0 = base + r_in if at_edge else pl.multiple_of(base + r_in, 8)
            a_r[pl.ds(w0, R), :] = cr + mr * pr - mi * pi
            a_i[pl.ds(w0, R), :] = ci + mr * pi + mi * pr
            return 0

        first_pass(edge, 0, at_edge=True)
        lax.fori_loop(0 if reverse else 1, nchunks - 1 if reverse else nchunks, first_pass, 0)
        mr, mi = _cmul(mr, mi, mr, mi)
        src, dst = (a_r, a_i), (b_r, b_i)
        for step in range(1, log2s):
            d = 1 << step
            last = step == log2s - 1

            def chunk(c, _, src=src, dst=dst, d=d, last=last, mr=mr, mi=mi):
                r0 = pl.multiple_of(base + c * R, 8)
                xr_ = _shift_rows(src[0], r0, R, d, reverse)
                xi_ = _shift_rows(src[1], r0, R, d, reverse)
                nr = src[0][pl.ds(r0, R), :] + mr * xr_ - mi * xi_
                ni = src[1][pl.ds(r0, R), :] + mr * xi_ + mi * xr_
                if last:
                    o0 = pl.multiple_of(c * R, 8)
                    o_r[pl.ds(o0, R), :] = nr
                    o_i[pl.ds(o0, R), :] = ni
                else:
                    dst[0][pl.ds(r0, R), :] = nr
                    dst[1][pl.ds(r0, R), :] = ni
                return 0

            lax.fori_loop(0, S // R, chunk, 0)
            mr, mi = _cmul(mr, mi, mr, mi)
            src, dst = dst, src
        if with_grad:
            def fold(v):
                return jnp.sum(v.reshape(R // 8, 8, cw), axis=0)

            def accum(c, carry, first=False):
                r0 = 0 if first else pl.multiple_of(c * R, 8)
                gr, gi = o_r[pl.ds(r0, R), :], o_i[pl.ds(r0, R), :]
                if first:
                    keep = lax.broadcasted_iota(jnp.int32, (R, 1), 0) >= 1
                    pr = jnp.where(keep, pltpu.roll(sr[0:R, :], 1, 0), 0.0)
                    pi = jnp.where(keep, pltpu.roll(si[0:R, :], 1, 0), 0.0)
                else:
                    pr = _shift_rows(sr, r0, R, 1, False)
                    pi = _shift_rows(si, r0, R, 1, False)
                return (carry[0] + fold(gr * pr + gi * pi), carry[1] + fold(gi * pr - gr * pi))

            acc = accum(0, (jnp.zeros((8, cw), F32), jnp.zeros((8, cw), F32)), first=True)
            acc = lax.fori_loop(1, S // R, accum, acc)
            dl_r[...] = acc[0]
            dl_i[...] = acc[1]

    seq = pl.BlockSpec((S, cw), lambda b, j: (b, j))
    in_specs = [seq, seq, pl.BlockSpec((2, cw), lambda b, j: (0, j))]
    args = [x_re, x_im, lam]
    out_specs = [seq, seq]
    out_shape = [jax.ShapeDtypeStruct(x_re.shape, F32)] * 2
    if with_grad:
        in_specs += [seq, seq]
        args += list(state)
        out_specs += [pl.BlockSpec((8, cw), lambda b, j: (b, j))] * 2
        out_shape += [jax.ShapeDtypeStruct((B * 8, C), F32)] * 2
    return pl.pallas_call(
        body, name=name, grid=(B, C // cw), in_specs=in_specs, out_specs=out_specs, out_shape=out_shape,
        scratch_shapes=[pltpu.VMEM((S + pad, cw), F32)] * 4,
        compiler_params=_cp("parallel", "parallel"),
    )(*args)


CONV_CW = 256


def _conv_taps(ref, r0, rows, first):
    cur = ref[pl.ds(r0, rows), :]
    if first:
        row = lax.broadcasted_iota(jnp.int32, (rows, 1), 0)
        p1 = jnp.where(row >= 1, pltpu.roll(cur, 1, 0), 0.0)
        p2 = jnp.where(row >= 2, pltpu.roll(cur, 2, 0), 0.0)
    else:
        p1 = _shift_rows(ref, r0, rows, 1, False)
        p2 = _shift_rows(ref, r0, rows, 2, False)
    return cur, p1, p2


def _conv_apply(w_ref, taps):
    return w_ref[2:3, :] * taps[0] + w_ref[1:2, :] * taps[1] + w_ref[0:1, :] * taps[2]


def conv_gate_fwd(up, conv_w, *, B, S):
    M, F2 = up.shape
    F = F2 // 2
    cw = _pick(F, (CONV_CW, LANE))
    nf = F // cw
    R = _pick(S, (SCAN_ROWS,))

    def body(g_ref, v_ref, wg_ref, wv_ref, o_ref):
        def chunk(c, _, first=False):
            r0 = 0 if first else pl.multiple_of(c * R, 8)
            cg = _conv_apply(wg_ref, _conv_taps(g_ref, r0, R, first))
            cv = _conv_apply(wv_ref, _conv_taps(v_ref, r0, R, first))
            o_ref[pl.ds(r0, R), :] = (cg * _sigmoid(cg) * cv).astype(o_ref.dtype)
            return 0

        chunk(0, 0, first=True)
        lax.fori_loop(1, S // R, chunk, 0)

    seq = lambda off: pl.BlockSpec((S, cw), lambda b, j: (b, off + j))
    wsp = lambda off: pl.BlockSpec((3, cw), lambda b, j: (0, off + j))
    return pl.pallas_call(
        body, name="conv_gate", grid=(B, nf), in_specs=[seq(0), seq(nf), wsp(0), wsp(nf)],
        out_specs=seq(0), out_shape=jax.ShapeDtypeStruct((M, F), BF16),
        compiler_params=_cp("parallel", "parallel"),
    )(up, up, conv_w, conv_w)


def conv_gate_bwd(up, conv_w, dact, *, B, S):
    M, F2 = up.shape
    F = F2 // 2
    cw = _pick(F, (CONV_CW, LANE))
    nf = F // cw
    R = _pick(S, (SCAN_ROWS,))
    nchunk = S // R

    def body(s_ref, p_ref, ws_ref, wp_ref, da_ref, du_ref, dw_ref, dc_ref, silu_ref):
        is_gate = pl.program_id(1) == 0
        ex = pl.program_id(2)
        dc_ref[S:S + 8, :] = jnp.zeros((8, cw), F32)

        def fold(v):
            return jnp.sum(v.reshape(R // 8, 8, cw), axis=0)

        first_b = ex == 0

        def pass1(gate_step):
            def chunk(c, acc, first=False):
                r0 = 0 if first else pl.multiple_of(c * R, 8)
                taps = _conv_taps(s_ref, r0, R, first)
                da = da_ref[pl.ds(r0, R), :]
                if gate_step:
                    cp = _conv_apply(wp_ref, _conv_taps(p_ref, r0, R, first))
                    cs = _conv_apply(ws_ref, taps)
                    sg = _sigmoid(cs)
                    silu_ref[ex, pl.ds(r0, R), :] = cs * sg
                    dc = da * cp * (sg * (1.0 + cs * (1.0 - sg)))
                else:
                    dc = da * silu_ref[ex, pl.ds(r0, R), :]
                dc_ref[pl.ds(r0, R), :] = dc
                return (acc[0] + fold(dc * taps[2]), acc[1] + fold(dc * taps[1]), acc[2] + fold(dc * taps[0]))

            z = jnp.zeros((8, cw), F32)
            acc = chunk(0, (z, z, z), first=True)
            acc = lax.fori_loop(1, nchunk, chunk, acc)
            for tap in range(3):
                tot = jnp.sum(acc[tap], axis=0, keepdims=True)

                @pl.when(first_b)
                def _(tap=tap, tot=tot):
                    dw_ref[tap:tap + 1, :] = tot

                @pl.when(jnp.logical_not(first_b))
                def _(tap=tap, tot=tot):
                    dw_ref[tap:tap + 1, :] += tot

        @pl.when(is_gate)
        def _():
            pass1(True)

        @pl.when(jnp.logical_not(is_gate))
        def _():
            pass1(False)

        def pass2(c, _):
            r0 = pl.multiple_of(c * R, 16)
            n0 = dc_ref[pl.ds(r0, R), :]
            n1 = _shift_rows(dc_ref, r0, R, 1, True)
            n2 = _shift_rows(dc_ref, r0, R, 2, True)
            du = ws_ref[2:3, :] * n0 + ws_ref[1:2, :] * n1 + ws_ref[0:1, :] * n2
            du_ref[pl.ds(r0, R), :] = du.astype(du_ref.dtype)
            return 0

        lax.fori_loop(0, nchunk, pass2, 0)

    seq = lambda f: pl.BlockSpec((S, cw), lambda j, t, b: (b, f(j, t)))
    wsp = lambda f: pl.BlockSpec((3, cw), lambda j, t, b: (0, f(j, t)))
    same, value, act_col = (lambda j, t: j + t * nf), (lambda j, t: j + nf), (lambda j, t: j)
    return pl.pallas_call(
        body, name="conv_gate_vjp", grid=(nf, 2, B),
        in_specs=[seq(same), seq(value), wsp(same), wsp(value), seq(act_col)],
        out_specs=[seq(same), wsp(same)],
        out_shape=[jax.ShapeDtypeStruct((M, F2), BF16), jax.ShapeDtypeStruct((3, F2), F32)],
        scratch_shapes=[pltpu.VMEM((S + 8, cw), F32), pltpu.VMEM((B, S, cw), F32)],
        compiler_params=_cp("parallel", "arbitrary", "arbitrary"),
    )(up, up, conv_w, conv_w, dact)


def loss_head(y, target):
    M, D = y.shape
    tm = _pick(M, (256, 128, 64, 32, 16, 8))

    def body(y_ref, t_ref, dy_ref, l_ref):
        diff = y_ref[...] - t_ref[...]
        dy_ref[...] = diff * (1.0 / D)
        part = jnp.sum(jnp.sum(diff * diff, axis=1, keepdims=True), axis=0, keepdims=True)

        @pl.when(pl.program_id(0) == 0)
        def _():
            l_ref[...] = jnp.zeros_like(l_ref)

        l_ref[...] += part

    row = pl.BlockSpec((tm, D), lambda i: (i, 0))
    return pl.pallas_call(
        body, name="loss_head", grid=(M // tm,), in_specs=[row, row],
        out_specs=[row, pl.BlockSpec((8, LANE), lambda i: (0, 0))],
        out_shape=[jax.ShapeDtypeStruct((M, D), F32), jax.ShapeDtypeStruct((8, LANE), F32)],
        compiler_params=_cp("arbitrary"),
    )(y, target)


def adamw(name, w, g, m, v):
    R, C = w.shape
    tr = _pick(R, (256, 128, 64, 32, 16, 8))

    def body(w_ref, g_ref, m_ref, v_ref, d_ref, nm_ref, nv_ref, go_ref):
        gv = g_ref[...]
        go_ref[...] = gv
        nm = ADAM_B1 * m_ref[...] + (1.0 - ADAM_B1) * gv
        nv = ADAM_B2 * v_ref[...] + (1.0 - ADAM_B2) * (gv * gv)
        m_hat = nm / (1.0 - ADAM_B1 ** ADAM_STEP)
        v_hat = nv / (1.0 - ADAM_B2 ** ADAM_STEP)
        d_ref[...] = -ADAM_LR * (m_hat / (jnp.sqrt(v_hat) + ADAM_EPS) + ADAM_WD * w_ref[...])
        nm_ref[...] = nm
        nv_ref[...] = nv

    blk = pl.BlockSpec((tr, C), lambda i: (i, 0))
    return pl.pallas_call(
        body, name=name, grid=(R // tr,), in_specs=[blk] * 4, out_specs=[blk] * 4,
        out_shape=[jax.ShapeDtypeStruct((R, C), F32)] * 4, compiler_params=_cp("parallel"),
    )(w, g, m, v)


def _seg(D):
    o = 3 * D
    return dict(ckv=o, fq=o + 256, fk=o + 512, fv=o + 768, u=o + 1024, small=o + 1280, cq=o + 1536, P=o + 1920)


def _pad_last(a, n):
    return jnp.pad(a, [(0, 0)] * (a.ndim - 1) + [(0, n - a.shape[-1])])


def _place(a, lo, n=LANE):
    return jnp.pad(a, [(0, 0)] * (a.ndim - 1) + [(lo, n - lo - a.shape[-1])])


def _in_segments(D):
    s = _seg(D)
    sm = s["small"]
    return ((0, 384, s["cq"]), (384, 640, s["ckv"]), (640, 672, sm + NOPE), (672, 1440, s["fq"]),
            (1440, 1444, sm + F_LANE0), (1444, 1700, s["u"]), (1700, 1700 + 3 * D, 0))


def _chip_pieces(win4, lo, hi):
    cw = win4.shape[-1]
    out = []
    for k in range(N_CHIPS):
        a, b = max(lo, k * cw), min(hi, (k + 1) * cw)
        if a < b:
            out.append(win4[:, k, :, a - k * cw:b - k * cw])
    return out


def prep_weights(w):
    L, D = w["attn_norm_g"].shape
    win = w["w_in"]
    z = lambda n: [jnp.zeros((L, D, n), win.dtype)]
    cols = lambda lo, hi: _chip_pieces(win, lo, hi)
    g0 = 1700
    wp = jnp.concatenate(
        cols(g0, g0 + 3 * D) + cols(384, 640) + cols(672, 1440) + cols(1444, 1700) + z(NOPE) + cols(640, 672)
        + cols(1440, 1444) + z(LANE - F_LANE0 - N_HEADS) + z(LANE) + cols(0, 384), axis=-1)
    wukv, wb = w["w_ukv"], w["w_branch"]
    row3 = lambda a: a[:, None, :]
    return dict(
        g1=row3(w["attn_norm_g"]), Wp=wp, gql=row3(w["q_lat_norm_g"]), gkvl=row3(w["kv_lat_norm_g"]),
        Wuq=_pad_last(w["w_uq"], LANE).reshape(L, Q_RANK, N_HEADS * LANE),
        Wk=_pad_last(wukv[..., :NOPE], LANE).reshape(L, KV_RANK, N_HEADS * LANE),
        Wv=_pad_last(wukv[..., NOPE:], LANE).reshape(L, KV_RANK, N_HEADS * LANE),
        gq=row3(_pad_last(w["mla_q_norm_g"], LANE)), gk=row3(_pad_last(w["mla_k_norm_g"], LANE)),
        gfq=row3(jnp.tile(w["fox_q_norm_g"], (1, 2))), gfk=row3(jnp.tile(w["fox_k_norm_g"], (1, 2))),
        fbias=row3(_place(w["fox_f_bias"], F_LANE0)),
        lre=w["s5_lambda_re"].reshape(L, 1, S5_C), lim=w["s5_lambda_im"].reshape(L, 1, S5_C),
        lstep=row3(_pad_last(w["s5_log_step"], LANE)),
        btr=jnp.transpose(w["s5_b_re"], (0, 3, 1, 2)).reshape(L, S5_H, S5_C),
        bti=jnp.transpose(w["s5_b_im"], (0, 3, 1, 2)).reshape(L, S5_H, S5_C),
        ctr=_pad_last(jnp.transpose(w["s5_c_re"], (0, 1, 3, 2)).reshape(L, S5_C, S5_H), LANE),
        cti=_pad_last(jnp.transpose(w["s5_c_im"], (0, 1, 3, 2)).reshape(L, S5_C, S5_H), LANE),
        s5d=w["s5_d"].reshape(L, 1, BW), Wglu=w["s5_w_glu"], bglu=row3(w["s5_b_glu"]),
        Wb0=jnp.pad(wb[:, 0].reshape(L, N_HEADS, V_DIM, D), ((0, 0), (0, 0), (0, LANE - V_DIM), (0, 0))
                    ).reshape(L, N_HEADS * LANE, D),
        Wb1=wb[:, 1], Wb2=wb[:, 2], Wout=w["w_out"], g2=row3(w["ffn_norm_g"]), Wup=w["w_up"],
        convw=w["ffn_conv_w"], Wdown=w["w_down"],
    )


BIG_KEYS = ("Wp", "Wuq", "Wk", "Wv", "Wout", "Wup", "Wdown")


def with_transposes(P):
    out = dict(P)
    for k in BIG_KEYS:
        out[k + "T"] = jnp.swapaxes(P[k], 1, 2)
    return out


def layer_params(P, l):
    return {k: (v if k in BIG_KEYS or k[:-1] in BIG_KEYS else v[l]) for k, v in P.items()}


def unprep_grads(G, D):
    L = G["g1"].shape[0]
    dwp = G["Wp"]
    segs = _in_segments(D)
    cw = segs[-1][1] // N_CHIPS
    chips = []
    for k in range(N_CHIPS):
        pieces = []
        for lo, hi, at in segs:
            a, b = max(lo, k * cw), min(hi, (k + 1) * cw)
            if a < b:
                pieces.append(dwp[..., at + a - lo:at + b - lo])
        chips.append(jnp.concatenate(pieces, axis=-1))
    w_in = jnp.stack(chips, axis=1)
    heads = lambda a, rows, keep: a.reshape(L, rows, N_HEADS, LANE)[..., :keep]
    wb0 = G["Wb0"].reshape(L, N_HEADS, LANE, D)[:, :, :V_DIM].reshape(L, BW, D)
    gf = lambda a: a[:, 0, :FOX_DIM] + a[:, 0, FOX_DIM:]
    return dict(
        attn_norm_g=G["g1"][:, 0], w_in=w_in, q_lat_norm_g=G["gql"][:, 0], w_uq=heads(G["Wuq"], Q_RANK, QK_DIM),
        kv_lat_norm_g=G["gkvl"][:, 0],
        w_ukv=jnp.concatenate([heads(G["Wk"], KV_RANK, NOPE), heads(G["Wv"], KV_RANK, V_DIM)], axis=-1),
        mla_q_norm_g=G["gq"][:, 0, :QK_DIM], mla_k_norm_g=G["gk"][:, 0, :QK_DIM],
        fox_q_norm_g=gf(G["gfq"]), fox_k_norm_g=gf(G["gfk"]),
        fox_f_bias=G["fbias"][:, 0, F_LANE0:F_LANE0 + N_HEADS],
        s5_lambda_re=G["lre"].reshape(L, S5_G, S5_P), s5_lambda_im=G["lim"].reshape(L, S5_G, S5_P),
        s5_b_re=jnp.transpose(G["btr"].reshape(L, S5_H, S5_G, S5_P), (0, 2, 3, 1)),
        s5_b_im=jnp.transpose(G["bti"].reshape(L, S5_H, S5_G, S5_P), (0, 2, 3, 1)),
        s5_c_re=jnp.transpose(G["ctr"][..., :S5_H].reshape(L, S5_G, S5_P, S5_H), (0, 1, 3, 2)),
        s5_c_im=jnp.transpose(G["cti"][..., :S5_H].reshape(L, S5_G, S5_P, S5_H), (0, 1, 3, 2)),
        s5_d=G["s5d"].reshape(L, S5_G, S5_H), s5_log_step=G["lstep"][:, 0, :S5_G],
        s5_w_glu=G["Wglu"], s5_b_glu=G["bglu"][:, 0],
        w_branch=jnp.stack([wb0, G["Wb1"], G["Wb2"]], axis=1), w_out=G["Wout"], ffn_norm_g=G["g2"][:, 0],
        w_up=G["Wup"], ffn_conv_w=G["convw"], w_down=G["Wdown"],
    )


def rope_tables(positions):
    inv_freq = ROPE_THETA ** (-jnp.arange(0, ROPE, 2, dtype=F32) / ROPE)
    ang = positions.astype(F32)[..., None] * inv_freq
    cos, sin = jnp.cos(ang), jnp.sin(ang)
    ones = jnp.ones(ang.shape[:-1] + (NOPE,), F32)
    zeros = jnp.zeros(ang.shape[:-1] + (LANE - NOPE - ROPE,), F32)
    cos_t = jnp.concatenate([ones, cos, cos, zeros], axis=-1)
    sin_t = jnp.concatenate([0.0 * ones, sin, sin, zeros], axis=-1)
    return cos_t.reshape(-1, LANE), sin_t.reshape(-1, LANE)


def fn_rms_res(n):
    return lambda x, g: (_rms(x, g, n), x)


def _s5_mats(p):
    return s5_params({k: p[k] for k in S5_PARAM_NAMES})


def layer_fwd(x, p, l, cos_t, sin_t, B, S):
    M, D = x.shape
    s = _seg(D)
    sm = s["small"] // LANE
    head = lambda j: j
    h = rowwise("rms_attn", fn_rms(D), [(x, D, 0)], [p["g1"]], [(D, BF16)])[0]
    proj = mm(h, p["Wp"], "nn", name="in_proj", layer=l)
    cnq = rowwise("latq_norm", fn_rms(Q_RANK), [(proj, Q_RANK, s["cq"] // Q_RANK)], [p["gql"]], [(Q_RANK, BF16)])[0]
    cnkv = rowwise("latkv_norm", fn_rms(KV_RANK), [(proj, KV_RANK, s["ckv"] // KV_RANK)], [p["gkvl"]],
                   [(KV_RANK, BF16)])[0]
    qraw = mm(cnq, p["Wuq"], "nn", name="q_up", layer=l)
    kn = mm(cnkv, p["Wk"], "nn", name="k_up", layer=l)
    v5 = mm(cnkv, p["Wv"], "nn", name="v_up", out_dtype=BF16, layer=l)
    qrot = rowwise("q_post", fn_qpost, [(qraw, LANE, head), (cos_t, LANE, 0), (sin_t, LANE, 0)], [p["gq"]],
                   [(LANE, BF16)], nj=N_HEADS)[0]
    krot = rowwise("k_post", fn_kpost, [(kn, LANE, head), (proj, LANE, sm), (cos_t, LANE, 0), (sin_t, LANE, 0)],
                   [p["gk"]], [(LANE, BF16)], nj=N_HEADS)[0]
    omla = attention("mla_attn", qrot, krot, v5, B=B, S=S, ntile=N_HEADS, hpt=1, dk=QK_DIM)
    fq0, fk0 = s["fq"] // LANE, s["fk"] // LANE
    qf = rowwise("foxq_norm", fn_foxnorm, [(proj, LANE, lambda j: fq0 + j)], [p["gfq"]], [(LANE, BF16)], nj=2)[0]
    kf = rowwise("foxk_norm", fn_foxnorm, [(proj, LANE, lambda j: fk0 + j)], [p["gfk"]], [(LANE, BF16)], nj=2)[0]
    lf = rowwise("fgate", fn_fgate, [(proj, LANE, sm)], [p["fbias"]], [(LANE, F32)])[0]
    cum = seq_cumsum("fox_cumsum", lf, B=B, S=S, reverse=False)
    ckt = _pad_rows8(jnp.transpose(cum.reshape(B, S, LANE)[:, :, F_LANE0:F_LANE0 + N_HEADS], (0, 2, 1)))
    ofox = attention("fox_attn", qf, kf, proj, B=B, S=S, ntile=2, hpt=2, dk=FOX_DIM, vc=s["fv"] // LANE,
                     cum=cum, ckt=ckt)
    lam, bre, bim, ctop, cbot = _s5_mats(p)
    u16 = proj[:, s["u"]:s["u"] + BW].astype(BF16)
    bur = mm(u16, bre, "nn", name="s5_bu_re")
    bui = mm(u16, bim, "nn", name="s5_bu_im")
    sr, si = s5_scan("s5_scan", bur, bui, lam, B=B, S=S, reverse=False)
    ypre = mm(si, cbot, "nn", name="s5_y_im", add=mm(sr, ctop, "nn", name="s5_y_re"))
    os5 = rowwise("s5_post", fn_s5post, [(ypre, BW, 0), (proj, BW, s["u"] // BW)],
                  [p["s5d"], p["Wglu"], p["bglu"]], [(BW, BF16)])[0]
    merged = rowwise("merge", fn_merge,
                     [(omla, N_HEADS * LANE, 0), (ofox, BW, 0), (os5, BW, 0), (proj, D, 0), (proj, D, 1), (proj, D, 2)],
                     [p["Wb0"], p["Wb1"], p["Wb2"]], [(D, BF16)])[0]
    xmid = mm(merged, p["Wout"], "nn", name="out_proj", add=x, layer=l)
    h2 = rowwise("rms_ffn", fn_rms(D), [(xmid, D, 0)], [p["g2"]], [(D, BF16)])[0]
    up = mm(h2, p["Wup"], "nn", name="ffn_up", layer=l)
    act = conv_gate_fwd(up, p["convw"], B=B, S=S)
    xout = mm(act, p["Wdown"], "nn", name="ffn_down", add=xmid, layer=l)
    saved = dict(x=x, h=h, proj=proj, cnq=cnq, cnkv=cnkv, qraw=qraw, kn=kn, v5=v5, qrot=qrot, krot=krot, qf=qf,
                 kf=kf, cum=cum, ckt=ckt, omla=omla, ofox=ofox, os5=os5, u16=u16, sr=sr, si=si, ypre=ypre,
                 merged=merged, xmid=xmid, h2=h2, up=up, act=act)
    return xout, saved


def _pad_rows8(a):
    return jnp.pad(a, ((0, 0), (0, 8 - a.shape[1]), (0, 0)))


STACKED = ("Wp", "Wout", "Wup", "Wdown")


def layer_bwd(dx, p, l, sv, cos_t, sin_t, B, S, stacks):
    M, D = dx.shape
    slot = lambda k: (stacks["L"], l, stacks.get(k))
    s = _seg(D)
    sm = s["small"] // LANE
    head = lambda j: j
    proj = sv["proj"]
    dact = mm(dx, p["WdownT"], "nn", name="ffn_down_dx", layer=l)
    d_wdown = mm(sv["act"], dx, "tn", name="ffn_down_dw", slot=slot("Wdown"))
    dup, d_convw = conv_gate_bwd(sv["up"], p["convw"], dact, B=B, S=S)
    dh2 = mm(dup, p["WupT"], "nn", name="ffn_up_dx", layer=l)
    d_wup = mm(sv["h2"], dup, "tn", name="ffn_up_dw", slot=slot("Wup"))
    (dxmid,), (d_g2,) = rowwise_vjp("rms_ffn_vjp", fn_rms_res(D), [(sv["xmid"], D, 0)], [p["g2"]],
                                    [(dh2, D, 0), (dx, D, 0)], [0])
    dmerged = mm(dxmid, p["WoutT"], "nn", name="out_proj_dx", layer=l)
    d_wout = mm(sv["merged"], dxmid, "tn", name="out_proj_dw", slot=slot("Wout"))
    (dom, dof, dos, dg0, dg1, dg2), (d_wb0, d_wb1, d_wb2) = rowwise_vjp(
        "merge_vjp", fn_merge,
        [(sv["omla"], N_HEADS * LANE, 0), (sv["ofox"], BW, 0), (sv["os5"], BW, 0), (proj, D, 0), (proj, D, 1),
         (proj, D, 2)], [p["Wb0"], p["Wb1"], p["Wb2"]], [(dmerged, D, 0)], [0, 1, 2, 3, 4, 5], tm=256,
        gdt=[BF16, BF16, F32, BF16, BF16, BF16])
    lam, bre, bim, ctop, cbot = _s5_mats(p)
    (dypre, du_a), (d_s5d, d_wglu, d_bglu) = rowwise_vjp(
        "s5_post_vjp", fn_s5post, [(sv["ypre"], BW, 0), (proj, BW, s["u"] // BW)],
        [p["s5d"], p["Wglu"], p["bglu"]], [(dos, BW, 0)], [0, 1], gdt=[BF16, F32])
    dsr = mm(dypre, ctop.T, "nn", name="s5_y_re_dx")
    dsi = mm(dypre, cbot.T, "nn", name="s5_y_im_dx")
    d_ctop = mm(sv["sr"], dypre, "tn", name="s5_y_re_dw")
    d_cbot = mm(sv["si"], dypre, "tn", name="s5_y_im_dw")
    gr, gi, dl_r, dl_i = s5_scan("s5_scan_vjp", dsr, dsi, lam, B=B, S=S, reverse=True, state=(sv["sr"], sv["si"]))
    du = mm(gi, bim.T, "nn", name="s5_bu_im_dx", out_dtype=BF16,
            add=mm(gr, bre.T, "nn", name="s5_bu_re_dx", add=du_a))
    d_bre = mm(sv["u16"], gr, "tn", name="s5_bu_re_dw")
    d_bim = mm(sv["u16"], gi, "tn", name="s5_bu_im_dw")
    d_s5 = s5_params_vjp({k: p[k] for k in S5_PARAM_NAMES}, dl_r, dl_i, d_bre, d_bim, d_ctop, d_cbot)
    dqf, dkf, dfv, dcq_t, dckt_t = attention("fox_attn_vjp", sv["qf"], sv["kf"], proj, B=B, S=S, ntile=2, hpt=2,
                                             dk=FOX_DIM, vc=s["fv"] // LANE, cum=sv["cum"], ckt=sv["ckt"], do=dof)
    dck = dckt_t.reshape(B, 2, 8, S)
    dck = jnp.transpose(dck[:, 0, :N_HEADS] + dck[:, 1, :N_HEADS], (0, 2, 1)).reshape(M, N_HEADS)
    dcum = rowwise("fox_dcum", lambda a, b: (a[:, 0:LANE] + a[:, LANE:2 * LANE] + b,),
                   [(dcq_t, 2 * LANE, 0), (_place(dck, F_LANE0), LANE, 0)], [], [(LANE, F32)])[0]
    dlf = seq_cumsum("fox_cumsum_vjp", dcum, B=B, S=S, reverse=True)
    (dsmall_f,), (d_fbias,) = rowwise_vjp("fgate_vjp", fn_fgate, [(proj, LANE, sm)], [p["fbias"]],
                                          [(dlf, LANE, 0)], [0])
    fq0, fk0 = s["fq"] // LANE, s["fk"] // LANE
    (dfq,), (d_gfq,) = rowwise_vjp("foxq_norm_vjp", fn_foxnorm, [(proj, LANE, lambda j: fq0 + j)], [p["gfq"]],
                                   [(dqf, LANE, head)], [0], nj=2, gdt=[BF16])
    (dfk,), (d_gfk,) = rowwise_vjp("foxk_norm_vjp", fn_foxnorm, [(proj, LANE, lambda j: fk0 + j)], [p["gfk"]],
                                   [(dkf, LANE, head)], [0], nj=2, gdt=[BF16])
    dqrot, dkrot, dv5 = attention("mla_attn_vjp", sv["qrot"], sv["krot"], sv["v5"], B=B, S=S, ntile=N_HEADS,
                                  hpt=1, dk=QK_DIM, do=dom)
    (dqraw,), (d_gq,) = rowwise_vjp("q_post_vjp", fn_qpost,
                                    [(sv["qraw"], LANE, head), (cos_t, LANE, 0), (sin_t, LANE, 0)], [p["gq"]],
                                    [(dqrot, LANE, head)], [0], nj=N_HEADS, gdt=[BF16])
    (dkn, dsmall_k), (d_gk,) = rowwise_vjp(
        "k_post_vjp", fn_kpost, [(sv["kn"], LANE, head), (proj, LANE, sm), (cos_t, LANE, 0), (sin_t, LANE, 0)],
        [p["gk"]], [(dkrot, LANE, head)], [0, 1], nj=N_HEADS, gdt=[BF16, F32])
    dsmall = rowwise("small_sum", fn_add5, [(dsmall_k, N_HEADS * LANE, 0), (dsmall_f, LANE, 0)], [], [(LANE, BF16)])[0]
    dcnq = mm(dqraw, p["WuqT"], "nn", name="q_up_dx", layer=l)
    d_wuq = mm(sv["cnq"], dqraw, "tn", name="q_up_dw")
    dcnkv = mm(dv5, p["WvT"], "nn", name="v_up_dx", layer=l, add=mm(dkn, p["WkT"], "nn", name="k_up_dx", layer=l))
    d_wk = mm(sv["cnkv"], dkn, "tn", name="k_up_dw")
    d_wv = mm(sv["cnkv"], dv5, "tn", name="v_up_dw")
    (dcq,), (d_gql,) = rowwise_vjp("latq_norm_vjp", fn_rms(Q_RANK), [(proj, Q_RANK, s["cq"] // Q_RANK)], [p["gql"]],
                                   [(dcnq, Q_RANK, 0)], [0], gdt=[BF16])
    (dckv,), (d_gkvl,) = rowwise_vjp("latkv_norm_vjp", fn_rms(KV_RANK), [(proj, KV_RANK, s["ckv"] // KV_RANK)],
                                     [p["gkvl"]], [(dcnkv, KV_RANK, 0)], [0], gdt=[BF16])
    dproj = jnp.concatenate([dg0, dg1, dg2, dckv, dfq, dfk, dfv.astype(BF16), du, dsmall,
                             jnp.zeros((M, LANE), BF16), dcq], axis=1)
    dh = mm(dproj, p["WpT"], "nn", name="in_proj_dx", layer=l)
    d_wp = mm(sv["h"], dproj, "tn", name="in_proj_dw", slot=slot("Wp"))
    (dxin,), (d_g1,) = rowwise_vjp("rms_attn_vjp", fn_rms_res(D), [(sv["x"], D, 0)], [p["g1"]],
                                   [(dh, D, 0), (dxmid, D, 0)], [0])
    grads = dict(g1=d_g1, Wp=d_wp, gql=d_gql, gkvl=d_gkvl, Wuq=d_wuq, Wk=d_wk, Wv=d_wv, gq=d_gq, gk=d_gk,
                 gfq=d_gfq, gfk=d_gfk, fbias=d_fbias, s5d=d_s5d, Wglu=d_wglu, bglu=d_bglu, Wb0=d_wb0, Wb1=d_wb1,
                 Wb2=d_wb2, Wout=d_wout, g2=d_g2, Wup=d_wup, convw=d_convw, Wdown=d_wdown)
    grads.update(dict(zip(S5_PARAM_NAMES, d_s5)))
    return dxin, grads


def local_step(x, positions, target, w):
    B, S, D = x.shape
    M = B * S
    P = with_transposes(prep_weights(w))
    L = P["g1"].shape[0]
    cos_t, sin_t = rope_tables(positions)
    xc, saved = x.reshape(M, D), []
    for l in range(L):
        xc, sv = layer_fwd(xc, layer_params(P, l), l, cos_t, sin_t, B, S)
        saved.append(sv)
    dxc, sq = loss_head(xc, target.reshape(M, D))
    grads, stacks = [None] * L, {"L": L}
    for l in reversed(range(L)):
        dxc, grads[l] = layer_bwd(dxc, layer_params(P, l), l, saved[l], cos_t, sin_t, B, S, stacks)
        stacks.update({k: grads[l][k] for k in STACKED})
    G = {k: (stacks[k] if k in STACKED else jnp.stack([g[k] for g in grads])) for k in grads[0]}
    return sq, dxc.reshape(B, S, D), unprep_grads(G, D)


MESH = pl.DeviceIdType.MESH
ANY = pl.BlockSpec(memory_space=pl.ANY)
N_CHIPS = 4
SHARDED = ("w_in", "w_uq", "w_ukv", "s5_w_glu", "w_branch", "w_out", "w_up", "ffn_conv_w", "w_down")
MINOR = ("w_branch", "w_up", "ffn_conv_w")
F32_TRAVEL = ("ffn_conv_w",)
WEIGHTS = ("attn_norm_g", "w_in", "q_lat_norm_g", "w_uq", "kv_lat_norm_g", "w_ukv", "mla_q_norm_g", "mla_k_norm_g",
           "fox_q_norm_g", "fox_k_norm_g", "fox_f_bias", "s5_lambda_re", "s5_lambda_im", "s5_b_re", "s5_b_im",
           "s5_c_re", "s5_c_im", "s5_d", "s5_log_step", "s5_w_glu", "s5_b_glu", "w_branch", "w_out", "ffn_norm_g",
           "w_up", "ffn_conv_w", "w_down")
SMALL = tuple(n for n in WEIGHTS if n not in SHARDED)
RS_BIG = tuple(n for n in SHARDED if n != "ffn_conv_w")
REDUCE_SMALL = SMALL + ("ffn_conv_w",)


def shard3(name, a):
    L = a.shape[0]
    if name in ("w_uq", "w_ukv"):
        return a.reshape(L, a.shape[1], -1)
    if name == "w_branch":
        return a.reshape(L, -1, a.shape[-1])
    return a


def full4(name, a):
    L = a.shape[0]
    if name == "w_in":
        return a
    if name in MINOR:
        return a.reshape(L, 1, -1, a.shape[-1])
    a = a.reshape(L, a.shape[1], -1)
    return a.reshape(L, N_CHIPS, a.shape[1] // N_CHIPS, a.shape[2])


def from_full4(name, a, ref_tail):
    L = a.shape[0]
    if name == "w_in":
        return a
    return a.reshape((L,) + tuple(ref_tail))


def _where():
    x, y, c = lax.axis_index("x"), lax.axis_index("y"), lax.axis_index("c")
    chips = [(1 - x, y), (x, 1 - y), (1 - x, 1 - y)]
    return (x, y, c), 2 * x + y, (x, y, 1 - c), chips, [2 * cx + cy for cx, cy in chips]


def _view(minor, ref4, layers, k):
    if minor:
        cs = ref4.shape[3] // N_CHIPS
        return ref4.at[layers, 0, :, pl.ds(pl.multiple_of(k * cs, LANE), cs)]
    return ref4.at[layers, k]


def _remote(src, dst, ssem, rsem, dev):
    return pltpu.make_async_remote_copy(src_ref=src, dst_ref=dst, send_sem=ssem, recv_sem=rsem,
                                        device_id=dev, device_id_type=MESH)


def gather_weights(shards, minor):
    n = len(shards)
    L = shards[0].shape[0]
    Lh = L // 2
    out_shape = []
    for a, mn in zip(shards, minor):
        _, r, cs = a.shape
        out_shape.append(jax.ShapeDtypeStruct((L, 1, r, N_CHIPS * cs) if mn else (L, N_CHIPS, r, cs), a.dtype))

    def body(*refs):
        w, g = refs[:n], refs[n:2 * n]
        send, recv = refs[2 * n:]
        (x, y, c), me, sib, chips, cidx = _where()
        mine, other, every = pl.ds(c * Lh, Lh), pl.ds((1 - c) * Lh, Lh), pl.ds(0, L)
        dst = lambda i, layers, k: _view(minor[i], g[i], layers, k)
        local = [_remote(w[i], dst(i, every, me), send.at[i, 6], recv.at[i, 6], sib) for i in range(n)]
        first = [_remote(w[i].at[mine], dst(i, mine, me), send.at[i, j], recv.at[i, j], (*chips[j], c))
                 for i in range(n) for j in range(3)]
        for cp in local + first:
            cp.start()
        passed = []
        for i in range(n):
            for j in range(3):
                blk = dst(i, mine, cidx[j])
                _remote(blk, blk, send.at[i, j], recv.at[i, j], (*chips[j], c)).wait_recv()
                fwd = _remote(blk, blk, send.at[i, 3 + j], recv.at[i, 3 + j], sib)
                fwd.start()
                passed.append(fwd)
        for i in range(n):
            for j in range(3):
                blk = dst(i, other, cidx[j])
                _remote(blk, blk, send.at[i, 3 + j], recv.at[i, 3 + j], sib).wait_recv()
        for cp in first + passed:
            cp.wait_send()
        for cp in local:
            cp.wait()

    return pl.pallas_call(
        body, name="gather_weights", in_specs=[ANY] * n, out_specs=[ANY] * n, out_shape=out_shape,
        scratch_shapes=[pltpu.SemaphoreType.DMA((n, 7)), pltpu.SemaphoreType.DMA((n, 7))],
    )(*shards)


def sibling_halves(grads):
    n = len(grads)
    L = grads[0].shape[0]
    Lh = L // 2
    half = [jax.ShapeDtypeStruct((Lh,) + a.shape[1:], a.dtype) for a in grads]

    def body(*refs):
        g, got = refs[:n], refs[n:2 * n]
        send, recv = refs[2 * n:]
        (x, y, c), me, sib, chips, cidx = _where()
        other = pl.ds((1 - c) * Lh, Lh)
        out = [_remote(g[i].at[other], got[i], send.at[i], recv.at[i], sib) for i in range(n)]
        for cp in out:
            cp.start()
        for cp in out:
            cp.wait()

    got = pl.pallas_call(
        body, name="grad_sibling_halves", in_specs=[ANY] * n, out_specs=[ANY] * n, out_shape=half,
        scratch_shapes=[pltpu.SemaphoreType.DMA((n,)), pltpu.SemaphoreType.DMA((n,))],
    )(*grads)
    return got


def scatter_chip_sums(travel, minor):
    n = len(travel)
    Lh = travel[0].shape[0]
    got_shape = []
    for t, mn in zip(travel, minor):
        r, cs = t.shape[2], (t.shape[3] // N_CHIPS if mn else t.shape[3])
        got_shape.append(jax.ShapeDtypeStruct((3, Lh, r, cs), t.dtype))

    def body(*refs):
        s16, got = refs[:n], refs[n:2 * n]
        send, recv = refs[2 * n:]
        (x, y, c), me, sib, chips, cidx = _where()
        every = pl.ds(0, Lh)
        out = [_remote(_view(minor[i], s16[i], every, cidx[j]), got[i].at[j], send.at[i, j], recv.at[i, j],
                       (*chips[j], c)) for i in range(n) for j in range(3)]
        for cp in out:
            cp.start()
        for cp in out:
            cp.wait()

    got = pl.pallas_call(
        body, name="grad_scatter", in_specs=[ANY] * n, out_specs=[ANY] * n, out_shape=got_shape,
        scratch_shapes=[pltpu.SemaphoreType.DMA((n, 3)), pltpu.SemaphoreType.DMA((n, 3))],
    )(*travel)
    return got


def share_halves(shards):
    n = len(shards)
    Lh = shards[0].shape[0] // 2

    def body(*refs):
        full = refs[n:2 * n]
        send, recv = refs[2 * n:]
        (x, y, c), me, sib, chips, cidx = _where()
        mine, other = pl.ds(c * Lh, Lh), pl.ds((1 - c) * Lh, Lh)
        out = [_remote(full[i].at[mine], full[i].at[mine], send.at[i], recv.at[i], sib) for i in range(n)]
        for cp in out:
            cp.start()
        for i in range(n):
            out[i].wait_send()
            _remote(full[i].at[other], full[i].at[other], send.at[i], recv.at[i], sib).wait_recv()

    return pl.pallas_call(
        body, name="grad_share_halves", in_specs=[ANY] * n, out_specs=[ANY] * n,
        out_shape=[jax.ShapeDtypeStruct(a.shape, a.dtype) for a in shards],
        input_output_aliases={i: i for i in range(n)},
        scratch_shapes=[pltpu.SemaphoreType.DMA((n,)), pltpu.SemaphoreType.DMA((n,))],
    )(*shards)


N_DEV = 8


def allreduce_small(v):
    R = v.shape[0]

    def body(x_ref, sum_ref, all_ref, send, recv, loc):
        (x, y, c), me, sib, chips, cidx = _where()

        def rows(px, py, pc):
            return all_ref.at[4 * px + 2 * py + pc]

        def copy(k, block, to, src=None):
            return _remote(rows(*block) if src is None else src, rows(*block), send.at[k], recv.at[k], to)

        mine = pltpu.make_async_copy(x_ref, rows(x, y, c), loc)
        mine.start()
        first = [copy(0, (x, y, c), sib, src=x_ref)]
        first += [copy(1 + j, (x, y, c), (*chip, c), src=x_ref) for j, chip in enumerate(chips)]
        for cp in first:
            cp.start()
        passed = [copy(4 + j, (*chip, c), sib) for j, chip in enumerate(chips)]
        for j, chip in enumerate(chips):
            copy(1 + j, (*chip, c), (x, y, c)).wait_recv()
            passed[j].start()
        copy(0, (x, y, 1 - c), (x, y, c)).wait_recv()
        for j, chip in enumerate(chips):
            copy(4 + j, (*chip, 1 - c), (x, y, c)).wait_recv()
        for cp in first + passed:
            cp.wait_send()
        mine.wait()
        acc = all_ref[0]
        for d in range(1, N_DEV):
            acc = acc + all_ref[d]
        sum_ref[...] = acc

    vm = pl.BlockSpec(memory_space=pltpu.VMEM)
    return pl.pallas_call(
        body, name="allreduce_small", in_specs=[vm], out_specs=[vm, vm],
        out_shape=[jax.ShapeDtypeStruct((R, LANE), F32), jax.ShapeDtypeStruct((N_DEV, R, LANE), F32)],
        scratch_shapes=[pltpu.SemaphoreType.DMA((7,)), pltpu.SemaphoreType.DMA((7,)), pltpu.SemaphoreType.DMA],
        compiler_params=pltpu.CompilerParams(vmem_limit_bytes=VMEM_LIMIT),
    )(v)[0]


EW_BLOCK_BYTES = 2 << 20


def _ew_rows(rows, cols):
    for tr in (1024, 512, 256, 128, 64, 32, 16, 8):
        if rows % tr == 0 and tr * cols * 4 <= EW_BLOCK_BYTES:
            return tr
    return rows


def add_pair(name, full, got, where, travel_dtype):
    R, C = got.shape
    tr = _ew_rows(R, C)
    nblk = R // tr

    def body(w_ref, a_ref, b_ref, s_ref, t_ref):
        s = a_ref[...] + b_ref[...]
        s_ref[...] = s
        t_ref[...] = s.astype(t_ref.dtype)

    blk = pl.BlockSpec((tr, C), lambda i, w: (i, 0))
    return pl.pallas_call(
        body, name=name,
        grid_spec=pltpu.PrefetchScalarGridSpec(
            num_scalar_prefetch=1, grid=(nblk,),
            in_specs=[pl.BlockSpec((tr, C), lambda i, w: (w[1] * nblk + i, 0)), blk], out_specs=[blk, blk]),
        out_shape=[jax.ShapeDtypeStruct((R, C), F32), jax.ShapeDtypeStruct((R, C), travel_dtype)],
        compiler_params=_cp("parallel"),
    )(where, full, got)


def add_four(name, sums, got, where, minor):
    Lh, _, r, C = sums.shape
    cs = got.shape[-1]
    tr = _ew_rows(r, cs)

    def body(w_ref, m_ref, g0, g1, g2, o_ref):
        o_ref[...] = ((m_ref[...] + g0[...].astype(F32)) + g1[...].astype(F32)) + g2[...].astype(F32)

    mine = (pl.BlockSpec((None, None, tr, cs), lambda l, i, w: (l, 0, i, w[0])) if minor
            else pl.BlockSpec((None, None, tr, cs), lambda l, i, w: (l, w[0], i, 0)))
    slot = lambda j: pl.BlockSpec((None, None, tr, cs), lambda l, i, w: (j, l, i, 0))
    return pl.pallas_call(
        body, name=name,
        grid_spec=pltpu.PrefetchScalarGridSpec(
            num_scalar_prefetch=1, grid=(Lh, r // tr), in_specs=[mine, slot(0), slot(1), slot(2)],
            out_specs=pl.BlockSpec((None, tr, cs), lambda l, i, w: (w[1] * Lh + l, i, 0))),
        out_shape=jax.ShapeDtypeStruct((2 * Lh, r, cs), F32), compiler_params=_cp("parallel", "parallel"),
    )(where, sums, got, got, got)


def reduce_scatter_grads(full_grads):
    names = list(RS_BIG)
    minor = [nm in MINOR for nm in names]
    where = jnp.stack([2 * lax.axis_index("x") + lax.axis_index("y"), lax.axis_index("c")]).astype(jnp.int32)
    g4 = [full4(nm, full_grads[nm]) for nm in names]
    got = sibling_halves(g4)
    sums, travel = [], []
    for nm, a, b in zip(names, g4, got):
        s, t = add_pair("chip_sum_" + nm, a.reshape(-1, a.shape[-1]), b.reshape(-1, b.shape[-1]), where,
                        F32 if nm in F32_TRAVEL else BF16)
        sums.append(s.reshape(b.shape))
        travel.append(t.reshape(b.shape))
    arrived = scatter_chip_sums(travel, minor)
    shards = [add_four("shard_sum_" + nm, s, b, where, mn) for nm, s, b, mn in zip(names, sums, arrived, minor)]
    return dict(zip(names, share_halves(shards)))


def pack_small(tree, extra=None, names=SMALL):
    parts = [tree[nm].reshape(-1) for nm in names]
    parts.append(jnp.zeros((1,), F32) if extra is None else extra.reshape(-1))
    blocks = []
    for p in parts:
        rows = _small_rows(p.shape[0])
        blocks.append(jnp.pad(p, (0, rows * LANE - p.shape[0])).reshape(rows, LANE))
    return jnp.concatenate(blocks, axis=0)


def _small_rows(size):
    return -(-size // (8 * LANE)) * 8


def unpack_small(packed, like, names=SMALL):
    out, at = {}, 0
    for nm in names:
        size = math.prod(like[nm].shape)
        rows = _small_rows(size)
        out[nm] = packed[at:at + rows].reshape(-1)[:size].reshape(like[nm].shape)
        at += rows
    return out, packed[at, 0]


def kernel(x, positions, attn_norm_g, w_in, q_lat_norm_g, w_uq, kv_lat_norm_g, w_ukv, mla_q_norm_g, mla_k_norm_g, fox_q_norm_g, fox_k_norm_g, fox_f_bias, s5_lambda_re, s5_lambda_im, s5_b_re, s5_b_im, s5_c_re, s5_c_im, s5_d, s5_log_step, s5_w_glu, s5_b_glu, w_branch, w_out, ffn_norm_g, w_up, ffn_conv_w, w_down, loss_target, m_attn_norm_g, m_w_in, m_q_lat_norm_g, m_w_uq, m_kv_lat_norm_g, m_w_ukv, m_mla_q_norm_g, m_mla_k_norm_g, m_fox_q_norm_g, m_fox_k_norm_g, m_fox_f_bias, m_s5_lambda_re, m_s5_lambda_im, m_s5_b_re, m_s5_b_im, m_s5_c_re, m_s5_c_im, m_s5_d, m_s5_log_step, m_s5_w_glu, m_s5_b_glu, m_w_branch, m_w_out, m_ffn_norm_g, m_w_up, m_ffn_conv_w, m_w_down, v_attn_norm_g, v_w_in, v_q_lat_norm_g, v_w_uq, v_kv_lat_norm_g, v_w_ukv, v_mla_q_norm_g, v_mla_k_norm_g, v_fox_q_norm_g, v_fox_k_norm_g, v_fox_f_bias, v_s5_lambda_re, v_s5_lambda_im, v_s5_b_re, v_s5_b_im, v_s5_c_re, v_s5_c_im, v_s5_d, v_s5_log_step, v_s5_w_glu, v_s5_b_glu, v_w_branch, v_w_out, v_ffn_norm_g, v_w_up, v_ffn_conv_w, v_w_down):
    given = dict(locals())
    w = {nm: given[nm] for nm in WEIGHTS}
    m = {nm: given["m_" + nm] for nm in WEIGHTS}
    v = {nm: given["v_" + nm] for nm in WEIGHTS}
    D = x.shape[-1]

    minor = [nm in MINOR for nm in SHARDED]
    shards = [shard3(nm, w[nm]).astype(F32 if nm in F32_TRAVEL else BF16) for nm in SHARDED]
    gathered = gather_weights(shards, minor)
    full = dict(w)
    for nm, g4 in zip(SHARDED, gathered):
        tail = list(w[nm].shape[1:])
        axis = (len(tail) - 1) if nm in MINOR or nm == "w_in" else 0
        tail[axis] *= N_CHIPS
        full[nm] = from_full4(nm, g4, tail)

    sq, grad_x, gw = local_step(x, positions, loss_target, full)

    big = reduce_scatter_grads(gw)
    total, sq_sum = unpack_small(allreduce_small(pack_small(gw, sq[0:1, 0:1], REDUCE_SMALL)), gw, REDUCE_SMALL)
    loss = 0.5 * sq_sum / D
    conv_cols = w["ffn_conv_w"].shape[-1]
    chip = 2 * lax.axis_index("x") + lax.axis_index("y")
    big["ffn_conv_w"] = lax.dynamic_slice_in_dim(total.pop("ffn_conv_w"), chip * conv_cols, conv_cols, axis=2)

    grads, delta, new_m, new_v = {}, {}, {}, {}
    for nm in SHARDED:
        g = big[nm].reshape(shard3(nm, w[nm]).shape)
        two = lambda a: shard3(nm, a).reshape(-1, g.shape[-1])
        d2, m2, v2, g2 = adamw("adamw_" + nm, two(w[nm]), g.reshape(-1, g.shape[-1]), two(m[nm]), two(v[nm]))
        grads[nm], delta[nm], new_m[nm], new_v[nm] = (a.reshape(w[nm].shape) for a in (g2, d2, m2, v2))
    d2, m2, v2, _ = adamw("adamw_small", pack_small(w), pack_small(total), pack_small(m), pack_small(v))
    for tree, packed in ((delta, d2), (new_m, m2), (new_v, v2)):
        tree.update(unpack_small(packed, w)[0])
    grads.update(total)
    return (loss, grad_x, *[grads[nm] for nm in WEIGHTS], *[delta[nm] for nm in WEIGHTS],
            *[new_m[nm] for nm in WEIGHTS], *[new_v[nm] for nm in WEIGHTS])
```
